```python
import jax, jax.numpy as jnp
from jax import lax
import numpy as np

D_MODEL = 2048
BATCH = 16
SEQ = 2048
DEPTH = 4

CONV_WIDTH = D_MODEL // 4
CONV_GROUPS = 8
CONV_KERNEL = 31
POOL_WIDTH = D_MODEL // 4
POOL_WINDOWS = (2, 4, 8, 16)
POOL_GROUP = POOL_WIDTH // len(POOL_WINDOWS)
QK_NOPE_DIM = 128
QK_ROPE_DIM = 64
V_HEAD_DIM = 128
MLA_HEADS = (D_MODEL - CONV_WIDTH - POOL_WIDTH) // V_HEAD_DIM
MLA_WIDTH = MLA_HEADS * V_HEAD_DIM
Q_LORA_RANK = D_MODEL // 4
KV_LORA_RANK = D_MODEL // 8
MIX_WIDTH = MLA_WIDTH + CONV_WIDTH + POOL_WIDTH
IN_COLS = Q_LORA_RANK + KV_LORA_RANK + QK_ROPE_DIM + 2 * CONV_WIDTH + POOL_WIDTH
D_FF = ((8 * D_MODEL // 3 + 255) // 256) * 256
FFN_CONV_KERNEL = 3
ROPE_THETA = 10000.0
Q_BLOCK = 128
LN_EPS = 1e-5
RMS_EPS = 1e-6
DEEPNORM_ALPHA = (2.0 * DEPTH) ** 0.25
DEEPNORM_BETA = (8.0 * DEPTH) ** -0.25

kernel_name = "hymba_style_mla_conformer_pool_hybrid"


def layer_norm(x, g, b):
    xf = x.astype(jnp.float32)
    mu = jnp.mean(xf, axis=-1, keepdims=True)
    var = jnp.mean(jnp.square(xf - mu), axis=-1, keepdims=True)
    return ((xf - mu) * lax.rsqrt(var + LN_EPS) * g.astype(jnp.float32) + b.astype(jnp.float32)).astype(x.dtype)


def rms_norm(x, g):
    xf = x.astype(jnp.float32)
    ms = jnp.mean(jnp.square(xf), axis=-1, keepdims=True)
    return (xf * lax.rsqrt(ms + RMS_EPS) * g.astype(jnp.float32)).astype(x.dtype)


def causal_dwconv(x, w, b):
    k, c = w.shape
    y = lax.conv_general_dilated(
        x, w[:, None, :].astype(x.dtype), window_strides=(1,), padding=[(k - 1, 0)],
        dimension_numbers=("NWC", "WIO", "NWC"), feature_group_count=c)
    return y + b.astype(x.dtype)


def rope_cos_sin(positions):
    inv = 1.0 / (ROPE_THETA ** (jnp.arange(0, QK_ROPE_DIM, 2, dtype=jnp.float32) / QK_ROPE_DIM))
    ang = positions.astype(jnp.float32)[..., None] * inv
    return jnp.cos(ang), jnp.sin(ang)


def apply_rope(x, cos, sin):
    xf = x.astype(jnp.float32)
    x1, x2 = jnp.split(xf, 2, axis=-1)
    return jnp.concatenate([x1 * cos - x2 * sin, x1 * sin + x2 * cos], axis=-1).astype(x.dtype)


def mla_mixer(c_q, c_kv, k_rope, q_norm_g, w_uq, kv_norm_g, w_ukv, cos, sin):
    bsz, seq, _ = c_q.shape
    q = jnp.einsum("bsr,rhd->bshd", rms_norm(c_q, q_norm_g), w_uq)
    q_nope = q[..., :QK_NOPE_DIM]
    q_rope = apply_rope(q[..., QK_NOPE_DIM:], cos[:, :, None, :], sin[:, :, None, :])
    kv = jnp.einsum("bsr,rhd->bshd", rms_norm(c_kv, kv_norm_g), w_ukv)
    k_nope = kv[..., :QK_NOPE_DIM]
    v = kv[..., QK_NOPE_DIM:]
    k_rope = apply_rope(k_rope, cos, sin)
    qb = min(Q_BLOCK, seq)
    nb = seq // qb
    qn_blocks = q_nope.reshape(bsz, nb, qb, MLA_HEADS, QK_NOPE_DIM).transpose(1, 0, 2, 3, 4)
    qr_blocks = q_rope.reshape(bsz, nb, qb, MLA_HEADS, QK_ROPE_DIM).transpose(1, 0, 2, 3, 4)
    key_idx = jnp.arange(seq)
    scale = (QK_NOPE_DIM + QK_ROPE_DIM) ** -0.5
    neg = jnp.finfo(jnp.float32).min

    def attend(args):
        qn, qr, start = args
        s = (jnp.einsum("bqhd,bkhd->bhqk", qn, k_nope, preferred_element_type=jnp.float32)
             + jnp.einsum("bqhr,bkr->bhqk", qr, k_rope, preferred_element_type=jnp.float32)) * scale
        q_idx = start + jnp.arange(qb)
        s = jnp.where(key_idx[None, :] <= q_idx[:, None], s, neg)
        p = jax.nn.softmax(s, axis=-1)
        return jnp.einsum("bhqk,bkhd->bqhd", p.astype(v.dtype), v)

    out = lax.map(attend, (qn_blocks, qr_blocks, jnp.arange(nb) * qb))
    return out.transpose(1, 0, 2, 3, 4).reshape(bsz, seq, MLA_WIDTH)


def conformer_conv_mixer(u, conv_w, conv_b, ln_g, ln_b):
    a, g = jnp.split(u, 2, axis=-1)
    h = a * jax.nn.sigmoid(g)
    h = causal_dwconv(h, conv_w, conv_b)
    h = layer_norm(h, ln_g, ln_b)
    return jax.nn.silu(h)


def pool_mixer(u, w_pool, scale):
    bsz, seq, c = u.shape
    uf = u.astype(jnp.float32)
    cs = jnp.concatenate([jnp.zeros((bsz, 1, c), jnp.float32), lax.cumsum(uf, axis=1)], axis=1)
    t = jnp.arange(seq)
    outs = []
    for gi, w in enumerate(POOL_WINDOWS):
        sl = slice(gi * POOL_GROUP, (gi + 1) * POOL_GROUP)
        lo = jnp.maximum(t + 1 - w, 0)
        win_sum = cs[:, 1:, sl] - cs[:, lo, sl]
        cnt = (t + 1 - lo).astype(jnp.float32)[None, :, None]
        outs.append(win_sum / cnt - uf[:, :, sl])
    d = jnp.stack(outs, axis=2).astype(u.dtype)
    y = jnp.einsum("bsgc,gcd->bsgd", d, w_pool).reshape(bsz, seq, c)
    return y * scale


def _fwd_setup_inputs(seed: int = 0) -> dict:
    key = jax.random.key(seed)
    ks = jax.random.split(key, 24)
    f32 = jnp.float32

    def nrm(k, shape, s):
        return jax.random.normal(k, shape, f32) * s

    def gain(k, shape):
        return 1.0 + 0.02 * jax.random.normal(k, shape, f32)

    L = DEPTH
    x = jax.random.normal(ks[0], (BATCH, SEQ, D_MODEL), f32)
    positions = jnp.broadcast_to(jnp.arange(SEQ, dtype=jnp.int32)[None, :], (BATCH, SEQ))
    return {
        "x": x,
        "positions": positions,
        "ln_in_g": gain(ks[1], (D_MODEL,)),
        "ln_in_b": nrm(ks[2], (D_MODEL,), 0.02),
        "w_in": nrm(ks[3], (L, D_MODEL, IN_COLS), D_MODEL ** -0.5),
        "q_norm_g": gain(ks[4], (L, Q_LORA_RANK)),
        "w_uq": nrm(ks[5], (L, Q_LORA_RANK, MLA_HEADS, QK_NOPE_DIM + QK_ROPE_DIM), Q_LORA_RANK ** -0.5),
        "kv_norm_g": gain(ks[6], (L, KV_LORA_RANK)),
        "w_ukv": nrm(ks[7], (L, KV_LORA_RANK, MLA_HEADS, QK_NOPE_DIM + V_HEAD_DIM), KV_LORA_RANK ** -0.5),
        "conv_w": nrm(ks[8], (L, CONV_KERNEL, CONV_WIDTH), CONV_KERNEL ** -0.5),
        "conv_b": nrm(ks[9], (L, CONV_WIDTH), 0.02),
        "conv_ln_g": gain(ks[10], (L, CONV_WIDTH)),
        "conv_ln_b": nrm(ks[11], (L, CONV_WIDTH), 0.02),
        "w_pool": nrm(ks[12], (L, len(POOL_WINDOWS), POOL_GROUP, POOL_GROUP), POOL_GROUP ** -0.5),
        "pool_scale": gain(ks[13], (L, POOL_WIDTH)),
        "w_out": nrm(ks[14], (L, MIX_WIDTH, D_MODEL), DEEPNORM_BETA * MIX_WIDTH ** -0.5),
        "ln1_g": gain(ks[15], (L, D_MODEL)),
        "ln1_b": nrm(ks[16], (L, D_MODEL), 0.02),
        "w_up": nrm(ks[17], (L, D_MODEL, 2 * D_FF), D_MODEL ** -0.5),
        "ffn_conv_w": nrm(ks[18], (L, FFN_CONV_KERNEL, 2 * D_FF), FFN_CONV_KERNEL ** -0.5),
        "ffn_conv_b": nrm(ks[19], (L, 2 * D_FF), 0.02),
        "w_down": nrm(ks[20], (L, D_FF, D_MODEL), DEEPNORM_BETA * D_FF ** -0.5),
        "ln2_g": gain(ks[21], (L, D_MODEL)),
        "ln2_b": nrm(ks[22], (L, D_MODEL), 0.02),
    }


def _fwd_reference(x, positions, ln_in_g, ln_in_b, w_in, q_norm_g, w_uq, kv_norm_g, w_ukv,
              conv_w, conv_b, conv_ln_g, conv_ln_b, w_pool, pool_scale, w_out, ln1_g, ln1_b,
              w_up, ffn_conv_w, ffn_conv_b, w_down, ln2_g, ln2_b):
    cos, sin = rope_cos_sin(positions)
    x = layer_norm(x, ln_in_g, ln_in_b)
    o1 = Q_LORA_RANK
    o2 = o1 + KV_LORA_RANK
    o3 = o2 + QK_ROPE_DIM
    o4 = o3 + 2 * CONV_WIDTH
    for l in range(DEPTH):
        h = jnp.einsum("bsd,dc->bsc", x, w_in[l])
        c_q, c_kv, k_rope = h[..., :o1], h[..., o1:o2], h[..., o2:o3]
        u_conv, u_pool = h[..., o3:o4], h[..., o4:]
        y_mla = mla_mixer(c_q, c_kv, k_rope, q_norm_g[l], w_uq[l], kv_norm_g[l], w_ukv[l], cos, sin)
        y_conv = conformer_conv_mixer(u_conv, conv_w[l], conv_b[l], conv_ln_g[l], conv_ln_b[l])
        y_pool = pool_mixer(u_pool, w_pool[l], pool_scale[l])
        mixed = jnp.concatenate([y_mla, y_conv, y_pool], axis=-1)
        y = jnp.einsum("bsc,cd->bsd", mixed, w_out[l])
        x = layer_norm(DEEPNORM_ALPHA * x + y, ln1_g[l], ln1_b[l])
        up = jnp.einsum("bsd,df->bsf", x, w_up[l])
        up = causal_dwconv(up, ffn_conv_w[l], ffn_conv_b[l])
        a, g = jnp.split(up, 2, axis=-1)
        y = jnp.einsum("bsf,fd->bsd", a * jax.nn.silu(g), w_down[l])
        x = layer_norm(DEEPNORM_ALPHA * x + y, ln2_g[l], ln2_b[l])
    return x


import jax as _jax
import jax.numpy as _jnp

TWIN_FORMAT = 'train_step'
FWD_PARAMS = ['x', 'positions', 'ln_in_g', 'ln_in_b', 'w_in', 'q_norm_g', 'w_uq', 'kv_norm_g', 'w_ukv', 'conv_w', 'conv_b', 'conv_ln_g', 'conv_ln_b', 'w_pool', 'pool_scale', 'w_out', 'ln1_g', 'ln1_b', 'w_up', 'ffn_conv_w', 'ffn_conv_b', 'w_down', 'ln2_g', 'ln2_b']
TWIN_WEIGHTS = ['ln_in_g', 'ln_in_b', 'w_in', 'q_norm_g', 'w_uq', 'kv_norm_g', 'w_ukv', 'conv_w', 'conv_b', 'conv_ln_g', 'conv_ln_b', 'w_pool', 'pool_scale', 'w_out', 'ln1_g', 'ln1_b', 'w_up', 'ffn_conv_w', 'ffn_conv_b', 'w_down', 'ln2_g', 'ln2_b']
TWIN_DIFF_INPUT = 'x'
TWIN_INPUTS = ['x', 'positions', 'ln_in_g', 'ln_in_b', 'w_in', 'q_norm_g', 'w_uq', 'kv_norm_g', 'w_ukv', 'conv_w', 'conv_b', 'conv_ln_g', 'conv_ln_b', 'w_pool', 'pool_scale', 'w_out', 'ln1_g', 'ln1_b', 'w_up', 'ffn_conv_w', 'ffn_conv_b', 'w_down', 'ln2_g', 'ln2_b', 'loss_target', 'm_ln_in_g', 'm_ln_in_b', 'm_w_in', 'm_q_norm_g', 'm_w_uq', 'm_kv_norm_g', 'm_w_ukv', 'm_conv_w', 'm_conv_b', 'm_conv_ln_g', 'm_conv_ln_b', 'm_w_pool', 'm_pool_scale', 'm_w_out', 'm_ln1_g', 'm_ln1_b', 'm_w_up', 'm_ffn_conv_w', 'm_ffn_conv_b', 'm_w_down', 'm_ln2_g', 'm_ln2_b', 'v_ln_in_g', 'v_ln_in_b', 'v_w_in', 'v_q_norm_g', 'v_w_uq', 'v_kv_norm_g', 'v_w_ukv', 'v_conv_w', 'v_conv_b', 'v_conv_ln_g', 'v_conv_ln_b', 'v_w_pool', 'v_pool_scale', 'v_w_out', 'v_ln1_g', 'v_ln1_b', 'v_w_up', 'v_ffn_conv_w', 'v_ffn_conv_b', 'v_w_down', 'v_ln2_g', 'v_ln2_b']
TWIN_OUTPUTS = ['loss', 'grad_x', 'grad_ln_in_g', 'grad_ln_in_b', 'grad_w_in', 'grad_q_norm_g', 'grad_w_uq', 'grad_kv_norm_g', 'grad_w_ukv', 'grad_conv_w', 'grad_conv_b', 'grad_conv_ln_g', 'grad_conv_ln_b', 'grad_w_pool', 'grad_pool_scale', 'grad_w_out', 'grad_ln1_g', 'grad_ln1_b', 'grad_w_up', 'grad_ffn_conv_w', 'grad_ffn_conv_b', 'grad_w_down', 'grad_ln2_g', 'grad_ln2_b', 'delta_ln_in_g', 'delta_ln_in_b', 'delta_w_in', 'delta_q_norm_g', 'delta_w_uq', 'delta_kv_norm_g', 'delta_w_ukv', 'delta_conv_w', 'delta_conv_b', 'delta_conv_ln_g', 'delta_conv_ln_b', 'delta_w_pool', 'delta_pool_scale', 'delta_w_out', 'delta_ln1_g', 'delta_ln1_b', 'delta_w_up', 'delta_ffn_conv_w', 'delta_ffn_conv_b', 'delta_w_down', 'delta_ln2_g', 'delta_ln2_b', 'new_m_ln_in_g', 'new_m_ln_in_b', 'new_m_w_in', 'new_m_q_norm_g', 'new_m_w_uq', 'new_m_kv_norm_g', 'new_m_w_ukv', 'new_m_conv_w', 'new_m_conv_b', 'new_m_conv_ln_g', 'new_m_conv_ln_b', 'new_m_w_pool', 'new_m_pool_scale', 'new_m_w_out', 'new_m_ln1_g', 'new_m_ln1_b', 'new_m_w_up', 'new_m_ffn_conv_w', 'new_m_ffn_conv_b', 'new_m_w_down', 'new_m_ln2_g', 'new_m_ln2_b', 'new_v_ln_in_g', 'new_v_ln_in_b', 'new_v_w_in', 'new_v_q_norm_g', 'new_v_w_uq', 'new_v_kv_norm_g', 'new_v_w_ukv', 'new_v_conv_w', 'new_v_conv_b', 'new_v_conv_ln_g', 'new_v_conv_ln_b', 'new_v_w_pool', 'new_v_pool_scale', 'new_v_w_out', 'new_v_ln1_g', 'new_v_ln1_b', 'new_v_w_up', 'new_v_ffn_conv_w', 'new_v_ffn_conv_b', 'new_v_w_down', 'new_v_ln2_g', 'new_v_ln2_b']
TWIN_LEAF_KINDS = {'loss': 'loss', 'grad_x': 'grad_x', 'grad_ln_in_g': 'grad_w', 'grad_ln_in_b': 'grad_w', 'grad_w_in': 'grad_w', 'grad_q_norm_g': 'grad_w', 'grad_w_uq': 'grad_w', 'grad_kv_norm_g': 'grad_w', 'grad_w_ukv': 'grad_w', 'grad_conv_w': 'grad_w', 'grad_conv_b': 'grad_w', 'grad_conv_ln_g': 'grad_w', 'grad_conv_ln_b': 'grad_w', 'grad_w_pool': 'grad_w', 'grad_pool_scale': 'grad_w', 'grad_w_out': 'grad_w', 'grad_ln1_g': 'grad_w', 'grad_ln1_b': 'grad_w', 'grad_w_up': 'grad_w', 'grad_ffn_conv_w': 'grad_w', 'grad_ffn_conv_b': 'grad_w', 'grad_w_down': 'grad_w', 'grad_ln2_g': 'grad_w', 'grad_ln2_b': 'grad_w', 'delta_ln_in_g': 'delta_w', 'delta_ln_in_b': 'delta_w', 'delta_w_in': 'delta_w', 'delta_q_norm_g': 'delta_w', 'delta_w_uq': 'delta_w', 'delta_kv_norm_g': 'delta_w', 'delta_w_ukv': 'delta_w', 'delta_conv_w': 'delta_w', 'delta_conv_b': 'delta_w', 'delta_conv_ln_g': 'delta_w', 'delta_conv_ln_b': 'delta_w', 'delta_w_pool': 'delta_w', 'delta_pool_scale': 'delta_w', 'delta_w_out': 'delta_w', 'delta_ln1_g': 'delta_w', 'delta_ln1_b': 'delta_w', 'delta_w_up': 'delta_w', 'delta_ffn_conv_w': 'delta_w', 'delta_ffn_conv_b': 'delta_w', 'delta_w_down': 'delta_w', 'delta_ln2_g': 'delta_w', 'delta_ln2_b': 'delta_w', 'new_m_ln_in_g': 'new_m', 'new_m_ln_in_b': 'new_m', 'new_m_w_in': 'new_m', 'new_m_q_norm_g': 'new_m', 'new_m_w_uq': 'new_m', 'new_m_kv_norm_g': 'new_m', 'new_m_w_ukv': 'new_m', 'new_m_conv_w': 'new_m', 'new_m_conv_b': 'new_m', 'new_m_conv_ln_g': 'new_m', 'new_m_conv_ln_b': 'new_m', 'new_m_w_pool': 'new_m', 'new_m_pool_scale': 'new_m', 'new_m_w_out': 'new_m', 'new_m_ln1_g': 'new_m', 'new_m_ln1_b': 'new_m', 'new_m_w_up': 'new_m', 'new_m_ffn_conv_w': 'new_m', 'new_m_ffn_conv_b': 'new_m', 'new_m_w_down': 'new_m', 'new_m_ln2_g': 'new_m', 'new_m_ln2_b': 'new_m', 'new_v_ln_in_g': 'new_v', 'new_v_ln_in_b': 'new_v', 'new_v_w_in': 'new_v', 'new_v_q_norm_g': 'new_v', 'new_v_w_uq': 'new_v', 'new_v_kv_norm_g': 'new_v', 'new_v_w_ukv': 'new_v', 'new_v_conv_w': 'new_v', 'new_v_conv_b': 'new_v', 'new_v_conv_ln_g': 'new_v', 'new_v_conv_ln_b': 'new_v', 'new_v_w_pool': 'new_v', 'new_v_pool_scale': 'new_v', 'new_v_w_out': 'new_v', 'new_v_ln1_g': 'new_v', 'new_v_ln1_b': 'new_v', 'new_v_w_up': 'new_v', 'new_v_ffn_conv_w': 'new_v', 'new_v_ffn_conv_b': 'new_v', 'new_v_w_down': 'new_v', 'new_v_ln2_g': 'new_v', 'new_v_ln2_b': 'new_v'}


def _forward(args):
    return _fwd_reference(*[args[k] for k in FWD_PARAMS])


def _output_shape():
    out = _jax.eval_shape(lambda: _forward(_fwd_setup_inputs(0)))
    return out.shape, out.dtype

N_MICROBATCH = 1
ADAM_LR = 0.001
ADAM_B1 = 0.9
ADAM_B2 = 0.999
ADAM_EPS = 1e-08
ADAM_WD = 0.01
ADAM_STEP = 10
PER_EXAMPLE_BATCH_AXIS = {'x': 0, 'positions': 0, 'loss_target': 0}
SHARED_INPUTS = []
_WEIGHT_DTYPES = {'ln_in_g': _jnp.float32, 'ln_in_b': _jnp.float32, 'w_in': _jnp.float32, 'q_norm_g': _jnp.float32, 'w_uq': _jnp.float32, 'kv_norm_g': _jnp.float32, 'w_ukv': _jnp.float32, 'conv_w': _jnp.float32, 'conv_b': _jnp.float32, 'conv_ln_g': _jnp.float32, 'conv_ln_b': _jnp.float32, 'w_pool': _jnp.float32, 'pool_scale': _jnp.float32, 'w_out': _jnp.float32, 'ln1_g': _jnp.float32, 'ln1_b': _jnp.float32, 'w_up': _jnp.float32, 'ffn_conv_w': _jnp.float32, 'ffn_conv_b': _jnp.float32, 'w_down': _jnp.float32, 'ln2_g': _jnp.float32, 'ln2_b': _jnp.float32}
MOMENT_SCALE = {'ln_in_g': 4.576314e-01, 'ln_in_b': 2.510992e-01, 'w_in': 1.257013e-02, 'q_norm_g': 5.840641e-03, 'w_uq': 3.317509e-03, 'kv_norm_g': 1.310655e-02, 'w_ukv': 4.427770e-03, 'conv_w': 1.436006e-02, 'conv_b': 4.596134e-02, 'conv_ln_g': 2.312875e-02, 'conv_ln_b': 2.914700e-02, 'w_pool': 1.994831e-02, 'pool_scale': 2.049180e-02, 'w_out': 3.018897e-02, 'ln1_g': 5.002015e-01, 'ln1_b': 2.478306e-01, 'w_up': 8.297189e-03, 'ffn_conv_w': 8.310830e-03, 'ffn_conv_b': 9.133611e-03, 'w_down': 3.225271e-02, 'ln2_g': 8.038615e+00, 'ln2_b': 5.802103e-01}


def _to_microbatches(a, axis):
    t = _jnp.moveaxis(a, axis, 0)
    t = t.reshape((N_MICROBATCH, t.shape[0] // N_MICROBATCH) + t.shape[1:])
    return _jnp.moveaxis(t, 1, axis + 1)


def setup_inputs(seed: int = 0) -> dict:
    inp = _fwd_setup_inputs(seed)
    key = _jax.random.fold_in(_jax.random.key(seed), 7919)
    shape, _ = _output_shape()
    out = dict(inp)
    out["loss_target"] = _jax.random.normal(_jax.random.fold_in(key, 0), shape, _jnp.float32)
    for i, name in enumerate(TWIN_WEIGHTS):
        w = inp[name].astype(_jnp.float32)
        if MOMENT_SCALE is None:
            s = _jnp.sqrt(_jnp.mean(_jnp.square(w)) + 1e-30)
        else:
            s = MOMENT_SCALE[name]
        km, kv = _jax.random.split(_jax.random.fold_in(key, i + 1))
        out[name] = w
        out["m_" + name] = s * _jax.random.normal(km, w.shape, _jnp.float32)
        out["v_" + name] = (s * s) * _jax.random.uniform(kv, w.shape, _jnp.float32, 0.5, 1.5)
    if N_MICROBATCH > 1:
        for name, axis in PER_EXAMPLE_BATCH_AXIS.items():
            out[name] = _to_microbatches(out[name], axis)
    return {'x': out['x'], 'positions': out['positions'], 'ln_in_g': out['ln_in_g'], 'ln_in_b': out['ln_in_b'], 'w_in': out['w_in'], 'q_norm_g': out['q_norm_g'], 'w_uq': out['w_uq'], 'kv_norm_g': out['kv_norm_g'], 'w_ukv': out['w_ukv'], 'conv_w': out['conv_w'], 'conv_b': out['conv_b'], 'conv_ln_g': out['conv_ln_g'], 'conv_ln_b': out['conv_ln_b'], 'w_pool': out['w_pool'], 'pool_scale': out['pool_scale'], 'w_out': out['w_out'], 'ln1_g': out['ln1_g'], 'ln1_b': out['ln1_b'], 'w_up': out['w_up'], 'ffn_conv_w': out['ffn_conv_w'], 'ffn_conv_b': out['ffn_conv_b'], 'w_down': out['w_down'], 'ln2_g': out['ln2_g'], 'ln2_b': out['ln2_b'], 'loss_target': out['loss_target'], 'm_ln_in_g': out['m_ln_in_g'], 'm_ln_in_b': out['m_ln_in_b'], 'm_w_in': out['m_w_in'], 'm_q_norm_g': out['m_q_norm_g'], 'm_w_uq': out['m_w_uq'], 'm_kv_norm_g': out['m_kv_norm_g'], 'm_w_ukv': out['m_w_ukv'], 'm_conv_w': out['m_conv_w'], 'm_conv_b': out['m_conv_b'], 'm_conv_ln_g': out['m_conv_ln_g'], 'm_conv_ln_b': out['m_conv_ln_b'], 'm_w_pool': out['m_w_pool'], 'm_pool_scale': out['m_pool_scale'], 'm_w_out': out['m_w_out'], 'm_ln1_g': out['m_ln1_g'], 'm_ln1_b': out['m_ln1_b'], 'm_w_up': out['m_w_up'], 'm_ffn_conv_w': out['m_ffn_conv_w'], 'm_ffn_conv_b': out['m_ffn_conv_b'], 'm_w_down': out['m_w_down'], 'm_ln2_g': out['m_ln2_g'], 'm_ln2_b': out['m_ln2_b'], 'v_ln_in_g': out['v_ln_in_g'], 'v_ln_in_b': out['v_ln_in_b'], 'v_w_in': out['v_w_in'], 'v_q_norm_g': out['v_q_norm_g'], 'v_w_uq': out['v_w_uq'], 'v_kv_norm_g': out['v_kv_norm_g'], 'v_w_ukv': out['v_w_ukv'], 'v_conv_w': out['v_conv_w'], 'v_conv_b': out['v_conv_b'], 'v_conv_ln_g': out['v_conv_ln_g'], 'v_conv_ln_b': out['v_conv_ln_b'], 'v_w_pool': out['v_w_pool'], 'v_pool_scale': out['v_pool_scale'], 'v_w_out': out['v_w_out'], 'v_ln1_g': out['v_ln1_g'], 'v_ln1_b': out['v_ln1_b'], 'v_w_up': out['v_w_up'], 'v_ffn_conv_w': out['v_ffn_conv_w'], 'v_ffn_conv_b': out['v_ffn_conv_b'], 'v_w_down': out['v_w_down'], 'v_ln2_g': out['v_ln2_g'], 'v_ln2_b': out['v_ln2_b']}


def _loss(weights, diff, rest, loss_target):
    with _jax.named_scope("forward"):
        args = {**rest, TWIN_DIFF_INPUT: diff, **{k: w.astype(_WEIGHT_DTYPES[k]) for k, w in weights.items()}}
        y = _forward(args)
    with _jax.named_scope("loss_head"):
        err = _jnp.square(y.astype(_jnp.float32) - loss_target)
        return 0.5 * _jnp.sum(_jnp.mean(err, axis=-1)) if err.ndim else 0.5 * err


def _adamw(w, g, m, v):
    m = ADAM_B1 * m + (1.0 - ADAM_B1) * g
    v = ADAM_B2 * v + (1.0 - ADAM_B2) * _jnp.square(g)
    m_hat = m / (1.0 - ADAM_B1 ** ADAM_STEP)
    v_hat = v / (1.0 - ADAM_B2 ** ADAM_STEP)
    delta = -ADAM_LR * (m_hat / (_jnp.sqrt(v_hat) + ADAM_EPS) + ADAM_WD * w)
    return delta, m, v


def reference(x, positions, ln_in_g, ln_in_b, w_in, q_norm_g, w_uq, kv_norm_g, w_ukv, conv_w, conv_b, conv_ln_g, conv_ln_b, w_pool, pool_scale, w_out, ln1_g, ln1_b, w_up, ffn_conv_w, ffn_conv_b, w_down, ln2_g, ln2_b, loss_target, m_ln_in_g, m_ln_in_b, m_w_in, m_q_norm_g, m_w_uq, m_kv_norm_g, m_w_ukv, m_conv_w, m_conv_b, m_conv_ln_g, m_conv_ln_b, m_w_pool, m_pool_scale, m_w_out, m_ln1_g, m_ln1_b, m_w_up, m_ffn_conv_w, m_ffn_conv_b, m_w_down, m_ln2_g, m_ln2_b, v_ln_in_g, v_ln_in_b, v_w_in, v_q_norm_g, v_w_uq, v_kv_norm_g, v_w_ukv, v_conv_w, v_conv_b, v_conv_ln_g, v_conv_ln_b, v_w_pool, v_pool_scale, v_w_out, v_ln1_g, v_ln1_b, v_w_up, v_ffn_conv_w, v_ffn_conv_b, v_w_down, v_ln2_g, v_ln2_b):
    given = dict(x=x, positions=positions, ln_in_g=ln_in_g, ln_in_b=ln_in_b, w_in=w_in, q_norm_g=q_norm_g, w_uq=w_uq, kv_norm_g=kv_norm_g, w_ukv=w_ukv, conv_w=conv_w, conv_b=conv_b, conv_ln_g=conv_ln_g, conv_ln_b=conv_ln_b, w_pool=w_pool, pool_scale=pool_scale, w_out=w_out, ln1_g=ln1_g, ln1_b=ln1_b, w_up=w_up, ffn_conv_w=ffn_conv_w, ffn_conv_b=ffn_conv_b, w_down=w_down, ln2_g=ln2_g, ln2_b=ln2_b, loss_target=loss_target, m_ln_in_g=m_ln_in_g, m_ln_in_b=m_ln_in_b, m_w_in=m_w_in, m_q_norm_g=m_q_norm_g, m_w_uq=m_w_uq, m_kv_norm_g=m_kv_norm_g, m_w_ukv=m_w_ukv, m_conv_w=m_conv_w, m_conv_b=m_conv_b, m_conv_ln_g=m_conv_ln_g, m_conv_ln_b=m_conv_ln_b, m_w_pool=m_w_pool, m_pool_scale=m_pool_scale, m_w_out=m_w_out, m_ln1_g=m_ln1_g, m_ln1_b=m_ln1_b, m_w_up=m_w_up, m_ffn_conv_w=m_ffn_conv_w, m_ffn_conv_b=m_ffn_conv_b, m_w_down=m_w_down, m_ln2_g=m_ln2_g, m_ln2_b=m_ln2_b, v_ln_in_g=v_ln_in_g, v_ln_in_b=v_ln_in_b, v_w_in=v_w_in, v_q_norm_g=v_q_norm_g, v_w_uq=v_w_uq, v_kv_norm_g=v_kv_norm_g, v_w_ukv=v_w_ukv, v_conv_w=v_conv_w, v_conv_b=v_conv_b, v_conv_ln_g=v_conv_ln_g, v_conv_ln_b=v_conv_ln_b, v_w_pool=v_w_pool, v_pool_scale=v_pool_scale, v_w_out=v_w_out, v_ln1_g=v_ln1_g, v_ln1_b=v_ln1_b, v_w_up=v_w_up, v_ffn_conv_w=v_ffn_conv_w, v_ffn_conv_b=v_ffn_conv_b, v_w_down=v_w_down, v_ln2_g=v_ln2_g, v_ln2_b=v_ln2_b)
    weights = {n: given[n] for n in TWIN_WEIGHTS}
    shared = {n: given[n] for n in SHARED_INPUTS}
    per_example = {n: given[n] for n in ['x', 'positions']}
    grad_fn = _jax.value_and_grad(_loss, argnums=(0, 1))

    def one_microbatch(ex, loss_target):
        ex = dict(ex)
        diff = ex.pop(TWIN_DIFF_INPUT)
        return grad_fn(weights, diff, {**shared, **ex}, loss_target)

    if N_MICROBATCH == 1:
        loss, (grad_w, grad_x) = one_microbatch(per_example, given["loss_target"])
    else:
        def body(carry, xs):
            loss_sum, grad_sum = carry
            l_k, (gw_k, gx_k) = one_microbatch(xs[0], xs[1])
            with _jax.named_scope("update"):
                return (loss_sum + l_k, _jax.tree.map(_jnp.add, grad_sum, gw_k)), gx_k

        init = (_jnp.zeros((), _jnp.float32), _jax.tree.map(_jnp.zeros_like, weights))
        (loss, grad_w), grad_x = _jax.lax.scan(body, init, (per_example, given["loss_target"]))
    with _jax.named_scope("update"):
        delta_w, new_m, new_v = {}, {}, {}
        for n in TWIN_WEIGHTS:
            delta_w[n], new_m[n], new_v[n] = _adamw(weights[n], grad_w[n], given["m_" + n], given["v_" + n])
    return (loss, grad_x, *[grad_w[n] for n in TWIN_WEIGHTS], *[delta_w[n] for n in TWIN_WEIGHTS],
            *[new_m[n] for n in TWIN_WEIGHTS], *[new_v[n] for n in TWIN_WEIGHTS])
```

```python
import functools
import math

import jax
import jax.numpy as jnp
from jax import lax
from jax.experimental import pallas as pl
from jax.experimental.pallas import tpu as pltpu

_BF = jnp.bfloat16
_F32 = jnp.float32
_VMEM_LIMIT = 56 * 1024 * 1024

QK_NOPE = 128
QK_ROPE = 64
V_HEAD = 128
HEAD_PAD = 256
ROPE_THETA = 10000.0
LN_EPS = 1e-5
RMS_EPS = 1e-6
POOL_WINDOWS = (2, 4, 8, 16)
ADAM_LR, ADAM_B1, ADAM_B2, ADAM_EPS, ADAM_WD, ADAM_STEP = 0.001, 0.9, 0.999, 1e-8, 0.01, 10


def _cparams(sem=None):
    kw = dict(vmem_limit_bytes=_VMEM_LIMIT)
    if sem is not None:
        kw["dimension_semantics"] = sem
    return pltpu.CompilerParams(**kw)


def _tile(n, target, unit=128):
    if n <= target:
        return n
    t = (target // unit) * unit
    while t >= unit:
        if n % t == 0:
            return t
        t -= unit
    return n


def matmul(a, b, *, ta=False, tb=False, out_dtype=_F32, tm=512, tn=1024, tk=512, name="mm"):
    if ta:
        K, M = a.shape
    else:
        M, K = a.shape
    if tb:
        N, K2 = b.shape
    else:
        K2, N = b.shape
    assert K == K2, (a.shape, b.shape, ta, tb)
    tm, tn, tk = _tile(M, tm), _tile(N, tn), _tile(K, tk)
    nk = K // tk
    dn = (((0,) if ta else (1,), (1,) if tb else (0,)), ((), ()))

    def body(a_ref, b_ref, o_ref, acc_ref):
        k = pl.program_id(2)

        @pl.when(k == 0)
        def _():
            acc_ref[...] = jnp.zeros_like(acc_ref)

        acc_ref[...] += lax.dot_general(a_ref[...].astype(_BF), b_ref[...].astype(_BF), dn,
                                        preferred_element_type=_F32)

        @pl.when(k == nk - 1)
        def _():
            o_ref[...] = acc_ref[...].astype(o_ref.dtype)

    a_spec = pl.BlockSpec((tk, tm), lambda i, j, k: (k, i)) if ta else pl.BlockSpec((tm, tk), lambda i, j, k: (i, k))
    b_spec = pl.BlockSpec((tn, tk), lambda i, j, k: (j, k)) if tb else pl.BlockSpec((tk, tn), lambda i, j, k: (k, j))
    return pl.pallas_call(
        body, name=name,
        grid=(M // tm, N // tn, nk),
        in_specs=[a_spec, b_spec],
        out_specs=pl.BlockSpec((tm, tn), lambda i, j, k: (i, j)),
        out_shape=jax.ShapeDtypeStruct((M, N), out_dtype),
        scratch_shapes=[pltpu.VMEM((tm, tn), _F32)],
        compiler_params=_cparams(("parallel", "parallel", "arbitrary")),
    )(a, b)


def _row_tile(T, C, budget_rows=256):
    return _tile(T, budget_rows, 16)


def ln_fwd(xs, coefs, g, b, *, want_r, name):
    T, C = xs[0].shape
    tr = _row_tile(T, C)
    n = len(xs)

    def body(*refs):
        x_refs, (g_ref, b_ref), outs = refs[:n], refs[n:n + 2], refs[n + 2:]
        r = coefs[0] * x_refs[0][...]
        for c, xr in zip(coefs[1:], x_refs[1:]):
            r = r + c * xr[...]
        mu = jnp.mean(r, axis=-1, keepdims=True)
        d = r - mu
        var = jnp.mean(d * d, axis=-1, keepdims=True)
        y = d * lax.rsqrt(var + LN_EPS) * g_ref[...] + b_ref[...]
        if want_r:
            outs[0][...] = r
        outs[-2][...] = y
        outs[-1][...] = y.astype(_BF)

    row = pl.BlockSpec((tr, C), lambda i: (i, 0))
    vec = pl.BlockSpec((1, C), lambda i: (0, 0))
    f = jax.ShapeDtypeStruct((T, C), _F32)
    out_shape = ([f] if want_r else []) + [f, jax.ShapeDtypeStruct((T, C), _BF)]
    return pl.pallas_call(
        body, name=name, grid=(T // tr,),
        in_specs=[row] * n + [vec, vec],
        out_specs=[row] * len(out_shape), out_shape=out_shape,
        compiler_params=_cparams(("parallel",)),
    )(*xs, g.reshape(1, C), b.reshape(1, C))


def ln_bwd(dys, coefs, r, g, *, name):
    T, C = r.shape
    tr = _row_tile(T, C)
    n = len(dys)

    def body(*refs):
        dy_refs, r_ref, g_ref = refs[:n], refs[n], refs[n + 1]
        dr_ref, drb_ref, dg_ref, db_ref = refs[n + 2:]
        dy = coefs[0] * dy_refs[0][...]
        for c, dr_ in zip(coefs[1:], dy_refs[1:]):
            dy = dy + c * dr_[...]
        rr = r_ref[...]
        mu = jnp.mean(rr, axis=-1, keepdims=True)
        d = rr - mu
        var = jnp.mean(d * d, axis=-1, keepdims=True)
        rstd = lax.rsqrt(var + LN_EPS)
        xh = d * rstd
        gdy = dy * g_ref[...]
        m1 = jnp.mean(gdy, axis=-1, keepdims=True)
        m2 = jnp.mean(gdy * xh, axis=-1, keepdims=True)
        dr = rstd * (gdy - m1 - xh * m2)
        dr_ref[...] = dr
        drb_ref[...] = dr.astype(_BF)

        @pl.when(pl.program_id(0) == 0)
        def _():
            dg_ref[...] = jnp.zeros_like(dg_ref)
            db_ref[...] = jnp.zeros_like(db_ref)

        dg_ref[...] += jnp.sum(dy * xh, axis=0, keepdims=True)
        db_ref[...] += jnp.sum(dy, axis=0, keepdims=True)

    row = pl.BlockSpec((tr, C), lambda i: (i, 0))
    vec = pl.BlockSpec((1, C), lambda i: (0, 0))
    return pl.pallas_call(
        body, name=name, grid=(T // tr,),
        in_specs=[row] * (n + 1) + [vec],
        out_specs=[row, row, vec, vec],
        out_shape=[jax.ShapeDtypeStruct((T, C), _F32), jax.ShapeDtypeStruct((T, C), _BF),
                   jax.ShapeDtypeStruct((1, C), _F32), jax.ShapeDtypeStruct((1, C), _F32)],
        compiler_params=_cparams(("arbitrary",)),
    )(*dys, r, g.reshape(1, C))


def rms_fwd(h, cb, W, g, *, name):
    T = h.shape[0]
    tr = _tile(T, 512, 16)

    def body(c_ref, g_ref, o_ref):
        c = c_ref[...]
        ms = jnp.mean(c * c, axis=-1, keepdims=True)
        o_ref[...] = (c * lax.rsqrt(ms + RMS_EPS) * g_ref[...]).astype(_BF)

    return pl.pallas_call(
        body, name=name, grid=(T // tr,),
        in_specs=[pl.BlockSpec((tr, W), lambda i: (i, cb)), pl.BlockSpec((1, W), lambda i: (0, 0))],
        out_specs=pl.BlockSpec((tr, W), lambda i: (i, 0)),
        out_shape=jax.ShapeDtypeStruct((T, W), _BF),
        compiler_params=_cparams(("parallel",)),
    )(h, g.reshape(1, W))


def rms_bwd(dy, h, cb, W, g, *, name):
    T = h.shape[0]
    tr = _tile(T, 512, 16)

    def body(dy_ref, c_ref, g_ref, dc_ref, dg_ref):
        c = c_ref[...]
        dyv = dy_ref[...]
        ms = jnp.mean(c * c, axis=-1, keepdims=True)
        r = lax.rsqrt(ms + RMS_EPS)
        u = dyv * g_ref[...]
        m = jnp.mean(c * u, axis=-1, keepdims=True)
        dc_ref[...] = (r * u - c * (r * r * r) * m).astype(_BF)

        @pl.when(pl.program_id(0) == 0)
        def _():
            dg_ref[...] = jnp.zeros_like(dg_ref)

        dg_ref[...] += jnp.sum(dyv * c * r, axis=0, keepdims=True)

    return pl.pallas_call(
        body, name=name, grid=(T // tr,),
        in_specs=[pl.BlockSpec((tr, W), lambda i: (i, 0)), pl.BlockSpec((tr, W), lambda i: (i, cb)),
                  pl.BlockSpec((1, W), lambda i: (0, 0))],
        out_specs=[pl.BlockSpec((tr, W), lambda i: (i, 0)), pl.BlockSpec((1, W), lambda i: (0, 0))],
        out_shape=[jax.ShapeDtypeStruct((T, W), _BF), jax.ShapeDtypeStruct((1, W), _F32)],
        compiler_params=_cparams(("arbitrary",)),
    )(dy, h, g.reshape(1, W))


def _rope(u, cc, sa, sb, sign):
    return u * cc + sign * (pltpu.roll(u, 96, 1) * sa + pltpu.roll(u, 32, 1) * sb)


def mla_pack(q, kv, h, kr_cb, cc, sa, sb, *, H, name):
    T = q.shape[0]
    tr = _tile(T, 256, 16)

    def body(q_ref, kv_ref, kr_ref, cc_ref, sa_ref, sb_ref, qp_ref, kp_ref, v_ref):
        cc_, sa_, sb_ = cc_ref[...], sa_ref[...], sb_ref[...]
        kr = _rope(kr_ref[...], cc_, sa_, sb_, 1.0).astype(_BF)
        for hh in range(H):
            o = hh * HEAD_PAD
            qp_ref[:, o:o + 128] = q_ref[:, o:o + 128].astype(_BF)
            qp_ref[:, o + 128:o + 256] = _rope(q_ref[:, o + 128:o + 256], cc_, sa_, sb_, 1.0).astype(_BF)
            kp_ref[:, o:o + 128] = kv_ref[:, o:o + 128].astype(_BF)
            kp_ref[:, o + 128:o + 256] = kr
            v_ref[:, hh * 128:(hh + 1) * 128] = kv_ref[:, o + 128:o + 256].astype(_BF)

    wide = pl.BlockSpec((tr, H * HEAD_PAD), lambda i: (i, 0))
    tab = pl.BlockSpec((tr, 128), lambda i: (i, 0))
    return pl.pallas_call(
        body, name=name, grid=(T // tr,),
        in_specs=[wide, wide, pl.BlockSpec((tr, 128), lambda i: (i, kr_cb)), tab, tab, tab],
        out_specs=[wide, wide, pl.BlockSpec((tr, H * 128), lambda i: (i, 0))],
        out_shape=[jax.ShapeDtypeStruct((T, H * HEAD_PAD), _BF), jax.ShapeDtypeStruct((T, H * HEAD_PAD), _BF),
                   jax.ShapeDtypeStruct((T, H * 128), _BF)],
        compiler_params=_cparams(("parallel",)),
    )(q, kv, h, cc, sa, sb)


def mla_unpack(dqp, dkp, dv, cc, sa, sb, *, H, name):
    T = dqp.shape[0]
    tr = _tile(T, 256, 16)

    def body(dq_ref, dk_ref, dv_ref, cc_ref, sa_ref, sb_ref, oq_ref, okv_ref, okr_ref):
        cc_, sa_, sb_ = cc_ref[...], sa_ref[...], sb_ref[...]
        kr = jnp.zeros((tr, 128), _F32)
        for hh in range(H):
            o = hh * HEAD_PAD
            oq_ref[:, o:o + 128] = dq_ref[:, o:o + 128].astype(_BF)
            oq_ref[:, o + 128:o + 256] = _rope(dq_ref[:, o + 128:o + 256], cc_, sa_, sb_, -1.0).astype(_BF)
            okv_ref[:, o:o + 128] = dk_ref[:, o:o + 128].astype(_BF)
            okv_ref[:, o + 128:o + 256] = dv_ref[:, hh * 128:(hh + 1) * 128].astype(_BF)
            kr = kr + dk_ref[:, o + 128:o + 256]
        okr_ref[...] = _rope(kr, cc_, sa_, sb_, -1.0).astype(_BF)

    wide = pl.BlockSpec((tr, H * HEAD_PAD), lambda i: (i, 0))
    tab = pl.BlockSpec((tr, 128), lambda i: (i, 0))
    return pl.pallas_call(
        body, name=name, grid=(T // tr,),
        in_specs=[wide, wide, pl.BlockSpec((tr, H * 128), lambda i: (i, 0)), tab, tab, tab],
        out_specs=[wide, wide, tab],
        out_shape=[jax.ShapeDtypeStruct((T, H * HEAD_PAD), _BF), jax.ShapeDtypeStruct((T, H * HEAD_PAD), _BF),
                   jax.ShapeDtypeStruct((T, 128), _BF)],
        compiler_params=_cparams(("parallel",)),
    )(dqp, dkp, dv, cc, sa, sb)


_NEG = -1e30


def _scores(q, k, scale, i, j, tq, tk):
    s = lax.dot_general(q, k, (((1,), (1,)), ((), ())), preferred_element_type=_F32) * scale
    row = i * tq + lax.broadcasted_iota(jnp.int32, (tq, tk), 0)
    col = j * tk + lax.broadcasted_iota(jnp.int32, (tq, tk), 1)
    return jnp.where(col <= row, s, _NEG)


def flash_fwd(qp, kp, v, *, B, S, H, scale, name):
    T = B * S
    tq = tk = _tile(S, 512, 128)
    nq, nk = S // tq, S // tk

    def body(q_ref, k_ref, v_ref, o_ref, lse_ref, m_sc, l_sc, acc_sc):
        i, j = pl.program_id(2), pl.program_id(3)

        @pl.when(j == 0)
        def _():
            m_sc[...] = jnp.full_like(m_sc, _NEG)
            l_sc[...] = jnp.zeros_like(l_sc)
            acc_sc[...] = jnp.zeros_like(acc_sc)

        @pl.when(j * tk <= i * tq + (tq - 1))
        def _():
            s = _scores(q_ref[...], k_ref[...], scale, i, j, tq, tk)
            m_old = m_sc[...]
            m_new = jnp.maximum(m_old, jnp.max(s, axis=-1, keepdims=True))
            p = jnp.exp(s - m_new)
            a = jnp.exp(m_old - m_new)
            l_sc[...] = a * l_sc[...] + jnp.sum(p, axis=-1, keepdims=True)
            acc_sc[...] = a * acc_sc[...] + jnp.dot(p.astype(_BF), v_ref[...], preferred_element_type=_F32)
            m_sc[...] = m_new

        @pl.when(j == nk - 1)
        def _():
            l = l_sc[...]
            o_ref[...] = acc_sc[...] / l
            lse_ref[...] = jnp.broadcast_to(m_sc[...] + jnp.log(l), lse_ref.shape)

    def kmap(b, h, i, j):
        return (b * nk + jnp.minimum(j, (i * tq + tq - 1) // tk), h)

    qmap = lambda b, h, i, j: (b * nq + i, h)
    return pl.pallas_call(
        body, name=name, grid=(B, H, nq, nk),
        in_specs=[pl.BlockSpec((tq, HEAD_PAD), qmap), pl.BlockSpec((tk, HEAD_PAD), kmap), pl.BlockSpec((tk, 128), kmap)],
        out_specs=[pl.BlockSpec((tq, 128), qmap), pl.BlockSpec((tq, 128), qmap)],
        out_shape=[jax.ShapeDtypeStruct((T, H * 128), _F32), jax.ShapeDtypeStruct((T, H * 128), _F32)],
        scratch_shapes=[pltpu.VMEM((tq, 1), _F32), pltpu.VMEM((tq, 1), _F32), pltpu.VMEM((tq, 128), _F32)],
        compiler_params=_cparams(("parallel", "parallel", "parallel", "arbitrary")),
    )(qp, kp, v)


def flash_bwd_dq(qp, kp, v, o, lse, do, do_cb0, *, B, S, H, scale, name):
    T = B * S
    tq = tk = _tile(S, 512, 128)
    nq, nk = S // tq, S // tk

    def body(q_ref, k_ref, v_ref, o_ref, lse_ref, do_ref, dq_ref, acc_sc, dl_sc):
        i, j = pl.program_id(2), pl.program_id(3)

        @pl.when(j == 0)
        def _():
            acc_sc[...] = jnp.zeros_like(acc_sc)
            dl_sc[...] = jnp.sum(do_ref[...].astype(_F32) * o_ref[...], axis=-1, keepdims=True)

        @pl.when(j * tk <= i * tq + (tq - 1))
        def _():
            k = k_ref[...]
            s = _scores(q_ref[...], k, scale, i, j, tq, tk)
            p = jnp.exp(s - lse_ref[:, 0:1])
            dp = lax.dot_general(do_ref[...].astype(_BF), v_ref[...], (((1,), (1,)), ((), ())),
                                 preferred_element_type=_F32)
            ds = p * (dp - dl_sc[...])
            acc_sc[...] += jnp.dot(ds.astype(_BF), k, preferred_element_type=_F32)

        @pl.when(j == nk - 1)
        def _():
            dq_ref[...] = acc_sc[...] * scale

    def kmap(b, h, i, j):
        return (b * nk + jnp.minimum(j, (i * tq + tq - 1) // tk), h)

    qmap = lambda b, h, i, j: (b * nq + i, h)
    domap = lambda b, h, i, j: (b * nq + i, do_cb0 + h)
    return pl.pallas_call(
        body, name=name, grid=(B, H, nq, nk),
        in_specs=[pl.BlockSpec((tq, HEAD_PAD), qmap), pl.BlockSpec((tk, HEAD_PAD), kmap), pl.BlockSpec((tk, 128), kmap),
                  pl.BlockSpec((tq, 128), qmap), pl.BlockSpec((tq, 128), qmap), pl.BlockSpec((tq, 128), domap)],
        out_specs=pl.BlockSpec((tq, HEAD_PAD), qmap),
        out_shape=jax.ShapeDtypeStruct((T, H * HEAD_PAD), _F32),
        scratch_shapes=[pltpu.VMEM((tq, HEAD_PAD), _F32), pltpu.VMEM((tq, 1), _F32)],
        compiler_params=_cparams(("parallel", "parallel", "parallel", "arbitrary")),
    )(qp, kp, v, o, lse, do)


def flash_bwd_dkv(qp, kp, v, o, lse, do, do_cb0, *, B, S, H, scale, name):
    T = B * S
    tq = tk = _tile(S, 512, 128)
    nq, nk = S // tq, S // tk

    def body(q_ref, k_ref, v_ref, o_ref, lse_ref, do_ref, dk_ref, dv_ref, dk_sc, dv_sc):
        j, i = pl.program_id(2), pl.program_id(3)

        @pl.when(i == 0)
        def _():
            dk_sc[...] = jnp.zeros_like(dk_sc)
            dv_sc[...] = jnp.zeros_like(dv_sc)

        @pl.when(j * tk <= i * tq + (tq - 1))
        def _():
            q = q_ref[...]
            dob = do_ref[...].astype(_BF)
            s = _scores(q, k_ref[...], scale, i, j, tq, tk)
            p = jnp.exp(s - lse_ref[:, 0:1])
            dl = jnp.sum(do_ref[...].astype(_F32) * o_ref[...], axis=-1, keepdims=True)
            dp = lax.dot_general(dob, v_ref[...], (((1,), (1,)), ((), ())), preferred_element_type=_F32)
            ds = p * (dp - dl)
            tn = (((0,), (0,)), ((), ()))
            dv_sc[...] += lax.dot_general(p.astype(_BF), dob, tn, preferred_element_type=_F32)
            dk_sc[...] += lax.dot_general(ds.astype(_BF), q, tn, preferred_element_type=_F32)

        @pl.when(i == nq - 1)
        def _():
            dk_ref[...] = dk_sc[...] * scale
            dv_ref[...] = dv_sc[...]

    def qrow(b, i, j):
        return b * nq + jnp.maximum(i, (j * tk) // tq)

    qmap = lambda b, h, j, i: (qrow(b, i, j), h)
    domap = lambda b, h, j, i: (qrow(b, i, j), do_cb0 + h)
    kmap = lambda b, h, j, i: (b * nk + j, h)
    return pl.pallas_call(
        body, name=name, grid=(B, H, nk, nq),
        in_specs=[pl.BlockSpec((tq, HEAD_PAD), qmap), pl.BlockSpec((tk, HEAD_PAD), kmap), pl.BlockSpec((tk, 128), kmap),
                  pl.BlockSpec((tq, 128), qmap), pl.BlockSpec((tq, 128), qmap), pl.BlockSpec((tq, 128), domap)],
        out_specs=[pl.BlockSpec((tk, HEAD_PAD), kmap), pl.BlockSpec((tk, 128), kmap)],
        out_shape=[jax.ShapeDtypeStruct((T, H * HEAD_PAD), _F32), jax.ShapeDtypeStruct((T, H * 128), _F32)],
        scratch_shapes=[pltpu.VMEM((tk, HEAD_PAD), _F32), pltpu.VMEM((tk, 128), _F32)],
        compiler_params=_cparams(("parallel", "parallel", "parallel", "arbitrary")),
    )(qp, kp, v, o, lse, do)


def _halo_specs(T, nT, tt, hr, cw, cb):
    k = tt // hr
    main = pl.BlockSpec((tt, cw), lambda b, t: (b * nT + t, cb))
    prev = pl.BlockSpec((hr, cw), lambda b, t: (jnp.maximum((b * nT + t) * k - 1, 0), cb))
    nxt = pl.BlockSpec((hr, cw), lambda b, t: (jnp.minimum((b * nT + t + 1) * k, T // hr - 1), cb))
    return main, prev, nxt


def _ln_rows(z, g, b):
    mu = jnp.mean(z, axis=-1, keepdims=True)
    d = z - mu
    var = jnp.mean(d * d, axis=-1, keepdims=True)
    rstd = lax.rsqrt(var + LN_EPS)
    xh = d * rstd
    return xh * g + b, xh, rstd


def conv_fwd(h, cb_a, cb_g, w, bias, lng, lnb, *, B, S, name):
    T = B * S
    K, C = w.shape
    hr = 32
    assert K - 1 <= hr
    tt = _tile(S, 512, hr)
    nT = S // tt
    a_m, a_p, _ = _halo_specs(T, nT, tt, hr, C, cb_a)
    g_m, g_p, _ = _halo_specs(T, nT, tt, hr, C, cb_g)

    def body(a_ref, g_ref, ap_ref, gp_ref, w_ref, b_ref, lg_ref, lb_ref, z_ref, y_ref, buf):
        t = pl.program_id(1)
        buf[pl.ds(hr, tt), :] = a_ref[...] * jax.nn.sigmoid(g_ref[...])
        hp = ap_ref[...] * jax.nn.sigmoid(gp_ref[...])
        buf[pl.ds(0, hr), :] = jnp.where(t == 0, 0.0, hp)
        z = jnp.broadcast_to(b_ref[...], (tt, C))
        for k in range(K):
            z = z + w_ref[k:k + 1, :] * buf[pl.ds(hr - (K - 1) + k, tt), :]
        z_ref[...] = z
        n, _, _ = _ln_rows(z, lg_ref[...], lb_ref[...])
        y_ref[...] = (n * jax.nn.sigmoid(n)).astype(_BF)

    vec = pl.BlockSpec((1, C), lambda b, t: (0, 0))
    out = pl.BlockSpec((tt, C), lambda b, t: (b * nT + t, 0))
    return pl.pallas_call(
        body, name=name, grid=(B, nT),
        in_specs=[a_m, g_m, a_p, g_p, pl.BlockSpec((K, C), lambda b, t: (0, 0)), vec, vec, vec],
        out_specs=[out, out],
        out_shape=[jax.ShapeDtypeStruct((T, C), _F32), jax.ShapeDtypeStruct((T, C), _BF)],
        scratch_shapes=[pltpu.VMEM((hr + tt, C), _F32)],
        compiler_params=_cparams(("parallel", "parallel")),
    )(h, h, h, h, w, bias.reshape(1, C), lng.reshape(1, C), lnb.reshape(1, C))


def conv_bwd(dmix, cb_dy, z, h, cb_a, cb_g, w, lng, lnb, *, B, S, name):
    T = B * S
    K, C = w.shape
    hr = 32
    tt = _tile(S, 512, hr)
    nT = S // tt
    a_m, a_p, _ = _halo_specs(T, nT, tt, hr, C, cb_a)
    g_m, g_p, _ = _halo_specs(T, nT, tt, hr, C, cb_g)
    dy_m, _, dy_n = _halo_specs(T, nT, tt, hr, C, cb_dy)
    z_m, _, z_n = _halo_specs(T, nT, tt, hr, C, 0)

    def body(dy_ref, dyn_ref, z_ref, zn_ref, a_ref, g_ref, ap_ref, gp_ref, w_ref, lg_ref, lb_ref,
             da_ref, dg_ref, dw_ref, db_ref, dlg_ref, dlb_ref, bufz, bufh):
        b, t = pl.program_id(0), pl.program_id(1)
        lg, lb = lg_ref[...], lb_ref[...]

        def dz_of(dy, zz):
            n, xh, rstd = _ln_rows(zz, lg, lb)
            sg = jax.nn.sigmoid(n)
            dn = dy.astype(_F32) * (sg * (1.0 + n * (1.0 - sg)))
            gdn = dn * lg
            m1 = jnp.mean(gdn, axis=-1, keepdims=True)
            m2 = jnp.mean(gdn * xh, axis=-1, keepdims=True)
            return rstd * (gdn - m1 - xh * m2), dn, xh

        dz, dn, xh = dz_of(dy_ref[...], z_ref[...])
        dzn, _, _ = dz_of(dyn_ref[...], zn_ref[...])
        bufz[pl.ds(0, tt), :] = dz
        bufz[pl.ds(tt, hr), :] = jnp.where(t == nT - 1, 0.0, dzn)
        a, g = a_ref[...], g_ref[...]
        sg = jax.nn.sigmoid(g)
        bufh[pl.ds(hr, tt), :] = a * sg
        bufh[pl.ds(0, hr), :] = jnp.where(t == 0, 0.0, ap_ref[...] * jax.nn.sigmoid(gp_ref[...]))

        @pl.when((b == 0) & (t == 0))
        def _():
            dw_ref[...] = jnp.zeros_like(dw_ref)
            db_ref[...] = jnp.zeros_like(db_ref)
            dlg_ref[...] = jnp.zeros_like(dlg_ref)
            dlb_ref[...] = jnp.zeros_like(dlb_ref)

        dhc = jnp.zeros((tt, C), _F32)
        for k in range(K):
            dhc = dhc + w_ref[k:k + 1, :] * bufz[pl.ds(K - 1 - k, tt), :]
            dw_ref[k:k + 1, :] += jnp.sum(dz * bufh[pl.ds(hr - (K - 1) + k, tt), :], axis=0, keepdims=True)
        da_ref[...] = (dhc * sg).astype(_BF)
        dg_ref[...] = (dhc * a * sg * (1.0 - sg)).astype(_BF)
        db_ref[...] += jnp.sum(dz, axis=0, keepdims=True)
        dlg_ref[...] += jnp.sum(dn * xh, axis=0, keepdims=True)
        dlb_ref[...] += jnp.sum(dn, axis=0, keepdims=True)

    vec = pl.BlockSpec((1, C), lambda b, t: (0, 0))
    out = pl.BlockSpec((tt, C), lambda b, t: (b * nT + t, 0))
    kc = pl.BlockSpec((K, C), lambda b, t: (0, 0))
    return pl.pallas_call(
        body, name=name, grid=(B, nT),
        in_specs=[dy_m, dy_n, z_m, z_n, a_m, g_m, a_p, g_p, kc, vec, vec],
        out_specs=[out, out, kc, vec, vec, vec],
        out_shape=[jax.ShapeDtypeStruct((T, C), _BF), jax.ShapeDtypeStruct((T, C), _BF),
                   jax.ShapeDtypeStruct((K, C), _F32)] + [jax.ShapeDtypeStruct((1, C), _F32)] * 3,
        scratch_shapes=[pltpu.VMEM((tt + hr, C), _F32), pltpu.VMEM((hr + tt, C), _F32)],
        compiler_params=_cparams(("arbitrary", "arbitrary")),
    )(dmix, dmix, z, z, h, h, h, h, w, lng.reshape(1, C), lnb.reshape(1, C))


def _pool_cnt(t, tt, w, rows):
    pos = t * tt + lax.broadcasted_iota(jnp.int32, (rows, 1), 0)
    return jnp.minimum(pos + 1, w).astype(_F32)


def pool_fwd(h, cb, wp, scale, *, B, S, name):
    T = B * S
    G, pg, _ = wp.shape
    C = G * pg
    assert pg == 128 and G == len(POOL_WINDOWS)
    hr = 16
    tt = _tile(S, 512, hr)
    nT = S // tt
    u_m, u_p, _ = _halo_specs(T, nT, tt, hr, C, cb)

    def body(u_ref, up_ref, wp_ref, sc_ref, y_ref, buf):
        t = pl.program_id(1)
        buf[pl.ds(hr, tt), :] = u_ref[...]
        buf[pl.ds(0, hr), :] = jnp.where(t == 0, 0.0, up_ref[...])
        for gi, w in enumerate(POOL_WINDOWS):
            ln = slice(gi * pg, (gi + 1) * pg)
            acc = buf[pl.ds(hr, tt), ln]
            for j in range(1, w):
                acc = acc + buf[pl.ds(hr - j, tt), ln]
            d = acc / _pool_cnt(t, tt, w, tt) - u_ref[:, ln]
            yg = jnp.dot(d.astype(_BF), wp_ref[gi].astype(_BF), preferred_element_type=_F32)
            y_ref[:, ln] = (yg * sc_ref[:, ln]).astype(_BF)

    return pl.pallas_call(
        body, name=name, grid=(B, nT),
        in_specs=[u_m, u_p, pl.BlockSpec((G, pg, pg), lambda b, t: (0, 0, 0)), pl.BlockSpec((1, C), lambda b, t: (0, 0))],
        out_specs=pl.BlockSpec((tt, C), lambda b, t: (b * nT + t, 0)),
        out_shape=jax.ShapeDtypeStruct((T, C), _BF),
        scratch_shapes=[pltpu.VMEM((hr + tt, C), _F32)],
        compiler_params=_cparams(("parallel", "parallel")),
    )(h, h, wp, scale.reshape(1, C))


def pool_bwd(dmix, cb_dy, h, cb, wp, scale, *, B, S, name):
    T = B * S
    G, pg, _ = wp.shape
    C = G * pg
    hr = 16
    tt = _tile(S, 512, hr)
    nT = S // tt
    u_m, u_p, _ = _halo_specs(T, nT, tt, hr, C, cb)
    dy_m, _, dy_n = _halo_specs(T, nT, tt, hr, C, cb_dy)

    def body(dy_ref, dyn_ref, u_ref, up_ref, wp_ref, sc_ref, du_ref, dwp_ref, dsc_ref, buf, bufe):
        b, t = pl.program_id(0), pl.program_id(1)
        buf[pl.ds(hr, tt), :] = u_ref[...]
        buf[pl.ds(0, hr), :] = jnp.where(t == 0, 0.0, up_ref[...])

        @pl.when((b == 0) & (t == 0))
        def _():
            dwp_ref[...] = jnp.zeros_like(dwp_ref)
            dsc_ref[...] = jnp.zeros_like(dsc_ref)

        nt = (((1,), (1,)), ((), ()))
        tn = (((0,), (0,)), ((), ()))
        for gi, w in enumerate(POOL_WINDOWS):
            ln = slice(gi * pg, (gi + 1) * pg)
            wg = wp_ref[gi].astype(_BF)
            sc = sc_ref[:, ln]
            dy = dy_ref[:, ln].astype(_F32)
            dz = (dy * sc).astype(_BF)
            dzn = (dyn_ref[:, ln].astype(_F32) * sc).astype(_BF)
            dd = lax.dot_general(dz, wg, nt, preferred_element_type=_F32)
            ddn = lax.dot_general(dzn, wg, nt, preferred_element_type=_F32)
            bufe[pl.ds(0, tt), ln] = dd / _pool_cnt(t, tt, w, tt)
            bufe[pl.ds(tt, hr), ln] = jnp.where(t == nT - 1, 0.0, ddn / _pool_cnt(t + 1, tt, w, hr))
            du = -dd
            for j in range(w):
                du = du + bufe[pl.ds(j, tt), ln]
            du_ref[:, ln] = du.astype(_BF)
            acc = buf[pl.ds(hr, tt), ln]
            for j in range(1, w):
                acc = acc + buf[pl.ds(hr - j, tt), ln]
            d = (acc / _pool_cnt(t, tt, w, tt) - u_ref[:, ln]).astype(_BF)
            dwp_ref[gi] += lax.dot_general(d, dz, tn, preferred_element_type=_F32)
            yg = jnp.dot(d, wg, preferred_element_type=_F32)
            dsc_ref[:, ln] += jnp.sum(dy * yg, axis=0, keepdims=True)

    return pl.pallas_call(
        body, name=name, grid=(B, nT),
        in_specs=[dy_m, dy_n, u_m, u_p, pl.BlockSpec((G, pg, pg), lambda b, t: (0, 0, 0)),
                  pl.BlockSpec((1, C), lambda b, t: (0, 0))],
        out_specs=[pl.BlockSpec((tt, C), lambda b, t: (b * nT + t, 0)), pl.BlockSpec((G, pg, pg), lambda b, t: (0, 0, 0)),
                   pl.BlockSpec((1, C), lambda b, t: (0, 0))],
        out_shape=[jax.ShapeDtypeStruct((T, C), _BF), jax.ShapeDtypeStruct((G, pg, pg), _F32),
                   jax.ShapeDtypeStruct((1, C), _F32)],
        scratch_shapes=[pltpu.VMEM((hr + tt, C), _F32), pltpu.VMEM((tt + hr, C), _F32)],
        compiler_params=_cparams(("arbitrary", "arbitrary")),
    )(dmix, dmix, h, h, wp, scale.reshape(1, C))


_FFN_HR = 16


def _silu_grad(x, sg):
    return sg * (1.0 + x * (1.0 - sg))


def _conv3(buf, w_ref, b_ref, off, rows):
    c = b_ref[...] + w_ref[0:1, :] * buf[pl.ds(off, rows), :]
    for k in (1, 2):
        c = c + w_ref[k:k + 1, :] * buf[pl.ds(off + k, rows), :]
    return c


def gate_fwd(up, w, bias, *, B, S, name):
    T, F2 = up.shape
    F = F2 // 2
    hr = _FFN_HR
    tt = _tile(S, 512, hr)
    nT = S // tt
    tn = _tile(F, 512, 128)
    nC = F // tn
    k = tt // hr

    def body(a_ref, g_ref, ap_ref, gp_ref, wa_ref, wg_ref, ba_ref, bg_ref, o_ref, bufa, bufg):
        t = pl.program_id(2)
        for buf, m_ref, p_ref in ((bufa, a_ref, ap_ref), (bufg, g_ref, gp_ref)):
            buf[pl.ds(hr, tt), :] = m_ref[...].astype(_F32)
            buf[pl.ds(0, hr), :] = jnp.where(t == 0, 0.0, p_ref[...].astype(_F32))
        ca = _conv3(bufa, wa_ref, ba_ref, hr - 2, tt)
        cg = _conv3(bufg, wg_ref, bg_ref, hr - 2, tt)
        o_ref[...] = (ca * cg * jax.nn.sigmoid(cg)).astype(_BF)

    def main(off):
        return pl.BlockSpec((tt, tn), lambda b, j, t: (b * nT + t, j + off))

    def prev(off):
        return pl.BlockSpec((hr, tn), lambda b, j, t: (jnp.maximum((b * nT + t) * k - 1, 0), j + off))

    def wspec(rows, off):
        return pl.BlockSpec((rows, tn), lambda b, j, t: (0, j + off))

    return pl.pallas_call(
        body, name=name, grid=(B, nC, nT),
        in_specs=[main(0), main(nC), prev(0), prev(nC), wspec(3, 0), wspec(3, nC), wspec(1, 0), wspec(1, nC)],
        out_specs=pl.BlockSpec((tt, tn), lambda b, j, t: (b * nT + t, j)),
        out_shape=jax.ShapeDtypeStruct((T, F), _BF),
        scratch_shapes=[pltpu.VMEM((hr + tt, tn), _F32)] * 2,
        compiler_params=_cparams(("parallel", "parallel", "parallel")),
    )(up, up, up, up, w, w, bias.reshape(1, F2), bias.reshape(1, F2))


def gate_bwd(up, dact, w, bias, *, B, S, name):
    T, F2 = up.shape
    F = F2 // 2
    hr = _FFN_HR
    tt = _tile(S, 512, hr)
    nT = S // tt
    tn = _tile(F, 512, 128)
    nC = F // tn
    k = tt // hr
    ext = tt + hr

    def body(a_ref, g_ref, ap_ref, gp_ref, an_ref, gn_ref, d_ref, dn_ref, wa_ref, wg_ref, ba_ref, bg_ref,
             dua_ref, dug_ref, dwa_ref, dwg_ref, dba_ref, dbg_ref, bufa, bufg, bufda, bufdg):
        b, t = pl.program_id(1), pl.program_id(2)
        last = t == nT - 1
        for buf, m_ref, p_ref, n_ref in ((bufa, a_ref, ap_ref, an_ref), (bufg, g_ref, gp_ref, gn_ref)):
            buf[pl.ds(hr, tt), :] = m_ref[...].astype(_F32)
            buf[pl.ds(0, hr), :] = jnp.where(t == 0, 0.0, p_ref[...].astype(_F32))
            buf[pl.ds(hr + tt, hr), :] = jnp.where(last, 0.0, n_ref[...].astype(_F32))
        ca = _conv3(bufa, wa_ref, ba_ref, hr - 2, ext)
        cg = _conv3(bufg, wg_ref, bg_ref, hr - 2, ext)
        sg = jax.nn.sigmoid(cg)
        bufda[pl.ds(0, tt), :] = d_ref[...].astype(_F32)
        bufda[pl.ds(tt, hr), :] = jnp.where(last, 0.0, dn_ref[...].astype(_F32))
        da = bufda[...]
        bufdg[...] = da * ca * _silu_grad(cg, sg)
        bufda[...] = da * cg * sg

        @pl.when((b == 0) & (t == 0))
        def _():
            for r in (dwa_ref, dwg_ref, dba_ref, dbg_ref):
                r[...] = jnp.zeros_like(r)

        for bufd, buf, w_ref, du_ref, dw_ref, db_ref in ((bufda, bufa, wa_ref, dua_ref, dwa_ref, dba_ref),
                                                         (bufdg, bufg, wg_ref, dug_ref, dwg_ref, dbg_ref)):
            dc = bufd[pl.ds(0, tt), :]
            du = w_ref[2:3, :] * dc + w_ref[1:2, :] * bufd[pl.ds(1, tt), :] + w_ref[0:1, :] * bufd[pl.ds(2, tt), :]
            du_ref[...] = du.astype(_BF)
            for kk in range(3):
                dw_ref[kk:kk + 1, :] += jnp.sum(dc * buf[pl.ds(hr - 2 + kk, tt), :], axis=0, keepdims=True)
            db_ref[...] += jnp.sum(dc, axis=0, keepdims=True)

    def main(off):
        return pl.BlockSpec((tt, tn), lambda j, b, t: (b * nT + t, j + off))

    def prev(off):
        return pl.BlockSpec((hr, tn), lambda j, b, t: (jnp.maximum((b * nT + t) * k - 1, 0), j + off))

    def nxt(off):
        return pl.BlockSpec((hr, tn), lambda j, b, t: (jnp.minimum((b * nT + t + 1) * k, T // hr - 1), j + off))

    def wspec(rows, off):
        return pl.BlockSpec((rows, tn), lambda j, b, t: (0, j + off))

    tf = jax.ShapeDtypeStruct((T, F), _BF)
    return pl.pallas_call(
        body, name=name, grid=(nC, B, nT),
        in_specs=[main(0), main(nC), prev(0), prev(nC), nxt(0), nxt(nC), main(0), nxt(0),
                  wspec(3, 0), wspec(3, nC), wspec(1, 0), wspec(1, nC)],
        out_specs=[main(0), main(0), wspec(3, 0), wspec(3, 0), wspec(1, 0), wspec(1, 0)],
        out_shape=[tf, tf, jax.ShapeDtypeStruct((3, F), _F32), jax.ShapeDtypeStruct((3, F), _F32),
                   jax.ShapeDtypeStruct((1, F), _F32), jax.ShapeDtypeStruct((1, F), _F32)],
        scratch_shapes=[pltpu.VMEM((hr + ext, tn), _F32)] * 2 + [pltpu.VMEM((ext, tn), _F32)] * 2,
        compiler_params=_cparams(("parallel", "arbitrary", "arbitrary")),
    )(up, up, up, up, up, up, dact, dact, w, w, bias.reshape(1, F2), bias.reshape(1, F2))


def loss_head(y, target, *, name):
    T, C = y.shape
    tr = _row_tile(T, C)

    def body(y_ref, t_ref, dy_ref, acc_ref):
        @pl.when(pl.program_id(0) == 0)
        def _():
            acc_ref[...] = jnp.zeros_like(acc_ref)

        e = y_ref[...] - t_ref[...]
        dy_ref[...] = e * (1.0 / C)
        acc_ref[...] += jnp.sum(e * e, axis=0, keepdims=True) * (0.5 / C)

    row = pl.BlockSpec((tr, C), lambda i: (i, 0))
    return pl.pallas_call(
        body, name=name, grid=(T // tr,),
        in_specs=[row, row], out_specs=[row, pl.BlockSpec((1, C), lambda i: (0, 0))],
        out_shape=[jax.ShapeDtypeStruct((T, C), _F32), jax.ShapeDtypeStruct((1, C), _F32)],
        compiler_params=_cparams(("arbitrary",)),
    )(y, target)


def adamw(w, g, m, v, *, name):
    R, C = w.shape
    tr = _tile(R, max(8, (256 * 1024) // C // 8 * 8), 8)
    c1 = 1.0 - ADAM_B1 ** ADAM_STEP
    c2 = 1.0 - ADAM_B2 ** ADAM_STEP

    def body(w_ref, g_ref, m_ref, v_ref, d_ref, mo_ref, vo_ref):
        gg = g_ref[...]
        mn = ADAM_B1 * m_ref[...] + (1.0 - ADAM_B1) * gg
        vn = ADAM_B2 * v_ref[...] + (1.0 - ADAM_B2) * (gg * gg)
        d_ref[...] = -ADAM_LR * ((mn / c1) / (jnp.sqrt(vn / c2) + ADAM_EPS) + ADAM_WD * w_ref[...])
        mo_ref[...] = mn
        vo_ref[...] = vn

    blk = pl.BlockSpec((tr, C), lambda i: (i, 0))
    s = jax.ShapeDtypeStruct((R, C), _F32)
    return pl.pallas_call(
        body, name=name, grid=(R // tr,),
        in_specs=[blk] * 4, out_specs=[blk] * 3, out_shape=[s, s, s],
        compiler_params=_cparams(("parallel",)),
    )(w, g, m, v)


_ANY = pl.BlockSpec(memory_space=pl.ANY)
_MESH = pl.DeviceIdType.MESH


def _place():
    return lax.axis_index("x"), lax.axis_index("y"), lax.axis_index("c")


def _other_chips(x, y):
    chips = [(1 - x, y), (x, 1 - y), (1 - x, 1 - y)]
    return chips, [2 * a + b for a, b in chips]


def _rcopy(src, dst, ssem, rsem, dev):
    return pltpu.make_async_remote_copy(src_ref=src, dst_ref=dst, send_sem=ssem, recv_sem=rsem,
                                        device_id=dev, device_id_type=_MESH)


def gather_weights(big, small, *, name):
    nb, ns = len(big), len(small)

    def body(*refs):
        b_in, s_in = refs[:nb], refs[nb:nb + ns]
        b_out, s_out = refs[nb + ns:2 * nb + ns], refs[2 * nb + ns:2 * (nb + ns)]
        send, recv, fsend, frecv, ssend, srecv, lsem = refs[2 * (nb + ns):]
        x, y, c = _place()
        me = 2 * x + y
        sib = (x, y, 1 - c)
        chips, cidx = _other_chips(x, y)
        local = [pltpu.make_async_copy(b_in[p], b_out[p].at[me], lsem.at[p]) for p in range(nb)]
        local += [pltpu.make_async_copy(s_in[q], s_out[q].at[me], lsem.at[nb + q]) for q in range(ns)]
        for cp in local:
            cp.start()
        first = []
        for k, chip in enumerate(chips):
            for p in range(nb):
                first.append(_rcopy(b_in[p].at[c], b_out[p].at[me, c], send.at[p * 3 + k], recv.at[p * 3 + k], (*chip, c)))
            for q in range(ns):
                first.append(_rcopy(s_in[q], s_out[q].at[me], ssend.at[q * 3 + k], srecv.at[q * 3 + k], (*chip, c)))
        for cp in first:
            cp.start()
        passed = []
        for k in range(3):
            for p in range(nb):
                got = b_out[p].at[cidx[k], c]
                _rcopy(got, got, send.at[p * 3 + k], recv.at[p * 3 + k], sib).wait_recv()
                fw = _rcopy(got, got, fsend.at[p * 3 + k], frecv.at[p * 3 + k], sib)
                fw.start()
                passed.append(fw)
        for k in range(3):
            for p in range(nb):
                got = b_out[p].at[cidx[k], 1 - c]
                _rcopy(got, got, fsend.at[p * 3 + k], frecv.at[p * 3 + k], sib).wait_recv()
            for q in range(ns):
                got = s_out[q].at[cidx[k]]
                _rcopy(got, got, ssend.at[q * 3 + k], srecv.at[q * 3 + k], sib).wait_recv()
        for cp in first + passed:
            cp.wait_send()
        for cp in local:
            cp.wait()

    out_shape = [jax.ShapeDtypeStruct((4,) + a.shape, a.dtype) for a in big + small]
    return pl.pallas_call(
        body, name=name,
        in_specs=[_ANY] * (nb + ns), out_specs=[_ANY] * (nb + ns), out_shape=out_shape,
        scratch_shapes=[pltpu.SemaphoreType.DMA((nb * 3,))] * 4 + [pltpu.SemaphoreType.DMA((max(ns, 1) * 3,))] * 2
        + [pltpu.SemaphoreType.DMA((nb + ns,))],
    )(*big, *small)


def sibling_send_half(gs, *, name):
    n = len(gs)

    def body(*refs):
        g_in, g_out, send, recv = refs[:n], refs[n:2 * n], refs[2 * n], refs[2 * n + 1]
        x, y, c = _place()
        sib = (x, y, 1 - c)
        cps = [_rcopy(g_in[p].at[1 - c], g_out[p], send.at[p], recv.at[p], sib) for p in range(n)]
        for cp in cps:
            cp.start()
        for cp in cps:
            cp.wait()

    return pl.pallas_call(
        body, name=name, in_specs=[_ANY] * n, out_specs=[_ANY] * n,
        out_shape=[jax.ShapeDtypeStruct(a.shape[1:], a.dtype) for a in gs],
        scratch_shapes=[pltpu.SemaphoreType.DMA((n,))] * 2,
    )(*gs)


def chip_exchange(ps, *, name):
    n = len(ps)

    def body(*refs):
        p_in, p_out, send, recv, lsem = refs[:n], refs[n:2 * n], refs[2 * n], refs[2 * n + 1], refs[2 * n + 2]
        x, y, c = _place()
        me = 2 * x + y
        chips, cidx = _other_chips(x, y)
        local = [pltpu.make_async_copy(p_in[p].at[me], p_out[p].at[me], lsem.at[p]) for p in range(n)]
        for cp in local:
            cp.start()
        cps = []
        for k, chip in enumerate(chips):
            for p in range(n):
                cps.append(_rcopy(p_in[p].at[cidx[k]], p_out[p].at[me], send.at[p * 3 + k], recv.at[p * 3 + k], (*chip, c)))
        for cp in cps:
            cp.start()
        for k in range(3):
            for p in range(n):
                got = p_out[p].at[cidx[k]]
                _rcopy(got, got, send.at[p * 3 + k], recv.at[p * 3 + k], (x, y, c)).wait_recv()
        for cp in cps:
            cp.wait_send()
        for cp in local:
            cp.wait()

    return pl.pallas_call(
        body, name=name, in_specs=[_ANY] * n, out_specs=[_ANY] * n,
        out_shape=[jax.ShapeDtypeStruct(a.shape, a.dtype) for a in ps],
        scratch_shapes=[pltpu.SemaphoreType.DMA((n * 3,))] * 2 + [pltpu.SemaphoreType.DMA((n,))],
    )(*ps)


def sibling_join(rs, *, name):
    n = len(rs)

    def body(*refs):
        r_in, r_out, send, recv, lsem = refs[:n], refs[n:2 * n], refs[2 * n], refs[2 * n + 1], refs[2 * n + 2]
        x, y, c = _place()
        sib = (x, y, 1 - c)
        local = [pltpu.make_async_copy(r_in[p], r_out[p].at[c], lsem.at[p]) for p in range(n)]
        cps = [_rcopy(r_in[p], r_out[p].at[c], send.at[p], recv.at[p], sib) for p in range(n)]
        for cp in local + cps:
            cp.start()
        for p in range(n):
            got = r_out[p].at[1 - c]
            _rcopy(got, got, send.at[p], recv.at[p], sib).wait_recv()
        for cp in cps:
            cp.wait_send()
        for cp in local:
            cp.wait()

    return pl.pallas_call(
        body, name=name, in_specs=[_ANY] * n, out_specs=[_ANY] * n,
        out_shape=[jax.ShapeDtypeStruct((2,) + a.shape, a.dtype) for a in rs],
        scratch_shapes=[pltpu.SemaphoreType.DMA((n,))] * 3,
    )(*rs)


def all_devices_exchange(v, *, name):
    def body(v_ref, o_ref, send, recv, lsem):
        x, y, c = _place()
        me = 4 * x + 2 * y + c
        local = pltpu.make_async_copy(v_ref, o_ref.at[me], lsem)
        local.start()
        peers = []
        for k in range(1, 8):
            px, py, pc = x ^ (k >> 2), y ^ ((k >> 1) & 1), c ^ (k & 1)
            peers.append((px, py, pc))
        cps = [_rcopy(v_ref, o_ref.at[me], send.at[k], recv.at[k], peer) for k, peer in enumerate(peers)]
        for cp in cps:
            cp.start()
        for k, (px, py, pc) in enumerate(peers):
            got = o_ref.at[4 * px + 2 * py + pc]
            _rcopy(got, got, send.at[k], recv.at[k], (x, y, c)).wait_recv()
        for cp in cps:
            cp.wait_send()
        local.wait()

    return pl.pallas_call(
        body, name=name, in_specs=[_ANY], out_specs=_ANY,
        out_shape=jax.ShapeDtypeStruct((8,) + v.shape, v.dtype),
        scratch_shapes=[pltpu.SemaphoreType.DMA((7,))] * 2 + [pltpu.SemaphoreType.DMA(())],
    )(v)


def add_halves(gs_and_rs, c_idx, *, name):
    outs = []
    for n_, (g, r) in enumerate(gs_and_rs):
        N, C = r.shape
        tr = _tile(N, max(16, (512 * 1024) // C // 16 * 16), 16)

        def body(c_ref, g_ref, r_ref, o_ref):
            o_ref[...] = (g_ref[...].astype(_F32) + r_ref[...].astype(_F32)).astype(o_ref.dtype)

        outs.append(pl.pallas_call(
            body, name=f"{name}_{n_}",
            grid_spec=pltpu.PrefetchScalarGridSpec(
                num_scalar_prefetch=1, grid=(N // tr,),
                in_specs=[pl.BlockSpec((None, tr, C), lambda i, c: (c[0], i, 0)), pl.BlockSpec((tr, C), lambda i, c: (i, 0))],
                out_specs=pl.BlockSpec((tr, C), lambda i, c: (i, 0))),
            out_shape=jax.ShapeDtypeStruct((N, C), r.dtype),
            compiler_params=_cparams(("parallel",)),
        )(c_idx, g, r))
    return outs


def sum_slots(a, *, name):
    n, N, C = a.shape
    tr = _tile(N, max(16, (512 * 1024) // C // 16 * 16), 16)

    def body(a_ref, o_ref):
        s = a_ref[0].astype(_F32)
        for k in range(1, n):
            s = s + a_ref[k].astype(_F32)
        o_ref[...] = s

    return pl.pallas_call(
        body, name=name, grid=(N // tr,),
        in_specs=[pl.BlockSpec((n, tr, C), lambda i: (0, i, 0))],
        out_specs=pl.BlockSpec((tr, C), lambda i: (i, 0)),
        out_shape=jax.ShapeDtypeStruct((N, C), _F32),
        compiler_params=_cparams(("parallel",)),
    )(a)


_WEIGHTS = ['ln_in_g', 'ln_in_b', 'w_in', 'q_norm_g', 'w_uq', 'kv_norm_g', 'w_ukv', 'conv_w', 'conv_b', 'conv_ln_g',
            'conv_ln_b', 'w_pool', 'pool_scale', 'w_out', 'ln1_g', 'ln1_b', 'w_up', 'ffn_conv_w', 'ffn_conv_b', 'w_down',
            'ln2_g', 'ln2_b']
_BIG = ['w_in', 'w_uq', 'w_ukv', 'w_out', 'w_up', 'w_down']
_SMALL_SHARDED = ['conv_w', 'ffn_conv_w']
_SMALL = [n for n in _WEIGHTS if n not in _BIG]


def _rope_tables(positions):
    half = QK_ROPE // 2
    inv = 1.0 / (ROPE_THETA ** (jnp.arange(0, QK_ROPE, 2, dtype=_F32) / QK_ROPE))
    ang = positions.reshape(-1).astype(_F32)[:, None] * inv
    c, s = jnp.cos(ang), jnp.sin(ang)
    z = jnp.zeros_like(c)
    cc = jnp.concatenate([c, c, z, z], axis=1)
    sa = jnp.concatenate([-s, z, z, z], axis=1)
    sb = jnp.concatenate([z, s, z, z], axis=1)
    assert cc.shape[1] == 128 and half == 32
    return cc, sa, sb


def _layer_weights(full, l, dims):
    D, QL, KVL, CW, PW, H, F = dims
    w_in = full['w_in'][:, l].transpose(1, 0, 2).reshape(D, -1)
    o1, o2, o3, o4 = QL, QL + KVL, QL + KVL + QK_ROPE, QL + KVL + QK_ROPE + 2 * CW
    w_in_p = jnp.concatenate([w_in[:, :o1], w_in[:, o3:o4], w_in[:, o4:], w_in[:, o1:o2], w_in[:, o2:o3],
                              jnp.zeros((D, 128 - QK_ROPE), w_in.dtype)], axis=1)
    w_uq = full['w_uq'][:, l].reshape(QL, H, QK_NOPE + QK_ROPE)
    w_uq_p = jnp.pad(w_uq, ((0, 0), (0, 0), (0, HEAD_PAD - QK_NOPE - QK_ROPE))).reshape(QL, H * HEAD_PAD)
    return dict(
        w_in=w_in_p, w_uq=w_uq_p,
        w_ukv=full['w_ukv'][:, l].reshape(KVL, H * (QK_NOPE + V_HEAD)),
        w_out=full['w_out'][:, l].reshape(D, D),
        w_up=full['w_up'][:, l].transpose(1, 0, 2).reshape(D, 2 * F),
        w_down=full['w_down'][:, l].reshape(F, D),
    )


def _unpermute_w_in_grad(g, dims):
    D, QL, KVL, CW, PW, H, F = dims
    a, b_, c_ = QL, QL + 2 * CW, QL + 2 * CW + PW
    return jnp.concatenate([g[:, :a], g[:, c_:c_ + KVL], g[:, c_ + KVL:c_ + KVL + QK_ROPE], g[:, a:b_], g[:, b_:c_]], axis=1)


def _local_step(x, positions, target, full, small, dims, B, S, L):
    D, QL, KVL, CW, PW, H, F = dims
    T = B * S
    alpha = (2.0 * L) ** 0.25
    scale = float(QK_NOPE + QK_ROPE) ** -0.5
    cc, sa, sb = _rope_tables(positions)
    cb_q, cb_a, cb_g, cb_p = 0, QL // CW, QL // CW + 1, (QL + 2 * CW) // PW
    cb_kv, cb_kr = (QL + 2 * CW + PW) // KVL, (QL + 2 * CW + PW + KVL) // 128
    assert QL % CW == 0 and (QL + 2 * CW) % PW == 0 and (QL + 2 * CW + PW) % KVL == 0 and (QL + 2 * CW + PW + KVL) % 128 == 0

    xs, xb = ln_fwd([x], [1.0], small['ln_in_g'], small['ln_in_b'], want_r=False, name="ln_in")
    saved = []
    Ws = [_layer_weights(full, l, dims) for l in range(L)]
    for l in range(L):
        W = Ws[l]
        h = matmul(xb, W['w_in'], name="mm_in")
        qn = rms_fwd(h, cb_q, QL, small['q_norm_g'][l], name="rms_q")
        kvn = rms_fwd(h, cb_kv, KVL, small['kv_norm_g'][l], name="rms_kv")
        q = matmul(qn, W['w_uq'], name="mm_uq")
        kv = matmul(kvn, W['w_ukv'], name="mm_ukv")
        qp, kp, v = mla_pack(q, kv, h, cb_kr, cc, sa, sb, H=H, name="mla_pack")
        o, lse = flash_fwd(qp, kp, v, B=B, S=S, H=H, scale=scale, name="flash_fwd")
        z, yc = conv_fwd(h, cb_a, cb_g, small['conv_w'][l], small['conv_b'][l], small['conv_ln_g'][l],
                         small['conv_ln_b'][l], B=B, S=S, name="conv_fwd")
        yp = pool_fwd(h, cb_p, small['w_pool'][l], small['pool_scale'][l], B=B, S=S, name="pool_fwd")
        mixed = jnp.concatenate([o.astype(_BF), yc, yp], axis=1)
        y1 = matmul(mixed, W['w_out'], name="mm_out")
        r1, x1, x1b = ln_fwd([xs, y1], [alpha, 1.0], small['ln1_g'][l], small['ln1_b'][l], want_r=True, name="ln1")
        up = matmul(x1b, W['w_up'], out_dtype=_BF, name="mm_up")
        act = gate_fwd(up, small['ffn_conv_w'][l], small['ffn_conv_b'][l], B=B, S=S, name="gate_fwd")
        y2 = matmul(act, W['w_down'], name="mm_down")
        r2, x2, x2b = ln_fwd([x1, y2], [alpha, 1.0], small['ln2_g'][l], small['ln2_b'][l], want_r=True, name="ln2")
        saved.append(dict(xb=xb, h=h, qn=qn, kvn=kvn, qp=qp, kp=kp, v=v, o=o, lse=lse, z=z, mixed=mixed, r1=r1,
                          x1b=x1b, up=up, act=act, r2=r2))
        xs, xb = x2, x2b

    dy, loss_cols = loss_head(xs, target, name="loss_head")
    gb = {n: [None] * L for n in _BIG}
    gs = {n: [None] * L for n in _SMALL if n not in ('ln_in_g', 'ln_in_b')}
    d_terms, d_coefs = [dy], [1.0]
    for l in reversed(range(L)):
        sv = saved[l]
        W = Ws[l]
        dr2, dr2b, gs['ln2_g'][l], gs['ln2_b'][l] = ln_bwd(d_terms, d_coefs, sv['r2'], small['ln2_g'][l], name="ln2_bwd")
        dact = matmul(dr2b, W['w_down'], tb=True, out_dtype=_BF, name="mm_down_dx")
        gb['w_down'][l] = matmul(sv['act'], dr2b, ta=True, out_dtype=_BF, name="mm_down_dw")
        dua, dug, dwa, dwg, dba, dbg = gate_bwd(sv['up'], dact, small['ffn_conv_w'][l], small['ffn_conv_b'][l],
                                                B=B, S=S, name="gate_bwd")
        gs['ffn_conv_w'][l] = jnp.concatenate([dwa, dwg], axis=1)
        gs['ffn_conv_b'][l] = jnp.concatenate([dba, dbg], axis=1)
        dup = jnp.concatenate([dua, dug], axis=1)
        gb['w_up'][l] = matmul(sv['x1b'], dup, ta=True, out_dtype=_BF, name="mm_up_dw")
        dx1 = matmul(dup, W['w_up'], tb=True, name="mm_up_dx")
        dr1, dr1b, gs['ln1_g'][l], gs['ln1_b'][l] = ln_bwd([dr2, dx1], [alpha, 1.0], sv['r1'], small['ln1_g'][l],
                                                            name="ln1_bwd")
        dmix = matmul(dr1b, W['w_out'], tb=True, name="mm_out_dx")
        gb['w_out'][l] = matmul(sv['mixed'], dr1b, ta=True, out_dtype=_BF, name="mm_out_dw")
        h = sv['h']
        ncb = (H * V_HEAD) // CW
        dca, dcg, gs['conv_w'][l], gs['conv_b'][l], gs['conv_ln_g'][l], gs['conv_ln_b'][l] = conv_bwd(
            dmix, ncb, sv['z'], h, cb_a, cb_g, small['conv_w'][l], small['conv_ln_g'][l], small['conv_ln_b'][l],
            B=B, S=S, name="conv_bwd")
        dpool, gs['w_pool'][l], gs['pool_scale'][l] = pool_bwd(
            dmix, (H * V_HEAD + CW) // PW, h, cb_p, small['w_pool'][l], small['pool_scale'][l], B=B, S=S, name="pool_bwd")
        fa = dict(B=B, S=S, H=H, scale=scale)
        dqp = flash_bwd_dq(sv['qp'], sv['kp'], sv['v'], sv['o'], sv['lse'], dmix, 0, name="flash_dq", **fa)
        dkp, dv = flash_bwd_dkv(sv['qp'], sv['kp'], sv['v'], sv['o'], sv['lse'], dmix, 0, name="flash_dkv", **fa)
        dq, dkv, dkr = mla_unpack(dqp, dkp, dv, cc, sa, sb, H=H, name="mla_unpack")
        dqn = matmul(dq, W['w_uq'], tb=True, name="mm_uq_dx")
        g_uq = matmul(sv['qn'], dq, ta=True, out_dtype=_BF, name="mm_uq_dw")
        gb['w_uq'][l] = g_uq.reshape(QL, H, HEAD_PAD)[:, :, :QK_NOPE + QK_ROPE].reshape(QL, -1)
        dkvn = matmul(dkv, W['w_ukv'], tb=True, name="mm_ukv_dx")
        gb['w_ukv'][l] = matmul(sv['kvn'], dkv, ta=True, out_dtype=_BF, name="mm_ukv_dw")
        dcq, gs['q_norm_g'][l] = rms_bwd(dqn, h, cb_q, QL, small['q_norm_g'][l], name="rms_q_bwd")
        dckv, gs['kv_norm_g'][l] = rms_bwd(dkvn, h, cb_kv, KVL, small['kv_norm_g'][l], name="rms_kv_bwd")
        dh = jnp.concatenate([dcq, dca, dcg, dpool, dckv, dkr], axis=1)
        gb['w_in'][l] = _unpermute_w_in_grad(matmul(sv['xb'], dh, ta=True, out_dtype=_BF, name="mm_in_dw"), dims)
        dxm = matmul(dh, W['w_in'], tb=True, name="mm_in_dx")
        d_terms, d_coefs = [dr1, dxm], [alpha, 1.0]
    gx, _, g_ln_g, g_ln_b = ln_bwd(d_terms, d_coefs, x, small['ln_in_g'], name="ln_in_bwd")
    gsm = {n: jnp.stack([a.reshape(small[n].shape[1:]) for a in gs[n]]) for n in gs}
    gsm['ln_in_g'], gsm['ln_in_b'] = g_ln_g.reshape(-1), g_ln_b.reshape(-1)
    return loss_cols, gx, gb, gsm


_COL_SHARDED = ('w_in', 'w_up')


def _flat_pad(arrs, mult=512 * 128):
    v = jnp.concatenate([a.reshape(-1) for a in arrs])
    n = v.shape[0]
    return jnp.pad(v, (0, (-n) % mult)).reshape(-1, 128)


def _split_like(flat, like):
    out, off = [], 0
    v = flat.reshape(-1)
    for a in like:
        out.append(v[off:off + a.size].reshape(a.shape))
        off += a.size
    return out


def kernel(x, positions, ln_in_g, ln_in_b, w_in, q_norm_g, w_uq, kv_norm_g, w_ukv, conv_w, conv_b, conv_ln_g, conv_ln_b, w_pool, pool_scale, w_out, ln1_g, ln1_b, w_up, ffn_conv_w, ffn_conv_b, w_down, ln2_g, ln2_b, loss_target, m_ln_in_g, m_ln_in_b, m_w_in, m_q_norm_g, m_w_uq, m_kv_norm_g, m_w_ukv, m_conv_w, m_conv_b, m_conv_ln_g, m_conv_ln_b, m_w_pool, m_pool_scale, m_w_out, m_ln1_g, m_ln1_b, m_w_up, m_ffn_conv_w, m_ffn_conv_b, m_w_down, m_ln2_g, m_ln2_b, v_ln_in_g, v_ln_in_b, v_w_in, v_q_norm_g, v_w_uq, v_kv_norm_g, v_w_ukv, v_conv_w, v_conv_b, v_conv_ln_g, v_conv_ln_b, v_w_pool, v_pool_scale, v_w_out, v_ln1_g, v_ln1_b, v_w_up, v_ffn_conv_w, v_ffn_conv_b, v_w_down, v_ln2_g, v_ln2_b):
    a = dict(locals())
    B, S, D = a['x'].shape
    T = B * S
    L = a['w_in'].shape[0]
    QL, H = 4 * a['w_uq'].shape[1], a['w_uq'].shape[2]
    KVL = 4 * a['w_ukv'].shape[1]
    CW, PW = a['conv_b'].shape[1], a['pool_scale'].shape[1]
    F = 4 * a['w_down'].shape[1]
    dims = (D, QL, KVL, CW, PW, H, F)
    chip = 2 * lax.axis_index("x") + lax.axis_index("y")
    c_idx = lax.axis_index("c").astype(jnp.int32).reshape(1)

    def shard2d(w):
        return w.reshape(-1, w.shape[-1]) if w.ndim == 3 else w.reshape(w.shape[0] * w.shape[1], -1)

    big_local = [shard2d(a[n]) for n in _BIG]
    big_in = [w.astype(_BF).reshape(2, w.shape[0] // 2, w.shape[1]) for w in big_local]
    small_in = [shard2d(a[n]) for n in _SMALL_SHARDED]
    outs = gather_weights(big_in, small_in, name="gather_weights")
    full = {n: o.reshape(4, L, w.shape[0] // L, w.shape[1]) for n, o, w in zip(_BIG, outs[:len(_BIG)], big_local)}
    small = {n: a[n] for n in _SMALL}
    for n, o in zip(_SMALL_SHARDED, outs[len(_BIG):]):
        k = a[n].shape[1]
        small[n] = o.reshape(4, L, k, -1).transpose(1, 2, 0, 3).reshape(L, k, -1)

    loss_cols, gx, gb, gsm = _local_step(a['x'].reshape(T, D), a['positions'], a['loss_target'].reshape(T, D),
                                         full, small, dims, B, S, L)
    loss = lax.psum(jnp.sum(loss_cols), ("x", "y", "c"))

    g_in = []
    for n, w in zip(_BIG, big_local):
        per_layer = []
        for g in gb[n]:
            if n in _COL_SHARDED:
                per_layer.append(g.reshape(g.shape[0], 4, -1).transpose(1, 0, 2))
            else:
                per_layer.append(g.reshape(4, g.shape[0] // 4, -1))
        st = jnp.stack(per_layer, axis=1)
        hR = w.shape[0] // 2
        g_in.append(st.reshape(4, 2, hR, w.shape[1]).transpose(1, 0, 2, 3).reshape(2, 4 * hR, w.shape[1]))
    from_sib = sibling_send_half(g_in, name="grad_sibling_send")
    chip_sums = add_halves(list(zip(g_in, from_sib)), c_idx, name="grad_presum")
    chip_sums = [p.reshape(4, p.shape[0] // 4, p.shape[1]) for p in chip_sums]
    slots = chip_exchange(chip_sums, name="grad_chip_exchange")
    halves = [sum_slots(s, name=f"grad_sum_{i}") for i, s in enumerate(slots)]
    joined = sibling_join(halves, name="grad_sibling_join")
    g_big = {n: j.reshape(a[n].shape) for n, j in zip(_BIG, joined)}

    sm_like = [gsm[n] for n in _SMALL]
    sm_sum = sum_slots(all_devices_exchange(_flat_pad(sm_like), name="small_exchange"), name="small_sum")
    g_small = dict(zip(_SMALL, _split_like(sm_sum, sm_like)))
    for n in _SMALL_SHARDED:
        w = a[n].shape[-1]
        g_small[n] = lax.dynamic_slice_in_dim(g_small[n], chip * w, w, axis=2)

    grads, delta, new_m, new_v = {}, {}, {}, {}
    for n in _BIG:
        grads[n] = g_big[n]
        d_, m_, v_ = adamw(shard2d(a[n]), shard2d(g_big[n]), shard2d(a['m_' + n]), shard2d(a['v_' + n]), name="adamw_" + n)
        delta[n], new_m[n], new_v[n] = (t.reshape(a[n].shape) for t in (d_, m_, v_))
    like = [a[n] for n in _SMALL]
    d_, m_, v_ = adamw(_flat_pad(like), _flat_pad([g_small[n] for n in _SMALL]), _flat_pad([a['m_' + n] for n in _SMALL]),
                       _flat_pad([a['v_' + n] for n in _SMALL]), name="adamw_small")
    for n, dd, mm, vv in zip(_SMALL, _split_like(d_, like), _split_like(m_, like), _split_like(v_, like)):
        grads[n], delta[n], new_m[n], new_v[n] = g_small[n], dd, mm, vv

    return (loss, gx.reshape(B, S, D), *[grads[n] for n in _WEIGHTS], *[delta[n] for n in _WEIGHTS],
            *[new_m[n] for n in _WEIGHTS], *[new_v[n] for n in _WEIGHTS])
```

```python
import functools
import math

import jax
import jax.numpy as jnp
from jax import lax
from jax.experimental import pallas as pl
from jax.experimental.pallas import tpu as pltpu

_BF = jnp.bfloat16
_F32 = jnp.float32
_VMEM_LIMIT = 56 * 1024 * 1024

QK_NOPE = 128
QK_ROPE = 64
V_HEAD = 128
HEAD_PAD = 256
ROPE_THETA = 10000.0
LN_EPS = 1e-5
RMS_EPS = 1e-6
POOL_WINDOWS = (2, 4, 8, 16)
ADAM_LR, ADAM_B1, ADAM_B2, ADAM_EPS, ADAM_WD, ADAM_STEP = 0.001, 0.9, 0.999, 1e-8, 0.01, 10


def _cparams(sem=None):
    kw = dict(vmem_limit_bytes=_VMEM_LIMIT)
    if sem is not None:
        kw["dimension_semantics"] = sem
    return pltpu.CompilerParams(**kw)


def _tile(n, target, unit=128):
    if n <= target:
        return n
    t = (target // unit) * unit
    while t >= unit:
        if n % t == 0:
            return t
        t -= unit
    return n


_MM_VMEM_BUDGET = 40 * 1024 * 1024


def matmul(a, b, *, ta=False, tb=False, out_dtype=_F32, tm=1024, tn=1536, tk=2048, name="mm"):
    if ta:
        K, M = a.shape
    else:
        M, K = a.shape
    if tb:
        N, K2 = b.shape
    else:
        K2, N = b.shape
    assert K == K2, (a.shape, b.shape, ta, tb)
    tm, tn, tk = _tile(M, tm), _tile(N, tn), _tile(K, tk)
    ab, bb, ob = a.dtype.itemsize, b.dtype.itemsize, jnp.dtype(out_dtype).itemsize

    def vmem(tk_):
        return 2 * (tm * tk_ * ab + tk_ * tn * bb) + 2 * tm * tn * ob + tm * tn * 4 * (2 if K // tk_ > 1 else 1)

    while vmem(tk) > _MM_VMEM_BUDGET and tk > 256 and _tile(K, tk // 2) < tk:
        tk = _tile(K, tk // 2)
    nk = K // tk
    dn = (((0,) if ta else (1,), (1,) if tb else (0,)), ((), ()))

    def body(a_ref, b_ref, o_ref, *acc):
        prod = lax.dot_general(a_ref[...].astype(_BF), b_ref[...].astype(_BF), dn, preferred_element_type=_F32)
        if nk == 1:
            o_ref[...] = prod.astype(o_ref.dtype)
            return
        acc_ref, k = acc[0], pl.program_id(2)

        @pl.when(k == 0)
        def _():
            acc_ref[...] = prod

        @pl.when(k > 0)
        def _():
            acc_ref[...] += prod

        @pl.when(k == nk - 1)
        def _():
            o_ref[...] = acc_ref[...].astype(o_ref.dtype)

    a_spec = pl.BlockSpec((tk, tm), lambda i, j, k: (k, i)) if ta else pl.BlockSpec((tm, tk), lambda i, j, k: (i, k))
    b_spec = pl.BlockSpec((tn, tk), lambda i, j, k: (j, k)) if tb else pl.BlockSpec((tk, tn), lambda i, j, k: (k, j))
    return pl.pallas_call(
        body, name=name,
        grid=(M // tm, N // tn, nk),
        in_specs=[a_spec, b_spec],
        out_specs=pl.BlockSpec((tm, tn), lambda i, j, k: (i, j)),
        out_shape=jax.ShapeDtypeStruct((M, N), out_dtype),
        scratch_shapes=[pltpu.VMEM((tm, tn), _F32)] if nk > 1 else [],
        compiler_params=_cparams(("parallel", "parallel", "arbitrary")),
    )(a, b)


def _row_tile(T, C, budget_rows=256):
    return _tile(T, budget_rows, 16)


def ln_fwd(xs, coefs, g, b, *, want_r, name):
    T, C = xs[0].shape
    tr = _row_tile(T, C)
    n = len(xs)

    def body(*refs):
        x_refs, (g_ref, b_ref), outs = refs[:n], refs[n:n + 2], refs[n + 2:]
        r = coefs[0] * x_refs[0][...]
        for c, xr in zip(coefs[1:], x_refs[1:]):
            r = r + c * xr[...]
        mu = jnp.mean(r, axis=-1, keepdims=True)
        d = r - mu
        var = jnp.mean(d * d, axis=-1, keepdims=True)
        y = d * lax.rsqrt(var + LN_EPS) * g_ref[...] + b_ref[...]
        if want_r:
            outs[0][...] = r
        outs[-2][...] = y
        outs[-1][...] = y.astype(_BF)

    row = pl.BlockSpec((tr, C), lambda i: (i, 0))
    vec = pl.BlockSpec((1, C), lambda i: (0, 0))
    f = jax.ShapeDtypeStruct((T, C), _F32)
    out_shape = ([f] if want_r else []) + [f, jax.ShapeDtypeStruct((T, C), _BF)]
    return pl.pallas_call(
        body, name=name, grid=(T // tr,),
        in_specs=[row] * n + [vec, vec],
        out_specs=[row] * len(out_shape), out_shape=out_shape,
        compiler_params=_cparams(("parallel",)),
    )(*xs, g.reshape(1, C), b.reshape(1, C))


def ln_bwd(dys, coefs, r, g, *, name):
    T, C = r.shape
    tr = _row_tile(T, C)
    n = len(dys)

    def body(*refs):
        dy_refs, r_ref, g_ref = refs[:n], refs[n], refs[n + 1]
        dr_ref, drb_ref, dg_ref, db_ref = refs[n + 2:]
        dy = coefs[0] * dy_refs[0][...]
        for c, dr_ in zip(coefs[1:], dy_refs[1:]):
            dy = dy + c * dr_[...]
        rr = r_ref[...]
        mu = jnp.mean(rr, axis=-1, keepdims=True)
        d = rr - mu
        var = jnp.mean(d * d, axis=-1, keepdims=True)
        rstd = lax.rsqrt(var + LN_EPS)
        xh = d * rstd
        gdy = dy * g_ref[...]
        m1 = jnp.mean(gdy, axis=-1, keepdims=True)
        m2 = jnp.mean(gdy * xh, axis=-1, keepdims=True)
        dr = rstd * (gdy - m1 - xh * m2)
        dr_ref[...] = dr
        drb_ref[...] = dr.astype(_BF)

        @pl.when(pl.program_id(0) == 0)
        def _():
            dg_ref[...] = jnp.zeros_like(dg_ref)
            db_ref[...] = jnp.zeros_like(db_ref)

        dg_ref[...] += jnp.sum(dy * xh, axis=0, keepdims=True)
        db_ref[...] += jnp.sum(dy, axis=0, keepdims=True)

    row = pl.BlockSpec((tr, C), lambda i: (i, 0))
    vec = pl.BlockSpec((1, C), lambda i: (0, 0))
    return pl.pallas_call(
        body, name=name, grid=(T // tr,),
        in_specs=[row] * (n + 1) + [vec],
        out_specs=[row, row, vec, vec],
        out_shape=[jax.ShapeDtypeStruct((T, C), _F32), jax.ShapeDtypeStruct((T, C), _BF),
                   jax.ShapeDtypeStruct((1, C), _F32), jax.ShapeDtypeStruct((1, C), _F32)],
        compiler_params=_cparams(("arbitrary",)),
    )(*dys, r, g.reshape(1, C))


def rms_fwd(h, cb, W, g, *, name):
    T = h.shape[0]
    tr = _tile(T, 512, 16)

    def body(c_ref, g_ref, o_ref):
        c = c_ref[...]
        ms = jnp.mean(c * c, axis=-1, keepdims=True)
        o_ref[...] = (c * lax.rsqrt(ms + RMS_EPS) * g_ref[...]).astype(_BF)

    return pl.pallas_call(
        body, name=name, grid=(T // tr,),
        in_specs=[pl.BlockSpec((tr, W), lambda i: (i, cb)), pl.BlockSpec((1, W), lambda i: (0, 0))],
        out_specs=pl.BlockSpec((tr, W), lambda i: (i, 0)),
        out_shape=jax.ShapeDtypeStruct((T, W), _BF),
        compiler_params=_cparams(("parallel",)),
    )(h, g.reshape(1, W))


def rms_bwd(dy, h, cb, W, g, *, name):
    T = h.shape[0]
    tr = _tile(T, 512, 16)

    def body(dy_ref, c_ref, g_ref, dc_ref, dg_ref):
        c = c_ref[...]
        dyv = dy_ref[...]
        ms = jnp.mean(c * c, axis=-1, keepdims=True)
        r = lax.rsqrt(ms + RMS_EPS)
        u = dyv * g_ref[...]
        m = jnp.mean(c * u, axis=-1, keepdims=True)
        dc_ref[...] = (r * u - c * (r * r * r) * m).astype(_BF)

        @pl.when(pl.program_id(0) == 0)
        def _():
            dg_ref[...] = jnp.zeros_like(dg_ref)

        dg_ref[...] += jnp.sum(dyv * c * r, axis=0, keepdims=True)

    return pl.pallas_call(
        body, name=name, grid=(T // tr,),
        in_specs=[pl.BlockSpec((tr, W), lambda i: (i, 0)), pl.BlockSpec((tr, W), lambda i: (i, cb)),
                  pl.BlockSpec((1, W), lambda i: (0, 0))],
        out_specs=[pl.BlockSpec((tr, W), lambda i: (i, 0)), pl.BlockSpec((1, W), lambda i: (0, 0))],
        out_shape=[jax.ShapeDtypeStruct((T, W), _BF), jax.ShapeDtypeStruct((1, W), _F32)],
        compiler_params=_cparams(("arbitrary",)),
    )(dy, h, g.reshape(1, W))


def _rope(u, cc, sa, sb, sign):
    return u * cc + sign * (pltpu.roll(u, 96, 1) * sa + pltpu.roll(u, 32, 1) * sb)


def mla_pack(q, kv, h, kr_cb, cc, sa, sb, *, H, scale, name):
    T = q.shape[0]
    tr = _tile(T, 256, 16)

    def body(q_ref, kv_ref, kr_ref, cc_ref, sa_ref, sb_ref, qp_ref, kp_ref, v_ref):
        cc_, sa_, sb_ = cc_ref[...], sa_ref[...], sb_ref[...]
        kr = _rope(kr_ref[...], cc_, sa_, sb_, 1.0).astype(_BF)
        for hh in range(H):
            o = hh * HEAD_PAD
            qp_ref[:, o:o + 128] = (q_ref[:, o:o + 128] * scale).astype(_BF)
            qp_ref[:, o + 128:o + 256] = (_rope(q_ref[:, o + 128:o + 256], cc_, sa_, sb_, 1.0) * scale).astype(_BF)
            kp_ref[:, o:o + 128] = kv_ref[:, o:o + 128].astype(_BF)
            kp_ref[:, o + 128:o + 256] = kr
            v_ref[:, hh * 128:(hh + 1) * 128] = kv_ref[:, o + 128:o + 256].astype(_BF)

    wide = pl.BlockSpec((tr, H * HEAD_PAD), lambda i: (i, 0))
    tab = pl.BlockSpec((tr, 128), lambda i: (i, 0))
    return pl.pallas_call(
        body, name=name, grid=(T // tr,),
        in_specs=[wide, wide, pl.BlockSpec((tr, 128), lambda i: (i, kr_cb)), tab, tab, tab],
        out_specs=[wide, wide, pl.BlockSpec((tr, H * 128), lambda i: (i, 0))],
        out_shape=[jax.ShapeDtypeStruct((T, H * HEAD_PAD), _BF), jax.ShapeDtypeStruct((T, H * HEAD_PAD), _BF),
                   jax.ShapeDtypeStruct((T, H * 128), _BF)],
        compiler_params=_cparams(("parallel",)),
    )(q, kv, h, cc, sa, sb)


def mla_unpack(dqp, dkp, dv, cc, sa, sb, *, H, name):
    T = dqp.shape[0]
    tr = _tile(T, 256, 16)

    def body(dq_ref, dk_ref, dv_ref, cc_ref, sa_ref, sb_ref, oq_ref, okv_ref, okr_ref):
        cc_, sa_, sb_ = cc_ref[...], sa_ref[...], sb_ref[...]
        kr = jnp.zeros((tr, 128), _F32)
        for hh in range(H):
            o = hh * HEAD_PAD
            oq_ref[:, o:o + 128] = dq_ref[:, o:o + 128].astype(_BF)
            oq_ref[:, o + 128:o + 256] = _rope(dq_ref[:, o + 128:o + 256], cc_, sa_, sb_, -1.0).astype(_BF)
            okv_ref[:, o:o + 128] = dk_ref[:, o:o + 128].astype(_BF)
            okv_ref[:, o + 128:o + 256] = dv_ref[:, hh * 128:(hh + 1) * 128].astype(_BF)
            kr = kr + dk_ref[:, o + 128:o + 256]
        okr_ref[...] = _rope(kr, cc_, sa_, sb_, -1.0).astype(_BF)

    wide = pl.BlockSpec((tr, H * HEAD_PAD), lambda i: (i, 0))
    tab = pl.BlockSpec((tr, 128), lambda i: (i, 0))
    return pl.pallas_call(
        body, name=name, grid=(T // tr,),
        in_specs=[wide, wide, pl.BlockSpec((tr, H * 128), lambda i: (i, 0)), tab, tab, tab],
        out_specs=[wide, wide, tab],
        out_shape=[jax.ShapeDtypeStruct((T, H * HEAD_PAD), _BF), jax.ShapeDtypeStruct((T, H * HEAD_PAD), _BF),
                   jax.ShapeDtypeStruct((T, 128), _BF)],
        compiler_params=_cparams(("parallel",)),
    )(dqp, dkp, dv, cc, sa, sb)


_NEG = -1e30


def _scores(q, k, i, j, tq, tk, masked):
    s = lax.dot_general(q, k, (((1,), (1,)), ((), ())), preferred_element_type=_F32)
    if not masked:
        return s
    row = i * tq + lax.broadcasted_iota(jnp.int32, (tq, tk), 0)
    col = j * tk + lax.broadcasted_iota(jnp.int32, (tq, tk), 1)
    return jnp.where(col <= row, s, _NEG)


def _causal_cases(i, j, tq, tk, step):
    live = j * tk <= i * tq + (tq - 1)
    cut = j * tk + (tk - 1) > i * tq

    @pl.when(live & cut)
    def _():
        step(True)

    @pl.when(live & jnp.logical_not(cut))
    def _():
        step(False)


def flash_fwd(qp, kp, v, *, B, S, H, scale, name):
    T = B * S
    tq = tk = _tile(S, 512, 128)
    nq, nk = S // tq, S // tk

    def body(q_ref, k_ref, v_ref, o_ref, lse_ref, m_sc, l_sc, acc_sc):
        i, j = pl.program_id(2), pl.program_id(3)

        @pl.when(j == 0)
        def _():
            m_sc[...] = jnp.full_like(m_sc, _NEG)
            l_sc[...] = jnp.zeros_like(l_sc)
            acc_sc[...] = jnp.zeros_like(acc_sc)

        def step(masked):
            s = _scores(q_ref[...], k_ref[...], i, j, tq, tk, masked)
            m_old = m_sc[...]
            m_new = jnp.maximum(m_old, jnp.max(s, axis=-1, keepdims=True))
            p = jnp.exp(s - m_new)
            a = jnp.exp(m_old - m_new)
            l_sc[...] = a * l_sc[...] + jnp.sum(p, axis=-1, keepdims=True)
            acc_sc[...] = a * acc_sc[...] + jnp.dot(p.astype(_BF), v_ref[...], preferred_element_type=_F32)
            m_sc[...] = m_new

        _causal_cases(i, j, tq, tk, step)

        @pl.when(j == nk - 1)
        def _():
            l = l_sc[...]
            o_ref[...] = acc_sc[...] / l
            lse_ref[...] = jnp.broadcast_to(m_sc[...] + jnp.log(l), lse_ref.shape)

    def kmap(b, h, i, j):
        return (b * nk + jnp.minimum(j, (i * tq + tq - 1) // tk), h)

    qmap = lambda b, h, i, j: (b * nq + i, h)
    return pl.pallas_call(
        body, name=name, grid=(B, H, nq, nk),
        in_specs=[pl.BlockSpec((tq, HEAD_PAD), qmap), pl.BlockSpec((tk, HEAD_PAD), kmap), pl.BlockSpec((tk, 128), kmap)],
        out_specs=[pl.BlockSpec((tq, 128), qmap), pl.BlockSpec((tq, 128), qmap)],
        out_shape=[jax.ShapeDtypeStruct((T, H * 128), _F32), jax.ShapeDtypeStruct((T, H * 128), _F32)],
        scratch_shapes=[pltpu.VMEM((tq, 1), _F32), pltpu.VMEM((tq, 1), _F32), pltpu.VMEM((tq, 128), _F32)],
        compiler_params=_cparams(("parallel", "parallel", "parallel", "arbitrary")),
    )(qp, kp, v)


def flash_bwd_dq(qp, kp, v, o, lse, do, do_cb0, *, B, S, H, scale, name):
    T = B * S
    tq = tk = _tile(S, 512, 128)
    nq, nk = S // tq, S // tk

    def body(q_ref, k_ref, v_ref, o_ref, lse_ref, do_ref, dq_ref, acc_sc, dl_sc):
        i, j = pl.program_id(2), pl.program_id(3)

        @pl.when(j == 0)
        def _():
            acc_sc[...] = jnp.zeros_like(acc_sc)
            dl_sc[...] = jnp.sum(do_ref[...].astype(_F32) * o_ref[...], axis=-1, keepdims=True)

        def step(masked):
            k = k_ref[...]
            s = _scores(q_ref[...], k, i, j, tq, tk, masked)
            p = jnp.exp(s - lse_ref[:, 0:1])
            dp = lax.dot_general(do_ref[...].astype(_BF), v_ref[...], (((1,), (1,)), ((), ())),
                                 preferred_element_type=_F32)
            ds = p * (dp - dl_sc[...])
            acc_sc[...] += jnp.dot(ds.astype(_BF), k, preferred_element_type=_F32)

        _causal_cases(i, j, tq, tk, step)

        @pl.when(j == nk - 1)
        def _():
            dq_ref[...] = acc_sc[...] * scale

    def kmap(b, h, i, j):
        return (b * nk + jnp.minimum(j, (i * tq + tq - 1) // tk), h)

    qmap = lambda b, h, i, j: (b * nq + i, h)
    domap = lambda b, h, i, j: (b * nq + i, do_cb0 + h)
    return pl.pallas_call(
        body, name=name, grid=(B, H, nq, nk),
        in_specs=[pl.BlockSpec((tq, HEAD_PAD), qmap), pl.BlockSpec((tk, HEAD_PAD), kmap), pl.BlockSpec((tk, 128), kmap),
                  pl.BlockSpec((tq, 128), qmap), pl.BlockSpec((tq, 128), qmap), pl.BlockSpec((tq, 128), domap)],
        out_specs=pl.BlockSpec((tq, HEAD_PAD), qmap),
        out_shape=jax.ShapeDtypeStruct((T, H * HEAD_PAD), _F32),
        scratch_shapes=[pltpu.VMEM((tq, HEAD_PAD), _F32), pltpu.VMEM((tq, 1), _F32)],
        compiler_params=_cparams(("parallel", "parallel", "parallel", "arbitrary")),
    )(qp, kp, v, o, lse, do)


def flash_bwd_dkv(qp, kp, v, o, lse, do, do_cb0, *, B, S, H, scale, name):
    T = B * S
    tq = tk = _tile(S, 512, 128)
    nq, nk = S // tq, S // tk

    def body(q_ref, k_ref, v_ref, o_ref, lse_ref, do_ref, dk_ref, dv_ref, dk_sc, dv_sc):
        j, i = pl.program_id(2), pl.program_id(3)

        @pl.when(i == 0)
        def _():
            dk_sc[...] = jnp.zeros_like(dk_sc)
            dv_sc[...] = jnp.zeros_like(dv_sc)

        def step(masked):
            q = q_ref[...]
            dob = do_ref[...].astype(_BF)
            s = _scores(q, k_ref[...], i, j, tq, tk, masked)
            p = jnp.exp(s - lse_ref[:, 0:1])
            dl = jnp.sum(do_ref[...].astype(_F32) * o_ref[...], axis=-1, keepdims=True)
            dp = lax.dot_general(dob, v_ref[...], (((1,), (1,)), ((), ())), preferred_element_type=_F32)
            ds = p * (dp - dl)
            tn = (((0,), (0,)), ((), ()))
            dv_sc[...] += lax.dot_general(p.astype(_BF), dob, tn, preferred_element_type=_F32)
            dk_sc[...] += lax.dot_general(ds.astype(_BF), q, tn, preferred_element_type=_F32)

        _causal_cases(i, j, tq, tk, step)

        @pl.when(i == nq - 1)
        def _():
            dk_ref[...] = dk_sc[...]
            dv_ref[...] = dv_sc[...]

    def qrow(b, i, j):
        return b * nq + jnp.maximum(i, (j * tk) // tq)

    qmap = lambda b, h, j, i: (qrow(b, i, j), h)
    domap = lambda b, h, j, i: (qrow(b, i, j), do_cb0 + h)
    kmap = lambda b, h, j, i: (b * nk + j, h)
    return pl.pallas_call(
        body, name=name, grid=(B, H, nk, nq),
        in_specs=[pl.BlockSpec((tq, HEAD_PAD), qmap), pl.BlockSpec((tk, HEAD_PAD), kmap), pl.BlockSpec((tk, 128), kmap),
                  pl.BlockSpec((tq, 128), qmap), pl.BlockSpec((tq, 128), qmap), pl.BlockSpec((tq, 128), domap)],
        out_specs=[pl.BlockSpec((tk, HEAD_PAD), kmap), pl.BlockSpec((tk, 128), kmap)],
        out_shape=[jax.ShapeDtypeStruct((T, H * HEAD_PAD), _F32), jax.ShapeDtypeStruct((T, H * 128), _F32)],
        scratch_shapes=[pltpu.VMEM((tk, HEAD_PAD), _F32), pltpu.VMEM((tk, 128), _F32)],
        compiler_params=_cparams(("parallel", "parallel", "parallel", "arbitrary")),
    )(qp, kp, v, o, lse, do)


def _halo_specs(T, nT, tt, hr, cw, cb):
    k = tt // hr
    main = pl.BlockSpec((tt, cw), lambda b, t: (b * nT + t, cb))
    prev = pl.BlockSpec((hr, cw), lambda b, t: (jnp.maximum((b * nT + t) * k - 1, 0), cb))
    nxt = pl.BlockSpec((hr, cw), lambda b, t: (jnp.minimum((b * nT + t + 1) * k, T // hr - 1), cb))
    return main, prev, nxt


def _ln_rows(z, g, b):
    mu = jnp.mean(z, axis=-1, keepdims=True)
    d = z - mu
    var = jnp.mean(d * d, axis=-1, keepdims=True)
    rstd = lax.rsqrt(var + LN_EPS)
    xh = d * rstd
    return xh * g + b, xh, rstd


def conv_fwd(h, cb_a, cb_g, w, bias, lng, lnb, *, B, S, name):
    T = B * S
    K, C = w.shape
    hr = 32
    assert K - 1 <= hr
    tt = _tile(S, 512, hr)
    nT = S // tt
    a_m, a_p, _ = _halo_specs(T, nT, tt, hr, C, cb_a)
    g_m, g_p, _ = _halo_specs(T, nT, tt, hr, C, cb_g)

    def body(a_ref, g_ref, ap_ref, gp_ref, w_ref, b_ref, lg_ref, lb_ref, z_ref, y_ref, buf):
        t = pl.program_id(1)
        buf[pl.ds(hr, tt), :] = a_ref[...] * jax.nn.sigmoid(g_ref[...])
        hp = ap_ref[...] * jax.nn.sigmoid(gp_ref[...])
        buf[pl.ds(0, hr), :] = jnp.where(t == 0, 0.0, hp)
        z = jnp.broadcast_to(b_ref[...], (tt, C))
        for k in range(K):
            z = z + w_ref[k:k + 1, :] * buf[pl.ds(hr - (K - 1) + k, tt), :]
        z_ref[...] = z
        n, _, _ = _ln_rows(z, lg_ref[...], lb_ref[...])
        y_ref[...] = (n * jax.nn.sigmoid(n)).astype(_BF)

    vec = pl.BlockSpec((1, C), lambda b, t: (0, 0))
    out = pl.BlockSpec((tt, C), lambda b, t: (b * nT + t, 0))
    return pl.pallas_call(
        body, name=name, grid=(B, nT),
        in_specs=[a_m, g_m, a_p, g_p, pl.BlockSpec((K, C), lambda b, t: (0, 0)), vec, vec, vec],
        out_specs=[out, out],
        out_shape=[jax.ShapeDtypeStruct((T, C), _F32), jax.ShapeDtypeStruct((T, C), _BF)],
        scratch_shapes=[pltpu.VMEM((hr + tt, C), _F32)],
        compiler_params=_cparams(("parallel", "parallel")),
    )(h, h, h, h, w, bias.reshape(1, C), lng.reshape(1, C), lnb.reshape(1, C))


def conv_bwd(dmix, cb_dy, z, h, cb_a, cb_g, w, lng, lnb, *, B, S, name):
    T = B * S
    K, C = w.shape
    hr = 32
    tt = _tile(S, 512, hr)
    nT = S // tt
    a_m, a_p, _ = _halo_specs(T, nT, tt, hr, C, cb_a)
    g_m, g_p, _ = _halo_specs(T, nT, tt, hr, C, cb_g)
    dy_m, _, dy_n = _halo_specs(T, nT, tt, hr, C, cb_dy)
    z_m, _, z_n = _halo_specs(T, nT, tt, hr, C, 0)

    def body(dy_ref, dyn_ref, z_ref, zn_ref, a_ref, g_ref, ap_ref, gp_ref, w_ref, lg_ref, lb_ref,
             da_ref, dg_ref, dw_ref, db_ref, dlg_ref, dlb_ref, bufz, bufh):
        b, t = pl.program_id(0), pl.program_id(1)
        lg, lb = lg_ref[...], lb_ref[...]

        def dz_of(dy, zz):
            n, xh, rstd = _ln_rows(zz, lg, lb)
            sg = jax.nn.sigmoid(n)
            dn = dy.astype(_F32) * (sg * (1.0 + n * (1.0 - sg)))
            gdn = dn * lg
            m1 = jnp.mean(gdn, axis=-1, keepdims=True)
            m2 = jnp.mean(gdn * xh, axis=-1, keepdims=True)
            return rstd * (gdn - m1 - xh * m2), dn, xh

        dz, dn, xh = dz_of(dy_ref[...], z_ref[...])
        dzn, _, _ = dz_of(dyn_ref[...], zn_ref[...])
        bufz[pl.ds(0, tt), :] = dz
        bufz[pl.ds(tt, hr), :] = jnp.where(t == nT - 1, 0.0, dzn)
        a, g = a_ref[...], g_ref[...]
        sg = jax.nn.sigmoid(g)
        bufh[pl.ds(hr, tt), :] = a * sg
        bufh[pl.ds(0, hr), :] = jnp.where(t == 0, 0.0, ap_ref[...] * jax.nn.sigmoid(gp_ref[...]))

        @pl.when((b == 0) & (t == 0))
        def _():
            dw_ref[...] = jnp.zeros_like(dw_ref)
            db_ref[...] = jnp.zeros_like(db_ref)
            dlg_ref[...] = jnp.zeros_like(dlg_ref)
            dlb_ref[...] = jnp.zeros_like(dlb_ref)

        dhc = jnp.zeros((tt, C), _F32)
        for k in range(K):
            dhc = dhc + w_ref[k:k + 1, :] * bufz[pl.ds(K - 1 - k, tt), :]
            dw_ref[k:k + 1, :] += jnp.sum(dz * bufh[pl.ds(hr - (K - 1) + k, tt), :], axis=0, keepdims=True)
        da_ref[...] = (dhc * sg).astype(_BF)
        dg_ref[...] = (dhc * a * sg * (1.0 - sg)).astype(_BF)
        db_ref[...] += jnp.sum(dz, axis=0, keepdims=True)
        dlg_ref[...] += jnp.sum(dn * xh, axis=0, keepdims=True)
        dlb_ref[...] += jnp.sum(dn, axis=0, keepdims=True)

    vec = pl.BlockSpec((1, C), lambda b, t: (0, 0))
    out = pl.BlockSpec((tt, C), lambda b, t: (b * nT + t, 0))
    kc = pl.BlockSpec((K, C), lambda b, t: (0, 0))
    return pl.pallas_call(
        body, name=name, grid=(B, nT),
        in_specs=[dy_m, dy_n, z_m, z_n, a_m, g_m, a_p, g_p, kc, vec, vec],
        out_specs=[out, out, kc, vec, vec, vec],
        out_shape=[jax.ShapeDtypeStruct((T, C), _BF), jax.ShapeDtypeStruct((T, C), _BF),
                   jax.ShapeDtypeStruct((K, C), _F32)] + [jax.ShapeDtypeStruct((1, C), _F32)] * 3,
        scratch_shapes=[pltpu.VMEM((tt + hr, C), _F32), pltpu.VMEM((hr + tt, C), _F32)],
        compiler_params=_cparams(("arbitrary", "arbitrary")),
    )(dmix, dmix, z, z, h, h, h, h, w, lng.reshape(1, C), lnb.reshape(1, C))


def _pool_cnt(t, tt, w, rows):
    pos = t * tt + lax.broadcasted_iota(jnp.int32, (rows, 1), 0)
    return jnp.minimum(pos + 1, w).astype(_F32)


def pool_fwd(h, cb, wp, scale, *, B, S, name):
    T = B * S
    G, pg, _ = wp.shape
    C = G * pg
    assert pg == 128 and G == len(POOL_WINDOWS)
    hr = 16
    tt = _tile(S, 512, hr)
    nT = S // tt
    u_m, u_p, _ = _halo_specs(T, nT, tt, hr, C, cb)

    def body(u_ref, up_ref, wp_ref, sc_ref, y_ref, buf):
        t = pl.program_id(1)
        buf[pl.ds(hr, tt), :] = u_ref[...]
        buf[pl.ds(0, hr), :] = jnp.where(t == 0, 0.0, up_ref[...])
        for gi, w in enumerate(POOL_WINDOWS):
            ln = slice(gi * pg, (gi + 1) * pg)
            acc = buf[pl.ds(hr, tt), ln]
            for j in range(1, w):
                acc = acc + buf[pl.ds(hr - j, tt), ln]
            d = acc / _pool_cnt(t, tt, w, tt) - u_ref[:, ln]
            yg = jnp.dot(d.astype(_BF), wp_ref[gi].astype(_BF), preferred_element_type=_F32)
            y_ref[:, ln] = (yg * sc_ref[:, ln]).astype(_BF)

    return pl.pallas_call(
        body, name=name, grid=(B, nT),
        in_specs=[u_m, u_p, pl.BlockSpec((G, pg, pg), lambda b, t: (0, 0, 0)), pl.BlockSpec((1, C), lambda b, t: (0, 0))],
        out_specs=pl.BlockSpec((tt, C), lambda b, t: (b * nT + t, 0)),
        out_shape=jax.ShapeDtypeStruct((T, C), _BF),
        scratch_shapes=[pltpu.VMEM((hr + tt, C), _F32)],
        compiler_params=_cparams(("parallel", "parallel")),
    )(h, h, wp, scale.reshape(1, C))


def pool_bwd(dmix, cb_dy, h, cb, wp, scale, *, B, S, name):
    T = B * S
    G, pg, _ = wp.shape
    C = G * pg
    hr = 16
    tt = _tile(S, 512, hr)
    nT = S // tt
    u_m, u_p, _ = _halo_specs(T, nT, tt, hr, C, cb)
    dy_m, _, dy_n = _halo_specs(T, nT, tt, hr, C, cb_dy)

    def body(dy_ref, dyn_ref, u_ref, up_ref, wp_ref, sc_ref, du_ref, dwp_ref, dsc_ref, buf, bufe):
        b, t = pl.program_id(0), pl.program_id(1)
        buf[pl.ds(hr, tt), :] = u_ref[...]
        buf[pl.ds(0, hr), :] = jnp.where(t == 0, 0.0, up_ref[...])

        @pl.when((b == 0) & (t == 0))
        def _():
            dwp_ref[...] = jnp.zeros_like(dwp_ref)
            dsc_ref[...] = jnp.zeros_like(dsc_ref)

        nt = (((1,), (1,)), ((), ()))
        tn = (((0,), (0,)), ((), ()))
        for gi, w in enumerate(POOL_WINDOWS):
            ln = slice(gi * pg, (gi + 1) * pg)
            wg = wp_ref[gi].astype(_BF)
            sc = sc_ref[:, ln]
            dy = dy_ref[:, ln].astype(_F32)
            dz = (dy * sc).astype(_BF)
            dzn = (dyn_ref[:, ln].astype(_F32) * sc).astype(_BF)
            dd = lax.dot_general(dz, wg, nt, preferred_element_type=_F32)
            ddn = lax.dot_general(dzn, wg, nt, preferred_element_type=_F32)
            bufe[pl.ds(0, tt), ln] = dd / _pool_cnt(t, tt, w, tt)
            bufe[pl.ds(tt, hr), ln] = jnp.where(t == nT - 1, 0.0, ddn / _pool_cnt(t + 1, tt, w, hr))
            du = -dd
            for j in range(w):
                du = du + bufe[pl.ds(j, tt), ln]
            du_ref[:, ln] = du.astype(_BF)
            acc = buf[pl.ds(hr, tt), ln]
            for j in range(1, w):
                acc = acc + buf[pl.ds(hr - j, tt), ln]
            d = (acc / _pool_cnt(t, tt, w, tt) - u_ref[:, ln]).astype(_BF)
            dwp_ref[gi] += lax.dot_general(d, dz, tn, preferred_element_type=_F32)
            yg = jnp.dot(d, wg, preferred_element_type=_F32)
            dsc_ref[:, ln] += jnp.sum(dy * yg, axis=0, keepdims=True)

    return pl.pallas_call(
        body, name=name, grid=(B, nT),
        in_specs=[dy_m, dy_n, u_m, u_p, pl.BlockSpec((G, pg, pg), lambda b, t: (0, 0, 0)),
                  pl.BlockSpec((1, C), lambda b, t: (0, 0))],
        out_specs=[pl.BlockSpec((tt, C), lambda b, t: (b * nT + t, 0)), pl.BlockSpec((G, pg, pg), lambda b, t: (0, 0, 0)),
                   pl.BlockSpec((1, C), lambda b, t: (0, 0))],
        out_shape=[jax.ShapeDtypeStruct((T, C), _BF), jax.ShapeDtypeStruct((G, pg, pg), _F32),
                   jax.ShapeDtypeStruct((1, C), _F32)],
        scratch_shapes=[pltpu.VMEM((hr + tt, C), _F32), pltpu.VMEM((tt + hr, C), _F32)],
        compiler_params=_cparams(("arbitrary", "arbitrary")),
    )(dmix, dmix, h, h, wp, scale.reshape(1, C))


_FFN_HR = 16


def _silu_grad(x, sg):
    return sg * (1.0 + x * (1.0 - sg))


def _conv3(buf, w_ref, b_ref, off, rows):
    c = b_ref[...] + w_ref[0:1, :] * buf[pl.ds(off, rows), :]
    for k in (1, 2):
        c = c + w_ref[k:k + 1, :] * buf[pl.ds(off + k, rows), :]
    return c


def gate_fwd(up, w, bias, *, B, S, name):
    T, F2 = up.shape
    F = F2 // 2
    hr = _FFN_HR
    tt = _tile(S, 512, hr)
    nT = S // tt
    tn = _tile(F, 512, 128)
    nC = F // tn
    k = tt // hr

    def body(a_ref, g_ref, ap_ref, gp_ref, wa_ref, wg_ref, ba_ref, bg_ref, o_ref, bufa, bufg):
        t = pl.program_id(2)
        for buf, m_ref, p_ref in ((bufa, a_ref, ap_ref), (bufg, g_ref, gp_ref)):
            buf[pl.ds(hr, tt), :] = m_ref[...].astype(_F32)
            buf[pl.ds(0, hr), :] = jnp.where(t == 0, 0.0, p_ref[...].astype(_F32))
        ca = _conv3(bufa, wa_ref, ba_ref, hr - 2, tt)
        cg = _conv3(bufg, wg_ref, bg_ref, hr - 2, tt)
        o_ref[...] = (ca * cg * jax.nn.sigmoid(cg)).astype(_BF)

    def main(off):
        return pl.BlockSpec((tt, tn), lambda b, j, t: (b * nT + t, j + off))

    def prev(off):
        return pl.BlockSpec((hr, tn), lambda b, j, t: (jnp.maximum((b * nT + t) * k - 1, 0), j + off))

    def wspec(rows, off):
        return pl.BlockSpec((rows, tn), lambda b, j, t: (0, j + off))

    return pl.pallas_call(
        body, name=name, grid=(B, nC, nT),
        in_specs=[main(0), main(nC), prev(0), prev(nC), wspec(3, 0), wspec(3, nC), wspec(1, 0), wspec(1, nC)],
        out_specs=pl.BlockSpec((tt, tn), lambda b, j, t: (b * nT + t, j)),
        out_shape=jax.ShapeDtypeStruct((T, F), _BF),
        scratch_shapes=[pltpu.VMEM((hr + tt, tn), _F32)] * 2,
        compiler_params=_cparams(("parallel", "parallel", "parallel")),
    )(up, up, up, up, w, w, bias.reshape(1, F2), bias.reshape(1, F2))


def gate_bwd(up, dact, w, bias, *, B, S, name):
    T, F2 = up.shape
    F = F2 // 2
    hr = _FFN_HR
    tt = _tile(S, 512, hr)
    nT = S // tt
    tn = _tile(F, 512, 128)
    nC = F // tn
    k = tt // hr
    ext = tt + hr

    def body(a_ref, g_ref, ap_ref, gp_ref, an_ref, gn_ref, d_ref, dn_ref, wa_ref, wg_ref, ba_ref, bg_ref,
             dua_ref, dug_ref, dwa_ref, dwg_ref, dba_ref, dbg_ref, bufa, bufg, bufda, bufdg):
        b, t = pl.program_id(1), pl.program_id(2)
        last = t == nT - 1
        for buf, m_ref, p_ref, n_ref in ((bufa, a_ref, ap_ref, an_ref), (bufg, g_ref, gp_ref, gn_ref)):
            buf[pl.ds(hr, tt), :] = m_ref[...].astype(_F32)
            buf[pl.ds(0, hr), :] = jnp.where(t == 0, 0.0, p_ref[...].astype(_F32))
            buf[pl.ds(hr + tt, hr), :] = jnp.where(last, 0.0, n_ref[...].astype(_F32))
        ca = _conv3(bufa, wa_ref, ba_ref, hr - 2, ext)
        cg = _conv3(bufg, wg_ref, bg_ref, hr - 2, ext)
        sg = jax.nn.sigmoid(cg)
        bufda[pl.ds(0, tt), :] = d_ref[...].astype(_F32)
        bufda[pl.ds(tt, hr), :] = jnp.where(last, 0.0, dn_ref[...].astype(_F32))
        da = bufda[...]
        bufdg[...] = da * ca * _silu_grad(cg, sg)
        bufda[...] = da * cg * sg

        @pl.when((b == 0) & (t == 0))
        def _():
            for r in (dwa_ref, dwg_ref, dba_ref, dbg_ref):
                r[...] = jnp.zeros_like(r)

        for bufd, buf, w_ref, du_ref, dw_ref, db_ref in ((bufda, bufa, wa_ref, dua_ref, dwa_ref, dba_ref),
                                                         (bufdg, bufg, wg_ref, dug_ref, dwg_ref, dbg_ref)):
            dc = bufd[pl.ds(0, tt), :]
            du = w_ref[2:3, :] * dc + w_ref[1:2, :] * bufd[pl.ds(1, tt), :] + w_ref[0:1, :] * bufd[pl.ds(2, tt), :]
            du_ref[...] = du.astype(_BF)
            for kk in range(3):
                dw_ref[kk:kk + 1, :] += jnp.sum(dc * buf[pl.ds(hr - 2 + kk, tt), :], axis=0, keepdims=True)
            db_ref[...] += jnp.sum(dc, axis=0, keepdims=True)

    def main(off):
        return pl.BlockSpec((tt, tn), lambda j, b, t: (b * nT + t, j + off))

    def prev(off):
        return pl.BlockSpec((hr, tn), lambda j, b, t: (jnp.maximum((b * nT + t) * k - 1, 0), j + off))

    def nxt(off):
        return pl.BlockSpec((hr, tn), lambda j, b, t: (jnp.minimum((b * nT + t + 1) * k, T // hr - 1), j + off))

    def wspec(rows, off):
        return pl.BlockSpec((rows, tn), lambda j, b, t: (0, j + off))

    tf = jax.ShapeDtypeStruct((T, F), _BF)
    return pl.pallas_call(
        body, name=name, grid=(nC, B, nT),
        in_specs=[main(0), main(nC), prev(0), prev(nC), nxt(0), nxt(nC), main(0), nxt(0),
                  wspec(3, 0), wspec(3, nC), wspec(1, 0), wspec(1, nC)],
        out_specs=[main(0), main(0), wspec(3, 0), wspec(3, 0), wspec(1, 0), wspec(1, 0)],
        out_shape=[tf, tf, jax.ShapeDtypeStruct((3, F), _F32), jax.ShapeDtypeStruct((3, F), _F32),
                   jax.ShapeDtypeStruct((1, F), _F32), jax.ShapeDtypeStruct((1, F), _F32)],
        scratch_shapes=[pltpu.VMEM((hr + ext, tn), _F32)] * 2 + [pltpu.VMEM((ext, tn), _F32)] * 2,
        compiler_params=_cparams(("parallel", "arbitrary", "arbitrary")),
    )(up, up, up, up, up, up, dact, dact, w, w, bias.reshape(1, F2), bias.reshape(1, F2))


def loss_head(y, target, *, name):
    T, C = y.shape
    tr = _row_tile(T, C)

    def body(y_ref, t_ref, dy_ref, acc_ref):
        @pl.when(pl.program_id(0) == 0)
        def _():
            acc_ref[...] = jnp.zeros_like(acc_ref)

        e = y_ref[...] - t_ref[...]
        dy_ref[...] = e * (1.0 / C)
        acc_ref[...] += jnp.sum(e * e, axis=0, keepdims=True) * (0.5 / C)

    row = pl.BlockSpec((tr, C), lambda i: (i, 0))
    return pl.pallas_call(
        body, name=name, grid=(T // tr,),
        in_specs=[row, row], out_specs=[row, pl.BlockSpec((1, C), lambda i: (0, 0))],
        out_shape=[jax.ShapeDtypeStruct((T, C), _F32), jax.ShapeDtypeStruct((1, C), _F32)],
        compiler_params=_cparams(("arbitrary",)),
    )(y, target)


def adamw(w, g, m, v, *, name):
    R, C = w.shape
    tr = _tile(R, max(8, (256 * 1024) // C // 8 * 8), 8)
    c1 = 1.0 - ADAM_B1 ** ADAM_STEP
    c2 = 1.0 - ADAM_B2 ** ADAM_STEP

    def body(w_ref, g_ref, m_ref, v_ref, d_ref, mo_ref, vo_ref):
        gg = g_ref[...]
        mn = ADAM_B1 * m_ref[...] + (1.0 - ADAM_B1) * gg
        vn = ADAM_B2 * v_ref[...] + (1.0 - ADAM_B2) * (gg * gg)
        d_ref[...] = -ADAM_LR * ((mn / c1) / (jnp.sqrt(vn / c2) + ADAM_EPS) + ADAM_WD * w_ref[...])
        mo_ref[...] = mn
        vo_ref[...] = vn

    blk = pl.BlockSpec((tr, C), lambda i: (i, 0))
    s = jax.ShapeDtypeStruct((R, C), _F32)
    return pl.pallas_call(
        body, name=name, grid=(R // tr,),
        in_specs=[blk] * 4, out_specs=[blk] * 3, out_shape=[s, s, s],
        compiler_params=_cparams(("parallel",)),
    )(w, g, m, v)


_ANY = pl.BlockSpec(memory_space=pl.ANY)
_MESH = pl.DeviceIdType.MESH


def _place():
    return lax.axis_index("x"), lax.axis_index("y"), lax.axis_index("c")


def _other_chips(x, y):
    chips = [(1 - x, y), (x, 1 - y), (1 - x, 1 - y)]
    return chips, [2 * a + b for a, b in chips]


def _rcopy(src, dst, ssem, rsem, dev):
    return pltpu.make_async_remote_copy(src_ref=src, dst_ref=dst, send_sem=ssem, recv_sem=rsem,
                                        device_id=dev, device_id_type=_MESH)


def place_shard(w, chip_idx, *, name):
    _, hR, C = w.shape
    tr = _tile(hR, max(16, (512 * 1024) // C // 16 * 16), 16)

    def body(ci_ref, w_ref, o_ref):
        o_ref[...] = w_ref[...].astype(_BF)

    return pl.pallas_call(
        body, name=name,
        grid_spec=pltpu.PrefetchScalarGridSpec(
            num_scalar_prefetch=1, grid=(2, hR // tr),
            in_specs=[pl.BlockSpec((None, tr, C), lambda h, i, ci: (h, i, 0))],
            out_specs=pl.BlockSpec((None, None, tr, C), lambda h, i, ci: (ci[0], h, i, 0))),
        out_shape=jax.ShapeDtypeStruct((4, 2, hR, C), _BF),
        compiler_params=_cparams(("parallel", "parallel")),
    )(chip_idx, w)


def gather_weights(big, small, *, name):
    nb, ns = len(big), len(small)

    def body(*refs):
        s_in = refs[nb:nb + ns]
        b_out, s_out = refs[nb + ns:2 * nb + ns], refs[2 * nb + ns:2 * (nb + ns)]
        send, recv, fsend, frecv, ssend, srecv, lsem = refs[2 * (nb + ns):]
        x, y, c = _place()
        me = 2 * x + y
        sib = (x, y, 1 - c)
        chips, cidx = _other_chips(x, y)
        local = [pltpu.make_async_copy(s_in[q], s_out[q].at[me], lsem.at[q]) for q in range(ns)]
        for cp in local:
            cp.start()
        first = []
        for k, chip in enumerate(chips):
            for p in range(nb):
                mine = b_out[p].at[me, c]
                first.append(_rcopy(mine, mine, send.at[p * 3 + k], recv.at[p * 3 + k], (*chip, c)))
            for q in range(ns):
                first.append(_rcopy(s_in[q], s_out[q].at[me], ssend.at[q * 3 + k], srecv.at[q * 3 + k], (*chip, c)))
        for cp in first:
            cp.start()
        passed = []
        for k in range(3):
            for p in range(nb):
                got = b_out[p].at[cidx[k], c]
                _rcopy(got, got, send.at[p * 3 + k], recv.at[p * 3 + k], sib).wait_recv()
                fw = _rcopy(got, got, fsend.at[p * 3 + k], frecv.at[p * 3 + k], sib)
                fw.start()
                passed.append(fw)
        for k in range(3):
            for p in range(nb):
                got = b_out[p].at[cidx[k], 1 - c]
                _rcopy(got, got, fsend.at[p * 3 + k], frecv.at[p * 3 + k], sib).wait_recv()
            for q in range(ns):
                got = s_out[q].at[cidx[k]]
                _rcopy(got, got, ssend.at[q * 3 + k], srecv.at[q * 3 + k], sib).wait_recv()
        for cp in first + passed:
            cp.wait_send()
        for cp in local:
            cp.wait()

    out_shape = [jax.ShapeDtypeStruct(a.shape, a.dtype) for a in big]
    out_shape += [jax.ShapeDtypeStruct((4,) + a.shape, a.dtype) for a in small]
    return pl.pallas_call(
        body, name=name,
        in_specs=[_ANY] * (nb + ns), out_specs=[_ANY] * (nb + ns), out_shape=out_shape,
        input_output_aliases={p: p for p in range(nb)},
        scratch_shapes=[pltpu.SemaphoreType.DMA((nb * 3,))] * 4 + [pltpu.SemaphoreType.DMA((max(ns, 1) * 3,))] * 2
        + [pltpu.SemaphoreType.DMA((max(ns, 1),))],
    )(*big, *small)


def sibling_send_half(gs, *, name):
    n = len(gs)

    def body(*refs):
        g_in, g_out, send, recv = refs[:n], refs[n:2 * n], refs[2 * n], refs[2 * n + 1]
        x, y, c = _place()
        sib = (x, y, 1 - c)
        cps = [_rcopy(g_in[p].at[1 - c], g_out[p], send.at[p], recv.at[p], sib) for p in range(n)]
        for cp in cps:
            cp.start()
        for cp in cps:
            cp.wait()

    return pl.pallas_call(
        body, name=name, in_specs=[_ANY] * n, out_specs=[_ANY] * n,
        out_shape=[jax.ShapeDtypeStruct(a.shape[1:], a.dtype) for a in gs],
        scratch_shapes=[pltpu.SemaphoreType.DMA((n,))] * 2,
    )(*gs)


def chip_exchange(ps, *, name):
    n = len(ps)

    def body(*refs):
        p_in, p_out, send, recv = refs[:n], refs[n:2 * n], refs[2 * n], refs[2 * n + 1]
        x, y, c = _place()
        me = 2 * x + y
        chips, cidx = _other_chips(x, y)
        cps = []
        for k, chip in enumerate(chips):
            for p in range(n):
                cps.append(_rcopy(p_in[p].at[cidx[k]], p_out[p].at[me], send.at[p * 3 + k], recv.at[p * 3 + k], (*chip, c)))
        for cp in cps:
            cp.start()
        for k in range(3):
            for p in range(n):
                got = p_out[p].at[cidx[k]]
                _rcopy(got, got, send.at[p * 3 + k], recv.at[p * 3 + k], (x, y, c)).wait_recv()
        for cp in cps:
            cp.wait_send()

    return pl.pallas_call(
        body, name=name, in_specs=[_ANY] * n, out_specs=[_ANY] * n,
        out_shape=[jax.ShapeDtypeStruct(a.shape, a.dtype) for a in ps],
        scratch_shapes=[pltpu.SemaphoreType.DMA((n * 3,))] * 2,
    )(*ps)


def sum_chips(p, slots, idx, *, name):
    _, N, C = p.shape
    tr = _tile(N, max(16, (512 * 1024) // C // 16 * 16), 16)

    def body(i0, i1, i2, i3, i4, p_ref, s0_ref, s1_ref, s2_ref, o_ref):
        s = p_ref[...].astype(_F32)
        for r in (s0_ref, s1_ref, s2_ref):
            s = s + r[...].astype(_F32)
        o_ref[...] = s

    def at(k):
        return pl.BlockSpec((None, tr, C), lambda i, *ix: (ix[k][0], i, 0))

    return pl.pallas_call(
        body, name=name,
        grid_spec=pltpu.PrefetchScalarGridSpec(
            num_scalar_prefetch=5, grid=(N // tr,),
            in_specs=[at(0), at(1), at(2), at(3)], out_specs=at(4)),
        out_shape=jax.ShapeDtypeStruct((2, N, C), _F32),
        compiler_params=_cparams(("parallel",)),
    )(*idx, p, slots, slots, slots)


def sibling_join(rs, *, name):
    n = len(rs)

    def body(*refs):
        r_out, send, recv = refs[n:2 * n], refs[2 * n], refs[2 * n + 1]
        x, y, c = _place()
        sib = (x, y, 1 - c)
        cps = [_rcopy(r_out[p].at[c], r_out[p].at[c], send.at[p], recv.at[p], sib) for p in range(n)]
        for cp in cps:
            cp.start()
        for p in range(n):
            got = r_out[p].at[1 - c]
            _rcopy(got, got, send.at[p], recv.at[p], sib).wait_recv()
        for cp in cps:
            cp.wait_send()

    return pl.pallas_call(
        body, name=name, in_specs=[_ANY] * n, out_specs=[_ANY] * n,
        out_shape=[jax.ShapeDtypeStruct(a.shape, a.dtype) for a in rs],
        input_output_aliases={p: p for p in range(n)},
        scratch_shapes=[pltpu.SemaphoreType.DMA((n,))] * 2,
    )(*rs)


def all_devices_exchange(v, *, name):
    def body(v_ref, o_ref, send, recv, lsem):
        x, y, c = _place()
        me = 4 * x + 2 * y + c
        local = pltpu.make_async_copy(v_ref, o_ref.at[me], lsem)
        local.start()
        peers = []
        for k in range(1, 8):
            px, py, pc = x ^ (k >> 2), y ^ ((k >> 1) & 1), c ^ (k & 1)
            peers.append((px, py, pc))
        cps = [_rcopy(v_ref, o_ref.at[me], send.at[k], recv.at[k], peer) for k, peer in enumerate(peers)]
        for cp in cps:
            cp.start()
        for k, (px, py, pc) in enumerate(peers):
            got = o_ref.at[4 * px + 2 * py + pc]
            _rcopy(got, got, send.at[k], recv.at[k], (x, y, c)).wait_recv()
        for cp in cps:
            cp.wait_send()
        local.wait()

    return pl.pallas_call(
        body, name=name, in_specs=[_ANY], out_specs=_ANY,
        out_shape=jax.ShapeDtypeStruct((8,) + v.shape, v.dtype),
        scratch_shapes=[pltpu.SemaphoreType.DMA((7,))] * 2 + [pltpu.SemaphoreType.DMA(())],
    )(v)


def add_halves(gs_and_rs, c_idx, *, name):
    outs = []
    for n_, (g, r) in enumerate(gs_and_rs):
        N, C = r.shape
        tr = _tile(N, max(16, (512 * 1024) // C // 16 * 16), 16)

        def body(c_ref, g_ref, r_ref, o_ref):
            o_ref[...] = (g_ref[...].astype(_F32) + r_ref[...].astype(_F32)).astype(o_ref.dtype)

        outs.append(pl.pallas_call(
            body, name=f"{name}_{n_}",
            grid_spec=pltpu.PrefetchScalarGridSpec(
                num_scalar_prefetch=1, grid=(N // tr,),
                in_specs=[pl.BlockSpec((None, tr, C), lambda i, c: (c[0], i, 0)), pl.BlockSpec((tr, C), lambda i, c: (i, 0))],
                out_specs=pl.BlockSpec((tr, C), lambda i, c: (i, 0))),
            out_shape=jax.ShapeDtypeStruct((N, C), r.dtype),
            compiler_params=_cparams(("parallel",)),
        )(c_idx, g, r))
    return outs


def sum_slots(a, *, name):
    n, N, C = a.shape
    tr = _tile(N, max(16, (512 * 1024) // C // 16 * 16), 16)

    def body(a_ref, o_ref):
        s = a_ref[0].astype(_F32)
        for k in range(1, n):
            s = s + a_ref[k].astype(_F32)
        o_ref[...] = s

    return pl.pallas_call(
        body, name=name, grid=(N // tr,),
        in_specs=[pl.BlockSpec((n, tr, C), lambda i: (0, i, 0))],
        out_specs=pl.BlockSpec((tr, C), lambda i: (i, 0)),
        out_shape=jax.ShapeDtypeStruct((N, C), _F32),
        compiler_params=_cparams(("parallel",)),
    )(a)


_WEIGHTS = ['ln_in_g', 'ln_in_b', 'w_in', 'q_norm_g', 'w_uq', 'kv_norm_g', 'w_ukv', 'conv_w', 'conv_b', 'conv_ln_g',
            'conv_ln_b', 'w_pool', 'pool_scale', 'w_out', 'ln1_g', 'ln1_b', 'w_up', 'ffn_conv_w', 'ffn_conv_b', 'w_down',
            'ln2_g', 'ln2_b']
_BIG = ['w_in', 'w_uq', 'w_ukv', 'w_out', 'w_up', 'w_down']
_SMALL_SHARDED = ['conv_w', 'ffn_conv_w']
_SMALL = [n for n in _WEIGHTS if n not in _BIG]


def _rope_tables(positions):
    half = QK_ROPE // 2
    inv = 1.0 / (ROPE_THETA ** (jnp.arange(0, QK_ROPE, 2, dtype=_F32) / QK_ROPE))
    ang = positions.reshape(-1).astype(_F32)[:, None] * inv
    c, s = jnp.cos(ang), jnp.sin(ang)
    z = jnp.zeros_like(c)
    cc = jnp.concatenate([c, c, z, z], axis=1)
    sa = jnp.concatenate([-s, z, z, z], axis=1)
    sb = jnp.concatenate([z, s, z, z], axis=1)
    assert cc.shape[1] == 128 and half == 32
    return cc, sa, sb


def _in_pad(dims):
    D, QL, KVL, CW, PW, H, F = dims
    return (-(QL + 2 * CW + PW + KVL + 128)) % 512


def _layer_weights(full, l, dims):
    D, QL, KVL, CW, PW, H, F = dims
    w_in = full['w_in'][:, l].transpose(1, 0, 2).reshape(D, -1)
    o1, o2, o3, o4 = QL, QL + KVL, QL + KVL + QK_ROPE, QL + KVL + QK_ROPE + 2 * CW
    w_in_p = jnp.concatenate([w_in[:, :o1], w_in[:, o3:o4], w_in[:, o4:], w_in[:, o1:o2], w_in[:, o2:o3],
                              jnp.zeros((D, 128 - QK_ROPE + _in_pad(dims)), w_in.dtype)], axis=1)
    w_uq = full['w_uq'][:, l].reshape(QL, H, QK_NOPE + QK_ROPE)
    w_uq_p = jnp.pad(w_uq, ((0, 0), (0, 0), (0, HEAD_PAD - QK_NOPE - QK_ROPE))).reshape(QL, H * HEAD_PAD)
    return dict(
        w_in=w_in_p, w_uq=w_uq_p,
        w_ukv=full['w_ukv'][:, l].reshape(KVL, H * (QK_NOPE + V_HEAD)),
        w_out=full['w_out'][:, l].reshape(D, D),
        w_up=full['w_up'][:, l].transpose(1, 0, 2).reshape(D, 2 * F),
        w_down=full['w_down'][:, l].reshape(F, D),
    )


def _unpermute_w_in_grad(g, dims):
    D, QL, KVL, CW, PW, H, F = dims
    a, b_, c_ = QL, QL + 2 * CW, QL + 2 * CW + PW
    return jnp.concatenate([g[:, :a], g[:, c_:c_ + KVL], g[:, c_ + KVL:c_ + KVL + QK_ROPE], g[:, a:b_], g[:, b_:c_]], axis=1)


def _local_step(x, positions, target, full, small, dims, B, S, L):
    D, QL, KVL, CW, PW, H, F = dims
    T = B * S
    alpha = (2.0 * L) ** 0.25
    scale = float(QK_NOPE + QK_ROPE) ** -0.5
    cc, sa, sb = _rope_tables(positions)
    cb_q, cb_a, cb_g, cb_p = 0, QL // CW, QL // CW + 1, (QL + 2 * CW) // PW
    cb_kv, cb_kr = (QL + 2 * CW + PW) // KVL, (QL + 2 * CW + PW + KVL) // 128
    assert QL % CW == 0 and (QL + 2 * CW) % PW == 0 and (QL + 2 * CW + PW) % KVL == 0 and (QL + 2 * CW + PW + KVL) % 128 == 0

    xs, xb = ln_fwd([x], [1.0], small['ln_in_g'], small['ln_in_b'], want_r=False, name="ln_in")
    saved = []
    Ws = [_layer_weights(full, l, dims) for l in range(L)]
    for l in range(L):
        W = Ws[l]
        h = matmul(xb, W['w_in'], name="mm_in")
        qn = rms_fwd(h, cb_q, QL, small['q_norm_g'][l], name="rms_q")
        kvn = rms_fwd(h, cb_kv, KVL, small['kv_norm_g'][l], name="rms_kv")
        q = matmul(qn, W['w_uq'], name="mm_uq")
        kv = matmul(kvn, W['w_ukv'], name="mm_ukv")
        qp, kp, v = mla_pack(q, kv, h, cb_kr, cc, sa, sb, H=H, scale=scale, name="mla_pack")
        o, lse = flash_fwd(qp, kp, v, B=B, S=S, H=H, scale=scale, name="flash_fwd")
        z, yc = conv_fwd(h, cb_a, cb_g, small['conv_w'][l], small['conv_b'][l], small['conv_ln_g'][l],
                         small['conv_ln_b'][l], B=B, S=S, name="conv_fwd")
        yp = pool_fwd(h, cb_p, small['w_pool'][l], small['pool_scale'][l], B=B, S=S, name="pool_fwd")
        mixed = jnp.concatenate([o.astype(_BF), yc, yp], axis=1)
        y1 = matmul(mixed, W['w_out'], name="mm_out")
        r1, x1, x1b = ln_fwd([xs, y1], [alpha, 1.0], small['ln1_g'][l], small['ln1_b'][l], want_r=True, name="ln1")
        up = matmul(x1b, W['w_up'], out_dtype=_BF, name="mm_up")
        act = gate_fwd(up, small['ffn_conv_w'][l], small['ffn_conv_b'][l], B=B, S=S, name="gate_fwd")
        y2 = matmul(act, W['w_down'], name="mm_down")
        r2, x2, x2b = ln_fwd([x1, y2], [alpha, 1.0], small['ln2_g'][l], small['ln2_b'][l], want_r=True, name="ln2")
        saved.append(dict(xb=xb, h=h, qn=qn, kvn=kvn, qp=qp, kp=kp, v=v, o=o, lse=lse, z=z, mixed=mixed, r1=r1,
                          x1b=x1b, up=up, act=act, r2=r2))
        xs, xb = x2, x2b

    dy, loss_cols = loss_head(xs, target, name="loss_head")
    gb = {n: [None] * L for n in _BIG}
    gs = {n: [None] * L for n in _SMALL if n not in ('ln_in_g', 'ln_in_b')}
    d_terms, d_coefs = [dy], [1.0]
    zpad = jnp.zeros((T, _in_pad(dims)), _BF) if _in_pad(dims) else None
    for l in reversed(range(L)):
        sv = saved[l]
        W = Ws[l]
        dr2, dr2b, gs['ln2_g'][l], gs['ln2_b'][l] = ln_bwd(d_terms, d_coefs, sv['r2'], small['ln2_g'][l], name="ln2_bwd")
        dact = matmul(dr2b, W['w_down'], tb=True, out_dtype=_BF, name="mm_down_dx")
        gb['w_down'][l] = matmul(sv['act'], dr2b, ta=True, out_dtype=_BF, name="mm_down_dw")
        dua, dug, dwa, dwg, dba, dbg = gate_bwd(sv['up'], dact, small['ffn_conv_w'][l], small['ffn_conv_b'][l],
                                                B=B, S=S, name="gate_bwd")
        gs['ffn_conv_w'][l] = jnp.concatenate([dwa, dwg], axis=1)
        gs['ffn_conv_b'][l] = jnp.concatenate([dba, dbg], axis=1)
        dup = jnp.concatenate([dua, dug], axis=1)
        gb['w_up'][l] = matmul(sv['x1b'], dup, ta=True, out_dtype=_BF, name="mm_up_dw")
        dx1 = matmul(dup, W['w_up'], tb=True, name="mm_up_dx")
        dr1, dr1b, gs['ln1_g'][l], gs['ln1_b'][l] = ln_bwd([dr2, dx1], [alpha, 1.0], sv['r1'], small['ln1_g'][l],
                                                            name="ln1_bwd")
        dmix = matmul(dr1b, W['w_out'], tb=True, name="mm_out_dx")
        gb['w_out'][l] = matmul(sv['mixed'], dr1b, ta=True, out_dtype=_BF, name="mm_out_dw")
        h = sv['h']
        ncb = (H * V_HEAD) // CW
        dca, dcg, gs['conv_w'][l], gs['conv_b'][l], gs['conv_ln_g'][l], gs['conv_ln_b'][l] = conv_bwd(
            dmix, ncb, sv['z'], h, cb_a, cb_g, small['conv_w'][l], small['conv_ln_g'][l], small['conv_ln_b'][l],
            B=B, S=S, name="conv_bwd")
        dpool, gs['w_pool'][l], gs['pool_scale'][l] = pool_bwd(
            dmix, (H * V_HEAD + CW) // PW, h, cb_p, small['w_pool'][l], small['pool_scale'][l], B=B, S=S, name="pool_bwd")
        fa = dict(B=B, S=S, H=H, scale=scale)
        dqp = flash_bwd_dq(sv['qp'], sv['kp'], sv['v'], sv['o'], sv['lse'], dmix, 0, name="flash_dq", **fa)
        dkp, dv = flash_bwd_dkv(sv['qp'], sv['kp'], sv['v'], sv['o'], sv['lse'], dmix, 0, name="flash_dkv", **fa)
        dq, dkv, dkr = mla_unpack(dqp, dkp, dv, cc, sa, sb, H=H, name="mla_unpack")
        dqn = matmul(dq, W['w_uq'], tb=True, name="mm_uq_dx")
        g_uq = matmul(sv['qn'], dq, ta=True, out_dtype=_BF, name="mm_uq_dw")
        gb['w_uq'][l] = g_uq.reshape(QL, H, HEAD_PAD)[:, :, :QK_NOPE + QK_ROPE].reshape(QL, -1)
        dkvn = matmul(dkv, W['w_ukv'], tb=True, name="mm_ukv_dx")
        gb['w_ukv'][l] = matmul(sv['kvn'], dkv, ta=True, out_dtype=_BF, name="mm_ukv_dw")
        dcq, gs['q_norm_g'][l] = rms_bwd(dqn, h, cb_q, QL, small['q_norm_g'][l], name="rms_q_bwd")
        dckv, gs['kv_norm_g'][l] = rms_bwd(dkvn, h, cb_kv, KVL, small['kv_norm_g'][l], name="rms_kv_bwd")
        dh = jnp.concatenate([dcq, dca, dcg, dpool, dckv, dkr] + ([zpad] if zpad is not None else []), axis=1)
        gb['w_in'][l] = _unpermute_w_in_grad(matmul(sv['xb'], dh, ta=True, out_dtype=_BF, name="mm_in_dw"), dims)
        dxm = matmul(dh, W['w_in'], tb=True, name="mm_in_dx")
        d_terms, d_coefs = [dr1, dxm], [alpha, 1.0]
    gx, _, g_ln_g, g_ln_b = ln_bwd(d_terms, d_coefs, x, small['ln_in_g'], name="ln_in_bwd")
    gsm = {n: jnp.stack([a.reshape(small[n].shape[1:]) for a in gs[n]]) for n in gs}
    gsm['ln_in_g'], gsm['ln_in_b'] = g_ln_g.reshape(-1), g_ln_b.reshape(-1)
    return loss_cols, gx, gb, gsm


_COL_SHARDED = ('w_in', 'w_up')


def _flat_pad(arrs, mult=512 * 128):
    v = jnp.concatenate([a.reshape(-1) for a in arrs])
    n = v.shape[0]
    return jnp.pad(v, (0, (-n) % mult)).reshape(-1, 128)


def _split_like(flat, like):
    out, off = [], 0
    v = flat.reshape(-1)
    for a in like:
        out.append(v[off:off + a.size].reshape(a.shape))
        off += a.size
    return out


def kernel(x, positions, ln_in_g, ln_in_b, w_in, q_norm_g, w_uq, kv_norm_g, w_ukv, conv_w, conv_b, conv_ln_g, conv_ln_b, w_pool, pool_scale, w_out, ln1_g, ln1_b, w_up, ffn_conv_w, ffn_conv_b, w_down, ln2_g, ln2_b, loss_target, m_ln_in_g, m_ln_in_b, m_w_in, m_q_norm_g, m_w_uq, m_kv_norm_g, m_w_ukv, m_conv_w, m_conv_b, m_conv_ln_g, m_conv_ln_b, m_w_pool, m_pool_scale, m_w_out, m_ln1_g, m_ln1_b, m_w_up, m_ffn_conv_w, m_ffn_conv_b, m_w_down, m_ln2_g, m_ln2_b, v_ln_in_g, v_ln_in_b, v_w_in, v_q_norm_g, v_w_uq, v_kv_norm_g, v_w_ukv, v_conv_w, v_conv_b, v_conv_ln_g, v_conv_ln_b, v_w_pool, v_pool_scale, v_w_out, v_ln1_g, v_ln1_b, v_w_up, v_ffn_conv_w, v_ffn_conv_b, v_w_down, v_ln2_g, v_ln2_b):
    a = dict(locals())
    B, S, D = a['x'].shape
    T = B * S
    L = a['w_in'].shape[0]
    QL, H = 4 * a['w_uq'].shape[1], a['w_uq'].shape[2]
    KVL = 4 * a['w_ukv'].shape[1]
    CW, PW = a['conv_b'].shape[1], a['pool_scale'].shape[1]
    F = 4 * a['w_down'].shape[1]
    dims = (D, QL, KVL, CW, PW, H, F)
    chip = 2 * lax.axis_index("x") + lax.axis_index("y")
    c_idx = lax.axis_index("c").astype(jnp.int32).reshape(1)

    def shard2d(w):
        return w.reshape(-1, w.shape[-1]) if w.ndim == 3 else w.reshape(w.shape[0] * w.shape[1], -1)

    big_local = [shard2d(a[n]) for n in _BIG]
    chip_idx = chip.astype(jnp.int32).reshape(1)
    big_in = [place_shard(w.reshape(2, w.shape[0] // 2, w.shape[1]), chip_idx, name="place_" + n)
              for n, w in zip(_BIG, big_local)]
    small_in = [shard2d(a[n]) for n in _SMALL_SHARDED]
    outs = gather_weights(big_in, small_in, name="gather_weights")
    full = {n: o.reshape(4, L, w.shape[0] // L, w.shape[1]) for n, o, w in zip(_BIG, outs[:len(_BIG)], big_local)}
    small = {n: a[n] for n in _SMALL}
    for n, o in zip(_SMALL_SHARDED, outs[len(_BIG):]):
        k = a[n].shape[1]
        small[n] = o.reshape(4, L, k, -1).transpose(1, 2, 0, 3).reshape(L, k, -1)

    loss_cols, gx, gb, gsm = _local_step(a['x'].reshape(T, D), a['positions'], a['loss_target'].reshape(T, D),
                                         full, small, dims, B, S, L)
    loss = lax.psum(jnp.sum(loss_cols), ("x", "y", "c"))

    g_in = []
    for n, w in zip(_BIG, big_local):
        per_layer = []
        for g in gb[n]:
            if n in _COL_SHARDED:
                per_layer.append(g.reshape(g.shape[0], 4, -1).transpose(1, 0, 2))
            else:
                per_layer.append(g.reshape(4, g.shape[0] // 4, -1))
        st = jnp.stack(per_layer, axis=1)
        hR = w.shape[0] // 2
        g_in.append(st.reshape(4, 2, hR, w.shape[1]).transpose(1, 0, 2, 3).reshape(2, 4 * hR, w.shape[1]))
    from_sib = sibling_send_half(g_in, name="grad_sibling_send")
    chip_sums = add_halves(list(zip(g_in, from_sib)), c_idx, name="grad_presum")
    chip_sums = [p.reshape(4, p.shape[0] // 4, p.shape[1]) for p in chip_sums]
    slots = chip_exchange(chip_sums, name="grad_chip_exchange")
    xi, yi = lax.axis_index("x"), lax.axis_index("y")
    sum_idx = [v.astype(jnp.int32).reshape(1) for v in [chip] + _other_chips(xi, yi)[1] + [lax.axis_index("c")]]
    halves = [sum_chips(p, s, sum_idx, name=f"grad_sum_{i}") for i, (p, s) in enumerate(zip(chip_sums, slots))]
    joined = sibling_join(halves, name="grad_sibling_join")
    g_big = {n: j.reshape(a[n].shape) for n, j in zip(_BIG, joined)}

    sm_like = [gsm[n] for n in _SMALL]
    sm_sum = sum_slots(all_devices_exchange(_flat_pad(sm_like), name="small_exchange"), name="small_sum")
    g_small = dict(zip(_SMALL, _split_like(sm_sum, sm_like)))
    for n in _SMALL_SHARDED:
        w = a[n].shape[-1]
        g_small[n] = lax.dynamic_slice_in_dim(g_small[n], chip * w, w, axis=2)

    grads, delta, new_m, new_v = {}, {}, {}, {}
    for n in _BIG:
        grads[n] = g_big[n]
        d_, m_, v_ = adamw(shard2d(a[n]), shard2d(g_big[n]), shard2d(a['m_' + n]), shard2d(a['v_' + n]), name="adamw_" + n)
        delta[n], new_m[n], new_v[n] = (t.reshape(a[n].shape) for t in (d_, m_, v_))
    like = [a[n] for n in _SMALL]
    d_, m_, v_ = adamw(_flat_pad(like), _flat_pad([g_small[n] for n in _SMALL]), _flat_pad([a['m_' + n] for n in _SMALL]),
                       _flat_pad([a['v_' + n] for n in _SMALL]), name="adamw_small")
    for n, dd, mm, vv in zip(_SMALL, _split_like(d_, like), _split_like(m_, like), _split_like(v_, like)):
        grads[n], delta[n], new_m[n], new_v[n] = g_small[n], dd, mm, vv

    return (loss, gx.reshape(B, S, D), *[grads[n] for n in _WEIGHTS], *[delta[n] for n in _WEIGHTS],
            *[new_m[n] for n in _WEIGHTS], *[new_v[n] for n in _WEIGHTS])
```

```python
import functools
import math

import jax
import jax.numpy as jnp
from jax import lax
from jax.experimental import pallas as pl
from jax.experimental.pallas import tpu as pltpu

_BF = jnp.bfloat16
_F32 = jnp.float32
_VMEM_LIMIT = 56 * 1024 * 1024

QK_NOPE = 128
QK_ROPE = 64
V_HEAD = 128
HEAD_PAD = 256
ROPE_THETA = 10000.0
LN_EPS = 1e-5
RMS_EPS = 1e-6
POOL_WINDOWS = (2, 4, 8, 16)
ADAM_LR, ADAM_B1, ADAM_B2, ADAM_EPS, ADAM_WD, ADAM_STEP = 0.001, 0.9, 0.999, 1e-8, 0.01, 10


def _cparams(sem=None):
    kw = dict(vmem_limit_bytes=_VMEM_LIMIT)
    if sem is not None:
        kw["dimension_semantics"] = sem
    return pltpu.CompilerParams(**kw)


def _tile(n, target, unit=128):
    if n <= target:
        return n
    t = (target // unit) * unit
    while t >= unit:
        if n % t == 0:
            return t
        t -= unit
    return n


_MM_VMEM_BUDGET = 40 * 1024 * 1024


def matmul(a, b, *, ta=False, tb=False, out_dtype=_F32, tm=1024, tn=1536, tk=2048, side=None, name="mm"):
    if ta:
        K, M = a.shape
    else:
        M, K = a.shape
    if tb:
        N, K2 = b.shape
    else:
        K2, N = b.shape
    assert K == K2, (a.shape, b.shape, ta, tb)
    tm, tn, tk = _tile(M, tm), _tile(N, tn), _tile(K, tk)
    ab, bb, ob = a.dtype.itemsize, b.dtype.itemsize, jnp.dtype(out_dtype).itemsize

    def vmem(tk_):
        return 2 * (tm * tk_ * ab + tk_ * tn * bb) + 2 * tm * tn * ob + tm * tn * 4 * (2 if K // tk_ > 1 else 1)

    while vmem(tk) > _MM_VMEM_BUDGET and tk > 256 and _tile(K, tk // 2) < tk:
        tk = _tile(K, tk // 2)
    nk = K // tk
    dn = (((0,) if ta else (1,), (1,) if tb else (0,)), ((), ()))

    s_ops, s_in, s_out, s_shapes, s_scratch, s_alias = _side_specs(side)
    ni, no, nacc = len(s_ops), len(s_shapes), int(nk > 1)
    grid = (M // tm, N // tn, nk)

    def body(a_ref, b_ref, *rest):
        s_ins, o_ref, s_outs = rest[:ni], rest[ni], rest[ni + 1:ni + 1 + no]
        acc, sems = rest[ni + 1 + no:ni + 1 + no + nacc], rest[ni + 1 + no + nacc:]
        i, j, k = pl.program_id(0), pl.program_id(1), pl.program_id(2)
        if side is not None:
            @pl.when((i == 0) & (j == 0) & (k == 0))
            def _():
                side.start(s_ins, s_outs, sems)

        prod = lax.dot_general(a_ref[...].astype(_BF), b_ref[...].astype(_BF), dn, preferred_element_type=_F32)
        if nk == 1:
            o_ref[...] = prod.astype(o_ref.dtype)
        else:
            acc_ref = acc[0]

            @pl.when(k == 0)
            def _():
                acc_ref[...] = prod

            @pl.when(k > 0)
            def _():
                acc_ref[...] += prod

            @pl.when(k == nk - 1)
            def _():
                o_ref[...] = acc_ref[...].astype(o_ref.dtype)

        if side is not None:
            @pl.when((i == grid[0] - 1) & (j == grid[1] - 1) & (k == nk - 1))
            def _():
                side.wait(s_ins, s_outs, sems)

    a_spec = pl.BlockSpec((tk, tm), lambda i, j, k: (k, i)) if ta else pl.BlockSpec((tm, tk), lambda i, j, k: (i, k))
    b_spec = pl.BlockSpec((tn, tk), lambda i, j, k: (j, k)) if tb else pl.BlockSpec((tk, tn), lambda i, j, k: (k, j))
    res = pl.pallas_call(
        body, name=name,
        grid=grid,
        in_specs=[a_spec, b_spec] + s_in,
        out_specs=[pl.BlockSpec((tm, tn), lambda i, j, k: (i, j))] + s_out,
        out_shape=[jax.ShapeDtypeStruct((M, N), out_dtype)] + s_shapes,
        input_output_aliases={2 + i_: 1 + o_ for i_, o_ in s_alias.items()},
        scratch_shapes=([pltpu.VMEM((tm, tn), _F32)] if nk > 1 else []) + s_scratch,
        compiler_params=_cparams(("arbitrary",) * 3 if side is not None else ("parallel", "parallel", "arbitrary")),
    )(a, b, *s_ops)
    return res[0] if side is None else (res[0], list(res[1:]))


def _row_tile(T, C, budget_rows=256):
    return _tile(T, budget_rows, 16)


def ln_fwd(xs, coefs, g, b, *, want_r, name):
    T, C = xs[0].shape
    tr = _row_tile(T, C)
    n = len(xs)

    def body(*refs):
        x_refs, (g_ref, b_ref), outs = refs[:n], refs[n:n + 2], refs[n + 2:]
        r = coefs[0] * x_refs[0][...]
        for c, xr in zip(coefs[1:], x_refs[1:]):
            r = r + c * xr[...]
        mu = jnp.mean(r, axis=-1, keepdims=True)
        d = r - mu
        var = jnp.mean(d * d, axis=-1, keepdims=True)
        y = d * lax.rsqrt(var + LN_EPS) * g_ref[...] + b_ref[...]
        if want_r:
            outs[0][...] = r
        outs[-2][...] = y
        outs[-1][...] = y.astype(_BF)

    row = pl.BlockSpec((tr, C), lambda i: (i, 0))
    vec = pl.BlockSpec((1, C), lambda i: (0, 0))
    f = jax.ShapeDtypeStruct((T, C), _F32)
    out_shape = ([f] if want_r else []) + [f, jax.ShapeDtypeStruct((T, C), _BF)]
    return pl.pallas_call(
        body, name=name, grid=(T // tr,),
        in_specs=[row] * n + [vec, vec],
        out_specs=[row] * len(out_shape), out_shape=out_shape,
        compiler_params=_cparams(("parallel",)),
    )(*xs, g.reshape(1, C), b.reshape(1, C))


def ln_bwd(dys, coefs, r, g, *, name):
    T, C = r.shape
    tr = _row_tile(T, C)
    n = len(dys)

    def body(*refs):
        dy_refs, r_ref, g_ref = refs[:n], refs[n], refs[n + 1]
        dr_ref, drb_ref, dg_ref, db_ref = refs[n + 2:]
        dy = coefs[0] * dy_refs[0][...]
        for c, dr_ in zip(coefs[1:], dy_refs[1:]):
            dy = dy + c * dr_[...]
        rr = r_ref[...]
        mu = jnp.mean(rr, axis=-1, keepdims=True)
        d = rr - mu
        var = jnp.mean(d * d, axis=-1, keepdims=True)
        rstd = lax.rsqrt(var + LN_EPS)
        xh = d * rstd
        gdy = dy * g_ref[...]
        m1 = jnp.mean(gdy, axis=-1, keepdims=True)
        m2 = jnp.mean(gdy * xh, axis=-1, keepdims=True)
        dr = rstd * (gdy - m1 - xh * m2)
        dr_ref[...] = dr
        drb_ref[...] = dr.astype(_BF)

        @pl.when(pl.program_id(0) == 0)
        def _():
            dg_ref[...] = jnp.zeros_like(dg_ref)
            db_ref[...] = jnp.zeros_like(db_ref)

        dg_ref[...] += jnp.sum(dy * xh, axis=0, keepdims=True)
        db_ref[...] += jnp.sum(dy, axis=0, keepdims=True)

    row = pl.BlockSpec((tr, C), lambda i: (i, 0))
    vec = pl.BlockSpec((1, C), lambda i: (0, 0))
    return pl.pallas_call(
        body, name=name, grid=(T // tr,),
        in_specs=[row] * (n + 1) + [vec],
        out_specs=[row, row, vec, vec],
        out_shape=[jax.ShapeDtypeStruct((T, C), _F32), jax.ShapeDtypeStruct((T, C), _BF),
                   jax.ShapeDtypeStruct((1, C), _F32), jax.ShapeDtypeStruct((1, C), _F32)],
        compiler_params=_cparams(("arbitrary",)),
    )(*dys, r, g.reshape(1, C))


def rms_fwd(h, cb, W, g, *, name):
    T = h.shape[0]
    tr = _tile(T, 512, 16)

    def body(c_ref, g_ref, o_ref):
        c = c_ref[...]
        ms = jnp.mean(c * c, axis=-1, keepdims=True)
        o_ref[...] = (c * lax.rsqrt(ms + RMS_EPS) * g_ref[...]).astype(_BF)

    return pl.pallas_call(
        body, name=name, grid=(T // tr,),
        in_specs=[pl.BlockSpec((tr, W), lambda i: (i, cb)), pl.BlockSpec((1, W), lambda i: (0, 0))],
        out_specs=pl.BlockSpec((tr, W), lambda i: (i, 0)),
        out_shape=jax.ShapeDtypeStruct((T, W), _BF),
        compiler_params=_cparams(("parallel",)),
    )(h, g.reshape(1, W))


def rms_bwd(dy, h, cb, W, g, *, name):
    T = h.shape[0]
    tr = _tile(T, 512, 16)

    def body(dy_ref, c_ref, g_ref, dc_ref, dg_ref):
        c = c_ref[...]
        dyv = dy_ref[...]
        ms = jnp.mean(c * c, axis=-1, keepdims=True)
        r = lax.rsqrt(ms + RMS_EPS)
        u = dyv * g_ref[...]
        m = jnp.mean(c * u, axis=-1, keepdims=True)
        dc_ref[...] = (r * u - c * (r * r * r) * m).astype(_BF)

        @pl.when(pl.program_id(0) == 0)
        def _():
            dg_ref[...] = jnp.zeros_like(dg_ref)

        dg_ref[...] += jnp.sum(dyv * c * r, axis=0, keepdims=True)

    return pl.pallas_call(
        body, name=name, grid=(T // tr,),
        in_specs=[pl.BlockSpec((tr, W), lambda i: (i, 0)), pl.BlockSpec((tr, W), lambda i: (i, cb)),
                  pl.BlockSpec((1, W), lambda i: (0, 0))],
        out_specs=[pl.BlockSpec((tr, W), lambda i: (i, 0)), pl.BlockSpec((1, W), lambda i: (0, 0))],
        out_shape=[jax.ShapeDtypeStruct((T, W), _BF), jax.ShapeDtypeStruct((1, W), _F32)],
        compiler_params=_cparams(("arbitrary",)),
    )(dy, h, g.reshape(1, W))


def _rope(u, cc, sa, sb, sign):
    return u * cc + sign * (pltpu.roll(u, 96, 1) * sa + pltpu.roll(u, 32, 1) * sb)


def mla_pack(q, kv, h, kr_cb, cc, sa, sb, *, H, scale, name):
    T = q.shape[0]
    tr = _tile(T, 256, 16)

    def body(q_ref, kv_ref, kr_ref, cc_ref, sa_ref, sb_ref, qp_ref, kp_ref, v_ref):
        cc_, sa_, sb_ = cc_ref[...], sa_ref[...], sb_ref[...]
        kr = _rope(kr_ref[...], cc_, sa_, sb_, 1.0).astype(_BF)
        for hh in range(H):
            o = hh * HEAD_PAD
            qp_ref[:, o:o + 128] = (q_ref[:, o:o + 128] * scale).astype(_BF)
            qp_ref[:, o + 128:o + 256] = (_rope(q_ref[:, o + 128:o + 256], cc_, sa_, sb_, 1.0) * scale).astype(_BF)
            kp_ref[:, o:o + 128] = kv_ref[:, o:o + 128].astype(_BF)
            kp_ref[:, o + 128:o + 256] = kr
            v_ref[:, hh * 128:(hh + 1) * 128] = kv_ref[:, o + 128:o + 256].astype(_BF)

    wide = pl.BlockSpec((tr, H * HEAD_PAD), lambda i: (i, 0))
    tab = pl.BlockSpec((tr, 128), lambda i: (i, 0))
    return pl.pallas_call(
        body, name=name, grid=(T // tr,),
        in_specs=[wide, wide, pl.BlockSpec((tr, 128), lambda i: (i, kr_cb)), tab, tab, tab],
        out_specs=[wide, wide, pl.BlockSpec((tr, H * 128), lambda i: (i, 0))],
        out_shape=[jax.ShapeDtypeStruct((T, H * HEAD_PAD), _BF), jax.ShapeDtypeStruct((T, H * HEAD_PAD), _BF),
                   jax.ShapeDtypeStruct((T, H * 128), _BF)],
        compiler_params=_cparams(("parallel",)),
    )(q, kv, h, cc, sa, sb)


def mla_unpack(dqp, dkp, dv, cc, sa, sb, *, H, name):
    T = dqp.shape[0]
    tr = _tile(T, 256, 16)

    def body(dq_ref, dk_ref, dv_ref, cc_ref, sa_ref, sb_ref, oq_ref, okv_ref, okr_ref):
        cc_, sa_, sb_ = cc_ref[...], sa_ref[...], sb_ref[...]
        kr = jnp.zeros((tr, 128), _F32)
        for hh in range(H):
            o = hh * HEAD_PAD
            oq_ref[:, o:o + 128] = dq_ref[:, o:o + 128].astype(_BF)
            oq_ref[:, o + 128:o + 256] = _rope(dq_ref[:, o + 128:o + 256], cc_, sa_, sb_, -1.0).astype(_BF)
            okv_ref[:, o:o + 128] = dk_ref[:, o:o + 128].astype(_BF)
            okv_ref[:, o + 128:o + 256] = dv_ref[:, hh * 128:(hh + 1) * 128].astype(_BF)
            kr = kr + dk_ref[:, o + 128:o + 256]
        okr_ref[...] = _rope(kr, cc_, sa_, sb_, -1.0).astype(_BF)

    wide = pl.BlockSpec((tr, H * HEAD_PAD), lambda i: (i, 0))
    tab = pl.BlockSpec((tr, 128), lambda i: (i, 0))
    return pl.pallas_call(
        body, name=name, grid=(T // tr,),
        in_specs=[wide, wide, pl.BlockSpec((tr, H * 128), lambda i: (i, 0)), tab, tab, tab],
        out_specs=[wide, wide, tab],
        out_shape=[jax.ShapeDtypeStruct((T, H * HEAD_PAD), _BF), jax.ShapeDtypeStruct((T, H * HEAD_PAD), _BF),
                   jax.ShapeDtypeStruct((T, 128), _BF)],
        compiler_params=_cparams(("parallel",)),
    )(dqp, dkp, dv, cc, sa, sb)


_NEG = -1e30


def _scores(q, k, i, j, tq, tk, masked):
    s = lax.dot_general(q, k, (((1,), (1,)), ((), ())), preferred_element_type=_F32)
    if not masked:
        return s
    row = i * tq + lax.broadcasted_iota(jnp.int32, (tq, tk), 0)
    col = j * tk + lax.broadcasted_iota(jnp.int32, (tq, tk), 1)
    return jnp.where(col <= row, s, _NEG)


def _rows(ref, j, t):
    return ref[pl.ds(pl.multiple_of(j * t, t), t), :]


def flash_fwd(qp, kp, v, *, B, S, H, scale, side=None, name):
    T = B * S
    t = _tile(S, 512, 128)
    nq = S // t
    s_ops, s_in, s_out, s_shapes, s_scratch, s_alias = _side_specs(side)
    ni, no = len(s_ops), len(s_shapes)

    def body(q_ref, k_ref, v_ref, *rest):
        s_ins, (o_ref, lse_ref), s_outs = rest[:ni], rest[ni:ni + 2], rest[ni + 2:ni + 2 + no]
        (m_sc, l_sc, acc_sc), sems = rest[ni + 2 + no:ni + 5 + no], rest[ni + 5 + no:]
        i = pl.program_id(2)
        first = (pl.program_id(0) == 0) & (pl.program_id(1) == 0) & (i == 0)
        last = (pl.program_id(0) == B - 1) & (pl.program_id(1) == H - 1) & (i == nq - 1)
        if side is not None:
            @pl.when(first)
            def _():
                side.start(s_ins, s_outs, sems)

        m_sc[...] = jnp.full_like(m_sc, _NEG)
        l_sc[...] = jnp.zeros_like(l_sc)
        acc_sc[...] = jnp.zeros_like(acc_sc)

        def step(j, masked):
            s = _scores(q_ref[...], _rows(k_ref, j, t), i, j, t, t, masked)
            m_old = m_sc[...]
            m_new = jnp.maximum(m_old, jnp.max(s, axis=-1, keepdims=True))
            p = jnp.exp(s - m_new)
            a = jnp.exp(m_old - m_new)
            l_sc[...] = a * l_sc[...] + jnp.sum(p, axis=-1, keepdims=True)
            acc_sc[...] = a * acc_sc[...] + jnp.dot(p.astype(_BF), _rows(v_ref, j, t), preferred_element_type=_F32)
            m_sc[...] = m_new

        @pl.loop(0, i)
        def _(j):
            step(j, False)

        step(i, True)
        l = l_sc[...]
        o_ref[...] = acc_sc[...] / l
        lse_ref[...] = jnp.broadcast_to(m_sc[...] + jnp.log(l), lse_ref.shape)
        if side is not None:
            @pl.when(last)
            def _():
                side.wait(s_ins, s_outs, sems)

    qmap = lambda b, h, i: (b * nq + i, h)
    smap = lambda b, h, i: (b, h)
    res = pl.pallas_call(
        body, name=name, grid=(B, H, nq),
        in_specs=[pl.BlockSpec((t, HEAD_PAD), qmap), pl.BlockSpec((S, HEAD_PAD), smap), pl.BlockSpec((S, 128), smap)] + s_in,
        out_specs=[pl.BlockSpec((t, 128), qmap), pl.BlockSpec((t, 128), qmap)] + s_out,
        out_shape=[jax.ShapeDtypeStruct((T, H * 128), _F32), jax.ShapeDtypeStruct((T, H * 128), _F32)] + s_shapes,
        input_output_aliases={3 + i_: 2 + o_ for i_, o_ in s_alias.items()},
        scratch_shapes=[pltpu.VMEM((t, 1), _F32), pltpu.VMEM((t, 1), _F32), pltpu.VMEM((t, 128), _F32)] + s_scratch,
        compiler_params=_cparams(("arbitrary",) * 3 if side is not None else ("parallel",) * 3),
    )(qp, kp, v, *s_ops)
    return (res[0], res[1]) if side is None else (res[0], res[1], list(res[2:]))


def flash_bwd_dq(qp, kp, v, o, lse, do, do_cb0, *, B, S, H, scale, name):
    T = B * S
    t = _tile(S, 512, 128)
    nq = S // t

    def body(q_ref, k_ref, v_ref, o_ref, lse_ref, do_ref, dq_ref, acc_sc, dl_sc):
        i = pl.program_id(2)
        acc_sc[...] = jnp.zeros_like(acc_sc)
        dl_sc[...] = jnp.sum(do_ref[...].astype(_F32) * o_ref[...], axis=-1, keepdims=True)

        def step(j, masked):
            k = _rows(k_ref, j, t)
            s = _scores(q_ref[...], k, i, j, t, t, masked)
            p = jnp.exp(s - lse_ref[:, 0:1])
            dp = lax.dot_general(do_ref[...].astype(_BF), _rows(v_ref, j, t), (((1,), (1,)), ((), ())),
                                 preferred_element_type=_F32)
            ds = p * (dp - dl_sc[...])
            acc_sc[...] += jnp.dot(ds.astype(_BF), k, preferred_element_type=_F32)

        @pl.loop(0, i)
        def _(j):
            step(j, False)

        step(i, True)
        dq_ref[...] = acc_sc[...] * scale

    qmap = lambda b, h, i: (b * nq + i, h)
    domap = lambda b, h, i: (b * nq + i, do_cb0 + h)
    smap = lambda b, h, i: (b, h)
    return pl.pallas_call(
        body, name=name, grid=(B, H, nq),
        in_specs=[pl.BlockSpec((t, HEAD_PAD), qmap), pl.BlockSpec((S, HEAD_PAD), smap), pl.BlockSpec((S, 128), smap),
                  pl.BlockSpec((t, 128), qmap), pl.BlockSpec((t, 128), qmap), pl.BlockSpec((t, 128), domap)],
        out_specs=pl.BlockSpec((t, HEAD_PAD), qmap),
        out_shape=jax.ShapeDtypeStruct((T, H * HEAD_PAD), _F32),
        scratch_shapes=[pltpu.VMEM((t, HEAD_PAD), _F32), pltpu.VMEM((t, 1), _F32)],
        compiler_params=_cparams(("parallel", "parallel", "parallel")),
    )(qp, kp, v, o, lse, do)


def flash_bwd_dkv(qp, kp, v, o, lse, do, do_cb0, *, B, S, H, scale, name):
    T = B * S
    t = _tile(S, 512, 128)
    nk = S // t

    def body(q_ref, k_ref, v_ref, o_ref, lse_ref, do_ref, dk_ref, dv_ref, dk_sc, dv_sc):
        j = pl.program_id(2)
        dk_sc[...] = jnp.zeros_like(dk_sc)
        dv_sc[...] = jnp.zeros_like(dv_sc)

        def step(i, masked):
            q = _rows(q_ref, i, t)
            do = _rows(do_ref, i, t).astype(_F32)
            dob = do.astype(_BF)
            s = _scores(q, k_ref[...], i, j, t, t, masked)
            p = jnp.exp(s - _rows(lse_ref, i, t)[:, 0:1])
            dl = jnp.sum(do * _rows(o_ref, i, t), axis=-1, keepdims=True)
            dp = lax.dot_general(dob, v_ref[...], (((1,), (1,)), ((), ())), preferred_element_type=_F32)
            ds = p * (dp - dl)
            tn = (((0,), (0,)), ((), ()))
            dv_sc[...] += lax.dot_general(p.astype(_BF), dob, tn, preferred_element_type=_F32)
            dk_sc[...] += lax.dot_general(ds.astype(_BF), q, tn, preferred_element_type=_F32)

        step(j, True)

        @pl.loop(j + 1, nk)
        def _(i):
            step(i, False)

        dk_ref[...] = dk_sc[...]
        dv_ref[...] = dv_sc[...]

    smap = lambda b, h, j: (b, h)
    domap = lambda b, h, j: (b, do_cb0 + h)
    kmap = lambda b, h, j: (b * nk + j, h)
    return pl.pallas_call(
        body, name=name, grid=(B, H, nk),
        in_specs=[pl.BlockSpec((S, HEAD_PAD), smap), pl.BlockSpec((t, HEAD_PAD), kmap), pl.BlockSpec((t, 128), kmap),
                  pl.BlockSpec((S, 128), smap), pl.BlockSpec((S, 128), smap), pl.BlockSpec((S, 128), domap)],
        out_specs=[pl.BlockSpec((t, HEAD_PAD), kmap), pl.BlockSpec((t, 128), kmap)],
        out_shape=[jax.ShapeDtypeStruct((T, H * HEAD_PAD), _F32), jax.ShapeDtypeStruct((T, H * 128), _F32)],
        scratch_shapes=[pltpu.VMEM((t, HEAD_PAD), _F32), pltpu.VMEM((t, 128), _F32)],
        compiler_params=_cparams(("parallel", "parallel", "parallel")),
    )(qp, kp, v, o, lse, do)


def _halo_specs(T, nT, tt, hr, cw, cb):
    k = tt // hr
    main = pl.BlockSpec((tt, cw), lambda b, t: (b * nT + t, cb))
    prev = pl.BlockSpec((hr, cw), lambda b, t: (jnp.maximum((b * nT + t) * k - 1, 0), cb))
    nxt = pl.BlockSpec((hr, cw), lambda b, t: (jnp.minimum((b * nT + t + 1) * k, T // hr - 1), cb))
    return main, prev, nxt


def _ln_rows(z, g, b):
    mu = jnp.mean(z, axis=-1, keepdims=True)
    d = z - mu
    var = jnp.mean(d * d, axis=-1, keepdims=True)
    rstd = lax.rsqrt(var + LN_EPS)
    xh = d * rstd
    return xh * g + b, xh, rstd


def conv_fwd(h, cb_a, cb_g, w, bias, lng, lnb, *, B, S, name):
    T = B * S
    K, C = w.shape
    hr = 32
    assert K - 1 <= hr
    tt = _tile(S, 512, hr)
    nT = S // tt
    a_m, a_p, _ = _halo_specs(T, nT, tt, hr, C, cb_a)
    g_m, g_p, _ = _halo_specs(T, nT, tt, hr, C, cb_g)

    def body(a_ref, g_ref, ap_ref, gp_ref, w_ref, b_ref, lg_ref, lb_ref, z_ref, y_ref, buf):
        t = pl.program_id(1)
        buf[pl.ds(hr, tt), :] = a_ref[...] * jax.nn.sigmoid(g_ref[...])
        hp = ap_ref[...] * jax.nn.sigmoid(gp_ref[...])
        buf[pl.ds(0, hr), :] = jnp.where(t == 0, 0.0, hp)
        z = jnp.broadcast_to(b_ref[...], (tt, C))
        for k in range(K):
            z = z + w_ref[k:k + 1, :] * buf[pl.ds(hr - (K - 1) + k, tt), :]
        z_ref[...] = z
        n, _, _ = _ln_rows(z, lg_ref[...], lb_ref[...])
        y_ref[...] = (n * jax.nn.sigmoid(n)).astype(_BF)

    vec = pl.BlockSpec((1, C), lambda b, t: (0, 0))
    out = pl.BlockSpec((tt, C), lambda b, t: (b * nT + t, 0))
    return pl.pallas_call(
        body, name=name, grid=(B, nT),
        in_specs=[a_m, g_m, a_p, g_p, pl.BlockSpec((K, C), lambda b, t: (0, 0)), vec, vec, vec],
        out_specs=[out, out],
        out_shape=[jax.ShapeDtypeStruct((T, C), _F32), jax.ShapeDtypeStruct((T, C), _BF)],
        scratch_shapes=[pltpu.VMEM((hr + tt, C), _F32)],
        compiler_params=_cparams(("parallel", "parallel")),
    )(h, h, h, h, w, bias.reshape(1, C), lng.reshape(1, C), lnb.reshape(1, C))


def conv_bwd(dmix, cb_dy, z, h, cb_a, cb_g, w, lng, lnb, *, B, S, name):
    T = B * S
    K, C = w.shape
    hr = 32
    tt = _tile(S, 512, hr)
    nT = S // tt
    a_m, a_p, _ = _halo_specs(T, nT, tt, hr, C, cb_a)
    g_m, g_p, _ = _halo_specs(T, nT, tt, hr, C, cb_g)
    dy_m, _, dy_n = _halo_specs(T, nT, tt, hr, C, cb_dy)
    z_m, _, z_n = _halo_specs(T, nT, tt, hr, C, 0)

    def body(dy_ref, dyn_ref, z_ref, zn_ref, a_ref, g_ref, ap_ref, gp_ref, w_ref, lg_ref, lb_ref,
             da_ref, dg_ref, dw_ref, db_ref, dlg_ref, dlb_ref, bufz, bufh):
        b, t = pl.program_id(0), pl.program_id(1)
        lg, lb = lg_ref[...], lb_ref[...]

        def dz_of(dy, zz):
            n, xh, rstd = _ln_rows(zz, lg, lb)
            sg = jax.nn.sigmoid(n)
            dn = dy.astype(_F32) * (sg * (1.0 + n * (1.0 - sg)))
            gdn = dn * lg
            m1 = jnp.mean(gdn, axis=-1, keepdims=True)
            m2 = jnp.mean(gdn * xh, axis=-1, keepdims=True)
            return rstd * (gdn - m1 - xh * m2), dn, xh

        dz, dn, xh = dz_of(dy_ref[...], z_ref[...])
        dzn, _, _ = dz_of(dyn_ref[...], zn_ref[...])
        bufz[pl.ds(0, tt), :] = dz
        bufz[pl.ds(tt, hr), :] = jnp.where(t == nT - 1, 0.0, dzn)
        a, g = a_ref[...], g_ref[...]
        sg = jax.nn.sigmoid(g)
        bufh[pl.ds(hr, tt), :] = a * sg
        bufh[pl.ds(0, hr), :] = jnp.where(t == 0, 0.0, ap_ref[...] * jax.nn.sigmoid(gp_ref[...]))

        @pl.when((b == 0) & (t == 0))
        def _():
            dw_ref[...] = jnp.zeros_like(dw_ref)
            db_ref[...] = jnp.zeros_like(db_ref)
            dlg_ref[...] = jnp.zeros_like(dlg_ref)
            dlb_ref[...] = jnp.zeros_like(dlb_ref)

        dhc = jnp.zeros((tt, C), _F32)
        for k in range(K):
            dhc = dhc + w_ref[k:k + 1, :] * bufz[pl.ds(K - 1 - k, tt), :]
            dw_ref[k:k + 1, :] += jnp.sum(dz * bufh[pl.ds(hr - (K - 1) + k, tt), :], axis=0, keepdims=True)
        da_ref[...] = (dhc * sg).astype(_BF)
        dg_ref[...] = (dhc * a * sg * (1.0 - sg)).astype(_BF)
        db_ref[...] += jnp.sum(dz, axis=0, keepdims=True)
        dlg_ref[...] += jnp.sum(dn * xh, axis=0, keepdims=True)
        dlb_ref[...] += jnp.sum(dn, axis=0, keepdims=True)

    vec = pl.BlockSpec((1, C), lambda b, t: (0, 0))
    out = pl.BlockSpec((tt, C), lambda b, t: (b * nT + t, 0))
    kc = pl.BlockSpec((K, C), lambda b, t: (0, 0))
    return pl.pallas_call(
        body, name=name, grid=(B, nT),
        in_specs=[dy_m, dy_n, z_m, z_n, a_m, g_m, a_p, g_p, kc, vec, vec],
        out_specs=[out, out, kc, vec, vec, vec],
        out_shape=[jax.ShapeDtypeStruct((T, C), _BF), jax.ShapeDtypeStruct((T, C), _BF),
                   jax.ShapeDtypeStruct((K, C), _F32)] + [jax.ShapeDtypeStruct((1, C), _F32)] * 3,
        scratch_shapes=[pltpu.VMEM((tt + hr, C), _F32), pltpu.VMEM((hr + tt, C), _F32)],
        compiler_params=_cparams(("arbitrary", "arbitrary")),
    )(dmix, dmix, z, z, h, h, h, h, w, lng.reshape(1, C), lnb.reshape(1, C))


def _pool_cnt(t, tt, w, rows):
    pos = t * tt + lax.broadcasted_iota(jnp.int32, (rows, 1), 0)
    return jnp.minimum(pos + 1, w).astype(_F32)


def pool_fwd(h, cb, wp, scale, *, B, S, name):
    T = B * S
    G, pg, _ = wp.shape
    C = G * pg
    assert pg == 128 and G == len(POOL_WINDOWS)
    hr = 16
    tt = _tile(S, 512, hr)
    nT = S // tt
    u_m, u_p, _ = _halo_specs(T, nT, tt, hr, C, cb)

    def body(u_ref, up_ref, wp_ref, sc_ref, y_ref, buf):
        t = pl.program_id(1)
        buf[pl.ds(hr, tt), :] = u_ref[...]
        buf[pl.ds(0, hr), :] = jnp.where(t == 0, 0.0, up_ref[...])
        for gi, w in enumerate(POOL_WINDOWS):
            ln = slice(gi * pg, (gi + 1) * pg)
            acc = buf[pl.ds(hr, tt), ln]
            for j in range(1, w):
                acc = acc + buf[pl.ds(hr - j, tt), ln]
            d = acc / _pool_cnt(t, tt, w, tt) - u_ref[:, ln]
            yg = jnp.dot(d.astype(_BF), wp_ref[gi].astype(_BF), preferred_element_type=_F32)
            y_ref[:, ln] = (yg * sc_ref[:, ln]).astype(_BF)

    return pl.pallas_call(
        body, name=name, grid=(B, nT),
        in_specs=[u_m, u_p, pl.BlockSpec((G, pg, pg), lambda b, t: (0, 0, 0)), pl.BlockSpec((1, C), lambda b, t: (0, 0))],
        out_specs=pl.BlockSpec((tt, C), lambda b, t: (b * nT + t, 0)),
        out_shape=jax.ShapeDtypeStruct((T, C), _BF),
        scratch_shapes=[pltpu.VMEM((hr + tt, C), _F32)],
        compiler_params=_cparams(("parallel", "parallel")),
    )(h, h, wp, scale.reshape(1, C))


def pool_bwd(dmix, cb_dy, h, cb, wp, scale, *, B, S, name):
    T = B * S
    G, pg, _ = wp.shape
    C = G * pg
    hr = 16
    tt = _tile(S, 512, hr)
    nT = S // tt
    u_m, u_p, _ = _halo_specs(T, nT, tt, hr, C, cb)
    dy_m, _, dy_n = _halo_specs(T, nT, tt, hr, C, cb_dy)

    def body(dy_ref, dyn_ref, u_ref, up_ref, wp_ref, sc_ref, du_ref, dwp_ref, dsc_ref, buf, bufe):
        b, t = pl.program_id(0), pl.program_id(1)
        buf[pl.ds(hr, tt), :] = u_ref[...]
        buf[pl.ds(0, hr), :] = jnp.where(t == 0, 0.0, up_ref[...])

        @pl.when((b == 0) & (t == 0))
        def _():
            dwp_ref[...] = jnp.zeros_like(dwp_ref)
            dsc_ref[...] = jnp.zeros_like(dsc_ref)

        nt = (((1,), (1,)), ((), ()))
        tn = (((0,), (0,)), ((), ()))
        for gi, w in enumerate(POOL_WINDOWS):
            ln = slice(gi * pg, (gi + 1) * pg)
            wg = wp_ref[gi].astype(_BF)
            sc = sc_ref[:, ln]
            dy = dy_ref[:, ln].astype(_F32)
            dz = (dy * sc).astype(_BF)
            dzn = (dyn_ref[:, ln].astype(_F32) * sc).astype(_BF)
            dd = lax.dot_general(dz, wg, nt, preferred_element_type=_F32)
            ddn = lax.dot_general(dzn, wg, nt, preferred_element_type=_F32)
            bufe[pl.ds(0, tt), ln] = dd / _pool_cnt(t, tt, w, tt)
            bufe[pl.ds(tt, hr), ln] = jnp.where(t == nT - 1, 0.0, ddn / _pool_cnt(t + 1, tt, w, hr))
            du = -dd
            for j in range(w):
                du = du + bufe[pl.ds(j, tt), ln]
            du_ref[:, ln] = du.astype(_BF)
            acc = buf[pl.ds(hr, tt), ln]
            for j in range(1, w):
                acc = acc + buf[pl.ds(hr - j, tt), ln]
            d = (acc / _pool_cnt(t, tt, w, tt) - u_ref[:, ln]).astype(_BF)
            dwp_ref[gi] += lax.dot_general(d, dz, tn, preferred_element_type=_F32)
            yg = jnp.dot(d, wg, preferred_element_type=_F32)
            dsc_ref[:, ln] += jnp.sum(dy * yg, axis=0, keepdims=True)

    return pl.pallas_call(
        body, name=name, grid=(B, nT),
        in_specs=[dy_m, dy_n, u_m, u_p, pl.BlockSpec((G, pg, pg), lambda b, t: (0, 0, 0)),
                  pl.BlockSpec((1, C), lambda b, t: (0, 0))],
        out_specs=[pl.BlockSpec((tt, C), lambda b, t: (b * nT + t, 0)), pl.BlockSpec((G, pg, pg), lambda b, t: (0, 0, 0)),
                   pl.BlockSpec((1, C), lambda b, t: (0, 0))],
        out_shape=[jax.ShapeDtypeStruct((T, C), _BF), jax.ShapeDtypeStruct((G, pg, pg), _F32),
                   jax.ShapeDtypeStruct((1, C), _F32)],
        scratch_shapes=[pltpu.VMEM((hr + tt, C), _F32), pltpu.VMEM((tt + hr, C), _F32)],
        compiler_params=_cparams(("arbitrary", "arbitrary")),
    )(dmix, dmix, h, h, wp, scale.reshape(1, C))


_FFN_HR = 16


def _silu_grad(x, sg):
    return sg * (1.0 + x * (1.0 - sg))


def _conv3(buf, w_ref, b_ref, off, rows):
    c = b_ref[...] + w_ref[0:1, :] * buf[pl.ds(off, rows), :]
    for k in (1, 2):
        c = c + w_ref[k:k + 1, :] * buf[pl.ds(off + k, rows), :]
    return c


def gate_fwd(up, w, bias, *, B, S, name):
    T, F2 = up.shape
    F = F2 // 2
    hr = _FFN_HR
    tt = _tile(S, 512, hr)
    nT = S // tt
    tn = _tile(F, 512, 128)
    nC = F // tn
    k = tt // hr

    def body(a_ref, g_ref, ap_ref, gp_ref, wa_ref, wg_ref, ba_ref, bg_ref, o_ref, bufa, bufg):
        t = pl.program_id(2)
        for buf, m_ref, p_ref in ((bufa, a_ref, ap_ref), (bufg, g_ref, gp_ref)):
            buf[pl.ds(hr, tt), :] = m_ref[...].astype(_F32)
            buf[pl.ds(0, hr), :] = jnp.where(t == 0, 0.0, p_ref[...].astype(_F32))
        ca = _conv3(bufa, wa_ref, ba_ref, hr - 2, tt)
        cg = _conv3(bufg, wg_ref, bg_ref, hr - 2, tt)
        o_ref[...] = (ca * cg * jax.nn.sigmoid(cg)).astype(_BF)

    def main(off):
        return pl.BlockSpec((tt, tn), lambda b, j, t: (b * nT + t, j + off))

    def prev(off):
        return pl.BlockSpec((hr, tn), lambda b, j, t: (jnp.maximum((b * nT + t) * k - 1, 0), j + off))

    def wspec(rows, off):
        return pl.BlockSpec((rows, tn), lambda b, j, t: (0, j + off))

    return pl.pallas_call(
        body, name=name, grid=(B, nC, nT),
        in_specs=[main(0), main(nC), prev(0), prev(nC), wspec(3, 0), wspec(3, nC), wspec(1, 0), wspec(1, nC)],
        out_specs=pl.BlockSpec((tt, tn), lambda b, j, t: (b * nT + t, j)),
        out_shape=jax.ShapeDtypeStruct((T, F), _BF),
        scratch_shapes=[pltpu.VMEM((hr + tt, tn), _F32)] * 2,
        compiler_params=_cparams(("parallel", "parallel", "parallel")),
    )(up, up, up, up, w, w, bias.reshape(1, F2), bias.reshape(1, F2))


def gate_bwd(up, dact, w, bias, *, B, S, name):
    T, F2 = up.shape
    F = F2 // 2
    hr = _FFN_HR
    tt = _tile(S, 512, hr)
    nT = S // tt
    tn = _tile(F, 512, 128)
    nC = F // tn
    k = tt // hr
    ext = tt + hr

    def body(a_ref, g_ref, ap_ref, gp_ref, an_ref, gn_ref, d_ref, dn_ref, wa_ref, wg_ref, ba_ref, bg_ref,
             dua_ref, dug_ref, dwa_ref, dwg_ref, dba_ref, dbg_ref, bufa, bufg, bufda, bufdg):
        b, t = pl.program_id(1), pl.program_id(2)
        last = t == nT - 1
        for buf, m_ref, p_ref, n_ref in ((bufa, a_ref, ap_ref, an_ref), (bufg, g_ref, gp_ref, gn_ref)):
            buf[pl.ds(hr, tt), :] = m_ref[...].astype(_F32)
            buf[pl.ds(0, hr), :] = jnp.where(t == 0, 0.0, p_ref[...].astype(_F32))
            buf[pl.ds(hr + tt, hr), :] = jnp.where(last, 0.0, n_ref[...].astype(_F32))
        ca = _conv3(bufa, wa_ref, ba_ref, hr - 2, ext)
        cg = _conv3(bufg, wg_ref, bg_ref, hr - 2, ext)
        sg = jax.nn.sigmoid(cg)
        bufda[pl.ds(0, tt), :] = d_ref[...].astype(_F32)
        bufda[pl.ds(tt, hr), :] = jnp.where(last, 0.0, dn_ref[...].astype(_F32))
        da = bufda[...]
        bufdg[...] = da * ca * _silu_grad(cg, sg)
        bufda[...] = da * cg * sg

        @pl.when((b == 0) & (t == 0))
        def _():
            for r in (dwa_ref, dwg_ref, dba_ref, dbg_ref):
                r[...] = jnp.zeros_like(r)

        for bufd, buf, w_ref, du_ref, dw_ref, db_ref in ((bufda, bufa, wa_ref, dua_ref, dwa_ref, dba_ref),
                                                         (bufdg, bufg, wg_ref, dug_ref, dwg_ref, dbg_ref)):
            dc = bufd[pl.ds(0, tt), :]
            du = w_ref[2:3, :] * dc + w_ref[1:2, :] * bufd[pl.ds(1, tt), :] + w_ref[0:1, :] * bufd[pl.ds(2, tt), :]
            du_ref[...] = du.astype(_BF)
            for kk in range(3):
                dw_ref[kk:kk + 1, :] += jnp.sum(dc * buf[pl.ds(hr - 2 + kk, tt), :], axis=0, keepdims=True)
            db_ref[...] += jnp.sum(dc, axis=0, keepdims=True)

    def main(off):
        return pl.BlockSpec((tt, tn), lambda j, b, t: (b * nT + t, j + off))

    def prev(off):
        return pl.BlockSpec((hr, tn), lambda j, b, t: (jnp.maximum((b * nT + t) * k - 1, 0), j + off))

    def nxt(off):
        return pl.BlockSpec((hr, tn), lambda j, b, t: (jnp.minimum((b * nT + t + 1) * k, T // hr - 1), j + off))

    def wspec(rows, off):
        return pl.BlockSpec((rows, tn), lambda j, b, t: (0, j + off))

    tf = jax.ShapeDtypeStruct((T, F), _BF)
    return pl.pallas_call(
        body, name=name, grid=(nC, B, nT),
        in_specs=[main(0), main(nC), prev(0), prev(nC), nxt(0), nxt(nC), main(0), nxt(0),
                  wspec(3, 0), wspec(3, nC), wspec(1, 0), wspec(1, nC)],
        out_specs=[main(0), main(0), wspec(3, 0), wspec(3, 0), wspec(1, 0), wspec(1, 0)],
        out_shape=[tf, tf, jax.ShapeDtypeStruct((3, F), _F32), jax.ShapeDtypeStruct((3, F), _F32),
                   jax.ShapeDtypeStruct((1, F), _F32), jax.ShapeDtypeStruct((1, F), _F32)],
        scratch_shapes=[pltpu.VMEM((hr + ext, tn), _F32)] * 2 + [pltpu.VMEM((ext, tn), _F32)] * 2,
        compiler_params=_cparams(("parallel", "arbitrary", "arbitrary")),
    )(up, up, up, up, up, up, dact, dact, w, w, bias.reshape(1, F2), bias.reshape(1, F2))


def loss_head(y, target, *, name):
    T, C = y.shape
    tr = _row_tile(T, C)

    def body(y_ref, t_ref, dy_ref, acc_ref):
        @pl.when(pl.program_id(0) == 0)
        def _():
            acc_ref[...] = jnp.zeros_like(acc_ref)

        e = y_ref[...] - t_ref[...]
        dy_ref[...] = e * (1.0 / C)
        acc_ref[...] += jnp.sum(e * e, axis=0, keepdims=True) * (0.5 / C)

    row = pl.BlockSpec((tr, C), lambda i: (i, 0))
    return pl.pallas_call(
        body, name=name, grid=(T // tr,),
        in_specs=[row, row], out_specs=[row, pl.BlockSpec((1, C), lambda i: (0, 0))],
        out_shape=[jax.ShapeDtypeStruct((T, C), _F32), jax.ShapeDtypeStruct((1, C), _F32)],
        compiler_params=_cparams(("arbitrary",)),
    )(y, target)


def adamw(w, g, m, v, *, name):
    R, C = w.shape
    tr = _tile(R, max(8, (256 * 1024) // C // 8 * 8), 8)
    c1 = 1.0 - ADAM_B1 ** ADAM_STEP
    c2 = 1.0 - ADAM_B2 ** ADAM_STEP

    def body(w_ref, g_ref, m_ref, v_ref, d_ref, mo_ref, vo_ref):
        gg = g_ref[...]
        mn = ADAM_B1 * m_ref[...] + (1.0 - ADAM_B1) * gg
        vn = ADAM_B2 * v_ref[...] + (1.0 - ADAM_B2) * (gg * gg)
        d_ref[...] = -ADAM_LR * ((mn / c1) / (jnp.sqrt(vn / c2) + ADAM_EPS) + ADAM_WD * w_ref[...])
        mo_ref[...] = mn
        vo_ref[...] = vn

    blk = pl.BlockSpec((tr, C), lambda i: (i, 0))
    s = jax.ShapeDtypeStruct((R, C), _F32)
    return pl.pallas_call(
        body, name=name, grid=(R // tr,),
        in_specs=[blk] * 4, out_specs=[blk] * 3, out_shape=[s, s, s],
        compiler_params=_cparams(("parallel",)),
    )(w, g, m, v)


_ANY = pl.BlockSpec(memory_space=pl.ANY)
_MESH = pl.DeviceIdType.MESH


def _place():
    return lax.axis_index("x"), lax.axis_index("y"), lax.axis_index("c")


def _other_chips(x, y):
    chips = [(1 - x, y), (x, 1 - y), (1 - x, 1 - y)]
    return chips, [2 * a + b for a, b in chips]


def _rcopy(src, dst, ssem, rsem, dev):
    return pltpu.make_async_remote_copy(src_ref=src, dst_ref=dst, send_sem=ssem, recv_sem=rsem,
                                        device_id=dev, device_id_type=_MESH)


def place_shard(w, l, chip_idx, *, name):
    _, _, hR, C = w.shape
    tr = _tile(hR, max(16, (512 * 1024) // C // 16 * 16), 16)

    def body(ci_ref, w_ref, o_ref):
        o_ref[...] = w_ref[...].astype(_BF)

    return pl.pallas_call(
        body, name=name,
        grid_spec=pltpu.PrefetchScalarGridSpec(
            num_scalar_prefetch=1, grid=(2, hR // tr),
            in_specs=[pl.BlockSpec((None, None, tr, C), lambda h, i, ci: (l, h, i, 0))],
            out_specs=pl.BlockSpec((None, None, tr, C), lambda h, i, ci: (ci[0], h, i, 0))),
        out_shape=jax.ShapeDtypeStruct((4, 2, hR, C), _BF),
        compiler_params=_cparams(("parallel", "parallel")),
    )(chip_idx, w)


class _Side:
    def __init__(self, arrays, out_shapes, aliases, n_sems, start, wait):
        self.arrays, self.out_shapes, self.aliases, self.n_sems = arrays, out_shapes, aliases, n_sems
        self.start, self.wait = start, wait


def gather_stage1_side(bufs):
    n = len(bufs)

    def copies(outs, sems, sending):
        x, y, c = _place()
        me = 2 * x + y
        chips, cidx = _other_chips(x, y)
        send, recv = sems
        out, back = [], []
        for k, chip in enumerate(chips):
            for p in range(n):
                mine, got = outs[p].at[me, c], outs[p].at[cidx[k], c]
                out.append(_rcopy(mine, mine, send.at[p * 3 + k], recv.at[p * 3 + k], (*chip, c)))
                if not sending:
                    back.append(_rcopy(got, got, send.at[p * 3 + k], recv.at[p * 3 + k], (*chip, c)))
        return out, back

    def start(ins, outs, sems):
        for cp in copies(outs, sems, True)[0]:
            cp.start()

    def wait(ins, outs, sems):
        out, back = copies(outs, sems, False)
        for cp in back:
            cp.wait_recv()
        for cp in out:
            cp.wait_send()

    return _Side(list(bufs), [jax.ShapeDtypeStruct(b.shape, b.dtype) for b in bufs], {p: p for p in range(n)},
                 [n * 3, n * 3], start, wait)


def gather_stage2_side(bufs):
    n = len(bufs)

    def copies(outs, sems, sending):
        x, y, c = _place()
        sib = (x, y, 1 - c)
        _, cidx = _other_chips(x, y)
        send, recv = sems
        out, back = [], []
        for k in range(3):
            for p in range(n):
                mine, got = outs[p].at[cidx[k], c], outs[p].at[cidx[k], 1 - c]
                out.append(_rcopy(mine, mine, send.at[p * 3 + k], recv.at[p * 3 + k], sib))
                if not sending:
                    back.append(_rcopy(got, got, send.at[p * 3 + k], recv.at[p * 3 + k], sib))
        return out, back

    def start(ins, outs, sems):
        for cp in copies(outs, sems, True)[0]:
            cp.start()

    def wait(ins, outs, sems):
        out, back = copies(outs, sems, False)
        for cp in back:
            cp.wait_recv()
        for cp in out:
            cp.wait_send()

    return _Side(list(bufs), [jax.ShapeDtypeStruct(b.shape, b.dtype) for b in bufs], {p: p for p in range(n)},
                 [n * 3, n * 3], start, wait)


def chip_exchange_side(ps):
    n = len(ps)

    def copies(ins, outs, sems, sending):
        x, y, c = _place()
        me = 2 * x + y
        chips, cidx = _other_chips(x, y)
        send, recv = sems
        out, back = [], []
        for k, chip in enumerate(chips):
            for p in range(n):
                got = outs[p].at[cidx[k]]
                out.append(_rcopy(ins[p].at[cidx[k]], outs[p].at[me], send.at[p * 3 + k], recv.at[p * 3 + k], (*chip, c)))
                if not sending:
                    back.append(_rcopy(got, got, send.at[p * 3 + k], recv.at[p * 3 + k], (*chip, c)))
        return out, back

    def start(ins, outs, sems):
        for cp in copies(ins, outs, sems, True)[0]:
            cp.start()

    def wait(ins, outs, sems):
        out, back = copies(ins, outs, sems, False)
        for cp in back:
            cp.wait_recv()
        for cp in out:
            cp.wait_send()

    return _Side(list(ps), [jax.ShapeDtypeStruct(a.shape, a.dtype) for a in ps], {}, [n * 3, n * 3], start, wait)


def _side_specs(side):
    if side is None:
        return [], [], [], [], [], {}
    return (side.arrays, [_ANY] * len(side.arrays), [_ANY] * len(side.out_shapes), side.out_shapes,
            [pltpu.SemaphoreType.DMA((k,)) for k in side.n_sems], side.aliases)


def run_side(side, *, name):
    ni, no = len(side.arrays), len(side.out_shapes)

    def body(*refs):
        ins, outs, sems = refs[:ni], refs[ni:ni + no], refs[ni + no:]
        side.start(ins, outs, sems)
        side.wait(ins, outs, sems)

    ops, in_specs, out_specs, out_shapes, scratch, aliases = _side_specs(side)
    return pl.pallas_call(body, name=name, in_specs=in_specs, out_specs=out_specs, out_shape=out_shapes,
                          input_output_aliases=aliases, scratch_shapes=scratch)(*ops)


def gather_small(small, *, name):
    ns = len(small)

    def body(*refs):
        s_in, s_out = refs[:ns], refs[ns:2 * ns]
        send, recv, lsem = refs[2 * ns:]
        x, y, c = _place()
        me = 2 * x + y
        chips, cidx = _other_chips(x, y)
        local = [pltpu.make_async_copy(s_in[q], s_out[q].at[me], lsem.at[q]) for q in range(ns)]
        cps = [_rcopy(s_in[q], s_out[q].at[me], send.at[q * 3 + k], recv.at[q * 3 + k], (*chip, c))
               for k, chip in enumerate(chips) for q in range(ns)]
        for cp in local + cps:
            cp.start()
        for k in range(3):
            for q in range(ns):
                got = s_out[q].at[cidx[k]]
                _rcopy(got, got, send.at[q * 3 + k], recv.at[q * 3 + k], (x, y, c)).wait_recv()
        for cp in cps:
            cp.wait_send()
        for cp in local:
            cp.wait()

    return pl.pallas_call(
        body, name=name, in_specs=[_ANY] * ns, out_specs=[_ANY] * ns,
        out_shape=[jax.ShapeDtypeStruct((4,) + a.shape, a.dtype) for a in small],
        scratch_shapes=[pltpu.SemaphoreType.DMA((ns * 3,))] * 2 + [pltpu.SemaphoreType.DMA((ns,))],
    )(*small)


def sibling_send_half(gs, *, name):
    n = len(gs)

    def body(*refs):
        g_in, g_out, send, recv = refs[:n], refs[n:2 * n], refs[2 * n], refs[2 * n + 1]
        x, y, c = _place()
        sib = (x, y, 1 - c)
        cps = [_rcopy(g_in[p].at[1 - c], g_out[p], send.at[p], recv.at[p], sib) for p in range(n)]
        for cp in cps:
            cp.start()
        for cp in cps:
            cp.wait()

    return pl.pallas_call(
        body, name=name, in_specs=[_ANY] * n, out_specs=[_ANY] * n,
        out_shape=[jax.ShapeDtypeStruct(a.shape[1:], a.dtype) for a in gs],
        scratch_shapes=[pltpu.SemaphoreType.DMA((n,))] * 2,
    )(*gs)


def sum_chips(p, slots, idx, *, name):
    _, N, C = p.shape
    tr = _tile(N, max(16, (512 * 1024) // C // 16 * 16), 16)

    def body(i0, i1, i2, i3, i4, p_ref, s0_ref, s1_ref, s2_ref, o_ref):
        s = p_ref[...].astype(_F32)
        for r in (s0_ref, s1_ref, s2_ref):
            s = s + r[...].astype(_F32)
        o_ref[...] = s

    def at(k):
        return pl.BlockSpec((None, tr, C), lambda i, *ix: (ix[k][0], i, 0))

    return pl.pallas_call(
        body, name=name,
        grid_spec=pltpu.PrefetchScalarGridSpec(
            num_scalar_prefetch=5, grid=(N // tr,),
            in_specs=[at(0), at(1), at(2), at(3)], out_specs=at(4)),
        out_shape=jax.ShapeDtypeStruct((2, N, C), _F32),
        compiler_params=_cparams(("parallel",)),
    )(*idx, p, slots, slots, slots)


def sibling_join(rs, *, name):
    n = len(rs)

    def body(*refs):
        r_out, send, recv = refs[n:2 * n], refs[2 * n], refs[2 * n + 1]
        x, y, c = _place()
        sib = (x, y, 1 - c)
        cps = [_rcopy(r_out[p].at[c], r_out[p].at[c], send.at[p], recv.at[p], sib) for p in range(n)]
        for cp in cps:
            cp.start()
        for p in range(n):
            got = r_out[p].at[1 - c]
            _rcopy(got, got, send.at[p], recv.at[p], sib).wait_recv()
        for cp in cps:
            cp.wait_send()

    return pl.pallas_call(
        body, name=name, in_specs=[_ANY] * n, out_specs=[_ANY] * n,
        out_shape=[jax.ShapeDtypeStruct(a.shape, a.dtype) for a in rs],
        input_output_aliases={p: p for p in range(n)},
        scratch_shapes=[pltpu.SemaphoreType.DMA((n,))] * 2,
    )(*rs)


def all_devices_exchange(v, *, name):
    def body(v_ref, o_ref, send, recv, lsem):
        x, y, c = _place()
        me = 4 * x + 2 * y + c
        local = pltpu.make_async_copy(v_ref, o_ref.at[me], lsem)
        local.start()
        peers = []
        for k in range(1, 8):
            px, py, pc = x ^ (k >> 2), y ^ ((k >> 1) & 1), c ^ (k & 1)
            peers.append((px, py, pc))
        cps = [_rcopy(v_ref, o_ref.at[me], send.at[k], recv.at[k], peer) for k, peer in enumerate(peers)]
        for cp in cps:
            cp.start()
        for k, (px, py, pc) in enumerate(peers):
            got = o_ref.at[4 * px + 2 * py + pc]
            _rcopy(got, got, send.at[k], recv.at[k], (x, y, c)).wait_recv()
        for cp in cps:
            cp.wait_send()
        local.wait()

    return pl.pallas_call(
        body, name=name, in_specs=[_ANY], out_specs=_ANY,
        out_shape=jax.ShapeDtypeStruct((8,) + v.shape, v.dtype),
        scratch_shapes=[pltpu.SemaphoreType.DMA((7,))] * 2 + [pltpu.SemaphoreType.DMA(())],
    )(v)


def add_halves(gs_and_rs, c_idx, *, name):
    outs = []
    for n_, (g, r) in enumerate(gs_and_rs):
        N, C = r.shape
        tr = _tile(N, max(16, (512 * 1024) // C // 16 * 16), 16)

        def body(c_ref, g_ref, r_ref, o_ref):
            o_ref[...] = (g_ref[...].astype(_F32) + r_ref[...].astype(_F32)).astype(o_ref.dtype)

        outs.append(pl.pallas_call(
            body, name=f"{name}_{n_}",
            grid_spec=pltpu.PrefetchScalarGridSpec(
                num_scalar_prefetch=1, grid=(N // tr,),
                in_specs=[pl.BlockSpec((None, tr, C), lambda i, c: (c[0], i, 0)), pl.BlockSpec((tr, C), lambda i, c: (i, 0))],
                out_specs=pl.BlockSpec((tr, C), lambda i, c: (i, 0))),
            out_shape=jax.ShapeDtypeStruct((N, C), r.dtype),
            compiler_params=_cparams(("parallel",)),
        )(c_idx, g, r))
    return outs


def sum_slots(a, *, name):
    n, N, C = a.shape
    tr = _tile(N, max(16, (512 * 1024) // C // 16 * 16), 16)

    def body(a_ref, o_ref):
        s = a_ref[0].astype(_F32)
        for k in range(1, n):
            s = s + a_ref[k].astype(_F32)
        o_ref[...] = s

    return pl.pallas_call(
        body, name=name, grid=(N // tr,),
        in_specs=[pl.BlockSpec((n, tr, C), lambda i: (0, i, 0))],
        out_specs=pl.BlockSpec((tr, C), lambda i: (i, 0)),
        out_shape=jax.ShapeDtypeStruct((N, C), _F32),
        compiler_params=_cparams(("parallel",)),
    )(a)


_WEIGHTS = ['ln_in_g', 'ln_in_b', 'w_in', 'q_norm_g', 'w_uq', 'kv_norm_g', 'w_ukv', 'conv_w', 'conv_b', 'conv_ln_g',
            'conv_ln_b', 'w_pool', 'pool_scale', 'w_out', 'ln1_g', 'ln1_b', 'w_up', 'ffn_conv_w', 'ffn_conv_b', 'w_down',
            'ln2_g', 'ln2_b']
_BIG = ['w_in', 'w_uq', 'w_ukv', 'w_out', 'w_up', 'w_down']
_SMALL_SHARDED = ['conv_w', 'ffn_conv_w']
_SMALL = [n for n in _WEIGHTS if n not in _BIG]


def _rope_tables(positions):
    half = QK_ROPE // 2
    inv = 1.0 / (ROPE_THETA ** (jnp.arange(0, QK_ROPE, 2, dtype=_F32) / QK_ROPE))
    ang = positions.reshape(-1).astype(_F32)[:, None] * inv
    c, s = jnp.cos(ang), jnp.sin(ang)
    z = jnp.zeros_like(c)
    cc = jnp.concatenate([c, c, z, z], axis=1)
    sa = jnp.concatenate([-s, z, z, z], axis=1)
    sb = jnp.concatenate([z, s, z, z], axis=1)
    assert cc.shape[1] == 128 and half == 32
    return cc, sa, sb


def _in_pad(dims):
    D, QL, KVL, CW, PW, H, F = dims
    return (-(QL + 2 * CW + PW + KVL + 128)) % 512


def _layer_weights(full, dims):
    D, QL, KVL, CW, PW, H, F = dims
    w_in = full['w_in'].transpose(1, 0, 2).reshape(D, -1)
    o1, o2, o3, o4 = QL, QL + KVL, QL + KVL + QK_ROPE, QL + KVL + QK_ROPE + 2 * CW
    w_in_p = jnp.concatenate([w_in[:, :o1], w_in[:, o3:o4], w_in[:, o4:], w_in[:, o1:o2], w_in[:, o2:o3],
                              jnp.zeros((D, 128 - QK_ROPE + _in_pad(dims)), w_in.dtype)], axis=1)
    w_uq = full['w_uq'].reshape(QL, H, QK_NOPE + QK_ROPE)
    w_uq_p = jnp.pad(w_uq, ((0, 0), (0, 0), (0, HEAD_PAD - QK_NOPE - QK_ROPE))).reshape(QL, H * HEAD_PAD)
    return dict(
        w_in=w_in_p, w_uq=w_uq_p,
        w_ukv=full['w_ukv'].reshape(KVL, H * (QK_NOPE + V_HEAD)),
        w_out=full['w_out'].reshape(D, D),
        w_up=full['w_up'].transpose(1, 0, 2).reshape(D, 2 * F),
        w_down=full['w_down'].reshape(F, D),
    )


def _unpermute_w_in_grad(g, dims):
    D, QL, KVL, CW, PW, H, F = dims
    a, b_, c_ = QL, QL + 2 * CW, QL + 2 * CW + PW
    return jnp.concatenate([g[:, :a], g[:, c_:c_ + KVL], g[:, c_ + KVL:c_ + KVL + QK_ROPE], g[:, a:b_], g[:, b_:c_]], axis=1)


class _NoExchange:
    def __init__(self, layer_weights):
        self.layer_weights, self.grads = layer_weights, {}

    def weights(self, l):
        return self.layer_weights[l]

    def side(self, where, l):
        return None

    def side_done(self, where, l, outs):
        pass

    def grads_ready(self, l, g):
        self.grads[l] = g


def _with_side(hooks, where, l, fn):
    sd = hooks.side(where, l)
    res = fn(sd)
    if sd is None:
        return res
    hooks.side_done(where, l, res[-1])
    return res[0] if len(res) == 2 else res[:-1]


def _local_step(x, positions, target, small, dims, B, S, L, hooks):
    D, QL, KVL, CW, PW, H, F = dims
    T = B * S
    alpha = (2.0 * L) ** 0.25
    scale = float(QK_NOPE + QK_ROPE) ** -0.5
    cc, sa, sb = _rope_tables(positions)
    cb_q, cb_a, cb_g, cb_p = 0, QL // CW, QL // CW + 1, (QL + 2 * CW) // PW
    cb_kv, cb_kr = (QL + 2 * CW + PW) // KVL, (QL + 2 * CW + PW + KVL) // 128
    assert QL % CW == 0 and (QL + 2 * CW) % PW == 0 and (QL + 2 * CW + PW) % KVL == 0 and (QL + 2 * CW + PW + KVL) % 128 == 0

    xs, xb = ln_fwd([x], [1.0], small['ln_in_g'], small['ln_in_b'], want_r=False, name="ln_in")
    saved = []
    fa = dict(B=B, S=S, H=H, scale=scale)
    for l in range(L):
        W = hooks.weights(l)
        h = matmul(xb, W['w_in'], name="mm_in")
        qn = rms_fwd(h, cb_q, QL, small['q_norm_g'][l], name="rms_q")
        kvn = rms_fwd(h, cb_kv, KVL, small['kv_norm_g'][l], name="rms_kv")
        q = matmul(qn, W['w_uq'], name="mm_uq")
        kv = matmul(kvn, W['w_ukv'], name="mm_ukv")
        qp, kp, v = mla_pack(q, kv, h, cb_kr, cc, sa, sb, H=H, scale=scale, name="mla_pack")
        o, lse = _with_side(hooks, 'flash', l, lambda sd: flash_fwd(qp, kp, v, side=sd, name="flash_fwd", **fa))
        z, yc = conv_fwd(h, cb_a, cb_g, small['conv_w'][l], small['conv_b'][l], small['conv_ln_g'][l],
                         small['conv_ln_b'][l], B=B, S=S, name="conv_fwd")
        yp = pool_fwd(h, cb_p, small['w_pool'][l], small['pool_scale'][l], B=B, S=S, name="pool_fwd")
        mixed = jnp.concatenate([o.astype(_BF), yc, yp], axis=1)
        y1 = matmul(mixed, W['w_out'], name="mm_out")
        r1, x1, x1b = ln_fwd([xs, y1], [alpha, 1.0], small['ln1_g'][l], small['ln1_b'][l], want_r=True, name="ln1")
        up = _with_side(hooks, 'up', l, lambda sd: matmul(x1b, W['w_up'], out_dtype=_BF, side=sd, name="mm_up"))
        act = gate_fwd(up, small['ffn_conv_w'][l], small['ffn_conv_b'][l], B=B, S=S, name="gate_fwd")
        y2 = _with_side(hooks, 'down', l, lambda sd: matmul(act, W['w_down'], side=sd, name="mm_down"))
        r2, x2, x2b = ln_fwd([x1, y2], [alpha, 1.0], small['ln2_g'][l], small['ln2_b'][l], want_r=True, name="ln2")
        saved.append(dict(W=W, xb=xb, h=h, qn=qn, kvn=kvn, qp=qp, kp=kp, v=v, o=o, lse=lse, z=z, mixed=mixed, r1=r1,
                          x1b=x1b, up=up, act=act, r2=r2))
        xs, xb = x2, x2b

    dy, loss_cols = loss_head(xs, target, name="loss_head")
    gs = {n: [None] * L for n in _SMALL if n not in ('ln_in_g', 'ln_in_b')}
    d_terms, d_coefs = [dy], [1.0]
    zpad = jnp.zeros((T, _in_pad(dims)), _BF) if _in_pad(dims) else None
    for l in reversed(range(L)):
        sv = saved[l]
        W = sv['W']
        gb = {}
        dr2, dr2b, gs['ln2_g'][l], gs['ln2_b'][l] = ln_bwd(d_terms, d_coefs, sv['r2'], small['ln2_g'][l], name="ln2_bwd")
        dact = matmul(dr2b, W['w_down'], tb=True, out_dtype=_BF, name="mm_down_dx")
        gb['w_down'] = matmul(sv['act'], dr2b, ta=True, out_dtype=_BF, name="mm_down_dw")
        dua, dug, dwa, dwg, dba, dbg = gate_bwd(sv['up'], dact, small['ffn_conv_w'][l], small['ffn_conv_b'][l],
                                                B=B, S=S, name="gate_bwd")
        gs['ffn_conv_w'][l] = jnp.concatenate([dwa, dwg], axis=1)
        gs['ffn_conv_b'][l] = jnp.concatenate([dba, dbg], axis=1)
        dup = jnp.concatenate([dua, dug], axis=1)
        gb['w_up'] = _with_side(hooks, 'up_dw', l, lambda sd: matmul(sv['x1b'], dup, ta=True, out_dtype=_BF, side=sd,
                                                                     name="mm_up_dw"))
        dx1 = _with_side(hooks, 'up_dx', l, lambda sd: matmul(dup, W['w_up'], tb=True, side=sd, name="mm_up_dx"))
        dr1, dr1b, gs['ln1_g'][l], gs['ln1_b'][l] = ln_bwd([dr2, dx1], [alpha, 1.0], sv['r1'], small['ln1_g'][l],
                                                            name="ln1_bwd")
        dmix = matmul(dr1b, W['w_out'], tb=True, name="mm_out_dx")
        gb['w_out'] = matmul(sv['mixed'], dr1b, ta=True, out_dtype=_BF, name="mm_out_dw")
        h = sv['h']
        ncb = (H * V_HEAD) // CW
        dca, dcg, gs['conv_w'][l], gs['conv_b'][l], gs['conv_ln_g'][l], gs['conv_ln_b'][l] = conv_bwd(
            dmix, ncb, sv['z'], h, cb_a, cb_g, small['conv_w'][l], small['conv_ln_g'][l], small['conv_ln_b'][l],
            B=B, S=S, name="conv_bwd")
        dpool, gs['w_pool'][l], gs['pool_scale'][l] = pool_bwd(
            dmix, (H * V_HEAD + CW) // PW, h, cb_p, small['w_pool'][l], small['pool_scale'][l], B=B, S=S, name="pool_bwd")
        dqp = flash_bwd_dq(sv['qp'], sv['kp'], sv['v'], sv['o'], sv['lse'], dmix, 0, name="flash_dq", **fa)
        dkp, dv = flash_bwd_dkv(sv['qp'], sv['kp'], sv['v'], sv['o'], sv['lse'], dmix, 0, name="flash_dkv", **fa)
        dq, dkv, dkr = mla_unpack(dqp, dkp, dv, cc, sa, sb, H=H, name="mla_unpack")
        dqn = matmul(dq, W['w_uq'], tb=True, name="mm_uq_dx")
        g_uq = matmul(sv['qn'], dq, ta=True, out_dtype=_BF, name="mm_uq_dw")
        gb['w_uq'] = g_uq.reshape(QL, H, HEAD_PAD)[:, :, :QK_NOPE + QK_ROPE].reshape(QL, -1)
        dkvn = matmul(dkv, W['w_ukv'], tb=True, name="mm_ukv_dx")
        gb['w_ukv'] = matmul(sv['kvn'], dkv, ta=True, out_dtype=_BF, name="mm_ukv_dw")
        dcq, gs['q_norm_g'][l] = rms_bwd(dqn, h, cb_q, QL, small['q_norm_g'][l], name="rms_q_bwd")
        dckv, gs['kv_norm_g'][l] = rms_bwd(dkvn, h, cb_kv, KVL, small['kv_norm_g'][l], name="rms_kv_bwd")
        dh = jnp.concatenate([dcq, dca, dcg, dpool, dckv, dkr] + ([zpad] if zpad is not None else []), axis=1)
        gb['w_in'] = _unpermute_w_in_grad(matmul(sv['xb'], dh, ta=True, out_dtype=_BF, name="mm_in_dw"), dims)
        dxm = matmul(dh, W['w_in'], tb=True, name="mm_in_dx")
        hooks.grads_ready(l, gb)
        d_terms, d_coefs = [dr1, dxm], [alpha, 1.0]
    gx, _, g_ln_g, g_ln_b = ln_bwd(d_terms, d_coefs, x, small['ln_in_g'], name="ln_in_bwd")
    gsm = {n: jnp.stack([a.reshape(small[n].shape[1:]) for a in gs[n]]) for n in gs}
    gsm['ln_in_g'], gsm['ln_in_b'] = g_ln_g.reshape(-1), g_ln_b.reshape(-1)
    return loss_cols, gx, gsm


_COL_SHARDED = ('w_in', 'w_up')


def _flat_pad(arrs, mult=512 * 128):
    v = jnp.concatenate([a.reshape(-1) for a in arrs])
    n = v.shape[0]
    return jnp.pad(v, (0, (-n) % mult)).reshape(-1, 128)


def _split_like(flat, like):
    out, off = [], 0
    v = flat.reshape(-1)
    for a in like:
        out.append(v[off:off + a.size].reshape(a.shape))
        off += a.size
    return out


_GROUP_A = ('w_up',)
_GROUP_B = tuple(n for n in _BIG if n not in _GROUP_A)


class _Exchange:
    def __init__(self, a, dims, L, chip_idx, c_idx, sum_idx):
        self.dims, self.L, self.c_idx, self.sum_idx = dims, L, c_idx, sum_idx
        self.rows = {n: (a[n].shape[1], math.prod(a[n].shape[2:])) for n in _BIG}
        self.bufs = []
        for l in range(L):
            self.bufs.append({n: place_shard(a[n].reshape(L, 2, self.rows[n][0] // 2, self.rows[n][1]), l, chip_idx,
                                             name="place_" + n) for n in _BIG})
        self._store(0, _BIG, run_side(gather_stage1_side(self._list(0, _BIG)), name="gather_first_ici"))
        self._store(0, _BIG, run_side(gather_stage2_side(self._list(0, _BIG)), name="gather_first_d2d"))
        self.pending = None
        self.reduced = {}

    def _list(self, l, names):
        return [self.bufs[l][n] for n in names]

    def _store(self, l, names, outs):
        self.bufs[l].update(zip(names, outs))

    def weights(self, l):
        return _layer_weights({n: self.bufs[l][n].reshape(4, *self.rows[n]) for n in _BIG}, self.dims)

    def side(self, where, l):
        if where in ('flash', 'up', 'down'):
            if l + 1 >= self.L:
                return None
            if where == 'down':
                return gather_stage2_side(self._list(l + 1, _BIG))
            return gather_stage1_side(self._list(l + 1, _GROUP_B if where == 'flash' else _GROUP_A))
        if self.pending is None:
            return None
        return chip_exchange_side([self.pending[1][n] for n in (_GROUP_B if where == 'up_dw' else _GROUP_A)])

    def side_done(self, where, l, outs):
        if where in ('flash', 'up', 'down'):
            self._store(l + 1, {'flash': _GROUP_B, 'up': _GROUP_A, 'down': _BIG}[where], outs)
            return
        self.pending[2].update(zip(_GROUP_B if where == 'up_dw' else _GROUP_A, outs))
        if where == 'up_dx':
            self._finish()

    def _finish(self):
        l, sums, slots = self.pending
        halves = [sum_chips(sums[n], slots[n], self.sum_idx, name="grad_sum_" + n) for n in _BIG]
        joined = sibling_join(halves, name="grad_sibling_join")
        self.reduced[l] = {n: j.reshape(self.rows[n]) for n, j in zip(_BIG, joined)}
        self.pending = None

    def grads_ready(self, l, g):
        g_in = []
        for n in _BIG:
            r, c = self.rows[n]
            st = g[n].reshape(g[n].shape[0], 4, c).transpose(1, 0, 2) if n in _COL_SHARDED else g[n].reshape(4, r, c)
            g_in.append(st.reshape(4, 2, r // 2, c).transpose(1, 0, 2, 3).reshape(2, 2 * r, c))
        from_sib = sibling_send_half(g_in, name="grad_sibling_send")
        sums = add_halves(list(zip(g_in, from_sib)), self.c_idx, name="grad_presum")
        self.pending = (l, {n: p.reshape(4, p.shape[0] // 4, p.shape[1]) for n, p in zip(_BIG, sums)}, {})
        if l == 0:
            outs = run_side(chip_exchange_side([self.pending[1][n] for n in _BIG]), name="grad_chip_exchange_last")
            self.pending[2].update(zip(_BIG, outs))
            self._finish()


def kernel(x, positions, ln_in_g, ln_in_b, w_in, q_norm_g, w_uq, kv_norm_g, w_ukv, conv_w, conv_b, conv_ln_g, conv_ln_b, w_pool, pool_scale, w_out, ln1_g, ln1_b, w_up, ffn_conv_w, ffn_conv_b, w_down, ln2_g, ln2_b, loss_target, m_ln_in_g, m_ln_in_b, m_w_in, m_q_norm_g, m_w_uq, m_kv_norm_g, m_w_ukv, m_conv_w, m_conv_b, m_conv_ln_g, m_conv_ln_b, m_w_pool, m_pool_scale, m_w_out, m_ln1_g, m_ln1_b, m_w_up, m_ffn_conv_w, m_ffn_conv_b, m_w_down, m_ln2_g, m_ln2_b, v_ln_in_g, v_ln_in_b, v_w_in, v_q_norm_g, v_w_uq, v_kv_norm_g, v_w_ukv, v_conv_w, v_conv_b, v_conv_ln_g, v_conv_ln_b, v_w_pool, v_pool_scale, v_w_out, v_ln1_g, v_ln1_b, v_w_up, v_ffn_conv_w, v_ffn_conv_b, v_w_down, v_ln2_g, v_ln2_b):
    a = dict(locals())
    B, S, D = a['x'].shape
    T = B * S
    L = a['w_in'].shape[0]
    QL, H = 4 * a['w_uq'].shape[1], a['w_uq'].shape[2]
    KVL = 4 * a['w_ukv'].shape[1]
    CW, PW = a['conv_b'].shape[1], a['pool_scale'].shape[1]
    F = 4 * a['w_down'].shape[1]
    dims = (D, QL, KVL, CW, PW, H, F)
    chip = 2 * lax.axis_index("x") + lax.axis_index("y")
    c_idx = lax.axis_index("c").astype(jnp.int32).reshape(1)

    def shard2d(w):
        return w.reshape(-1, w.shape[-1]) if w.ndim == 3 else w.reshape(w.shape[0] * w.shape[1], -1)

    small = {n: a[n] for n in _SMALL}
    for n, o in zip(_SMALL_SHARDED, gather_small([shard2d(a[n]) for n in _SMALL_SHARDED], name="gather_small")):
        k = a[n].shape[1]
        small[n] = o.reshape(4, L, k, -1).transpose(1, 2, 0, 3).reshape(L, k, -1)
    xi, yi = lax.axis_index("x"), lax.axis_index("y")
    chip_idx = chip.astype(jnp.int32).reshape(1)
    sum_idx = [v.astype(jnp.int32).reshape(1) for v in [chip] + _other_chips(xi, yi)[1] + [lax.axis_index("c")]]
    ex = _Exchange(a, dims, L, chip_idx, c_idx, sum_idx)

    loss_cols, gx, gsm = _local_step(a['x'].reshape(T, D), a['positions'], a['loss_target'].reshape(T, D),
                                     small, dims, B, S, L, ex)
    loss = lax.psum(jnp.sum(loss_cols), ("x", "y", "c"))
    g_big = {n: jnp.concatenate([ex.reduced[l][n] for l in range(L)]).reshape(a[n].shape) for n in _BIG}

    sm_like = [gsm[n] for n in _SMALL]
    sm_sum = sum_slots(all_devices_exchange(_flat_pad(sm_like), name="small_exchange"), name="small_sum")
    g_small = dict(zip(_SMALL, _split_like(sm_sum, sm_like)))
    for n in _SMALL_SHARDED:
        w = a[n].shape[-1]
        g_small[n] = lax.dynamic_slice_in_dim(g_small[n], chip * w, w, axis=2)

    grads, delta, new_m, new_v = {}, {}, {}, {}
    for n in _BIG:
        grads[n] = g_big[n]
        d_, m_, v_ = adamw(shard2d(a[n]), shard2d(g_big[n]), shard2d(a['m_' + n]), shard2d(a['v_' + n]), name="adamw_" + n)
        delta[n], new_m[n], new_v[n] = (t.reshape(a[n].shape) for t in (d_, m_, v_))
    like = [a[n] for n in _SMALL]
    d_, m_, v_ = adamw(_flat_pad(like), _flat_pad([g_small[n] for n in _SMALL]), _flat_pad([a['m_' + n] for n in _SMALL]),
                       _flat_pad([a['v_' + n] for n in _SMALL]), name="adamw_small")
    for n, dd, mm, vv in zip(_SMALL, _split_like(d_, like), _split_like(m_, like), _split_like(v_, like)):
        grads[n], delta[n], new_m[n], new_v[n] = g_small[n], dd, mm, vv

    return (loss, gx.reshape(B, S, D), *[grads[n] for n in _WEIGHTS], *[delta[n] for n in _WEIGHTS],
            *[new_m[n] for n in _WEIGHTS], *[new_v[n] for n in _WEIGHTS])
```

```python
import functools
import math

import jax
import jax.numpy as jnp
from jax import lax
from jax.experimental import pallas as pl
from jax.experimental.pallas import tpu as pltpu

_BF = jnp.bfloat16
_F32 = jnp.float32
_VMEM_LIMIT = 56 * 1024 * 1024

QK_NOPE = 128
QK_ROPE = 64
V_HEAD = 128
HEAD_PAD = 256
ROPE_THETA = 10000.0
LN_EPS = 1e-5
RMS_EPS = 1e-6
POOL_WINDOWS = (2, 4, 8, 16)
ADAM_LR, ADAM_B1, ADAM_B2, ADAM_EPS, ADAM_WD, ADAM_STEP = 0.001, 0.9, 0.999, 1e-8, 0.01, 10


def _cparams(sem=None):
    kw = dict(vmem_limit_bytes=_VMEM_LIMIT)
    if sem is not None:
        kw["dimension_semantics"] = sem
    return pltpu.CompilerParams(**kw)


def _tile(n, target, unit=128):
    if n <= target:
        return n
    t = (target // unit) * unit
    while t >= unit:
        if n % t == 0:
            return t
        t -= unit
    return n


_MM_VMEM_BUDGET = 40 * 1024 * 1024


def matmul(a, b, *, ta=False, tb=False, out_dtype=_F32, tm=1024, tn=1536, tk=2048, side=None, name="mm"):
    if ta:
        K, M = a.shape
    else:
        M, K = a.shape
    if tb:
        N, K2 = b.shape
    else:
        K2, N = b.shape
    assert K == K2, (a.shape, b.shape, ta, tb)
    tm, tn, tk = _tile(M, tm), _tile(N, tn), _tile(K, tk)
    ab, bb, ob = a.dtype.itemsize, b.dtype.itemsize, jnp.dtype(out_dtype).itemsize

    def vmem(tk_):
        return 2 * (tm * tk_ * ab + tk_ * tn * bb) + 2 * tm * tn * ob + tm * tn * 4 * (2 if K // tk_ > 1 else 1)

    while vmem(tk) > _MM_VMEM_BUDGET and tk > 256 and _tile(K, tk // 2) < tk:
        tk = _tile(K, tk // 2)
    nk = K // tk
    dn = (((0,) if ta else (1,), (1,) if tb else (0,)), ((), ()))

    s_ops, s_in, s_out, s_shapes, s_scratch, s_alias = _side_specs(side)
    ni, no, nacc = len(s_ops), len(s_shapes), int(nk > 1)
    grid = (M // tm, N // tn, nk)

    def body(a_ref, b_ref, *rest):
        s_ins, o_ref, s_outs = rest[:ni], rest[ni], rest[ni + 1:ni + 1 + no]
        acc, sems = rest[ni + 1 + no:ni + 1 + no + nacc], rest[ni + 1 + no + nacc:]
        i, j, k = pl.program_id(0), pl.program_id(1), pl.program_id(2)
        if side is not None:
            @pl.when((i == 0) & (j == 0) & (k == 0))
            def _():
                side.start(s_ins, s_outs, sems)

        prod = lax.dot_general(a_ref[...].astype(_BF), b_ref[...].astype(_BF), dn, preferred_element_type=_F32)
        if nk == 1:
            o_ref[...] = prod.astype(o_ref.dtype)
        else:
            acc_ref = acc[0]

            @pl.when(k == 0)
            def _():
                acc_ref[...] = prod

            @pl.when(k > 0)
            def _():
                acc_ref[...] += prod

            @pl.when(k == nk - 1)
            def _():
                o_ref[...] = acc_ref[...].astype(o_ref.dtype)

        if side is not None:
            @pl.when((i == grid[0] - 1) & (j == grid[1] - 1) & (k == nk - 1))
            def _():
                side.wait(s_ins, s_outs, sems)

    a_spec = pl.BlockSpec((tk, tm), lambda i, j, k: (k, i)) if ta else pl.BlockSpec((tm, tk), lambda i, j, k: (i, k))
    b_spec = pl.BlockSpec((tn, tk), lambda i, j, k: (j, k)) if tb else pl.BlockSpec((tk, tn), lambda i, j, k: (k, j))
    res = pl.pallas_call(
        body, name=name,
        grid=grid,
        in_specs=[a_spec, b_spec] + s_in,
        out_specs=[pl.BlockSpec((tm, tn), lambda i, j, k: (i, j))] + s_out,
        out_shape=[jax.ShapeDtypeStruct((M, N), out_dtype)] + s_shapes,
        input_output_aliases={2 + i_: 1 + o_ for i_, o_ in s_alias.items()},
        scratch_shapes=([pltpu.VMEM((tm, tn), _F32)] if nk > 1 else []) + s_scratch,
        compiler_params=_cparams(("arbitrary",) * 3 if side is not None else ("parallel", "parallel", "arbitrary")),
    )(a, b, *s_ops)
    return res[0] if side is None else (res[0], list(res[1:]))


def _row_tile(T, C, budget_rows=256):
    return _tile(T, budget_rows, 16)


def ln_fwd(xs, coefs, g, b, *, want_r, name):
    T, C = xs[0].shape
    tr = _row_tile(T, C)
    n = len(xs)

    def body(*refs):
        x_refs, (g_ref, b_ref), outs = refs[:n], refs[n:n + 2], refs[n + 2:]
        r = coefs[0] * x_refs[0][...]
        for c, xr in zip(coefs[1:], x_refs[1:]):
            r = r + c * xr[...]
        mu = jnp.mean(r, axis=-1, keepdims=True)
        d = r - mu
        var = jnp.mean(d * d, axis=-1, keepdims=True)
        y = d * lax.rsqrt(var + LN_EPS) * g_ref[...] + b_ref[...]
        if want_r:
            outs[0][...] = r
        outs[-2][...] = y
        outs[-1][...] = y.astype(_BF)

    row = pl.BlockSpec((tr, C), lambda i: (i, 0))
    vec = pl.BlockSpec((1, C), lambda i: (0, 0))
    f = jax.ShapeDtypeStruct((T, C), _F32)
    out_shape = ([f] if want_r else []) + [f, jax.ShapeDtypeStruct((T, C), _BF)]
    return pl.pallas_call(
        body, name=name, grid=(T // tr,),
        in_specs=[row] * n + [vec, vec],
        out_specs=[row] * len(out_shape), out_shape=out_shape,
        compiler_params=_cparams(("parallel",)),
    )(*xs, g.reshape(1, C), b.reshape(1, C))


def ln_bwd(dys, coefs, r, g, *, name):
    T, C = r.shape
    tr = _row_tile(T, C)
    n = len(dys)

    def body(*refs):
        dy_refs, r_ref, g_ref = refs[:n], refs[n], refs[n + 1]
        dr_ref, drb_ref, dg_ref, db_ref = refs[n + 2:]
        dy = coefs[0] * dy_refs[0][...]
        for c, dr_ in zip(coefs[1:], dy_refs[1:]):
            dy = dy + c * dr_[...]
        rr = r_ref[...]
        mu = jnp.mean(rr, axis=-1, keepdims=True)
        d = rr - mu
        var = jnp.mean(d * d, axis=-1, keepdims=True)
        rstd = lax.rsqrt(var + LN_EPS)
        xh = d * rstd
        gdy = dy * g_ref[...]
        m1 = jnp.mean(gdy, axis=-1, keepdims=True)
        m2 = jnp.mean(gdy * xh, axis=-1, keepdims=True)
        dr = rstd * (gdy - m1 - xh * m2)
        dr_ref[...] = dr
        drb_ref[...] = dr.astype(_BF)

        @pl.when(pl.program_id(0) == 0)
        def _():
            dg_ref[...] = jnp.zeros_like(dg_ref)
            db_ref[...] = jnp.zeros_like(db_ref)

        dg_ref[...] += jnp.sum(dy * xh, axis=0, keepdims=True)
        db_ref[...] += jnp.sum(dy, axis=0, keepdims=True)

    row = pl.BlockSpec((tr, C), lambda i: (i, 0))
    vec = pl.BlockSpec((1, C), lambda i: (0, 0))
    return pl.pallas_call(
        body, name=name, grid=(T // tr,),
        in_specs=[row] * (n + 1) + [vec],
        out_specs=[row, row, vec, vec],
        out_shape=[jax.ShapeDtypeStruct((T, C), _F32), jax.ShapeDtypeStruct((T, C), _BF),
                   jax.ShapeDtypeStruct((1, C), _F32), jax.ShapeDtypeStruct((1, C), _F32)],
        compiler_params=_cparams(("arbitrary",)),
    )(*dys, r, g.reshape(1, C))


def rms_fwd(h, cb, W, g, *, name):
    T = h.shape[0]
    tr = _tile(T, 512, 16)

    def body(c_ref, g_ref, o_ref):
        c = c_ref[...]
        ms = jnp.mean(c * c, axis=-1, keepdims=True)
        o_ref[...] = (c * lax.rsqrt(ms + RMS_EPS) * g_ref[...]).astype(_BF)

    return pl.pallas_call(
        body, name=name, grid=(T // tr,),
        in_specs=[pl.BlockSpec((tr, W), lambda i: (i, cb)), pl.BlockSpec((1, W), lambda i: (0, 0))],
        out_specs=pl.BlockSpec((tr, W), lambda i: (i, 0)),
        out_shape=jax.ShapeDtypeStruct((T, W), _BF),
        compiler_params=_cparams(("parallel",)),
    )(h, g.reshape(1, W))


def rms_bwd(dy, h, cb, W, g, *, name):
    T = h.shape[0]
    tr = _tile(T, 512, 16)

    def body(dy_ref, c_ref, g_ref, dc_ref, dg_ref):
        c = c_ref[...]
        dyv = dy_ref[...]
        ms = jnp.mean(c * c, axis=-1, keepdims=True)
        r = lax.rsqrt(ms + RMS_EPS)
        u = dyv * g_ref[...]
        m = jnp.mean(c * u, axis=-1, keepdims=True)
        dc_ref[...] = (r * u - c * (r * r * r) * m).astype(_BF)

        @pl.when(pl.program_id(0) == 0)
        def _():
            dg_ref[...] = jnp.zeros_like(dg_ref)

        dg_ref[...] += jnp.sum(dyv * c * r, axis=0, keepdims=True)

    return pl.pallas_call(
        body, name=name, grid=(T // tr,),
        in_specs=[pl.BlockSpec((tr, W), lambda i: (i, 0)), pl.BlockSpec((tr, W), lambda i: (i, cb)),
                  pl.BlockSpec((1, W), lambda i: (0, 0))],
        out_specs=[pl.BlockSpec((tr, W), lambda i: (i, 0)), pl.BlockSpec((1, W), lambda i: (0, 0))],
        out_shape=[jax.ShapeDtypeStruct((T, W), _BF), jax.ShapeDtypeStruct((1, W), _F32)],
        compiler_params=_cparams(("arbitrary",)),
    )(dy, h, g.reshape(1, W))


def _rope(u, cc, sa, sb, sign):
    return u * cc + sign * (pltpu.roll(u, 96, 1) * sa + pltpu.roll(u, 32, 1) * sb)


def mla_pack(q, kv, h, kr_cb, cc, sa, sb, *, H, scale, name):
    T = q.shape[0]
    tr = _tile(T, 256, 16)

    def body(q_ref, kv_ref, kr_ref, cc_ref, sa_ref, sb_ref, qp_ref, kp_ref, v_ref):
        cc_, sa_, sb_ = cc_ref[...], sa_ref[...], sb_ref[...]
        kr = _rope(kr_ref[...], cc_, sa_, sb_, 1.0).astype(_BF)
        for hh in range(H):
            o = hh * HEAD_PAD
            qp_ref[:, o:o + 128] = (q_ref[:, o:o + 128] * scale).astype(_BF)
            qp_ref[:, o + 128:o + 256] = (_rope(q_ref[:, o + 128:o + 256], cc_, sa_, sb_, 1.0) * scale).astype(_BF)
            kp_ref[:, o:o + 128] = kv_ref[:, o:o + 128].astype(_BF)
            kp_ref[:, o + 128:o + 256] = kr
            v_ref[:, hh * 128:(hh + 1) * 128] = kv_ref[:, o + 128:o + 256].astype(_BF)

    wide = pl.BlockSpec((tr, H * HEAD_PAD), lambda i: (i, 0))
    tab = pl.BlockSpec((tr, 128), lambda i: (i, 0))
    return pl.pallas_call(
        body, name=name, grid=(T // tr,),
        in_specs=[wide, wide, pl.BlockSpec((tr, 128), lambda i: (i, kr_cb)), tab, tab, tab],
        out_specs=[wide, wide, pl.BlockSpec((tr, H * 128), lambda i: (i, 0))],
        out_shape=[jax.ShapeDtypeStruct((T, H * HEAD_PAD), _BF), jax.ShapeDtypeStruct((T, H * HEAD_PAD), _BF),
                   jax.ShapeDtypeStruct((T, H * 128), _BF)],
        compiler_params=_cparams(("parallel",)),
    )(q, kv, h, cc, sa, sb)


def mla_unpack(dqp, dkp, dv, cc, sa, sb, *, H, name):
    T = dqp.shape[0]
    tr = _tile(T, 256, 16)

    def body(dq_ref, dk_ref, dv_ref, cc_ref, sa_ref, sb_ref, oq_ref, okv_ref, okr_ref):
        cc_, sa_, sb_ = cc_ref[...], sa_ref[...], sb_ref[...]
        kr = jnp.zeros((tr, 128), _F32)
        for hh in range(H):
            o = hh * HEAD_PAD
            oq_ref[:, o:o + 128] = dq_ref[:, o:o + 128].astype(_BF)
            oq_ref[:, o + 128:o + 256] = _rope(dq_ref[:, o + 128:o + 256], cc_, sa_, sb_, -1.0).astype(_BF)
            okv_ref[:, o:o + 128] = dk_ref[:, o:o + 128].astype(_BF)
            okv_ref[:, o + 128:o + 256] = dv_ref[:, hh * 128:(hh + 1) * 128].astype(_BF)
            kr = kr + dk_ref[:, o + 128:o + 256]
        okr_ref[...] = _rope(kr, cc_, sa_, sb_, -1.0).astype(_BF)

    wide = pl.BlockSpec((tr, H * HEAD_PAD), lambda i: (i, 0))
    tab = pl.BlockSpec((tr, 128), lambda i: (i, 0))
    return pl.pallas_call(
        body, name=name, grid=(T // tr,),
        in_specs=[wide, wide, pl.BlockSpec((tr, H * 128), lambda i: (i, 0)), tab, tab, tab],
        out_specs=[wide, wide, tab],
        out_shape=[jax.ShapeDtypeStruct((T, H * HEAD_PAD), _BF), jax.ShapeDtypeStruct((T, H * HEAD_PAD), _BF),
                   jax.ShapeDtypeStruct((T, 128), _BF)],
        compiler_params=_cparams(("parallel",)),
    )(dqp, dkp, dv, cc, sa, sb)


_NEG = -1e30


def _rows(ref, j, t):
    return ref[pl.ds(pl.multiple_of(j * t, t), t), :]


_FLASH_ROWS = 32


def _chunk(r, base=0):
    return pl.ds(pl.multiple_of(base + r * _FLASH_ROWS, _FLASH_ROWS), _FLASH_ROWS)


def _qk(q, k):
    return lax.dot_general(q, k, (((1,), (1,)), ((), ())), preferred_element_type=_F32)


def _mask_chunk(s, r, t):
    row = r * _FLASH_ROWS + lax.broadcasted_iota(jnp.int32, (_FLASH_ROWS, t), 0)
    col = lax.broadcasted_iota(jnp.int32, (_FLASH_ROWS, t), 1)
    return jnp.where(col <= row, s, _NEG)


def flash_fwd(qp, kp, v, *, B, S, H, scale, side=None, name):
    T = B * S
    t = _tile(S, 512, 128)
    nq = S // t
    s_ops, s_in, s_out, s_shapes, s_scratch, s_alias = _side_specs(side)
    ni, no = len(s_ops), len(s_shapes)

    def body(q_ref, k_ref, v_ref, *rest):
        s_ins, (o_ref, lse_ref), s_outs = rest[:ni], rest[ni:ni + 2], rest[ni + 2:ni + 2 + no]
        (m_sc, l_sc, acc_sc, s_sc, p_sc), sems = rest[ni + 2 + no:ni + 7 + no], rest[ni + 7 + no:]
        i = pl.program_id(2)
        first = (pl.program_id(0) == 0) & (pl.program_id(1) == 0) & (i == 0)
        last = (pl.program_id(0) == B - 1) & (pl.program_id(1) == H - 1) & (i == nq - 1)
        if side is not None:
            @pl.when(first)
            def _():
                side.start(s_ins, s_outs, sems)

        m_sc[...] = jnp.full_like(m_sc, _NEG)
        l_sc[...] = jnp.zeros_like(l_sc)
        acc_sc[...] = jnp.zeros_like(acc_sc)

        def step(j, masked):
            s_sc[...] = _qk(q_ref[...], _rows(k_ref, j, t))

            @pl.loop(0, t // _FLASH_ROWS)
            def _(r):
                rows = _chunk(r)
                s = _mask_chunk(s_sc[rows, :], r, t) if masked else s_sc[rows, :]
                m_old = m_sc[rows, :]
                m_new = jnp.maximum(m_old, jnp.max(s, axis=-1, keepdims=True))
                p = jnp.exp(s - m_new)
                a = jnp.exp(m_old - m_new)
                l_sc[rows, :] = a * l_sc[rows, :] + jnp.sum(p, axis=-1, keepdims=True)
                m_sc[rows, :] = m_new
                acc_sc[rows, :] = a * acc_sc[rows, :]
                p_sc[rows, :] = p.astype(_BF)

            acc_sc[...] += jnp.dot(p_sc[...], _rows(v_ref, j, t), preferred_element_type=_F32)

        @pl.loop(0, i)
        def _(j):
            step(j, False)

        step(i, True)
        l = l_sc[...]
        o_ref[...] = acc_sc[...] / l
        lse_ref[...] = jnp.broadcast_to(m_sc[...] + jnp.log(l), lse_ref.shape)
        if side is not None:
            @pl.when(last)
            def _():
                side.wait(s_ins, s_outs, sems)

    qmap = lambda b, h, i: (b * nq + i, h)
    smap = lambda b, h, i: (b, h)
    res = pl.pallas_call(
        body, name=name, grid=(B, H, nq),
        in_specs=[pl.BlockSpec((t, HEAD_PAD), qmap), pl.BlockSpec((S, HEAD_PAD), smap), pl.BlockSpec((S, 128), smap)] + s_in,
        out_specs=[pl.BlockSpec((t, 128), qmap), pl.BlockSpec((t, 128), qmap)] + s_out,
        out_shape=[jax.ShapeDtypeStruct((T, H * 128), _F32), jax.ShapeDtypeStruct((T, H * 128), _F32)] + s_shapes,
        input_output_aliases={3 + i_: 2 + o_ for i_, o_ in s_alias.items()},
        scratch_shapes=[pltpu.VMEM((t, 1), _F32), pltpu.VMEM((t, 1), _F32), pltpu.VMEM((t, 128), _F32),
                        pltpu.VMEM((t, t), _F32), pltpu.VMEM((t, t), _BF)] + s_scratch,
        compiler_params=_cparams(("arbitrary",) * 3 if side is not None else ("parallel",) * 3),
    )(qp, kp, v, *s_ops)
    return (res[0], res[1]) if side is None else (res[0], res[1], list(res[2:]))


def flash_bwd_dq(qp, kp, v, o, lse, do, do_cb0, *, B, S, H, scale, name):
    T = B * S
    t = _tile(S, 512, 128)
    nq = S // t

    def body(q_ref, k_ref, v_ref, o_ref, lse_ref, do_ref, dq_ref, acc_sc, dl_sc, s_sc, dp_sc, ds_sc):
        i = pl.program_id(2)
        acc_sc[...] = jnp.zeros_like(acc_sc)
        dl_sc[...] = jnp.sum(do_ref[...].astype(_F32) * o_ref[...], axis=-1, keepdims=True)

        def step(j, masked):
            k = _rows(k_ref, j, t)
            s_sc[...] = _qk(q_ref[...], k)
            dp_sc[...] = _qk(do_ref[...].astype(_BF), _rows(v_ref, j, t))

            @pl.loop(0, t // _FLASH_ROWS)
            def _(r):
                rows = _chunk(r)
                s = _mask_chunk(s_sc[rows, :], r, t) if masked else s_sc[rows, :]
                p = jnp.exp(s - lse_ref[rows, 0:1])
                ds_sc[rows, :] = (p * (dp_sc[rows, :] - dl_sc[rows, :])).astype(_BF)

            acc_sc[...] += jnp.dot(ds_sc[...], k, preferred_element_type=_F32)

        @pl.loop(0, i)
        def _(j):
            step(j, False)

        step(i, True)
        dq_ref[...] = acc_sc[...] * scale

    qmap = lambda b, h, i: (b * nq + i, h)
    domap = lambda b, h, i: (b * nq + i, do_cb0 + h)
    smap = lambda b, h, i: (b, h)
    return pl.pallas_call(
        body, name=name, grid=(B, H, nq),
        in_specs=[pl.BlockSpec((t, HEAD_PAD), qmap), pl.BlockSpec((S, HEAD_PAD), smap), pl.BlockSpec((S, 128), smap),
                  pl.BlockSpec((t, 128), qmap), pl.BlockSpec((t, 128), qmap), pl.BlockSpec((t, 128), domap)],
        out_specs=pl.BlockSpec((t, HEAD_PAD), qmap),
        out_shape=jax.ShapeDtypeStruct((T, H * HEAD_PAD), _F32),
        scratch_shapes=[pltpu.VMEM((t, HEAD_PAD), _F32), pltpu.VMEM((t, 1), _F32), pltpu.VMEM((t, t), _F32),
                        pltpu.VMEM((t, t), _F32), pltpu.VMEM((t, t), _BF)],
        compiler_params=_cparams(("parallel", "parallel", "parallel")),
    )(qp, kp, v, o, lse, do)


def flash_bwd_dkv(qp, kp, v, o, lse, do, do_cb0, *, B, S, H, scale, name):
    T = B * S
    t = _tile(S, 512, 128)
    nk = S // t

    def body(q_ref, k_ref, v_ref, o_ref, lse_ref, do_ref, dk_ref, dv_ref, dk_sc, dv_sc, s_sc, dp_sc, p_sc, ds_sc):
        j = pl.program_id(2)
        dk_sc[...] = jnp.zeros_like(dk_sc)
        dv_sc[...] = jnp.zeros_like(dv_sc)

        def step(i, masked):
            q = _rows(q_ref, i, t)
            dob = _rows(do_ref, i, t).astype(_BF)
            s_sc[...] = _qk(q, k_ref[...])
            dp_sc[...] = _qk(dob, v_ref[...])

            @pl.loop(0, t // _FLASH_ROWS)
            def _(r):
                rows, seq = _chunk(r), _chunk(r, i * t)
                s = _mask_chunk(s_sc[rows, :], r, t) if masked else s_sc[rows, :]
                p = jnp.exp(s - lse_ref[seq, 0:1])
                dl = jnp.sum(do_ref[seq, :].astype(_F32) * o_ref[seq, :], axis=-1, keepdims=True)
                p_sc[rows, :] = p.astype(_BF)
                ds_sc[rows, :] = (p * (dp_sc[rows, :] - dl)).astype(_BF)

            tn = (((0,), (0,)), ((), ()))
            dv_sc[...] += lax.dot_general(p_sc[...], dob, tn, preferred_element_type=_F32)
            dk_sc[...] += lax.dot_general(ds_sc[...], q, tn, preferred_element_type=_F32)

        step(j, True)

        @pl.loop(j + 1, nk)
        def _(i):
            step(i, False)

        dk_ref[...] = dk_sc[...]
        dv_ref[...] = dv_sc[...]

    smap = lambda b, h, j: (b, h)
    domap = lambda b, h, j: (b, do_cb0 + h)
    kmap = lambda b, h, j: (b * nk + j, h)
    return pl.pallas_call(
        body, name=name, grid=(B, H, nk),
        in_specs=[pl.BlockSpec((S, HEAD_PAD), smap), pl.BlockSpec((t, HEAD_PAD), kmap), pl.BlockSpec((t, 128), kmap),
                  pl.BlockSpec((S, 128), smap), pl.BlockSpec((S, 128), smap), pl.BlockSpec((S, 128), domap)],
        out_specs=[pl.BlockSpec((t, HEAD_PAD), kmap), pl.BlockSpec((t, 128), kmap)],
        out_shape=[jax.ShapeDtypeStruct((T, H * HEAD_PAD), _F32), jax.ShapeDtypeStruct((T, H * 128), _F32)],
        scratch_shapes=[pltpu.VMEM((t, HEAD_PAD), _F32), pltpu.VMEM((t, 128), _F32), pltpu.VMEM((t, t), _F32),
                        pltpu.VMEM((t, t), _F32), pltpu.VMEM((t, t), _BF), pltpu.VMEM((t, t), _BF)],
        compiler_params=_cparams(("parallel", "parallel", "parallel")),
    )(qp, kp, v, o, lse, do)


def _halo_specs(T, nT, tt, hr, cw, cb):
    k = tt // hr
    main = pl.BlockSpec((tt, cw), lambda b, t: (b * nT + t, cb))
    prev = pl.BlockSpec((hr, cw), lambda b, t: (jnp.maximum((b * nT + t) * k - 1, 0), cb))
    nxt = pl.BlockSpec((hr, cw), lambda b, t: (jnp.minimum((b * nT + t + 1) * k, T // hr - 1), cb))
    return main, prev, nxt


def _ln_rows(z, g, b):
    mu = jnp.mean(z, axis=-1, keepdims=True)
    d = z - mu
    var = jnp.mean(d * d, axis=-1, keepdims=True)
    rstd = lax.rsqrt(var + LN_EPS)
    xh = d * rstd
    return xh * g + b, xh, rstd


def conv_fwd(h, cb_a, cb_g, w, bias, lng, lnb, *, B, S, name):
    T = B * S
    K, C = w.shape
    hr = 32
    assert K - 1 <= hr
    tt = _tile(S, 512, hr)
    nT = S // tt
    a_m, a_p, _ = _halo_specs(T, nT, tt, hr, C, cb_a)
    g_m, g_p, _ = _halo_specs(T, nT, tt, hr, C, cb_g)

    def body(a_ref, g_ref, ap_ref, gp_ref, w_ref, b_ref, lg_ref, lb_ref, z_ref, y_ref, buf):
        t = pl.program_id(1)
        buf[pl.ds(hr, tt), :] = a_ref[...] * jax.nn.sigmoid(g_ref[...])
        hp = ap_ref[...] * jax.nn.sigmoid(gp_ref[...])
        buf[pl.ds(0, hr), :] = jnp.where(t == 0, 0.0, hp)
        z = jnp.broadcast_to(b_ref[...], (tt, C))
        for k in range(K):
            z = z + w_ref[k:k + 1, :] * buf[pl.ds(hr - (K - 1) + k, tt), :]
        z_ref[...] = z
        n, _, _ = _ln_rows(z, lg_ref[...], lb_ref[...])
        y_ref[...] = (n * jax.nn.sigmoid(n)).astype(_BF)

    vec = pl.BlockSpec((1, C), lambda b, t: (0, 0))
    out = pl.BlockSpec((tt, C), lambda b, t: (b * nT + t, 0))
    return pl.pallas_call(
        body, name=name, grid=(B, nT),
        in_specs=[a_m, g_m, a_p, g_p, pl.BlockSpec((K, C), lambda b, t: (0, 0)), vec, vec, vec],
        out_specs=[out, out],
        out_shape=[jax.ShapeDtypeStruct((T, C), _F32), jax.ShapeDtypeStruct((T, C), _BF)],
        scratch_shapes=[pltpu.VMEM((hr + tt, C), _F32)],
        compiler_params=_cparams(("parallel", "parallel")),
    )(h, h, h, h, w, bias.reshape(1, C), lng.reshape(1, C), lnb.reshape(1, C))


def conv_bwd(dmix, cb_dy, z, h, cb_a, cb_g, w, lng, lnb, *, B, S, name):
    T = B * S
    K, C = w.shape
    hr = 32
    tt = _tile(S, 512, hr)
    nT = S // tt
    a_m, a_p, _ = _halo_specs(T, nT, tt, hr, C, cb_a)
    g_m, g_p, _ = _halo_specs(T, nT, tt, hr, C, cb_g)
    dy_m, _, dy_n = _halo_specs(T, nT, tt, hr, C, cb_dy)
    z_m, _, z_n = _halo_specs(T, nT, tt, hr, C, 0)

    def body(dy_ref, dyn_ref, z_ref, zn_ref, a_ref, g_ref, ap_ref, gp_ref, w_ref, lg_ref, lb_ref,
             da_ref, dg_ref, dw_ref, db_ref, dlg_ref, dlb_ref, bufz, bufh):
        b, t = pl.program_id(0), pl.program_id(1)
        lg, lb = lg_ref[...], lb_ref[...]

        def dz_of(dy, zz):
            n, xh, rstd = _ln_rows(zz, lg, lb)
            sg = jax.nn.sigmoid(n)
            dn = dy.astype(_F32) * (sg * (1.0 + n * (1.0 - sg)))
            gdn = dn * lg
            m1 = jnp.mean(gdn, axis=-1, keepdims=True)
            m2 = jnp.mean(gdn * xh, axis=-1, keepdims=True)
            return rstd * (gdn - m1 - xh * m2), dn, xh

        dz, dn, xh = dz_of(dy_ref[...], z_ref[...])
        dzn, _, _ = dz_of(dyn_ref[...], zn_ref[...])
        bufz[pl.ds(0, tt), :] = dz
        bufz[pl.ds(tt, hr), :] = jnp.where(t == nT - 1, 0.0, dzn)
        a, g = a_ref[...], g_ref[...]
        sg = jax.nn.sigmoid(g)
        bufh[pl.ds(hr, tt), :] = a * sg
        bufh[pl.ds(0, hr), :] = jnp.where(t == 0, 0.0, ap_ref[...] * jax.nn.sigmoid(gp_ref[...]))

        @pl.when((b == 0) & (t == 0))
        def _():
            dw_ref[...] = jnp.zeros_like(dw_ref)
            db_ref[...] = jnp.zeros_like(db_ref)
            dlg_ref[...] = jnp.zeros_like(dlg_ref)
            dlb_ref[...] = jnp.zeros_like(dlb_ref)

        dhc = jnp.zeros((tt, C), _F32)
        for k in range(K):
            dhc = dhc + w_ref[k:k + 1, :] * bufz[pl.ds(K - 1 - k, tt), :]
            dw_ref[k:k + 1, :] += jnp.sum(dz * bufh[pl.ds(hr - (K - 1) + k, tt), :], axis=0, keepdims=True)
        da_ref[...] = (dhc * sg).astype(_BF)
        dg_ref[...] = (dhc * a * sg * (1.0 - sg)).astype(_BF)
        db_ref[...] += jnp.sum(dz, axis=0, keepdims=True)
        dlg_ref[...] += jnp.sum(dn * xh, axis=0, keepdims=True)
        dlb_ref[...] += jnp.sum(dn, axis=0, keepdims=True)

    vec = pl.BlockSpec((1, C), lambda b, t: (0, 0))
    out = pl.BlockSpec((tt, C), lambda b, t: (b * nT + t, 0))
    kc = pl.BlockSpec((K, C), lambda b, t: (0, 0))
    return pl.pallas_call(
        body, name=name, grid=(B, nT),
        in_specs=[dy_m, dy_n, z_m, z_n, a_m, g_m, a_p, g_p, kc, vec, vec],
        out_specs=[out, out, kc, vec, vec, vec],
        out_shape=[jax.ShapeDtypeStruct((T, C), _BF), jax.ShapeDtypeStruct((T, C), _BF),
                   jax.ShapeDtypeStruct((K, C), _F32)] + [jax.ShapeDtypeStruct((1, C), _F32)] * 3,
        scratch_shapes=[pltpu.VMEM((tt + hr, C), _F32), pltpu.VMEM((hr + tt, C), _F32)],
        compiler_params=_cparams(("arbitrary", "arbitrary")),
    )(dmix, dmix, z, z, h, h, h, h, w, lng.reshape(1, C), lnb.reshape(1, C))


def _pool_cnt(t, tt, w, rows):
    pos = t * tt + lax.broadcasted_iota(jnp.int32, (rows, 1), 0)
    return jnp.minimum(pos + 1, w).astype(_F32)


def pool_fwd(h, cb, wp, scale, *, B, S, name):
    T = B * S
    G, pg, _ = wp.shape
    C = G * pg
    assert pg == 128 and G == len(POOL_WINDOWS)
    hr = 16
    tt = _tile(S, 512, hr)
    nT = S // tt
    u_m, u_p, _ = _halo_specs(T, nT, tt, hr, C, cb)

    def body(u_ref, up_ref, wp_ref, sc_ref, y_ref, buf):
        t = pl.program_id(1)
        buf[pl.ds(hr, tt), :] = u_ref[...]
        buf[pl.ds(0, hr), :] = jnp.where(t == 0, 0.0, up_ref[...])
        for gi, w in enumerate(POOL_WINDOWS):
            ln = slice(gi * pg, (gi + 1) * pg)
            acc = buf[pl.ds(hr, tt), ln]
            for j in range(1, w):
                acc = acc + buf[pl.ds(hr - j, tt), ln]
            d = acc / _pool_cnt(t, tt, w, tt) - u_ref[:, ln]
            yg = jnp.dot(d.astype(_BF), wp_ref[gi].astype(_BF), preferred_element_type=_F32)
            y_ref[:, ln] = (yg * sc_ref[:, ln]).astype(_BF)

    return pl.pallas_call(
        body, name=name, grid=(B, nT),
        in_specs=[u_m, u_p, pl.BlockSpec((G, pg, pg), lambda b, t: (0, 0, 0)), pl.BlockSpec((1, C), lambda b, t: (0, 0))],
        out_specs=pl.BlockSpec((tt, C), lambda b, t: (b * nT + t, 0)),
        out_shape=jax.ShapeDtypeStruct((T, C), _BF),
        scratch_shapes=[pltpu.VMEM((hr + tt, C), _F32)],
        compiler_params=_cparams(("parallel", "parallel")),
    )(h, h, wp, scale.reshape(1, C))


def pool_bwd(dmix, cb_dy, h, cb, wp, scale, *, B, S, name):
    T = B * S
    G, pg, _ = wp.shape
    C = G * pg
    hr = 16
    tt = _tile(S, 512, hr)
    nT = S // tt
    u_m, u_p, _ = _halo_specs(T, nT, tt, hr, C, cb)
    dy_m, _, dy_n = _halo_specs(T, nT, tt, hr, C, cb_dy)

    def body(dy_ref, dyn_ref, u_ref, up_ref, wp_ref, sc_ref, du_ref, dwp_ref, dsc_ref, buf, bufe):
        b, t = pl.program_id(0), pl.program_id(1)
        buf[pl.ds(hr, tt), :] = u_ref[...]
        buf[pl.ds(0, hr), :] = jnp.where(t == 0, 0.0, up_ref[...])

        @pl.when((b == 0) & (t == 0))
        def _():
            dwp_ref[...] = jnp.zeros_like(dwp_ref)
            dsc_ref[...] = jnp.zeros_like(dsc_ref)

        nt = (((1,), (1,)), ((), ()))
        tn = (((0,), (0,)), ((), ()))
        for gi, w in enumerate(POOL_WINDOWS):
            ln = slice(gi * pg, (gi + 1) * pg)
            wg = wp_ref[gi].astype(_BF)
            sc = sc_ref[:, ln]
            dy = dy_ref[:, ln].astype(_F32)
            dz = (dy * sc).astype(_BF)
            dzn = (dyn_ref[:, ln].astype(_F32) * sc).astype(_BF)
            dd = lax.dot_general(dz, wg, nt, preferred_element_type=_F32)
            ddn = lax.dot_general(dzn, wg, nt, preferred_element_type=_F32)
            bufe[pl.ds(0, tt), ln] = dd / _pool_cnt(t, tt, w, tt)
            bufe[pl.ds(tt, hr), ln] = jnp.where(t == nT - 1, 0.0, ddn / _pool_cnt(t + 1, tt, w, hr))
            du = -dd
            for j in range(w):
                du = du + bufe[pl.ds(j, tt), ln]
            du_ref[:, ln] = du.astype(_BF)
            acc = buf[pl.ds(hr, tt), ln]
            for j in range(1, w):
                acc = acc + buf[pl.ds(hr - j, tt), ln]
            d = (acc / _pool_cnt(t, tt, w, tt) - u_ref[:, ln]).astype(_BF)
            dwp_ref[gi] += lax.dot_general(d, dz, tn, preferred_element_type=_F32)
            yg = jnp.dot(d, wg, preferred_element_type=_F32)
            dsc_ref[:, ln] += jnp.sum(dy * yg, axis=0, keepdims=True)

    return pl.pallas_call(
        body, name=name, grid=(B, nT),
        in_specs=[dy_m, dy_n, u_m, u_p, pl.BlockSpec((G, pg, pg), lambda b, t: (0, 0, 0)),
                  pl.BlockSpec((1, C), lambda b, t: (0, 0))],
        out_specs=[pl.BlockSpec((tt, C), lambda b, t: (b * nT + t, 0)), pl.BlockSpec((G, pg, pg), lambda b, t: (0, 0, 0)),
                   pl.BlockSpec((1, C), lambda b, t: (0, 0))],
        out_shape=[jax.ShapeDtypeStruct((T, C), _BF), jax.ShapeDtypeStruct((G, pg, pg), _F32),
                   jax.ShapeDtypeStruct((1, C), _F32)],
        scratch_shapes=[pltpu.VMEM((hr + tt, C), _F32), pltpu.VMEM((tt + hr, C), _F32)],
        compiler_params=_cparams(("arbitrary", "arbitrary")),
    )(dmix, dmix, h, h, wp, scale.reshape(1, C))


_FFN_HR = 16
_FFN_ROWS, _FFN_LANES = 32, 256


def _fold8(x):
    out = x[0:8]
    for r in range(8, x.shape[0], 8):
        out = out + x[r:r + 8]
    return out


def _silu_grad(x, sg):
    return sg * (1.0 + x * (1.0 - sg))


def _conv3(buf, w_ref, b_ref, off, rows, ln):
    c = b_ref[:, ln] + w_ref[0:1, ln] * buf[pl.ds(off, rows), ln]
    for k in (1, 2):
        c = c + w_ref[k:k + 1, ln] * buf[pl.ds(off + k, rows), ln]
    return c


def _ffn_chunks(nrows, ncols):
    lw = min(ncols, _FFN_LANES)
    return [(r, min(_FFN_ROWS, nrows - r), slice(l0, l0 + lw))
            for l0 in range(0, ncols, lw) for r in range(0, nrows, _FFN_ROWS)]


def gate_fwd(up, w, bias, *, B, S, name):
    T, F2 = up.shape
    F = F2 // 2
    hr = _FFN_HR
    tt = _tile(S, 512, hr)
    nT = S // tt
    tn = _tile(F, 512, 128)
    nC = F // tn
    k = tt // hr

    def body(a_ref, g_ref, ap_ref, gp_ref, wa_ref, wg_ref, ba_ref, bg_ref, o_ref, bufa, bufg):
        t = pl.program_id(2)
        for buf, m_ref, p_ref in ((bufa, a_ref, ap_ref), (bufg, g_ref, gp_ref)):
            buf[pl.ds(hr, tt), :] = m_ref[...].astype(_F32)
            buf[pl.ds(0, hr), :] = jnp.where(t == 0, 0.0, p_ref[...].astype(_F32))
        for r, rc, ln in _ffn_chunks(tt, tn):
            ca = _conv3(bufa, wa_ref, ba_ref, hr - 2 + r, rc, ln)
            cg = _conv3(bufg, wg_ref, bg_ref, hr - 2 + r, rc, ln)
            o_ref[pl.ds(r, rc), ln] = (ca * cg * jax.nn.sigmoid(cg)).astype(_BF)

    def main(off):
        return pl.BlockSpec((tt, tn), lambda b, j, t: (b * nT + t, j + off))

    def prev(off):
        return pl.BlockSpec((hr, tn), lambda b, j, t: (jnp.maximum((b * nT + t) * k - 1, 0), j + off))

    def wspec(rows, off):
        return pl.BlockSpec((rows, tn), lambda b, j, t: (0, j + off))

    return pl.pallas_call(
        body, name=name, grid=(B, nC, nT),
        in_specs=[main(0), main(nC), prev(0), prev(nC), wspec(3, 0), wspec(3, nC), wspec(1, 0), wspec(1, nC)],
        out_specs=pl.BlockSpec((tt, tn), lambda b, j, t: (b * nT + t, j)),
        out_shape=jax.ShapeDtypeStruct((T, F), _BF),
        scratch_shapes=[pltpu.VMEM((hr + tt, tn), _F32)] * 2,
        compiler_params=_cparams(("parallel", "parallel", "parallel")),
    )(up, up, up, up, w, w, bias.reshape(1, F2), bias.reshape(1, F2))


def gate_bwd(up, dact, w, bias, *, B, S, name):
    T, F2 = up.shape
    F = F2 // 2
    hr = _FFN_HR
    tt = _tile(S, 512, hr)
    nT = S // tt
    tn = _tile(F, 512, 128)
    nC = F // tn
    k = tt // hr
    ext = tt + hr

    def body(a_ref, g_ref, ap_ref, gp_ref, an_ref, gn_ref, d_ref, dn_ref, wa_ref, wg_ref, ba_ref, bg_ref,
             dua_ref, dug_ref, dwa_ref, dwg_ref, dba_ref, dbg_ref, bufa, bufg, bufda, bufdg):
        b, t = pl.program_id(1), pl.program_id(2)
        last = t == nT - 1
        for buf, m_ref, p_ref, n_ref in ((bufa, a_ref, ap_ref, an_ref), (bufg, g_ref, gp_ref, gn_ref)):
            buf[pl.ds(hr, tt), :] = m_ref[...].astype(_F32)
            buf[pl.ds(0, hr), :] = jnp.where(t == 0, 0.0, p_ref[...].astype(_F32))
            buf[pl.ds(hr + tt, hr), :] = jnp.where(last, 0.0, n_ref[...].astype(_F32))
        for r, rc, ln in _ffn_chunks(ext, tn):
            ca = _conv3(bufa, wa_ref, ba_ref, hr - 2 + r, rc, ln)
            cg = _conv3(bufg, wg_ref, bg_ref, hr - 2 + r, rc, ln)
            sg = jax.nn.sigmoid(cg)
            if r < tt:
                da = d_ref[pl.ds(r, rc), ln].astype(_F32)
            else:
                da = jnp.where(last, 0.0, dn_ref[pl.ds(r - tt, rc), ln].astype(_F32))
            bufda[pl.ds(r, rc), ln] = da * cg * sg
            bufdg[pl.ds(r, rc), ln] = da * ca * _silu_grad(cg, sg)

        @pl.when((b == 0) & (t == 0))
        def _():
            for r in (dwa_ref, dwg_ref, dba_ref, dbg_ref):
                r[...] = jnp.zeros_like(r)

        lw, rc = min(tn, _FFN_LANES), _FFN_ROWS
        assert tt % rc == 0
        for bufd, buf, w_ref, du_ref, dw_ref, db_ref in ((bufda, bufa, wa_ref, dua_ref, dwa_ref, dba_ref),
                                                         (bufdg, bufg, wg_ref, dug_ref, dwg_ref, dbg_ref)):
            for l0 in range(0, tn, lw):
                ln = slice(l0, l0 + lw)
                acc = [jnp.zeros((8, lw), _F32) for _ in range(4)]
                for r in range(0, tt, rc):
                    dc = [bufd[pl.ds(r + s_, rc), ln] for s_ in range(3)]
                    u = buf[pl.ds(hr + r, rc), ln]
                    du = w_ref[2:3, ln] * dc[0] + w_ref[1:2, ln] * dc[1] + w_ref[0:1, ln] * dc[2]
                    du_ref[pl.ds(r, rc), ln] = du.astype(_BF)
                    for kk in range(3):
                        acc[kk] = acc[kk] + _fold8(dc[2 - kk] * u)
                    acc[3] = acc[3] + _fold8(dc[0])
                for kk in range(3):
                    dw_ref[kk:kk + 1, ln] += jnp.sum(acc[kk], axis=0, keepdims=True)
                db_ref[:, ln] += jnp.sum(acc[3], axis=0, keepdims=True)

    def main(off):
        return pl.BlockSpec((tt, tn), lambda j, b, t: (b * nT + t, j + off))

    def prev(off):
        return pl.BlockSpec((hr, tn), lambda j, b, t: (jnp.maximum((b * nT + t) * k - 1, 0), j + off))

    def nxt(off):
        return pl.BlockSpec((hr, tn), lambda j, b, t: (jnp.minimum((b * nT + t + 1) * k, T // hr - 1), j + off))

    def wspec(rows, off):
        return pl.BlockSpec((rows, tn), lambda j, b, t: (0, j + off))

    tf = jax.ShapeDtypeStruct((T, F), _BF)
    return pl.pallas_call(
        body, name=name, grid=(nC, B, nT),
        in_specs=[main(0), main(nC), prev(0), prev(nC), nxt(0), nxt(nC), main(0), nxt(0),
                  wspec(3, 0), wspec(3, nC), wspec(1, 0), wspec(1, nC)],
        out_specs=[main(0), main(0), wspec(3, 0), wspec(3, 0), wspec(1, 0), wspec(1, 0)],
        out_shape=[tf, tf, jax.ShapeDtypeStruct((3, F), _F32), jax.ShapeDtypeStruct((3, F), _F32),
                   jax.ShapeDtypeStruct((1, F), _F32), jax.ShapeDtypeStruct((1, F), _F32)],
        scratch_shapes=[pltpu.VMEM((hr + ext, tn), _F32)] * 2 + [pltpu.VMEM((ext, tn), _F32)] * 2,
        compiler_params=_cparams(("parallel", "arbitrary", "arbitrary")),
    )(up, up, up, up, up, up, dact, dact, w, w, bias.reshape(1, F2), bias.reshape(1, F2))


def loss_head(y, target, *, name):
    T, C = y.shape
    tr = _row_tile(T, C)

    def body(y_ref, t_ref, dy_ref, acc_ref):
        @pl.when(pl.program_id(0) == 0)
        def _():
            acc_ref[...] = jnp.zeros_like(acc_ref)

        e = y_ref[...] - t_ref[...]
        dy_ref[...] = e * (1.0 / C)
        acc_ref[...] += jnp.sum(e * e, axis=0, keepdims=True) * (0.5 / C)

    row = pl.BlockSpec((tr, C), lambda i: (i, 0))
    return pl.pallas_call(
        body, name=name, grid=(T // tr,),
        in_specs=[row, row], out_specs=[row, pl.BlockSpec((1, C), lambda i: (0, 0))],
        out_shape=[jax.ShapeDtypeStruct((T, C), _F32), jax.ShapeDtypeStruct((1, C), _F32)],
        compiler_params=_cparams(("arbitrary",)),
    )(y, target)


def adamw(w, g, m, v, *, name):
    R, C = w.shape
    tr = _tile(R, max(8, (256 * 1024) // C // 8 * 8), 8)
    c1 = 1.0 - ADAM_B1 ** ADAM_STEP
    c2 = 1.0 - ADAM_B2 ** ADAM_STEP

    def body(w_ref, g_ref, m_ref, v_ref, d_ref, mo_ref, vo_ref):
        gg = g_ref[...]
        mn = ADAM_B1 * m_ref[...] + (1.0 - ADAM_B1) * gg
        vn = ADAM_B2 * v_ref[...] + (1.0 - ADAM_B2) * (gg * gg)
        d_ref[...] = -ADAM_LR * ((mn / c1) / (jnp.sqrt(vn / c2) + ADAM_EPS) + ADAM_WD * w_ref[...])
        mo_ref[...] = mn
        vo_ref[...] = vn

    blk = pl.BlockSpec((tr, C), lambda i: (i, 0))
    s = jax.ShapeDtypeStruct((R, C), _F32)
    return pl.pallas_call(
        body, name=name, grid=(R // tr,),
        in_specs=[blk] * 4, out_specs=[blk] * 3, out_shape=[s, s, s],
        compiler_params=_cparams(("parallel",)),
    )(w, g, m, v)


_ANY = pl.BlockSpec(memory_space=pl.ANY)
_MESH = pl.DeviceIdType.MESH


def _place():
    return lax.axis_index("x"), lax.axis_index("y"), lax.axis_index("c")


def _other_chips(x, y):
    chips = [(1 - x, y), (x, 1 - y), (1 - x, 1 - y)]
    return chips, [2 * a + b for a, b in chips]


def _rcopy(src, dst, ssem, rsem, dev):
    return pltpu.make_async_remote_copy(src_ref=src, dst_ref=dst, send_sem=ssem, recv_sem=rsem,
                                        device_id=dev, device_id_type=_MESH)


def place_shard(w, l, chip_idx, *, name):
    _, _, hR, C = w.shape
    tr = _tile(hR, max(16, (512 * 1024) // C // 16 * 16), 16)

    def body(ci_ref, w_ref, o_ref):
        o_ref[...] = w_ref[...].astype(_BF)

    return pl.pallas_call(
        body, name=name,
        grid_spec=pltpu.PrefetchScalarGridSpec(
            num_scalar_prefetch=1, grid=(2, hR // tr),
            in_specs=[pl.BlockSpec((None, None, tr, C), lambda h, i, ci: (l, h, i, 0))],
            out_specs=pl.BlockSpec((None, None, tr, C), lambda h, i, ci: (ci[0], h, i, 0))),
        out_shape=jax.ShapeDtypeStruct((4, 2, hR, C), _BF),
        compiler_params=_cparams(("parallel", "parallel")),
    )(chip_idx, w)


class _Side:
    def __init__(self, arrays, out_shapes, aliases, n_sems, start, wait):
        self.arrays, self.out_shapes, self.aliases, self.n_sems = arrays, out_shapes, aliases, n_sems
        self.start, self.wait = start, wait


def gather_stage1_side(bufs):
    n = len(bufs)

    def copies(outs, sems, sending):
        x, y, c = _place()
        me = 2 * x + y
        chips, cidx = _other_chips(x, y)
        send, recv = sems
        out, back = [], []
        for k, chip in enumerate(chips):
            for p in range(n):
                mine, got = outs[p].at[me, c], outs[p].at[cidx[k], c]
                out.append(_rcopy(mine, mine, send.at[p * 3 + k], recv.at[p * 3 + k], (*chip, c)))
                if not sending:
                    back.append(_rcopy(got, got, send.at[p * 3 + k], recv.at[p * 3 + k], (*chip, c)))
        return out, back

    def start(ins, outs, sems):
        for cp in copies(outs, sems, True)[0]:
            cp.start()

    def wait(ins, outs, sems):
        out, back = copies(outs, sems, False)
        for cp in back:
            cp.wait_recv()
        for cp in out:
            cp.wait_send()

    return _Side(list(bufs), [jax.ShapeDtypeStruct(b.shape, b.dtype) for b in bufs], {p: p for p in range(n)},
                 [n * 3, n * 3], start, wait)


def gather_stage2_side(bufs):
    n = len(bufs)

    def copies(outs, sems, sending):
        x, y, c = _place()
        sib = (x, y, 1 - c)
        _, cidx = _other_chips(x, y)
        send, recv = sems
        out, back = [], []
        for k in range(3):
            for p in range(n):
                mine, got = outs[p].at[cidx[k], c], outs[p].at[cidx[k], 1 - c]
                out.append(_rcopy(mine, mine, send.at[p * 3 + k], recv.at[p * 3 + k], sib))
                if not sending:
                    back.append(_rcopy(got, got, send.at[p * 3 + k], recv.at[p * 3 + k], sib))
        return out, back

    def start(ins, outs, sems):
        for cp in copies(outs, sems, True)[0]:
            cp.start()

    def wait(ins, outs, sems):
        out, back = copies(outs, sems, False)
        for cp in back:
            cp.wait_recv()
        for cp in out:
            cp.wait_send()

    return _Side(list(bufs), [jax.ShapeDtypeStruct(b.shape, b.dtype) for b in bufs], {p: p for p in range(n)},
                 [n * 3, n * 3], start, wait)


def chip_exchange_side(ps):
    n = len(ps)

    def copies(ins, outs, sems, sending):
        x, y, c = _place()
        me = 2 * x + y
        chips, cidx = _other_chips(x, y)
        send, recv = sems
        out, back = [], []
        for k, chip in enumerate(chips):
            for p in range(n):
                got = outs[p].at[cidx[k]]
                out.append(_rcopy(ins[p].at[cidx[k]], outs[p].at[me], send.at[p * 3 + k], recv.at[p * 3 + k], (*chip, c)))
                if not sending:
                    back.append(_rcopy(got, got, send.at[p * 3 + k], recv.at[p * 3 + k], (*chip, c)))
        return out, back

    def start(ins, outs, sems):
        for cp in copies(ins, outs, sems, True)[0]:
            cp.start()

    def wait(ins, outs, sems):
        out, back = copies(ins, outs, sems, False)
        for cp in back:
            cp.wait_recv()
        for cp in out:
            cp.wait_send()

    return _Side(list(ps), [jax.ShapeDtypeStruct(a.shape, a.dtype) for a in ps], {}, [n * 3, n * 3], start, wait)


def _side_specs(side):
    if side is None:
        return [], [], [], [], [], {}
    return (side.arrays, [_ANY] * len(side.arrays), [_ANY] * len(side.out_shapes), side.out_shapes,
            [pltpu.SemaphoreType.DMA((k,)) for k in side.n_sems], side.aliases)


def run_side(side, *, name):
    ni, no = len(side.arrays), len(side.out_shapes)

    def body(*refs):
        ins, outs, sems = refs[:ni], refs[ni:ni + no], refs[ni + no:]
        side.start(ins, outs, sems)
        side.wait(ins, outs, sems)

    ops, in_specs, out_specs, out_shapes, scratch, aliases = _side_specs(side)
    return pl.pallas_call(body, name=name, in_specs=in_specs, out_specs=out_specs, out_shape=out_shapes,
                          input_output_aliases=aliases, scratch_shapes=scratch)(*ops)


def gather_small(small, *, name):
    ns = len(small)

    def body(*refs):
        s_in, s_out = refs[:ns], refs[ns:2 * ns]
        send, recv, lsem = refs[2 * ns:]
        x, y, c = _place()
        me = 2 * x + y
        chips, cidx = _other_chips(x, y)
        local = [pltpu.make_async_copy(s_in[q], s_out[q].at[me], lsem.at[q]) for q in range(ns)]
        cps = [_rcopy(s_in[q], s_out[q].at[me], send.at[q * 3 + k], recv.at[q * 3 + k], (*chip, c))
               for k, chip in enumerate(chips) for q in range(ns)]
        for cp in local + cps:
            cp.start()
        for k in range(3):
            for q in range(ns):
                got = s_out[q].at[cidx[k]]
                _rcopy(got, got, send.at[q * 3 + k], recv.at[q * 3 + k], (x, y, c)).wait_recv()
        for cp in cps:
            cp.wait_send()
        for cp in local:
            cp.wait()

    return pl.pallas_call(
        body, name=name, in_specs=[_ANY] * ns, out_specs=[_ANY] * ns,
        out_shape=[jax.ShapeDtypeStruct((4,) + a.shape, a.dtype) for a in small],
        scratch_shapes=[pltpu.SemaphoreType.DMA((ns * 3,))] * 2 + [pltpu.SemaphoreType.DMA((ns,))],
    )(*small)


def sibling_send_half(gs, *, name):
    n = len(gs)

    def body(*refs):
        g_in, g_out, send, recv = refs[:n], refs[n:2 * n], refs[2 * n], refs[2 * n + 1]
        x, y, c = _place()
        sib = (x, y, 1 - c)
        cps = [_rcopy(g_in[p].at[1 - c], g_out[p], send.at[p], recv.at[p], sib) for p in range(n)]
        for cp in cps:
            cp.start()
        for cp in cps:
            cp.wait()

    return pl.pallas_call(
        body, name=name, in_specs=[_ANY] * n, out_specs=[_ANY] * n,
        out_shape=[jax.ShapeDtypeStruct(a.shape[1:], a.dtype) for a in gs],
        scratch_shapes=[pltpu.SemaphoreType.DMA((n,))] * 2,
    )(*gs)


def sum_chips(p, slots, idx, *, name):
    _, N, C = p.shape
    tr = _tile(N, max(16, (512 * 1024) // C // 16 * 16), 16)

    def body(i0, i1, i2, i3, i4, p_ref, s0_ref, s1_ref, s2_ref, o_ref):
        s = p_ref[...].astype(_F32)
        for r in (s0_ref, s1_ref, s2_ref):
            s = s + r[...].astype(_F32)
        o_ref[...] = s

    def at(k):
        return pl.BlockSpec((None, tr, C), lambda i, *ix: (ix[k][0], i, 0))

    return pl.pallas_call(
        body, name=name,
        grid_spec=pltpu.PrefetchScalarGridSpec(
            num_scalar_prefetch=5, grid=(N // tr,),
            in_specs=[at(0), at(1), at(2), at(3)], out_specs=at(4)),
        out_shape=jax.ShapeDtypeStruct((2, N, C), _F32),
        compiler_params=_cparams(("parallel",)),
    )(*idx, p, slots, slots, slots)


def sibling_join(rs, *, name):
    n = len(rs)

    def body(*refs):
        r_out, send, recv = refs[n:2 * n], refs[2 * n], refs[2 * n + 1]
        x, y, c = _place()
        sib = (x, y, 1 - c)
        cps = [_rcopy(r_out[p].at[c], r_out[p].at[c], send.at[p], recv.at[p], sib) for p in range(n)]
        for cp in cps:
            cp.start()
        for p in range(n):
            got = r_out[p].at[1 - c]
            _rcopy(got, got, send.at[p], recv.at[p], sib).wait_recv()
        for cp in cps:
            cp.wait_send()

    return pl.pallas_call(
        body, name=name, in_specs=[_ANY] * n, out_specs=[_ANY] * n,
        out_shape=[jax.ShapeDtypeStruct(a.shape, a.dtype) for a in rs],
        input_output_aliases={p: p for p in range(n)},
        scratch_shapes=[pltpu.SemaphoreType.DMA((n,))] * 2,
    )(*rs)


def all_devices_exchange(v, *, name):
    def body(v_ref, o_ref, send, recv, lsem):
        x, y, c = _place()
        me = 4 * x + 2 * y + c
        local = pltpu.make_async_copy(v_ref, o_ref.at[me], lsem)
        local.start()
        peers = []
        for k in range(1, 8):
            px, py, pc = x ^ (k >> 2), y ^ ((k >> 1) & 1), c ^ (k & 1)
            peers.append((px, py, pc))
        cps = [_rcopy(v_ref, o_ref.at[me], send.at[k], recv.at[k], peer) for k, peer in enumerate(peers)]
        for cp in cps:
            cp.start()
        for k, (px, py, pc) in enumerate(peers):
            got = o_ref.at[4 * px + 2 * py + pc]
            _rcopy(got, got, send.at[k], recv.at[k], (x, y, c)).wait_recv()
        for cp in cps:
            cp.wait_send()
        local.wait()

    return pl.pallas_call(
        body, name=name, in_specs=[_ANY], out_specs=_ANY,
        out_shape=jax.ShapeDtypeStruct((8,) + v.shape, v.dtype),
        scratch_shapes=[pltpu.SemaphoreType.DMA((7,))] * 2 + [pltpu.SemaphoreType.DMA(())],
    )(v)


def add_halves(gs_and_rs, c_idx, *, name):
    outs = []
    for n_, (g, r) in enumerate(gs_and_rs):
        N, C = r.shape
        tr = _tile(N, max(16, (512 * 1024) // C // 16 * 16), 16)

        def body(c_ref, g_ref, r_ref, o_ref):
            o_ref[...] = (g_ref[...].astype(_F32) + r_ref[...].astype(_F32)).astype(o_ref.dtype)

        outs.append(pl.pallas_call(
            body, name=f"{name}_{n_}",
            grid_spec=pltpu.PrefetchScalarGridSpec(
                num_scalar_prefetch=1, grid=(N // tr,),
                in_specs=[pl.BlockSpec((None, tr, C), lambda i, c: (c[0], i, 0)), pl.BlockSpec((tr, C), lambda i, c: (i, 0))],
                out_specs=pl.BlockSpec((tr, C), lambda i, c: (i, 0))),
            out_shape=jax.ShapeDtypeStruct((N, C), r.dtype),
            compiler_params=_cparams(("parallel",)),
        )(c_idx, g, r))
    return outs


def sum_slots(a, *, name):
    n, N, C = a.shape
    tr = _tile(N, max(16, (512 * 1024) // C // 16 * 16), 16)

    def body(a_ref, o_ref):
        s = a_ref[0].astype(_F32)
        for k in range(1, n):
            s = s + a_ref[k].astype(_F32)
        o_ref[...] = s

    return pl.pallas_call(
        body, name=name, grid=(N // tr,),
        in_specs=[pl.BlockSpec((n, tr, C), lambda i: (0, i, 0))],
        out_specs=pl.BlockSpec((tr, C), lambda i: (i, 0)),
        out_shape=jax.ShapeDtypeStruct((N, C), _F32),
        compiler_params=_cparams(("parallel",)),
    )(a)


_WEIGHTS = ['ln_in_g', 'ln_in_b', 'w_in', 'q_norm_g', 'w_uq', 'kv_norm_g', 'w_ukv', 'conv_w', 'conv_b', 'conv_ln_g',
            'conv_ln_b', 'w_pool', 'pool_scale', 'w_out', 'ln1_g', 'ln1_b', 'w_up', 'ffn_conv_w', 'ffn_conv_b', 'w_down',
            'ln2_g', 'ln2_b']
_BIG = ['w_in', 'w_uq', 'w_ukv', 'w_out', 'w_up', 'w_down']
_SMALL_SHARDED = ['conv_w', 'ffn_conv_w']
_SMALL = [n for n in _WEIGHTS if n not in _BIG]


def _rope_tables(positions):
    half = QK_ROPE // 2
    inv = 1.0 / (ROPE_THETA ** (jnp.arange(0, QK_ROPE, 2, dtype=_F32) / QK_ROPE))
    ang = positions.reshape(-1).astype(_F32)[:, None] * inv
    c, s = jnp.cos(ang), jnp.sin(ang)
    z = jnp.zeros_like(c)
    cc = jnp.concatenate([c, c, z, z], axis=1)
    sa = jnp.concatenate([-s, z, z, z], axis=1)
    sb = jnp.concatenate([z, s, z, z], axis=1)
    assert cc.shape[1] == 128 and half == 32
    return cc, sa, sb


def _in_pad(dims):
    D, QL, KVL, CW, PW, H, F = dims
    return (-(QL + 2 * CW + PW + KVL + 128)) % 512


def _layer_weights(full, dims):
    D, QL, KVL, CW, PW, H, F = dims
    w_in = full['w_in'].transpose(1, 0, 2).reshape(D, -1)
    o1, o2, o3, o4 = QL, QL + KVL, QL + KVL + QK_ROPE, QL + KVL + QK_ROPE + 2 * CW
    w_in_p = jnp.concatenate([w_in[:, :o1], w_in[:, o3:o4], w_in[:, o4:], w_in[:, o1:o2], w_in[:, o2:o3],
                              jnp.zeros((D, 128 - QK_ROPE + _in_pad(dims)), w_in.dtype)], axis=1)
    w_uq = full['w_uq'].reshape(QL, H, QK_NOPE + QK_ROPE)
    w_uq_p = jnp.pad(w_uq, ((0, 0), (0, 0), (0, HEAD_PAD - QK_NOPE - QK_ROPE))).reshape(QL, H * HEAD_PAD)
    return dict(
        w_in=w_in_p, w_uq=w_uq_p,
        w_ukv=full['w_ukv'].reshape(KVL, H * (QK_NOPE + V_HEAD)),
        w_out=full['w_out'].reshape(D, D),
        w_up=full['w_up'].transpose(1, 0, 2).reshape(D, 2 * F),
        w_down=full['w_down'].reshape(F, D),
    )


def _unpermute_w_in_grad(g, dims):
    D, QL, KVL, CW, PW, H, F = dims
    a, b_, c_ = QL, QL + 2 * CW, QL + 2 * CW + PW
    return jnp.concatenate([g[:, :a], g[:, c_:c_ + KVL], g[:, c_ + KVL:c_ + KVL + QK_ROPE], g[:, a:b_], g[:, b_:c_]], axis=1)


class _NoExchange:
    def __init__(self, layer_weights):
        self.layer_weights, self.grads = layer_weights, {}

    def weights(self, l):
        return self.layer_weights[l]

    def side(self, where, l):
        return None

    def side_done(self, where, l, outs):
        pass

    def grads_ready(self, l, g):
        self.grads[l] = g


def _with_side(hooks, where, l, fn):
    sd = hooks.side(where, l)
    res = fn(sd)
    if sd is None:
        return res
    hooks.side_done(where, l, res[-1])
    return res[0] if len(res) == 2 else res[:-1]


def _local_step(x, positions, target, small, dims, B, S, L, hooks):
    D, QL, KVL, CW, PW, H, F = dims
    T = B * S
    alpha = (2.0 * L) ** 0.25
    scale = float(QK_NOPE + QK_ROPE) ** -0.5
    cc, sa, sb = _rope_tables(positions)
    cb_q, cb_a, cb_g, cb_p = 0, QL // CW, QL // CW + 1, (QL + 2 * CW) // PW
    cb_kv, cb_kr = (QL + 2 * CW + PW) // KVL, (QL + 2 * CW + PW + KVL) // 128
    assert QL % CW == 0 and (QL + 2 * CW) % PW == 0 and (QL + 2 * CW + PW) % KVL == 0 and (QL + 2 * CW + PW + KVL) % 128 == 0

    xs, xb = ln_fwd([x], [1.0], small['ln_in_g'], small['ln_in_b'], want_r=False, name="ln_in")
    saved = []
    fa = dict(B=B, S=S, H=H, scale=scale)
    for l in range(L):
        W = hooks.weights(l)
        h = matmul(xb, W['w_in'], name="mm_in")
        qn = rms_fwd(h, cb_q, QL, small['q_norm_g'][l], name="rms_q")
        kvn = rms_fwd(h, cb_kv, KVL, small['kv_norm_g'][l], name="rms_kv")
        q = matmul(qn, W['w_uq'], name="mm_uq")
        kv = matmul(kvn, W['w_ukv'], name="mm_ukv")
        qp, kp, v = mla_pack(q, kv, h, cb_kr, cc, sa, sb, H=H, scale=scale, name="mla_pack")
        o, lse = _with_side(hooks, 'flash', l, lambda sd: flash_fwd(qp, kp, v, side=sd, name="flash_fwd", **fa))
        z, yc = conv_fwd(h, cb_a, cb_g, small['conv_w'][l], small['conv_b'][l], small['conv_ln_g'][l],
                         small['conv_ln_b'][l], B=B, S=S, name="conv_fwd")
        yp = pool_fwd(h, cb_p, small['w_pool'][l], small['pool_scale'][l], B=B, S=S, name="pool_fwd")
        mixed = jnp.concatenate([o.astype(_BF), yc, yp], axis=1)
        y1 = matmul(mixed, W['w_out'], name="mm_out")
        r1, x1, x1b = ln_fwd([xs, y1], [alpha, 1.0], small['ln1_g'][l], small['ln1_b'][l], want_r=True, name="ln1")
        up = _with_side(hooks, 'up', l, lambda sd: matmul(x1b, W['w_up'], out_dtype=_BF, side=sd, name="mm_up"))
        act = gate_fwd(up, small['ffn_conv_w'][l], small['ffn_conv_b'][l], B=B, S=S, name="gate_fwd")
        y2 = _with_side(hooks, 'down', l, lambda sd: matmul(act, W['w_down'], side=sd, name="mm_down"))
        r2, x2, x2b = ln_fwd([x1, y2], [alpha, 1.0], small['ln2_g'][l], small['ln2_b'][l], want_r=True, name="ln2")
        saved.append(dict(W=W, xb=xb, h=h, qn=qn, kvn=kvn, qp=qp, kp=kp, v=v, o=o, lse=lse, z=z, mixed=mixed, r1=r1,
                          x1b=x1b, up=up, act=act, r2=r2))
        xs, xb = x2, x2b

    dy, loss_cols = loss_head(xs, target, name="loss_head")
    gs = {n: [None] * L for n in _SMALL if n not in ('ln_in_g', 'ln_in_b')}
    d_terms, d_coefs = [dy], [1.0]
    zpad = jnp.zeros((T, _in_pad(dims)), _BF) if _in_pad(dims) else None
    for l in reversed(range(L)):
        sv = saved[l]
        W = sv['W']
        gb = {}
        dr2, dr2b, gs['ln2_g'][l], gs['ln2_b'][l] = ln_bwd(d_terms, d_coefs, sv['r2'], small['ln2_g'][l], name="ln2_bwd")
        dact = matmul(dr2b, W['w_down'], tb=True, out_dtype=_BF, name="mm_down_dx")
        gb['w_down'] = matmul(sv['act'], dr2b, ta=True, out_dtype=_BF, name="mm_down_dw")
        dua, dug, dwa, dwg, dba, dbg = gate_bwd(sv['up'], dact, small['ffn_conv_w'][l], small['ffn_conv_b'][l],
                                                B=B, S=S, name="gate_bwd")
        gs['ffn_conv_w'][l] = jnp.concatenate([dwa, dwg], axis=1)
        gs['ffn_conv_b'][l] = jnp.concatenate([dba, dbg], axis=1)
        dup = jnp.concatenate([dua, dug], axis=1)
        gb['w_up'] = _with_side(hooks, 'up_dw', l, lambda sd: matmul(sv['x1b'], dup, ta=True, out_dtype=_BF, side=sd,
                                                                     name="mm_up_dw"))
        dx1 = _with_side(hooks, 'up_dx', l, lambda sd: matmul(dup, W['w_up'], tb=True, side=sd, name="mm_up_dx"))
        dr1, dr1b, gs['ln1_g'][l], gs['ln1_b'][l] = ln_bwd([dr2, dx1], [alpha, 1.0], sv['r1'], small['ln1_g'][l],
                                                            name="ln1_bwd")
        dmix = matmul(dr1b, W['w_out'], tb=True, name="mm_out_dx")
        gb['w_out'] = matmul(sv['mixed'], dr1b, ta=True, out_dtype=_BF, name="mm_out_dw")
        h = sv['h']
        ncb = (H * V_HEAD) // CW
        dca, dcg, gs['conv_w'][l], gs['conv_b'][l], gs['conv_ln_g'][l], gs['conv_ln_b'][l] = conv_bwd(
            dmix, ncb, sv['z'], h, cb_a, cb_g, small['conv_w'][l], small['conv_ln_g'][l], small['conv_ln_b'][l],
            B=B, S=S, name="conv_bwd")
        dpool, gs['w_pool'][l], gs['pool_scale'][l] = pool_bwd(
            dmix, (H * V_HEAD + CW) // PW, h, cb_p, small['w_pool'][l], small['pool_scale'][l], B=B, S=S, name="pool_bwd")
        dqp = flash_bwd_dq(sv['qp'], sv['kp'], sv['v'], sv['o'], sv['lse'], dmix, 0, name="flash_dq", **fa)
        dkp, dv = flash_bwd_dkv(sv['qp'], sv['kp'], sv['v'], sv['o'], sv['lse'], dmix, 0, name="flash_dkv", **fa)
        dq, dkv, dkr = mla_unpack(dqp, dkp, dv, cc, sa, sb, H=H, name="mla_unpack")
        dqn = matmul(dq, W['w_uq'], tb=True, name="mm_uq_dx")
        g_uq = matmul(sv['qn'], dq, ta=True, out_dtype=_BF, name="mm_uq_dw")
        gb['w_uq'] = g_uq.reshape(QL, H, HEAD_PAD)[:, :, :QK_NOPE + QK_ROPE].reshape(QL, -1)
        dkvn = matmul(dkv, W['w_ukv'], tb=True, name="mm_ukv_dx")
        gb['w_ukv'] = matmul(sv['kvn'], dkv, ta=True, out_dtype=_BF, name="mm_ukv_dw")
        dcq, gs['q_norm_g'][l] = rms_bwd(dqn, h, cb_q, QL, small['q_norm_g'][l], name="rms_q_bwd")
        dckv, gs['kv_norm_g'][l] = rms_bwd(dkvn, h, cb_kv, KVL, small['kv_norm_g'][l], name="rms_kv_bwd")
        dh = jnp.concatenate([dcq, dca, dcg, dpool, dckv, dkr] + ([zpad] if zpad is not None else []), axis=1)
        gb['w_in'] = _unpermute_w_in_grad(matmul(sv['xb'], dh, ta=True, out_dtype=_BF, name="mm_in_dw"), dims)
        dxm = matmul(dh, W['w_in'], tb=True, name="mm_in_dx")
        hooks.grads_ready(l, gb)
        d_terms, d_coefs = [dr1, dxm], [alpha, 1.0]
    gx, _, g_ln_g, g_ln_b = ln_bwd(d_terms, d_coefs, x, small['ln_in_g'], name="ln_in_bwd")
    gsm = {n: jnp.stack([a.reshape(small[n].shape[1:]) for a in gs[n]]) for n in gs}
    gsm['ln_in_g'], gsm['ln_in_b'] = g_ln_g.reshape(-1), g_ln_b.reshape(-1)
    return loss_cols, gx, gsm


_COL_SHARDED = ('w_in', 'w_up')


def _flat_pad(arrs, mult=512 * 128):
    v = jnp.concatenate([a.reshape(-1) for a in arrs])
    n = v.shape[0]
    return jnp.pad(v, (0, (-n) % mult)).reshape(-1, 128)


def _split_like(flat, like):
    out, off = [], 0
    v = flat.reshape(-1)
    for a in like:
        out.append(v[off:off + a.size].reshape(a.shape))
        off += a.size
    return out


_GROUP_A = ('w_up',)
_GROUP_B = tuple(n for n in _BIG if n not in _GROUP_A)


class _Exchange:
    def __init__(self, a, dims, L, chip_idx, c_idx, sum_idx):
        self.dims, self.L, self.c_idx, self.sum_idx = dims, L, c_idx, sum_idx
        self.rows = {n: (a[n].shape[1], math.prod(a[n].shape[2:])) for n in _BIG}
        self.bufs = []
        for l in range(L):
            self.bufs.append({n: place_shard(a[n].reshape(L, 2, self.rows[n][0] // 2, self.rows[n][1]), l, chip_idx,
                                             name="place_" + n) for n in _BIG})
        self._store(0, _BIG, run_side(gather_stage1_side(self._list(0, _BIG)), name="gather_first_ici"))
        self._store(0, _BIG, run_side(gather_stage2_side(self._list(0, _BIG)), name="gather_first_d2d"))
        self.pending = None
        self.reduced = {}

    def _list(self, l, names):
        return [self.bufs[l][n] for n in names]

    def _store(self, l, names, outs):
        self.bufs[l].update(zip(names, outs))

    def weights(self, l):
        return _layer_weights({n: self.bufs[l][n].reshape(4, *self.rows[n]) for n in _BIG}, self.dims)

    def side(self, where, l):
        if where in ('flash', 'up', 'down'):
            if l + 1 >= self.L:
                return None
            if where == 'down':
                return gather_stage2_side(self._list(l + 1, _BIG))
            return gather_stage1_side(self._list(l + 1, _GROUP_B if where == 'flash' else _GROUP_A))
        if self.pending is None:
            return None
        return chip_exchange_side([self.pending[1][n] for n in (_GROUP_B if where == 'up_dw' else _GROUP_A)])

    def side_done(self, where, l, outs):
        if where in ('flash', 'up', 'down'):
            self._store(l + 1, {'flash': _GROUP_B, 'up': _GROUP_A, 'down': _BIG}[where], outs)
            return
        self.pending[2].update(zip(_GROUP_B if where == 'up_dw' else _GROUP_A, outs))
        if where == 'up_dx':
            self._finish()

    def _finish(self):
        l, sums, slots = self.pending
        halves = [sum_chips(sums[n], slots[n], self.sum_idx, name="grad_sum_" + n) for n in _BIG]
        joined = sibling_join(halves, name="grad_sibling_join")
        self.reduced[l] = {n: j.reshape(self.rows[n]) for n, j in zip(_BIG, joined)}
        self.pending = None

    def grads_ready(self, l, g):
        g_in = []
        for n in _BIG:
            r, c = self.rows[n]
            st = g[n].reshape(g[n].shape[0], 4, c).transpose(1, 0, 2) if n in _COL_SHARDED else g[n].reshape(4, r, c)
            g_in.append(st.reshape(4, 2, r // 2, c).transpose(1, 0, 2, 3).reshape(2, 2 * r, c))
        from_sib = sibling_send_half(g_in, name="grad_sibling_send")
        sums = add_halves(list(zip(g_in, from_sib)), self.c_idx, name="grad_presum")
        self.pending = (l, {n: p.reshape(4, p.shape[0] // 4, p.shape[1]) for n, p in zip(_BIG, sums)}, {})
        if l == 0:
            outs = run_side(chip_exchange_side([self.pending[1][n] for n in _BIG]), name="grad_chip_exchange_last")
            self.pending[2].update(zip(_BIG, outs))
            self._finish()


def kernel(x, positions, ln_in_g, ln_in_b, w_in, q_norm_g, w_uq, kv_norm_g, w_ukv, conv_w, conv_b, conv_ln_g, conv_ln_b, w_pool, pool_scale, w_out, ln1_g, ln1_b, w_up, ffn_conv_w, ffn_conv_b, w_down, ln2_g, ln2_b, loss_target, m_ln_in_g, m_ln_in_b, m_w_in, m_q_norm_g, m_w_uq, m_kv_norm_g, m_w_ukv, m_conv_w, m_conv_b, m_conv_ln_g, m_conv_ln_b, m_w_pool, m_pool_scale, m_w_out, m_ln1_g, m_ln1_b, m_w_up, m_ffn_conv_w, m_ffn_conv_b, m_w_down, m_ln2_g, m_ln2_b, v_ln_in_g, v_ln_in_b, v_w_in, v_q_norm_g, v_w_uq, v_kv_norm_g, v_w_ukv, v_conv_w, v_conv_b, v_conv_ln_g, v_conv_ln_b, v_w_pool, v_pool_scale, v_w_out, v_ln1_g, v_ln1_b, v_w_up, v_ffn_conv_w, v_ffn_conv_b, v_w_down, v_ln2_g, v_ln2_b):
    a = dict(locals())
    B, S, D = a['x'].shape
    T = B * S
    L = a['w_in'].shape[0]
    QL, H = 4 * a['w_uq'].shape[1], a['w_uq'].shape[2]
    KVL = 4 * a['w_ukv'].shape[1]
    CW, PW = a['conv_b'].shape[1], a['pool_scale'].shape[1]
    F = 4 * a['w_down'].shape[1]
    dims = (D, QL, KVL, CW, PW, H, F)
    chip = 2 * lax.axis_index("x") + lax.axis_index("y")
    c_idx = lax.axis_index("c").astype(jnp.int32).reshape(1)

    def shard2d(w):
        return w.reshape(-1, w.shape[-1]) if w.ndim == 3 else w.reshape(w.shape[0] * w.shape[1], -1)

    small = {n: a[n] for n in _SMALL}
    for n, o in zip(_SMALL_SHARDED, gather_small([shard2d(a[n]) for n in _SMALL_SHARDED], name="gather_small")):
        k = a[n].shape[1]
        small[n] = o.reshape(4, L, k, -1).transpose(1, 2, 0, 3).reshape(L, k, -1)
    xi, yi = lax.axis_index("x"), lax.axis_index("y")
    chip_idx = chip.astype(jnp.int32).reshape(1)
    sum_idx = [v.astype(jnp.int32).reshape(1) for v in [chip] + _other_chips(xi, yi)[1] + [lax.axis_index("c")]]
    ex = _Exchange(a, dims, L, chip_idx, c_idx, sum_idx)

    loss_cols, gx, gsm = _local_step(a['x'].reshape(T, D), a['positions'], a['loss_target'].reshape(T, D),
                                     small, dims, B, S, L, ex)
    loss = lax.psum(jnp.sum(loss_cols), ("x", "y", "c"))
    g_big = {n: jnp.concatenate([ex.reduced[l][n] for l in range(L)]).reshape(a[n].shape) for n in _BIG}

    sm_like = [gsm[n] for n in _SMALL]
    sm_sum = sum_slots(all_devices_exchange(_flat_pad(sm_like), name="small_exchange"), name="small_sum")
    g_small = dict(zip(_SMALL, _split_like(sm_sum, sm_like)))
    for n in _SMALL_SHARDED:
        w = a[n].shape[-1]
        g_small[n] = lax.dynamic_slice_in_dim(g_small[n], chip * w, w, axis=2)

    grads, delta, new_m, new_v = {}, {}, {}, {}
    for n in _BIG:
        grads[n] = g_big[n]
        d_, m_, v_ = adamw(shard2d(a[n]), shard2d(g_big[n]), shard2d(a['m_' + n]), shard2d(a['v_' + n]), name="adamw_" + n)
        delta[n], new_m[n], new_v[n] = (t.reshape(a[n].shape) for t in (d_, m_, v_))
    like = [a[n] for n in _SMALL]
    d_, m_, v_ = adamw(_flat_pad(like), _flat_pad([g_small[n] for n in _SMALL]), _flat_pad([a['m_' + n] for n in _SMALL]),
                       _flat_pad([a['v_' + n] for n in _SMALL]), name="adamw_small")
    for n, dd, mm, vv in zip(_SMALL, _split_like(d_, like), _split_like(m_, like), _split_like(v_, like)):
        grads[n], delta[n], new_m[n], new_v[n] = g_small[n], dd, mm, vv

    return (loss, gx.reshape(B, S, D), *[grads[n] for n in _WEIGHTS], *[delta[n] for n in _WEIGHTS],
            *[new_m[n] for n in _WEIGHTS], *[new_v[n] for n in _WEIGHTS])
```

```python
import functools
import math

import jax
import jax.numpy as jnp
from jax import lax
from jax.experimental import pallas as pl
from jax.experimental.pallas import tpu as pltpu

_BF = jnp.bfloat16
_F32 = jnp.float32
_VMEM_LIMIT = 56 * 1024 * 1024

QK_NOPE = 128
QK_ROPE = 64
V_HEAD = 128
HEAD_PAD = 256
ROPE_THETA = 10000.0
LN_EPS = 1e-5
RMS_EPS = 1e-6
POOL_WINDOWS = (2, 4, 8, 16)
ADAM_LR, ADAM_B1, ADAM_B2, ADAM_EPS, ADAM_WD, ADAM_STEP = 0.001, 0.9, 0.999, 1e-8, 0.01, 10


def _cparams(sem=None):
    kw = dict(vmem_limit_bytes=_VMEM_LIMIT)
    if sem is not None:
        kw["dimension_semantics"] = sem
    return pltpu.CompilerParams(**kw)


def _tile(n, target, unit=128):
    if n <= target:
        return n
    t = (target // unit) * unit
    while t >= unit:
        if n % t == 0:
            return t
        t -= unit
    return n


_MM_VMEM_BUDGET = 40 * 1024 * 1024


def matmul(a, b, *, ta=False, tb=False, out_dtype=_F32, tm=1024, tn=1536, tk=4096, side=None, name="mm"):
    if ta:
        K, M = a.shape
    else:
        M, K = a.shape
    if tb:
        N, K2 = b.shape
    else:
        K2, N = b.shape
    assert K == K2, (a.shape, b.shape, ta, tb)
    tm, tn, tk = _tile(M, tm), _tile(N, tn), _tile(K, tk)
    ab, bb, ob = a.dtype.itemsize, b.dtype.itemsize, jnp.dtype(out_dtype).itemsize

    def vmem(tk_):
        return 2 * (tm * tk_ * ab + tk_ * tn * bb) + 2 * tm * tn * ob + tm * tn * 4 * (2 if K // tk_ > 1 else 1)

    while vmem(tk) > _MM_VMEM_BUDGET and tk > 256 and _tile(K, tk // 2) < tk:
        tk = _tile(K, tk // 2)
    nk = K // tk
    dn = (((0,) if ta else (1,), (1,) if tb else (0,)), ((), ()))

    s_ops, s_in, s_out, s_shapes, s_scratch, s_alias = _side_specs(side)
    ni, no, nacc = len(s_ops), len(s_shapes), int(nk > 1)
    grid = (M // tm, N // tn, nk)

    def body(a_ref, b_ref, *rest):
        s_ins, o_ref, s_outs = rest[:ni], rest[ni], rest[ni + 1:ni + 1 + no]
        acc, sems = rest[ni + 1 + no:ni + 1 + no + nacc], rest[ni + 1 + no + nacc:]
        i, j, k = pl.program_id(0), pl.program_id(1), pl.program_id(2)
        if side is not None:
            @pl.when((i == 0) & (j == 0) & (k == 0))
            def _():
                side.start(s_ins, s_outs, sems)

        prod = lax.dot_general(a_ref[...].astype(_BF), b_ref[...].astype(_BF), dn, preferred_element_type=_F32)
        if nk == 1:
            o_ref[...] = prod.astype(o_ref.dtype)
        else:
            acc_ref = acc[0]

            @pl.when(k == 0)
            def _():
                acc_ref[...] = prod

            @pl.when(k > 0)
            def _():
                acc_ref[...] += prod

            @pl.when(k == nk - 1)
            def _():
                o_ref[...] = acc_ref[...].astype(o_ref.dtype)

        if side is not None:
            @pl.when((i == grid[0] - 1) & (j == grid[1] - 1) & (k == nk - 1))
            def _():
                side.wait(s_ins, s_outs, sems)

    a_spec = pl.BlockSpec((tk, tm), lambda i, j, k: (k, i)) if ta else pl.BlockSpec((tm, tk), lambda i, j, k: (i, k))
    b_spec = pl.BlockSpec((tn, tk), lambda i, j, k: (j, k)) if tb else pl.BlockSpec((tk, tn), lambda i, j, k: (k, j))
    res = pl.pallas_call(
        body, name=name,
        grid=grid,
        in_specs=[a_spec, b_spec] + s_in,
        out_specs=[pl.BlockSpec((tm, tn), lambda i, j, k: (i, j))] + s_out,
        out_shape=[jax.ShapeDtypeStruct((M, N), out_dtype)] + s_shapes,
        input_output_aliases={2 + i_: 1 + o_ for i_, o_ in s_alias.items()},
        scratch_shapes=([pltpu.VMEM((tm, tn), _F32)] if nk > 1 else []) + s_scratch,
        compiler_params=_cparams(("arbitrary",) * 3 if side is not None else ("parallel", "parallel", "arbitrary")),
    )(a, b, *s_ops)
    return res[0] if side is None else (res[0], list(res[1:]))


def _row_tile(T, C, budget_rows=256):
    return _tile(T, budget_rows, 16)


def ln_fwd(xs, coefs, g, b, *, want_r, name):
    T, C = xs[0].shape
    tr = _row_tile(T, C)
    n = len(xs)

    def body(*refs):
        x_refs, (g_ref, b_ref), outs = refs[:n], refs[n:n + 2], refs[n + 2:]
        r = coefs[0] * x_refs[0][...]
        for c, xr in zip(coefs[1:], x_refs[1:]):
            r = r + c * xr[...]
        mu = jnp.mean(r, axis=-1, keepdims=True)
        d = r - mu
        var = jnp.mean(d * d, axis=-1, keepdims=True)
        y = d * lax.rsqrt(var + LN_EPS) * g_ref[...] + b_ref[...]
        if want_r:
            outs[0][...] = r
        outs[-2][...] = y
        outs[-1][...] = y.astype(_BF)

    row = pl.BlockSpec((tr, C), lambda i: (i, 0))
    vec = pl.BlockSpec((1, C), lambda i: (0, 0))
    f = jax.ShapeDtypeStruct((T, C), _F32)
    out_shape = ([f] if want_r else []) + [f, jax.ShapeDtypeStruct((T, C), _BF)]
    return pl.pallas_call(
        body, name=name, grid=(T // tr,),
        in_specs=[row] * n + [vec, vec],
        out_specs=[row] * len(out_shape), out_shape=out_shape,
        compiler_params=_cparams(("parallel",)),
    )(*xs, g.reshape(1, C), b.reshape(1, C))


def ln_bwd(dys, coefs, r, g, *, name):
    T, C = r.shape
    tr = _row_tile(T, C)
    n = len(dys)

    def body(*refs):
        dy_refs, r_ref, g_ref = refs[:n], refs[n], refs[n + 1]
        dr_ref, drb_ref, dg_ref, db_ref = refs[n + 2:]
        dy = coefs[0] * dy_refs[0][...]
        for c, dr_ in zip(coefs[1:], dy_refs[1:]):
            dy = dy + c * dr_[...]
        rr = r_ref[...]
        mu = jnp.mean(rr, axis=-1, keepdims=True)
        d = rr - mu
        var = jnp.mean(d * d, axis=-1, keepdims=True)
        rstd = lax.rsqrt(var + LN_EPS)
        xh = d * rstd
        gdy = dy * g_ref[...]
        m1 = jnp.mean(gdy, axis=-1, keepdims=True)
        m2 = jnp.mean(gdy * xh, axis=-1, keepdims=True)
        dr = rstd * (gdy - m1 - xh * m2)
        dr_ref[...] = dr
        drb_ref[...] = dr.astype(_BF)

        @pl.when(pl.program_id(0) == 0)
        def _():
            dg_ref[...] = jnp.zeros_like(dg_ref)
            db_ref[...] = jnp.zeros_like(db_ref)

        dg_ref[...] += jnp.sum(dy * xh, axis=0, keepdims=True)
        db_ref[...] += jnp.sum(dy, axis=0, keepdims=True)

    row = pl.BlockSpec((tr, C), lambda i: (i, 0))
    vec = pl.BlockSpec((1, C), lambda i: (0, 0))
    return pl.pallas_call(
        body, name=name, grid=(T // tr,),
        in_specs=[row] * (n + 1) + [vec],
        out_specs=[row, row, vec, vec],
        out_shape=[jax.ShapeDtypeStruct((T, C), _F32), jax.ShapeDtypeStruct((T, C), _BF),
                   jax.ShapeDtypeStruct((1, C), _F32), jax.ShapeDtypeStruct((1, C), _F32)],
        compiler_params=_cparams(("arbitrary",)),
    )(*dys, r, g.reshape(1, C))


def rms_fwd(h, cb, W, g, *, name):
    T = h.shape[0]
    tr = _tile(T, 512, 16)

    def body(c_ref, g_ref, o_ref):
        c = c_ref[...]
        ms = jnp.mean(c * c, axis=-1, keepdims=True)
        o_ref[...] = (c * lax.rsqrt(ms + RMS_EPS) * g_ref[...]).astype(_BF)

    return pl.pallas_call(
        body, name=name, grid=(T // tr,),
        in_specs=[pl.BlockSpec((tr, W), lambda i: (i, cb)), pl.BlockSpec((1, W), lambda i: (0, 0))],
        out_specs=pl.BlockSpec((tr, W), lambda i: (i, 0)),
        out_shape=jax.ShapeDtypeStruct((T, W), _BF),
        compiler_params=_cparams(("parallel",)),
    )(h, g.reshape(1, W))


def rms_bwd(dy, h, cb, W, g, *, name):
    T = h.shape[0]
    tr = _tile(T, 512, 16)

    def body(dy_ref, c_ref, g_ref, dc_ref, dg_ref):
        c = c_ref[...]
        dyv = dy_ref[...]
        ms = jnp.mean(c * c, axis=-1, keepdims=True)
        r = lax.rsqrt(ms + RMS_EPS)
        u = dyv * g_ref[...]
        m = jnp.mean(c * u, axis=-1, keepdims=True)
        dc_ref[...] = (r * u - c * (r * r * r) * m).astype(_BF)

        @pl.when(pl.program_id(0) == 0)
        def _():
            dg_ref[...] = jnp.zeros_like(dg_ref)

        dg_ref[...] += jnp.sum(dyv * c * r, axis=0, keepdims=True)

    return pl.pallas_call(
        body, name=name, grid=(T // tr,),
        in_specs=[pl.BlockSpec((tr, W), lambda i: (i, 0)), pl.BlockSpec((tr, W), lambda i: (i, cb)),
                  pl.BlockSpec((1, W), lambda i: (0, 0))],
        out_specs=[pl.BlockSpec((tr, W), lambda i: (i, 0)), pl.BlockSpec((1, W), lambda i: (0, 0))],
        out_shape=[jax.ShapeDtypeStruct((T, W), _BF), jax.ShapeDtypeStruct((1, W), _F32)],
        compiler_params=_cparams(("arbitrary",)),
    )(dy, h, g.reshape(1, W))


def _rope(u, cc, sa, sb, sign):
    return u * cc + sign * (pltpu.roll(u, 96, 1) * sa + pltpu.roll(u, 32, 1) * sb)


def mla_pack(q, kv, h, kr_cb, cc, sa, sb, *, H, scale, name):
    T = q.shape[0]
    tr = _tile(T, 256, 16)

    def body(q_ref, kv_ref, kr_ref, cc_ref, sa_ref, sb_ref, qp_ref, kp_ref, v_ref):
        cc_, sa_, sb_ = cc_ref[...], sa_ref[...], sb_ref[...]
        kr = _rope(kr_ref[...], cc_, sa_, sb_, 1.0).astype(_BF)
        for hh in range(H):
            o = hh * HEAD_PAD
            qp_ref[:, o:o + 128] = (q_ref[:, o:o + 128] * scale).astype(_BF)
            qp_ref[:, o + 128:o + 256] = (_rope(q_ref[:, o + 128:o + 256], cc_, sa_, sb_, 1.0) * scale).astype(_BF)
            kp_ref[:, o:o + 128] = kv_ref[:, o:o + 128].astype(_BF)
            kp_ref[:, o + 128:o + 256] = kr
            v_ref[:, hh * 128:(hh + 1) * 128] = kv_ref[:, o + 128:o + 256].astype(_BF)

    wide = pl.BlockSpec((tr, H * HEAD_PAD), lambda i: (i, 0))
    tab = pl.BlockSpec((tr, 128), lambda i: (i, 0))
    return pl.pallas_call(
        body, name=name, grid=(T // tr,),
        in_specs=[wide, wide, pl.BlockSpec((tr, 128), lambda i: (i, kr_cb)), tab, tab, tab],
        out_specs=[wide, wide, pl.BlockSpec((tr, H * 128), lambda i: (i, 0))],
        out_shape=[jax.ShapeDtypeStruct((T, H * HEAD_PAD), _BF), jax.ShapeDtypeStruct((T, H * HEAD_PAD), _BF),
                   jax.ShapeDtypeStruct((T, H * 128), _BF)],
        compiler_params=_cparams(("parallel",)),
    )(q, kv, h, cc, sa, sb)


def mla_unpack(dqp, dkp, dv, cc, sa, sb, *, H, name):
    T = dqp.shape[0]
    tr = _tile(T, 256, 16)

    def body(dq_ref, dk_ref, dv_ref, cc_ref, sa_ref, sb_ref, oq_ref, okv_ref, okr_ref):
        cc_, sa_, sb_ = cc_ref[...], sa_ref[...], sb_ref[...]
        kr = jnp.zeros((tr, 128), _F32)
        for hh in range(H):
            o = hh * HEAD_PAD
            oq_ref[:, o:o + 128] = dq_ref[:, o:o + 128].astype(_BF)
            oq_ref[:, o + 128:o + 256] = _rope(dq_ref[:, o + 128:o + 256], cc_, sa_, sb_, -1.0).astype(_BF)
            okv_ref[:, o:o + 128] = dk_ref[:, o:o + 128].astype(_BF)
            okv_ref[:, o + 128:o + 256] = dv_ref[:, hh * 128:(hh + 1) * 128].astype(_BF)
            kr = kr + dk_ref[:, o + 128:o + 256]
        okr_ref[...] = _rope(kr, cc_, sa_, sb_, -1.0).astype(_BF)

    wide = pl.BlockSpec((tr, H * HEAD_PAD), lambda i: (i, 0))
    tab = pl.BlockSpec((tr, 128), lambda i: (i, 0))
    return pl.pallas_call(
        body, name=name, grid=(T // tr,),
        in_specs=[wide, wide, pl.BlockSpec((tr, H * 128), lambda i: (i, 0)), tab, tab, tab],
        out_specs=[wide, wide, tab],
        out_shape=[jax.ShapeDtypeStruct((T, H * HEAD_PAD), _BF), jax.ShapeDtypeStruct((T, H * HEAD_PAD), _BF),
                   jax.ShapeDtypeStruct((T, 128), _BF)],
        compiler_params=_cparams(("parallel",)),
    )(dqp, dkp, dv, cc, sa, sb)


_NEG = -1e30


def _rows(ref, j, t):
    return ref[pl.ds(pl.multiple_of(j * t, t), t), :]


def _qk(q, k):
    return lax.dot_general(q, k, (((1,), (1,)), ((), ())), preferred_element_type=_F32)


def _scores(q, k, t, masked):
    s = _qk(q, k)
    if not masked:
        return s
    row = lax.broadcasted_iota(jnp.int32, (t, t), 0)
    col = lax.broadcasted_iota(jnp.int32, (t, t), 1)
    return jnp.where(col <= row, s, _NEG)


def flash_fwd(qp, kp, v, *, B, S, H, scale, side=None, name):
    T = B * S
    t = _tile(S, 512, 128)
    nq = S // t
    s_ops, s_in, s_out, s_shapes, s_scratch, s_alias = _side_specs(side)
    ni, no = len(s_ops), len(s_shapes)

    def body(q_ref, k_ref, v_ref, *rest):
        s_ins, (o_ref, lse_ref), s_outs = rest[:ni], rest[ni:ni + 2], rest[ni + 2:ni + 2 + no]
        (m_sc, l_sc, acc_sc), sems = rest[ni + 2 + no:ni + 5 + no], rest[ni + 5 + no:]
        i = pl.program_id(2)
        first = (pl.program_id(0) == 0) & (pl.program_id(1) == 0) & (i == 0)
        last = (pl.program_id(0) == B - 1) & (pl.program_id(1) == H - 1) & (i == nq - 1)
        if side is not None:
            @pl.when(first)
            def _():
                side.start(s_ins, s_outs, sems)

        m_sc[...] = jnp.full_like(m_sc, _NEG)
        l_sc[...] = jnp.zeros_like(l_sc)
        acc_sc[...] = jnp.zeros_like(acc_sc)

        def step(j, masked):
            s = _scores(q_ref[...], _rows(k_ref, j, t), t, masked)
            m_old = m_sc[...]
            m_new = jnp.maximum(m_old, jnp.max(s, axis=-1, keepdims=True))
            p = jnp.exp(s - m_new)
            a = jnp.exp(m_old - m_new)
            l_sc[...] = a * l_sc[...] + jnp.sum(p, axis=-1, keepdims=True)
            acc_sc[...] = a * acc_sc[...] + jnp.dot(p.astype(_BF), _rows(v_ref, j, t), preferred_element_type=_F32)
            m_sc[...] = m_new

        @pl.loop(0, i)
        def _(j):
            step(j, False)

        step(i, True)
        l = l_sc[...]
        o_ref[...] = acc_sc[...] / l
        lse_ref[...] = jnp.broadcast_to(m_sc[...] + jnp.log(l), lse_ref.shape)
        if side is not None:
            @pl.when(last)
            def _():
                side.wait(s_ins, s_outs, sems)

    qmap = lambda b, h, i: (b * nq + i, h)
    smap = lambda b, h, i: (b, h)
    res = pl.pallas_call(
        body, name=name, grid=(B, H, nq),
        in_specs=[pl.BlockSpec((t, HEAD_PAD), qmap), pl.BlockSpec((S, HEAD_PAD), smap), pl.BlockSpec((S, 128), smap)] + s_in,
        out_specs=[pl.BlockSpec((t, 128), qmap), pl.BlockSpec((t, 128), qmap)] + s_out,
        out_shape=[jax.ShapeDtypeStruct((T, H * 128), _F32), jax.ShapeDtypeStruct((T, H * 128), _F32)] + s_shapes,
        input_output_aliases={3 + i_: 2 + o_ for i_, o_ in s_alias.items()},
        scratch_shapes=[pltpu.VMEM((t, 1), _F32), pltpu.VMEM((t, 1), _F32), pltpu.VMEM((t, 128), _F32)] + s_scratch,
        compiler_params=_cparams(("arbitrary",) * 3 if side is not None else ("parallel",) * 3),
    )(qp, kp, v, *s_ops)
    return (res[0], res[1]) if side is None else (res[0], res[1], list(res[2:]))


def flash_bwd_dq(qp, kp, v, o, lse, do, do_cb0, *, B, S, H, scale, name):
    T = B * S
    t = _tile(S, 512, 128)
    nq = S // t

    def body(q_ref, k_ref, v_ref, o_ref, lse_ref, do_ref, dq_ref, acc_sc, dl_sc):
        i = pl.program_id(2)
        acc_sc[...] = jnp.zeros_like(acc_sc)
        dl_sc[...] = jnp.sum(do_ref[...].astype(_F32) * o_ref[...], axis=-1, keepdims=True)

        def step(j, masked):
            k = _rows(k_ref, j, t)
            s = _scores(q_ref[...], k, t, masked)
            p = jnp.exp(s - lse_ref[:, 0:1])
            dp = _qk(do_ref[...].astype(_BF), _rows(v_ref, j, t))
            ds = p * (dp - dl_sc[...])
            acc_sc[...] += jnp.dot(ds.astype(_BF), k, preferred_element_type=_F32)

        @pl.loop(0, i)
        def _(j):
            step(j, False)

        step(i, True)
        dq_ref[...] = acc_sc[...] * scale

    qmap = lambda b, h, i: (b * nq + i, h)
    domap = lambda b, h, i: (b * nq + i, do_cb0 + h)
    smap = lambda b, h, i: (b, h)
    return pl.pallas_call(
        body, name=name, grid=(B, H, nq),
        in_specs=[pl.BlockSpec((t, HEAD_PAD), qmap), pl.BlockSpec((S, HEAD_PAD), smap), pl.BlockSpec((S, 128), smap),
                  pl.BlockSpec((t, 128), qmap), pl.BlockSpec((t, 128), qmap), pl.BlockSpec((t, 128), domap)],
        out_specs=pl.BlockSpec((t, HEAD_PAD), qmap),
        out_shape=jax.ShapeDtypeStruct((T, H * HEAD_PAD), _F32),
        scratch_shapes=[pltpu.VMEM((t, HEAD_PAD), _F32), pltpu.VMEM((t, 1), _F32)],
        compiler_params=_cparams(("parallel", "parallel", "parallel")),
    )(qp, kp, v, o, lse, do)


def flash_bwd_dkv(qp, kp, v, o, lse, do, do_cb0, *, B, S, H, scale, name):
    T = B * S
    t = _tile(S, 512, 128)
    nk = S // t

    def body(q_ref, k_ref, v_ref, o_ref, lse_ref, do_ref, dk_ref, dv_ref, dk_sc, dv_sc):
        j = pl.program_id(2)
        dk_sc[...] = jnp.zeros_like(dk_sc)
        dv_sc[...] = jnp.zeros_like(dv_sc)

        def step(i, masked):
            q = _rows(q_ref, i, t)
            do = _rows(do_ref, i, t).astype(_F32)
            dob = do.astype(_BF)
            s = _scores(q, k_ref[...], t, masked)
            p = jnp.exp(s - _rows(lse_ref, i, t)[:, 0:1])
            dl = jnp.sum(do * _rows(o_ref, i, t), axis=-1, keepdims=True)
            dp = _qk(dob, v_ref[...])
            ds = p * (dp - dl)
            tn = (((0,), (0,)), ((), ()))
            dv_sc[...] += lax.dot_general(p.astype(_BF), dob, tn, preferred_element_type=_F32)
            dk_sc[...] += lax.dot_general(ds.astype(_BF), q, tn, preferred_element_type=_F32)

        step(j, True)

        @pl.loop(j + 1, nk)
        def _(i):
            step(i, False)

        dk_ref[...] = dk_sc[...]
        dv_ref[...] = dv_sc[...]

    smap = lambda b, h, j: (b, h)
    domap = lambda b, h, j: (b, do_cb0 + h)
    kmap = lambda b, h, j: (b * nk + j, h)
    return pl.pallas_call(
        body, name=name, grid=(B, H, nk),
        in_specs=[pl.BlockSpec((S, HEAD_PAD), smap), pl.BlockSpec((t, HEAD_PAD), kmap), pl.BlockSpec((t, 128), kmap),
                  pl.BlockSpec((S, 128), smap), pl.BlockSpec((S, 128), smap), pl.BlockSpec((S, 128), domap)],
        out_specs=[pl.BlockSpec((t, HEAD_PAD), kmap), pl.BlockSpec((t, 128), kmap)],
        out_shape=[jax.ShapeDtypeStruct((T, H * HEAD_PAD), _F32), jax.ShapeDtypeStruct((T, H * 128), _F32)],
        scratch_shapes=[pltpu.VMEM((t, HEAD_PAD), _F32), pltpu.VMEM((t, 128), _F32)],
        compiler_params=_cparams(("parallel", "parallel", "parallel")),
    )(qp, kp, v, o, lse, do)


def _halo_specs(T, nT, tt, hr, cw, cb):
    k = tt // hr
    main = pl.BlockSpec((tt, cw), lambda b, t: (b * nT + t, cb))
    prev = pl.BlockSpec((hr, cw), lambda b, t: (jnp.maximum((b * nT + t) * k - 1, 0), cb))
    nxt = pl.BlockSpec((hr, cw), lambda b, t: (jnp.minimum((b * nT + t + 1) * k, T // hr - 1), cb))
    return main, prev, nxt


def _ln_rows(z, g, b):
    mu = jnp.mean(z, axis=-1, keepdims=True)
    d = z - mu
    var = jnp.mean(d * d, axis=-1, keepdims=True)
    rstd = lax.rsqrt(var + LN_EPS)
    xh = d * rstd
    return xh * g + b, xh, rstd


def conv_fwd(h, cb_a, cb_g, w, bias, lng, lnb, *, B, S, name):
    T = B * S
    K, C = w.shape
    hr = 32
    assert K - 1 <= hr
    tt = _tile(S, 512, hr)
    nT = S // tt
    a_m, a_p, _ = _halo_specs(T, nT, tt, hr, C, cb_a)
    g_m, g_p, _ = _halo_specs(T, nT, tt, hr, C, cb_g)

    def body(a_ref, g_ref, ap_ref, gp_ref, w_ref, b_ref, lg_ref, lb_ref, z_ref, y_ref, buf):
        t = pl.program_id(1)
        buf[pl.ds(hr, tt), :] = a_ref[...] * jax.nn.sigmoid(g_ref[...])
        hp = ap_ref[...] * jax.nn.sigmoid(gp_ref[...])
        buf[pl.ds(0, hr), :] = jnp.where(t == 0, 0.0, hp)
        z = jnp.broadcast_to(b_ref[...], (tt, C))
        for k in range(K):
            z = z + w_ref[k:k + 1, :] * buf[pl.ds(hr - (K - 1) + k, tt), :]
        z_ref[...] = z
        n, _, _ = _ln_rows(z, lg_ref[...], lb_ref[...])
        y_ref[...] = (n * jax.nn.sigmoid(n)).astype(_BF)

    vec = pl.BlockSpec((1, C), lambda b, t: (0, 0))
    out = pl.BlockSpec((tt, C), lambda b, t: (b * nT + t, 0))
    return pl.pallas_call(
        body, name=name, grid=(B, nT),
        in_specs=[a_m, g_m, a_p, g_p, pl.BlockSpec((K, C), lambda b, t: (0, 0)), vec, vec, vec],
        out_specs=[out, out],
        out_shape=[jax.ShapeDtypeStruct((T, C), _F32), jax.ShapeDtypeStruct((T, C), _BF)],
        scratch_shapes=[pltpu.VMEM((hr + tt, C), _F32)],
        compiler_params=_cparams(("parallel", "parallel")),
    )(h, h, h, h, w, bias.reshape(1, C), lng.reshape(1, C), lnb.reshape(1, C))


def conv_bwd(dmix, cb_dy, z, h, cb_a, cb_g, w, lng, lnb, *, B, S, name):
    T = B * S
    K, C = w.shape
    hr = 32
    tt = _tile(S, 512, hr)
    nT = S // tt
    a_m, a_p, _ = _halo_specs(T, nT, tt, hr, C, cb_a)
    g_m, g_p, _ = _halo_specs(T, nT, tt, hr, C, cb_g)
    dy_m, _, dy_n = _halo_specs(T, nT, tt, hr, C, cb_dy)
    z_m, _, z_n = _halo_specs(T, nT, tt, hr, C, 0)

    def body(dy_ref, dyn_ref, z_ref, zn_ref, a_ref, g_ref, ap_ref, gp_ref, w_ref, lg_ref, lb_ref,
             da_ref, dg_ref, dw_ref, db_ref, dlg_ref, dlb_ref, bufz, bufh):
        b, t = pl.program_id(0), pl.program_id(1)
        lg, lb = lg_ref[...], lb_ref[...]

        def dz_of(dy, zz):
            n, xh, rstd = _ln_rows(zz, lg, lb)
            sg = jax.nn.sigmoid(n)
            dn = dy.astype(_F32) * (sg * (1.0 + n * (1.0 - sg)))
            gdn = dn * lg
            m1 = jnp.mean(gdn, axis=-1, keepdims=True)
            m2 = jnp.mean(gdn * xh, axis=-1, keepdims=True)
            return rstd * (gdn - m1 - xh * m2), dn, xh

        dz, dn, xh = dz_of(dy_ref[...], z_ref[...])
        dzn, _, _ = dz_of(dyn_ref[...], zn_ref[...])
        bufz[pl.ds(0, tt), :] = dz
        bufz[pl.ds(tt, hr), :] = jnp.where(t == nT - 1, 0.0, dzn)
        a, g = a_ref[...], g_ref[...]
        sg = jax.nn.sigmoid(g)
        bufh[pl.ds(hr, tt), :] = a * sg
        bufh[pl.ds(0, hr), :] = jnp.where(t == 0, 0.0, ap_ref[...] * jax.nn.sigmoid(gp_ref[...]))

        @pl.when((b == 0) & (t == 0))
        def _():
            dw_ref[...] = jnp.zeros_like(dw_ref)
            db_ref[...] = jnp.zeros_like(db_ref)
            dlg_ref[...] = jnp.zeros_like(dlg_ref)
            dlb_ref[...] = jnp.zeros_like(dlb_ref)

        dhc = jnp.zeros((tt, C), _F32)
        for k in range(K):
            dhc = dhc + w_ref[k:k + 1, :] * bufz[pl.ds(K - 1 - k, tt), :]
            dw_ref[k:k + 1, :] += jnp.sum(dz * bufh[pl.ds(hr - (K - 1) + k, tt), :], axis=0, keepdims=True)
        da_ref[...] = (dhc * sg).astype(_BF)
        dg_ref[...] = (dhc * a * sg * (1.0 - sg)).astype(_BF)
        db_ref[...] += jnp.sum(dz, axis=0, keepdims=True)
        dlg_ref[...] += jnp.sum(dn * xh, axis=0, keepdims=True)
        dlb_ref[...] += jnp.sum(dn, axis=0, keepdims=True)

    vec = pl.BlockSpec((1, C), lambda b, t: (0, 0))
    out = pl.BlockSpec((tt, C), lambda b, t: (b * nT + t, 0))
    kc = pl.BlockSpec((K, C), lambda b, t: (0, 0))
    return pl.pallas_call(
        body, name=name, grid=(B, nT),
        in_specs=[dy_m, dy_n, z_m, z_n, a_m, g_m, a_p, g_p, kc, vec, vec],
        out_specs=[out, out, kc, vec, vec, vec],
        out_shape=[jax.ShapeDtypeStruct((T, C), _BF), jax.ShapeDtypeStruct((T, C), _BF),
                   jax.ShapeDtypeStruct((K, C), _F32)] + [jax.ShapeDtypeStruct((1, C), _F32)] * 3,
        scratch_shapes=[pltpu.VMEM((tt + hr, C), _F32), pltpu.VMEM((hr + tt, C), _F32)],
        compiler_params=_cparams(("arbitrary", "arbitrary")),
    )(dmix, dmix, z, z, h, h, h, h, w, lng.reshape(1, C), lnb.reshape(1, C))


def _pool_cnt(t, tt, w, rows):
    pos = t * tt + lax.broadcasted_iota(jnp.int32, (rows, 1), 0)
    return jnp.minimum(pos + 1, w).astype(_F32)


def pool_fwd(h, cb, wp, scale, *, B, S, name):
    T = B * S
    G, pg, _ = wp.shape
    C = G * pg
    assert pg == 128 and G == len(POOL_WINDOWS)
    hr = 16
    tt = _tile(S, 512, hr)
    nT = S // tt
    u_m, u_p, _ = _halo_specs(T, nT, tt, hr, C, cb)

    def body(u_ref, up_ref, wp_ref, sc_ref, y_ref, buf):
        t = pl.program_id(1)
        buf[pl.ds(hr, tt), :] = u_ref[...]
        buf[pl.ds(0, hr), :] = jnp.where(t == 0, 0.0, up_ref[...])
        for gi, w in enumerate(POOL_WINDOWS):
            ln = slice(gi * pg, (gi + 1) * pg)
            acc = buf[pl.ds(hr, tt), ln]
            for j in range(1, w):
                acc = acc + buf[pl.ds(hr - j, tt), ln]
            d = acc / _pool_cnt(t, tt, w, tt) - u_ref[:, ln]
            yg = jnp.dot(d.astype(_BF), wp_ref[gi].astype(_BF), preferred_element_type=_F32)
            y_ref[:, ln] = (yg * sc_ref[:, ln]).astype(_BF)

    return pl.pallas_call(
        body, name=name, grid=(B, nT),
        in_specs=[u_m, u_p, pl.BlockSpec((G, pg, pg), lambda b, t: (0, 0, 0)), pl.BlockSpec((1, C), lambda b, t: (0, 0))],
        out_specs=pl.BlockSpec((tt, C), lambda b, t: (b * nT + t, 0)),
        out_shape=jax.ShapeDtypeStruct((T, C), _BF),
        scratch_shapes=[pltpu.VMEM((hr + tt, C), _F32)],
        compiler_params=_cparams(("parallel", "parallel")),
    )(h, h, wp, scale.reshape(1, C))


def pool_bwd(dmix, cb_dy, h, cb, wp, scale, *, B, S, name):
    T = B * S
    G, pg, _ = wp.shape
    C = G * pg
    hr = 16
    tt = _tile(S, 512, hr)
    nT = S // tt
    u_m, u_p, _ = _halo_specs(T, nT, tt, hr, C, cb)
    dy_m, _, dy_n = _halo_specs(T, nT, tt, hr, C, cb_dy)

    def body(dy_ref, dyn_ref, u_ref, up_ref, wp_ref, sc_ref, du_ref, dwp_ref, dsc_ref, buf, bufe):
        b, t = pl.program_id(0), pl.program_id(1)
        buf[pl.ds(hr, tt), :] = u_ref[...]
        buf[pl.ds(0, hr), :] = jnp.where(t == 0, 0.0, up_ref[...])

        @pl.when((b == 0) & (t == 0))
        def _():
            dwp_ref[...] = jnp.zeros_like(dwp_ref)
            dsc_ref[...] = jnp.zeros_like(dsc_ref)

        nt = (((1,), (1,)), ((), ()))
        tn = (((0,), (0,)), ((), ()))
        for gi, w in enumerate(POOL_WINDOWS):
            ln = slice(gi * pg, (gi + 1) * pg)
            wg = wp_ref[gi].astype(_BF)
            sc = sc_ref[:, ln]
            dy = dy_ref[:, ln].astype(_F32)
            dz = (dy * sc).astype(_BF)
            dzn = (dyn_ref[:, ln].astype(_F32) * sc).astype(_BF)
            dd = lax.dot_general(dz, wg, nt, preferred_element_type=_F32)
            ddn = lax.dot_general(dzn, wg, nt, preferred_element_type=_F32)
            bufe[pl.ds(0, tt), ln] = dd / _pool_cnt(t, tt, w, tt)
            bufe[pl.ds(tt, hr), ln] = jnp.where(t == nT - 1, 0.0, ddn / _pool_cnt(t + 1, tt, w, hr))
            du = -dd
            for j in range(w):
                du = du + bufe[pl.ds(j, tt), ln]
            du_ref[:, ln] = du.astype(_BF)
            acc = buf[pl.ds(hr, tt), ln]
            for j in range(1, w):
                acc = acc + buf[pl.ds(hr - j, tt), ln]
            d = (acc / _pool_cnt(t, tt, w, tt) - u_ref[:, ln]).astype(_BF)
            dwp_ref[gi] += lax.dot_general(d, dz, tn, preferred_element_type=_F32)
            yg = jnp.dot(d, wg, preferred_element_type=_F32)
            dsc_ref[:, ln] += jnp.sum(dy * yg, axis=0, keepdims=True)

    return pl.pallas_call(
        body, name=name, grid=(B, nT),
        in_specs=[dy_m, dy_n, u_m, u_p, pl.BlockSpec((G, pg, pg), lambda b, t: (0, 0, 0)),
                  pl.BlockSpec((1, C), lambda b, t: (0, 0))],
        out_specs=[pl.BlockSpec((tt, C), lambda b, t: (b * nT + t, 0)), pl.BlockSpec((G, pg, pg), lambda b, t: (0, 0, 0)),
                   pl.BlockSpec((1, C), lambda b, t: (0, 0))],
        out_shape=[jax.ShapeDtypeStruct((T, C), _BF), jax.ShapeDtypeStruct((G, pg, pg), _F32),
                   jax.ShapeDtypeStruct((1, C), _F32)],
        scratch_shapes=[pltpu.VMEM((hr + tt, C), _F32), pltpu.VMEM((tt + hr, C), _F32)],
        compiler_params=_cparams(("arbitrary", "arbitrary")),
    )(dmix, dmix, h, h, wp, scale.reshape(1, C))


_FFN_HR = 16
_FFN_ROWS, _FFN_LANES = 32, 256


def _fold8(x):
    out = x[0:8]
    for r in range(8, x.shape[0], 8):
        out = out + x[r:r + 8]
    return out


def _silu_grad(x, sg):
    return sg * (1.0 + x * (1.0 - sg))


def _conv3(buf, w_ref, b_ref, off, rows, ln):
    c = b_ref[:, ln] + w_ref[0:1, ln] * buf[pl.ds(off, rows), ln]
    for k in (1, 2):
        c = c + w_ref[k:k + 1, ln] * buf[pl.ds(off + k, rows), ln]
    return c


def _ffn_chunks(nrows, ncols):
    lw = min(ncols, _FFN_LANES)
    return [(r, min(_FFN_ROWS, nrows - r), slice(l0, l0 + lw))
            for l0 in range(0, ncols, lw) for r in range(0, nrows, _FFN_ROWS)]


def gate_fwd(up, w, bias, *, B, S, name):
    T, F2 = up.shape
    F = F2 // 2
    hr = _FFN_HR
    tt = _tile(S, 512, hr)
    nT = S // tt
    tn = _tile(F, 512, 128)
    nC = F // tn
    k = tt // hr

    def body(a_ref, g_ref, ap_ref, gp_ref, wa_ref, wg_ref, ba_ref, bg_ref, o_ref, bufa, bufg):
        t = pl.program_id(2)
        for buf, m_ref, p_ref in ((bufa, a_ref, ap_ref), (bufg, g_ref, gp_ref)):
            buf[pl.ds(hr, tt), :] = m_ref[...].astype(_F32)
            buf[pl.ds(0, hr), :] = jnp.where(t == 0, 0.0, p_ref[...].astype(_F32))
        for r, rc, ln in _ffn_chunks(tt, tn):
            ca = _conv3(bufa, wa_ref, ba_ref, hr - 2 + r, rc, ln)
            cg = _conv3(bufg, wg_ref, bg_ref, hr - 2 + r, rc, ln)
            o_ref[pl.ds(r, rc), ln] = (ca * cg * jax.nn.sigmoid(cg)).astype(_BF)

    def main(off):
        return pl.BlockSpec((tt, tn), lambda b, j, t: (b * nT + t, j + off))

    def prev(off):
        return pl.BlockSpec((hr, tn), lambda b, j, t: (jnp.maximum((b * nT + t) * k - 1, 0), j + off))

    def wspec(rows, off):
        return pl.BlockSpec((rows, tn), lambda b, j, t: (0, j + off))

    return pl.pallas_call(
        body, name=name, grid=(B, nC, nT),
        in_specs=[main(0), main(nC), prev(0), prev(nC), wspec(3, 0), wspec(3, nC), wspec(1, 0), wspec(1, nC)],
        out_specs=pl.BlockSpec((tt, tn), lambda b, j, t: (b * nT + t, j)),
        out_shape=jax.ShapeDtypeStruct((T, F), _BF),
        scratch_shapes=[pltpu.VMEM((hr + tt, tn), _F32)] * 2,
        compiler_params=_cparams(("parallel", "parallel", "parallel")),
    )(up, up, up, up, w, w, bias.reshape(1, F2), bias.reshape(1, F2))


def gate_bwd(up, dact, w, bias, *, B, S, name):
    T, F2 = up.shape
    F = F2 // 2
    hr = _FFN_HR
    tt = _tile(S, 512, hr)
    nT = S // tt
    tn = _tile(F, 512, 128)
    nC = F // tn
    k = tt // hr
    ext = tt + hr

    def body(a_ref, g_ref, ap_ref, gp_ref, an_ref, gn_ref, d_ref, dn_ref, wa_ref, wg_ref, ba_ref, bg_ref,
             dua_ref, dug_ref, dwa_ref, dwg_ref, dba_ref, dbg_ref, bufa, bufg, bufda, bufdg):
        b, t = pl.program_id(1), pl.program_id(2)
        last = t == nT - 1
        for buf, m_ref, p_ref, n_ref in ((bufa, a_ref, ap_ref, an_ref), (bufg, g_ref, gp_ref, gn_ref)):
            buf[pl.ds(hr, tt), :] = m_ref[...].astype(_F32)
            buf[pl.ds(0, hr), :] = jnp.where(t == 0, 0.0, p_ref[...].astype(_F32))
            buf[pl.ds(hr + tt, hr), :] = jnp.where(last, 0.0, n_ref[...].astype(_F32))
        for r, rc, ln in _ffn_chunks(ext, tn):
            ca = _conv3(bufa, wa_ref, ba_ref, hr - 2 + r, rc, ln)
            cg = _conv3(bufg, wg_ref, bg_ref, hr - 2 + r, rc, ln)
            sg = jax.nn.sigmoid(cg)
            if r < tt:
                da = d_ref[pl.ds(r, rc), ln].astype(_F32)
            else:
                da = jnp.where(last, 0.0, dn_ref[pl.ds(r - tt, rc), ln].astype(_F32))
            bufda[pl.ds(r, rc), ln] = da * cg * sg
            bufdg[pl.ds(r, rc), ln] = da * ca * _silu_grad(cg, sg)

        @pl.when((b == 0) & (t == 0))
        def _():
            for r in (dwa_ref, dwg_ref, dba_ref, dbg_ref):
                r[...] = jnp.zeros_like(r)

        lw, rc = min(tn, _FFN_LANES), _FFN_ROWS
        assert tt % rc == 0
        for bufd, buf, w_ref, du_ref, dw_ref, db_ref in ((bufda, bufa, wa_ref, dua_ref, dwa_ref, dba_ref),
                                                         (bufdg, bufg, wg_ref, dug_ref, dwg_ref, dbg_ref)):
            for l0 in range(0, tn, lw):
                ln = slice(l0, l0 + lw)
                acc = [jnp.zeros((8, lw), _F32) for _ in range(4)]
                for r in range(0, tt, rc):
                    dc = [bufd[pl.ds(r + s_, rc), ln] for s_ in range(3)]
                    u = buf[pl.ds(hr + r, rc), ln]
                    du = w_ref[2:3, ln] * dc[0] + w_ref[1:2, ln] * dc[1] + w_ref[0:1, ln] * dc[2]
                    du_ref[pl.ds(r, rc), ln] = du.astype(_BF)
                    for kk in range(3):
                        acc[kk] = acc[kk] + _fold8(dc[2 - kk] * u)
                    acc[3] = acc[3] + _fold8(dc[0])
                for kk in range(3):
                    dw_ref[kk:kk + 1, ln] += jnp.sum(acc[kk], axis=0, keepdims=True)
                db_ref[:, ln] += jnp.sum(acc[3], axis=0, keepdims=True)

    def main(off):
        return pl.BlockSpec((tt, tn), lambda j, b, t: (b * nT + t, j + off))

    def prev(off):
        return pl.BlockSpec((hr, tn), lambda j, b, t: (jnp.maximum((b * nT + t) * k - 1, 0), j + off))

    def nxt(off):
        return pl.BlockSpec((hr, tn), lambda j, b, t: (jnp.minimum((b * nT + t + 1) * k, T // hr - 1), j + off))

    def wspec(rows, off):
        return pl.BlockSpec((rows, tn), lambda j, b, t: (0, j + off))

    tf = jax.ShapeDtypeStruct((T, F), _BF)
    return pl.pallas_call(
        body, name=name, grid=(nC, B, nT),
        in_specs=[main(0), main(nC), prev(0), prev(nC), nxt(0), nxt(nC), main(0), nxt(0),
                  wspec(3, 0), wspec(3, nC), wspec(1, 0), wspec(1, nC)],
        out_specs=[main(0), main(0), wspec(3, 0), wspec(3, 0), wspec(1, 0), wspec(1, 0)],
        out_shape=[tf, tf, jax.ShapeDtypeStruct((3, F), _F32), jax.ShapeDtypeStruct((3, F), _F32),
                   jax.ShapeDtypeStruct((1, F), _F32), jax.ShapeDtypeStruct((1, F), _F32)],
        scratch_shapes=[pltpu.VMEM((hr + ext, tn), _F32)] * 2 + [pltpu.VMEM((ext, tn), _F32)] * 2,
        compiler_params=_cparams(("parallel", "arbitrary", "arbitrary")),
    )(up, up, up, up, up, up, dact, dact, w, w, bias.reshape(1, F2), bias.reshape(1, F2))


def loss_head(y, target, *, name):
    T, C = y.shape
    tr = _row_tile(T, C)

    def body(y_ref, t_ref, dy_ref, acc_ref):
        @pl.when(pl.program_id(0) == 0)
        def _():
            acc_ref[...] = jnp.zeros_like(acc_ref)

        e = y_ref[...] - t_ref[...]
        dy_ref[...] = e * (1.0 / C)
        acc_ref[...] += jnp.sum(e * e, axis=0, keepdims=True) * (0.5 / C)

    row = pl.BlockSpec((tr, C), lambda i: (i, 0))
    return pl.pallas_call(
        body, name=name, grid=(T // tr,),
        in_specs=[row, row], out_specs=[row, pl.BlockSpec((1, C), lambda i: (0, 0))],
        out_shape=[jax.ShapeDtypeStruct((T, C), _F32), jax.ShapeDtypeStruct((1, C), _F32)],
        compiler_params=_cparams(("arbitrary",)),
    )(y, target)


def _adamw_update(w_ref, g_ref, m_ref, v_ref, d_ref, mo_ref, vo_ref):
    c1 = 1.0 - ADAM_B1 ** ADAM_STEP
    c2 = 1.0 - ADAM_B2 ** ADAM_STEP
    gg = g_ref[...]
    mn = ADAM_B1 * m_ref[...] + (1.0 - ADAM_B1) * gg
    vn = ADAM_B2 * v_ref[...] + (1.0 - ADAM_B2) * (gg * gg)
    d_ref[...] = -ADAM_LR * ((mn / c1) / (jnp.sqrt(vn / c2) + ADAM_EPS) + ADAM_WD * w_ref[...])
    mo_ref[...] = mn
    vo_ref[...] = vn


def adamw_small(ws, gs, ms, vs, *, name):
    n = len(ws)

    def body(*refs):
        for p in range(n):
            _adamw_update(*[refs[k * n + p] for k in range(7)])

    vm = pl.BlockSpec(memory_space=pltpu.VMEM)
    shapes = [jax.ShapeDtypeStruct(w.shape, _F32) for w in ws]
    res = pl.pallas_call(
        body, name=name, in_specs=[vm] * (4 * n), out_specs=[vm] * (3 * n), out_shape=shapes * 3,
        compiler_params=_cparams(),
    )(*ws, *gs, *ms, *vs)
    return res[:n], res[n:2 * n], res[2 * n:]


def adamw(w, g, m, v, *, name):
    R, C = w.shape
    tr = _tile(R, max(8, (256 * 1024) // C // 8 * 8), 8)

    def body(w_ref, g_ref, m_ref, v_ref, d_ref, mo_ref, vo_ref):
        _adamw_update(w_ref, g_ref, m_ref, v_ref, d_ref, mo_ref, vo_ref)

    blk = pl.BlockSpec((tr, C), lambda i: (i, 0))
    s = jax.ShapeDtypeStruct((R, C), _F32)
    return pl.pallas_call(
        body, name=name, grid=(R // tr,),
        in_specs=[blk] * 4, out_specs=[blk] * 3, out_shape=[s, s, s],
        compiler_params=_cparams(("parallel",)),
    )(w, g, m, v)


_ANY = pl.BlockSpec(memory_space=pl.ANY)
_MESH = pl.DeviceIdType.MESH


def _place():
    return lax.axis_index("x"), lax.axis_index("y"), lax.axis_index("c")


def _other_chips(x, y):
    chips = [(1 - x, y), (x, 1 - y), (1 - x, 1 - y)]
    return chips, [2 * a + b for a, b in chips]


def _rcopy(src, dst, ssem, rsem, dev):
    return pltpu.make_async_remote_copy(src_ref=src, dst_ref=dst, send_sem=ssem, recv_sem=rsem,
                                        device_id=dev, device_id_type=_MESH)


def place_shard(w, l, chip_idx, *, name):
    _, _, hR, C = w.shape
    tr = _tile(hR, max(16, (512 * 1024) // C // 16 * 16), 16)

    def body(ci_ref, w_ref, o_ref):
        o_ref[...] = w_ref[...].astype(_BF)

    return pl.pallas_call(
        body, name=name,
        grid_spec=pltpu.PrefetchScalarGridSpec(
            num_scalar_prefetch=1, grid=(2, hR // tr),
            in_specs=[pl.BlockSpec((None, None, tr, C), lambda h, i, ci: (l, h, i, 0))],
            out_specs=pl.BlockSpec((None, None, tr, C), lambda h, i, ci: (ci[0], h, i, 0))),
        out_shape=jax.ShapeDtypeStruct((4, 2, hR, C), _BF),
        compiler_params=_cparams(("parallel", "parallel")),
    )(chip_idx, w)


class _Side:
    def __init__(self, arrays, out_shapes, aliases, n_sems, start, wait):
        self.arrays, self.out_shapes, self.aliases, self.n_sems = arrays, out_shapes, aliases, n_sems
        self.start, self.wait = start, wait


def gather_stage1_side(bufs):
    n = len(bufs)

    def copies(outs, sems, sending):
        x, y, c = _place()
        me = 2 * x + y
        chips, cidx = _other_chips(x, y)
        send, recv = sems
        out, back = [], []
        for k, chip in enumerate(chips):
            for p in range(n):
                mine, got = outs[p].at[me, c], outs[p].at[cidx[k], c]
                out.append(_rcopy(mine, mine, send.at[p * 3 + k], recv.at[p * 3 + k], (*chip, c)))
                if not sending:
                    back.append(_rcopy(got, got, send.at[p * 3 + k], recv.at[p * 3 + k], (*chip, c)))
        return out, back

    def start(ins, outs, sems):
        for cp in copies(outs, sems, True)[0]:
            cp.start()

    def wait(ins, outs, sems):
        out, back = copies(outs, sems, False)
        for cp in back:
            cp.wait_recv()
        for cp in out:
            cp.wait_send()

    return _Side(list(bufs), [jax.ShapeDtypeStruct(b.shape, b.dtype) for b in bufs], {p: p for p in range(n)},
                 [n * 3, n * 3], start, wait)


def gather_stage2_side(bufs):
    n = len(bufs)

    def copies(outs, sems, sending):
        x, y, c = _place()
        sib = (x, y, 1 - c)
        _, cidx = _other_chips(x, y)
        send, recv = sems
        out, back = [], []
        for k in range(3):
            for p in range(n):
                mine, got = outs[p].at[cidx[k], c], outs[p].at[cidx[k], 1 - c]
                out.append(_rcopy(mine, mine, send.at[p * 3 + k], recv.at[p * 3 + k], sib))
                if not sending:
                    back.append(_rcopy(got, got, send.at[p * 3 + k], recv.at[p * 3 + k], sib))
        return out, back

    def start(ins, outs, sems):
        for cp in copies(outs, sems, True)[0]:
            cp.start()

    def wait(ins, outs, sems):
        out, back = copies(outs, sems, False)
        for cp in back:
            cp.wait_recv()
        for cp in out:
            cp.wait_send()

    return _Side(list(bufs), [jax.ShapeDtypeStruct(b.shape, b.dtype) for b in bufs], {p: p for p in range(n)},
                 [n * 3, n * 3], start, wait)


def chip_exchange_side(ps):
    n = len(ps)

    def copies(ins, outs, sems, sending):
        x, y, c = _place()
        me = 2 * x + y
        chips, cidx = _other_chips(x, y)
        send, recv = sems
        out, back = [], []
        for k, chip in enumerate(chips):
            for p in range(n):
                got = outs[p].at[cidx[k]]
                out.append(_rcopy(ins[p].at[cidx[k]], outs[p].at[me], send.at[p * 3 + k], recv.at[p * 3 + k], (*chip, c)))
                if not sending:
                    back.append(_rcopy(got, got, send.at[p * 3 + k], recv.at[p * 3 + k], (*chip, c)))
        return out, back

    def start(ins, outs, sems):
        for cp in copies(ins, outs, sems, True)[0]:
            cp.start()

    def wait(ins, outs, sems):
        out, back = copies(ins, outs, sems, False)
        for cp in back:
            cp.wait_recv()
        for cp in out:
            cp.wait_send()

    return _Side(list(ps), [jax.ShapeDtypeStruct(a.shape, a.dtype) for a in ps], {}, [n * 3, n * 3], start, wait)


def _side_specs(side):
    if side is None:
        return [], [], [], [], [], {}
    return (side.arrays, [_ANY] * len(side.arrays), [_ANY] * len(side.out_shapes), side.out_shapes,
            [pltpu.SemaphoreType.DMA((k,)) for k in side.n_sems], side.aliases)


def run_side(side, *, name):
    ni, no = len(side.arrays), len(side.out_shapes)

    def body(*refs):
        ins, outs, sems = refs[:ni], refs[ni:ni + no], refs[ni + no:]
        side.start(ins, outs, sems)
        side.wait(ins, outs, sems)

    ops, in_specs, out_specs, out_shapes, scratch, aliases = _side_specs(side)
    return pl.pallas_call(body, name=name, in_specs=in_specs, out_specs=out_specs, out_shape=out_shapes,
                          input_output_aliases=aliases, scratch_shapes=scratch)(*ops)


def gather_small(small, *, name):
    ns = len(small)

    def body(*refs):
        s_in, s_out = refs[:ns], refs[ns:2 * ns]
        send, recv, lsem = refs[2 * ns:]
        x, y, c = _place()
        me = 2 * x + y
        chips, cidx = _other_chips(x, y)
        local = [pltpu.make_async_copy(s_in[q], s_out[q].at[me], lsem.at[q]) for q in range(ns)]
        cps = [_rcopy(s_in[q], s_out[q].at[me], send.at[q * 3 + k], recv.at[q * 3 + k], (*chip, c))
               for k, chip in enumerate(chips) for q in range(ns)]
        for cp in local + cps:
            cp.start()
        for k in range(3):
            for q in range(ns):
                got = s_out[q].at[cidx[k]]
                _rcopy(got, got, send.at[q * 3 + k], recv.at[q * 3 + k], (x, y, c)).wait_recv()
        for cp in cps:
            cp.wait_send()
        for cp in local:
            cp.wait()

    return pl.pallas_call(
        body, name=name, in_specs=[_ANY] * ns, out_specs=[_ANY] * ns,
        out_shape=[jax.ShapeDtypeStruct((4,) + a.shape, a.dtype) for a in small],
        scratch_shapes=[pltpu.SemaphoreType.DMA((ns * 3,))] * 2 + [pltpu.SemaphoreType.DMA((ns,))],
    )(*small)


def sibling_send_half(gs, *, name):
    n = len(gs)

    def body(*refs):
        g_in, g_out, send, recv = refs[:n], refs[n:2 * n], refs[2 * n], refs[2 * n + 1]
        x, y, c = _place()
        sib = (x, y, 1 - c)
        cps = [_rcopy(g_in[p].at[1 - c], g_out[p], send.at[p], recv.at[p], sib) for p in range(n)]
        for cp in cps:
            cp.start()
        for cp in cps:
            cp.wait()

    return pl.pallas_call(
        body, name=name, in_specs=[_ANY] * n, out_specs=[_ANY] * n,
        out_shape=[jax.ShapeDtypeStruct(a.shape[1:], a.dtype) for a in gs],
        scratch_shapes=[pltpu.SemaphoreType.DMA((n,))] * 2,
    )(*gs)


def sum_chips(p, slots, idx, *, name):
    _, N, C = p.shape
    tr = _tile(N, max(16, (512 * 1024) // C // 16 * 16), 16)

    def body(i0, i1, i2, i3, i4, p_ref, s0_ref, s1_ref, s2_ref, o_ref):
        s = p_ref[...].astype(_F32)
        for r in (s0_ref, s1_ref, s2_ref):
            s = s + r[...].astype(_F32)
        o_ref[...] = s

    def at(k):
        return pl.BlockSpec((None, tr, C), lambda i, *ix: (ix[k][0], i, 0))

    return pl.pallas_call(
        body, name=name,
        grid_spec=pltpu.PrefetchScalarGridSpec(
            num_scalar_prefetch=5, grid=(N // tr,),
            in_specs=[at(0), at(1), at(2), at(3)], out_specs=at(4)),
        out_shape=jax.ShapeDtypeStruct((2, N, C), _F32),
        compiler_params=_cparams(("parallel",)),
    )(*idx, p, slots, slots, slots)


def sibling_join(rs, *, name):
    n = len(rs)

    def body(*refs):
        r_out, send, recv = refs[n:2 * n], refs[2 * n], refs[2 * n + 1]
        x, y, c = _place()
        sib = (x, y, 1 - c)
        cps = [_rcopy(r_out[p].at[c], r_out[p].at[c], send.at[p], recv.at[p], sib) for p in range(n)]
        for cp in cps:
            cp.start()
        for p in range(n):
            got = r_out[p].at[1 - c]
            _rcopy(got, got, send.at[p], recv.at[p], sib).wait_recv()
        for cp in cps:
            cp.wait_send()

    return pl.pallas_call(
        body, name=name, in_specs=[_ANY] * n, out_specs=[_ANY] * n,
        out_shape=[jax.ShapeDtypeStruct(a.shape, a.dtype) for a in rs],
        input_output_aliases={p: p for p in range(n)},
        scratch_shapes=[pltpu.SemaphoreType.DMA((n,))] * 2,
    )(*rs)


def all_devices_exchange(v, *, name):
    def body(v_ref, o_ref, send, recv, lsem):
        x, y, c = _place()
        me = 4 * x + 2 * y + c
        local = pltpu.make_async_copy(v_ref, o_ref.at[me], lsem)
        local.start()
        peers = []
        for k in range(1, 8):
            px, py, pc = x ^ (k >> 2), y ^ ((k >> 1) & 1), c ^ (k & 1)
            peers.append((px, py, pc))
        cps = [_rcopy(v_ref, o_ref.at[me], send.at[k], recv.at[k], peer) for k, peer in enumerate(peers)]
        for cp in cps:
            cp.start()
        for k, (px, py, pc) in enumerate(peers):
            got = o_ref.at[4 * px + 2 * py + pc]
            _rcopy(got, got, send.at[k], recv.at[k], (x, y, c)).wait_recv()
        for cp in cps:
            cp.wait_send()
        local.wait()

    return pl.pallas_call(
        body, name=name, in_specs=[_ANY], out_specs=_ANY,
        out_shape=jax.ShapeDtypeStruct((8,) + v.shape, v.dtype),
        scratch_shapes=[pltpu.SemaphoreType.DMA((7,))] * 2 + [pltpu.SemaphoreType.DMA(())],
    )(v)


def add_halves(gs_and_rs, c_idx, *, name):
    outs = []
    for n_, (g, r) in enumerate(gs_and_rs):
        N, C = r.shape
        tr = _tile(N, max(16, (512 * 1024) // C // 16 * 16), 16)

        def body(c_ref, g_ref, r_ref, o_ref):
            o_ref[...] = (g_ref[...].astype(_F32) + r_ref[...].astype(_F32)).astype(o_ref.dtype)

        outs.append(pl.pallas_call(
            body, name=f"{name}_{n_}",
            grid_spec=pltpu.PrefetchScalarGridSpec(
                num_scalar_prefetch=1, grid=(N // tr,),
                in_specs=[pl.BlockSpec((None, tr, C), lambda i, c: (c[0], i, 0)), pl.BlockSpec((tr, C), lambda i, c: (i, 0))],
                out_specs=pl.BlockSpec((tr, C), lambda i, c: (i, 0))),
            out_shape=jax.ShapeDtypeStruct((N, C), r.dtype),
            compiler_params=_cparams(("parallel",)),
        )(c_idx, g, r))
    return outs


def sum_slots(a, *, name):
    n, N, C = a.shape
    tr = _tile(N, max(16, (512 * 1024) // C // 16 * 16), 16)

    def body(a_ref, o_ref):
        s = a_ref[0].astype(_F32)
        for k in range(1, n):
            s = s + a_ref[k].astype(_F32)
        o_ref[...] = s

    return pl.pallas_call(
        body, name=name, grid=(N // tr,),
        in_specs=[pl.BlockSpec((n, tr, C), lambda i: (0, i, 0))],
        out_specs=pl.BlockSpec((tr, C), lambda i: (i, 0)),
        out_shape=jax.ShapeDtypeStruct((N, C), _F32),
        compiler_params=_cparams(("parallel",)),
    )(a)


_WEIGHTS = ['ln_in_g', 'ln_in_b', 'w_in', 'q_norm_g', 'w_uq', 'kv_norm_g', 'w_ukv', 'conv_w', 'conv_b', 'conv_ln_g',
            'conv_ln_b', 'w_pool', 'pool_scale', 'w_out', 'ln1_g', 'ln1_b', 'w_up', 'ffn_conv_w', 'ffn_conv_b', 'w_down',
            'ln2_g', 'ln2_b']
_BIG = ['w_in', 'w_uq', 'w_ukv', 'w_out', 'w_up', 'w_down']
_SMALL_SHARDED = ['conv_w', 'ffn_conv_w']
_SMALL = [n for n in _WEIGHTS if n not in _BIG]


def _rope_tables(positions):
    half = QK_ROPE // 2
    inv = 1.0 / (ROPE_THETA ** (jnp.arange(0, QK_ROPE, 2, dtype=_F32) / QK_ROPE))
    ang = positions.reshape(-1).astype(_F32)[:, None] * inv
    c, s = jnp.cos(ang), jnp.sin(ang)
    z = jnp.zeros_like(c)
    cc = jnp.concatenate([c, c, z, z], axis=1)
    sa = jnp.concatenate([-s, z, z, z], axis=1)
    sb = jnp.concatenate([z, s, z, z], axis=1)
    assert cc.shape[1] == 128 and half == 32
    return cc, sa, sb


def _in_pad(dims):
    D, QL, KVL, CW, PW, H, F = dims
    return (-(QL + 2 * CW + PW + KVL + 128)) % 512


def _layer_weights(full, dims):
    D, QL, KVL, CW, PW, H, F = dims
    w_in = full['w_in'].transpose(1, 0, 2).reshape(D, -1)
    o1, o2, o3, o4 = QL, QL + KVL, QL + KVL + QK_ROPE, QL + KVL + QK_ROPE + 2 * CW
    w_in_p = jnp.concatenate([w_in[:, :o1], w_in[:, o3:o4], w_in[:, o4:], w_in[:, o1:o2], w_in[:, o2:o3],
                              jnp.zeros((D, 128 - QK_ROPE + _in_pad(dims)), w_in.dtype)], axis=1)
    w_uq = full['w_uq'].reshape(QL, H, QK_NOPE + QK_ROPE)
    w_uq_p = jnp.pad(w_uq, ((0, 0), (0, 0), (0, HEAD_PAD - QK_NOPE - QK_ROPE))).reshape(QL, H * HEAD_PAD)
    return dict(
        w_in=w_in_p, w_uq=w_uq_p,
        w_ukv=full['w_ukv'].reshape(KVL, H * (QK_NOPE + V_HEAD)),
        w_out=full['w_out'].reshape(D, D),
        w_up=full['w_up'].transpose(1, 0, 2).reshape(D, 2 * F),
        w_down=full['w_down'].reshape(F, D),
    )


def _unpermute_w_in_grad(g, dims):
    D, QL, KVL, CW, PW, H, F = dims
    a, b_, c_ = QL, QL + 2 * CW, QL + 2 * CW + PW
    return jnp.concatenate([g[:, :a], g[:, c_:c_ + KVL], g[:, c_ + KVL:c_ + KVL + QK_ROPE], g[:, a:b_], g[:, b_:c_]], axis=1)


class _NoExchange:
    def __init__(self, layer_weights):
        self.layer_weights, self.grads = layer_weights, {}

    def weights(self, l):
        return self.layer_weights[l]

    def side(self, where, l):
        return None

    def side_done(self, where, l, outs):
        pass

    def grads_ready(self, l, g):
        self.grads[l] = g


def _with_side(hooks, where, l, fn):
    sd = hooks.side(where, l)
    res = fn(sd)
    if sd is None:
        return res
    hooks.side_done(where, l, res[-1])
    return res[0] if len(res) == 2 else res[:-1]


def _local_step(x, positions, target, small, dims, B, S, L, hooks):
    D, QL, KVL, CW, PW, H, F = dims
    T = B * S
    alpha = (2.0 * L) ** 0.25
    scale = float(QK_NOPE + QK_ROPE) ** -0.5
    cc, sa, sb = _rope_tables(positions)
    cb_q, cb_a, cb_g, cb_p = 0, QL // CW, QL // CW + 1, (QL + 2 * CW) // PW
    cb_kv, cb_kr = (QL + 2 * CW + PW) // KVL, (QL + 2 * CW + PW + KVL) // 128
    assert QL % CW == 0 and (QL + 2 * CW) % PW == 0 and (QL + 2 * CW + PW) % KVL == 0 and (QL + 2 * CW + PW + KVL) % 128 == 0

    xs, xb = ln_fwd([x], [1.0], small['ln_in_g'], small['ln_in_b'], want_r=False, name="ln_in")
    saved = []
    fa = dict(B=B, S=S, H=H, scale=scale)
    for l in range(L):
        W = hooks.weights(l)
        h = matmul(xb, W['w_in'], name="mm_in")
        qn = rms_fwd(h, cb_q, QL, small['q_norm_g'][l], name="rms_q")
        kvn = rms_fwd(h, cb_kv, KVL, small['kv_norm_g'][l], name="rms_kv")
        q = matmul(qn, W['w_uq'], name="mm_uq")
        kv = matmul(kvn, W['w_ukv'], name="mm_ukv")
        qp, kp, v = mla_pack(q, kv, h, cb_kr, cc, sa, sb, H=H, scale=scale, name="mla_pack")
        o, lse = _with_side(hooks, 'flash', l, lambda sd: flash_fwd(qp, kp, v, side=sd, name="flash_fwd", **fa))
        z, yc = conv_fwd(h, cb_a, cb_g, small['conv_w'][l], small['conv_b'][l], small['conv_ln_g'][l],
                         small['conv_ln_b'][l], B=B, S=S, name="conv_fwd")
        yp = pool_fwd(h, cb_p, small['w_pool'][l], small['pool_scale'][l], B=B, S=S, name="pool_fwd")
        mixed = jnp.concatenate([o.astype(_BF), yc, yp], axis=1)
        y1 = matmul(mixed, W['w_out'], name="mm_out")
        r1, x1, x1b = ln_fwd([xs, y1], [alpha, 1.0], small['ln1_g'][l], small['ln1_b'][l], want_r=True, name="ln1")
        up = _with_side(hooks, 'up', l, lambda sd: matmul(x1b, W['w_up'], out_dtype=_BF, side=sd, name="mm_up"))
        act = gate_fwd(up, small['ffn_conv_w'][l], small['ffn_conv_b'][l], B=B, S=S, name="gate_fwd")
        y2 = _with_side(hooks, 'down', l, lambda sd: matmul(act, W['w_down'], side=sd, name="mm_down"))
        r2, x2, x2b = ln_fwd([x1, y2], [alpha, 1.0], small['ln2_g'][l], small['ln2_b'][l], want_r=True, name="ln2")
        saved.append(dict(W=W, xb=xb, h=h, qn=qn, kvn=kvn, qp=qp, kp=kp, v=v, o=o, lse=lse, z=z, mixed=mixed, r1=r1,
                          x1b=x1b, up=up, act=act, r2=r2))
        xs, xb = x2, x2b

    dy, loss_cols = loss_head(xs, target, name="loss_head")
    gs = {n: [None] * L for n in _SMALL if n not in ('ln_in_g', 'ln_in_b')}
    d_terms, d_coefs = [dy], [1.0]
    zpad = jnp.zeros((T, _in_pad(dims)), _BF) if _in_pad(dims) else None
    for l in reversed(range(L)):
        sv = saved[l]
        W = sv['W']
        gb = {}
        dr2, dr2b, gs['ln2_g'][l], gs['ln2_b'][l] = ln_bwd(d_terms, d_coefs, sv['r2'], small['ln2_g'][l], name="ln2_bwd")
        dact = matmul(dr2b, W['w_down'], tb=True, out_dtype=_BF, name="mm_down_dx")
        gb['w_down'] = matmul(sv['act'], dr2b, ta=True, out_dtype=_BF, name="mm_down_dw")
        dua, dug, dwa, dwg, dba, dbg = gate_bwd(sv['up'], dact, small['ffn_conv_w'][l], small['ffn_conv_b'][l],
                                                B=B, S=S, name="gate_bwd")
        gs['ffn_conv_w'][l] = jnp.concatenate([dwa, dwg], axis=1)
        gs['ffn_conv_b'][l] = jnp.concatenate([dba, dbg], axis=1)
        dup = jnp.concatenate([dua, dug], axis=1)
        gb['w_up'] = _with_side(hooks, 'up_dw', l, lambda sd: matmul(sv['x1b'], dup, ta=True, out_dtype=_BF, side=sd,
                                                                     name="mm_up_dw"))
        dx1 = _with_side(hooks, 'up_dx', l, lambda sd: matmul(dup, W['w_up'], tb=True, side=sd, name="mm_up_dx"))
        dr1, dr1b, gs['ln1_g'][l], gs['ln1_b'][l] = ln_bwd([dr2, dx1], [alpha, 1.0], sv['r1'], small['ln1_g'][l],
                                                            name="ln1_bwd")
        dmix = matmul(dr1b, W['w_out'], tb=True, name="mm_out_dx")
        gb['w_out'] = matmul(sv['mixed'], dr1b, ta=True, out_dtype=_BF, name="mm_out_dw")
        h = sv['h']
        ncb = (H * V_HEAD) // CW
        dca, dcg, gs['conv_w'][l], gs['conv_b'][l], gs['conv_ln_g'][l], gs['conv_ln_b'][l] = conv_bwd(
            dmix, ncb, sv['z'], h, cb_a, cb_g, small['conv_w'][l], small['conv_ln_g'][l], small['conv_ln_b'][l],
            B=B, S=S, name="conv_bwd")
        dpool, gs['w_pool'][l], gs['pool_scale'][l] = pool_bwd(
            dmix, (H * V_HEAD + CW) // PW, h, cb_p, small['w_pool'][l], small['pool_scale'][l], B=B, S=S, name="pool_bwd")
        dqp = flash_bwd_dq(sv['qp'], sv['kp'], sv['v'], sv['o'], sv['lse'], dmix, 0, name="flash_dq", **fa)
        dkp, dv = flash_bwd_dkv(sv['qp'], sv['kp'], sv['v'], sv['o'], sv['lse'], dmix, 0, name="flash_dkv", **fa)
        dq, dkv, dkr = mla_unpack(dqp, dkp, dv, cc, sa, sb, H=H, name="mla_unpack")
        dqn = matmul(dq, W['w_uq'], tb=True, name="mm_uq_dx")
        g_uq = matmul(sv['qn'], dq, ta=True, out_dtype=_BF, name="mm_uq_dw")
        gb['w_uq'] = g_uq.reshape(QL, H, HEAD_PAD)[:, :, :QK_NOPE + QK_ROPE].reshape(QL, -1)
        dkvn = matmul(dkv, W['w_ukv'], tb=True, name="mm_ukv_dx")
        gb['w_ukv'] = matmul(sv['kvn'], dkv, ta=True, out_dtype=_BF, name="mm_ukv_dw")
        dcq, gs['q_norm_g'][l] = rms_bwd(dqn, h, cb_q, QL, small['q_norm_g'][l], name="rms_q_bwd")
        dckv, gs['kv_norm_g'][l] = rms_bwd(dkvn, h, cb_kv, KVL, small['kv_norm_g'][l], name="rms_kv_bwd")
        dh = jnp.concatenate([dcq, dca, dcg, dpool, dckv, dkr] + ([zpad] if zpad is not None else []), axis=1)
        gb['w_in'] = _unpermute_w_in_grad(matmul(sv['xb'], dh, ta=True, out_dtype=_BF, name="mm_in_dw"), dims)
        dxm = matmul(dh, W['w_in'], tb=True, name="mm_in_dx")
        hooks.grads_ready(l, gb)
        d_terms, d_coefs = [dr1, dxm], [alpha, 1.0]
    gx, _, g_ln_g, g_ln_b = ln_bwd(d_terms, d_coefs, x, small['ln_in_g'], name="ln_in_bwd")
    gsm = {n: jnp.stack([a.reshape(small[n].shape[1:]) for a in gs[n]]) for n in gs}
    gsm['ln_in_g'], gsm['ln_in_b'] = g_ln_g.reshape(-1), g_ln_b.reshape(-1)
    return loss_cols, gx, gsm


_COL_SHARDED = ('w_in', 'w_up')


def _flat_pad(arrs, mult=512 * 128):
    v = jnp.concatenate([a.reshape(-1) for a in arrs])
    n = v.shape[0]
    return jnp.pad(v, (0, (-n) % mult)).reshape(-1, 128)


def _split_like(flat, like):
    out, off = [], 0
    v = flat.reshape(-1)
    for a in like:
        out.append(v[off:off + a.size].reshape(a.shape))
        off += a.size
    return out


_GROUP_A = ('w_up',)
_GROUP_B = tuple(n for n in _BIG if n not in _GROUP_A)


class _Exchange:
    def __init__(self, a, dims, L, chip_idx, c_idx, sum_idx):
        self.dims, self.L, self.c_idx, self.sum_idx = dims, L, c_idx, sum_idx
        self.rows = {n: (a[n].shape[1], math.prod(a[n].shape[2:])) for n in _BIG}
        self.bufs = []
        for l in range(L):
            self.bufs.append({n: place_shard(a[n].reshape(L, 2, self.rows[n][0] // 2, self.rows[n][1]), l, chip_idx,
                                             name="place_" + n) for n in _BIG})
        self._store(0, _BIG, run_side(gather_stage1_side(self._list(0, _BIG)), name="gather_first_ici"))
        self._store(0, _BIG, run_side(gather_stage2_side(self._list(0, _BIG)), name="gather_first_d2d"))
        self.pending = None
        self.reduced = {}

    def _list(self, l, names):
        return [self.bufs[l][n] for n in names]

    def _store(self, l, names, outs):
        self.bufs[l].update(zip(names, outs))

    def weights(self, l):
        return _layer_weights({n: self.bufs[l][n].reshape(4, *self.rows[n]) for n in _BIG}, self.dims)

    def side(self, where, l):
        if where in ('flash', 'up', 'down'):
            if l + 1 >= self.L:
                return None
            if where == 'down':
                return gather_stage2_side(self._list(l + 1, _BIG))
            return gather_stage1_side(self._list(l + 1, _GROUP_B if where == 'flash' else _GROUP_A))
        if self.pending is None:
            return None
        return chip_exchange_side([self.pending[1][n] for n in (_GROUP_B if where == 'up_dw' else _GROUP_A)])

    def side_done(self, where, l, outs):
        if where in ('flash', 'up', 'down'):
            self._store(l + 1, {'flash': _GROUP_B, 'up': _GROUP_A, 'down': _BIG}[where], outs)
            return
        self.pending[2].update(zip(_GROUP_B if where == 'up_dw' else _GROUP_A, outs))
        if where == 'up_dx':
            self._finish()

    def _finish(self):
        l, sums, slots = self.pending
        halves = [sum_chips(sums[n], slots[n], self.sum_idx, name="grad_sum_" + n) for n in _BIG]
        joined = sibling_join(halves, name="grad_sibling_join")
        self.reduced[l] = {n: j.reshape(self.rows[n]) for n, j in zip(_BIG, joined)}
        self.pending = None

    def grads_ready(self, l, g):
        g_in = []
        for n in _BIG:
            r, c = self.rows[n]
            st = g[n].reshape(g[n].shape[0], 4, c).transpose(1, 0, 2) if n in _COL_SHARDED else g[n].reshape(4, r, c)
            g_in.append(st.reshape(4, 2, r // 2, c).transpose(1, 0, 2, 3).reshape(2, 2 * r, c))
        from_sib = sibling_send_half(g_in, name="grad_sibling_send")
        sums = add_halves(list(zip(g_in, from_sib)), self.c_idx, name="grad_presum")
        self.pending = (l, {n: p.reshape(4, p.shape[0] // 4, p.shape[1]) for n, p in zip(_BIG, sums)}, {})
        if l == 0:
            outs = run_side(chip_exchange_side([self.pending[1][n] for n in _BIG]), name="grad_chip_exchange_last")
            self.pending[2].update(zip(_BIG, outs))
            self._finish()


def kernel(x, positions, ln_in_g, ln_in_b, w_in, q_norm_g, w_uq, kv_norm_g, w_ukv, conv_w, conv_b, conv_ln_g, conv_ln_b, w_pool, pool_scale, w_out, ln1_g, ln1_b, w_up, ffn_conv_w, ffn_conv_b, w_down, ln2_g, ln2_b, loss_target, m_ln_in_g, m_ln_in_b, m_w_in, m_q_norm_g, m_w_uq, m_kv_norm_g, m_w_ukv, m_conv_w, m_conv_b, m_conv_ln_g, m_conv_ln_b, m_w_pool, m_pool_scale, m_w_out, m_ln1_g, m_ln1_b, m_w_up, m_ffn_conv_w, m_ffn_conv_b, m_w_down, m_ln2_g, m_ln2_b, v_ln_in_g, v_ln_in_b, v_w_in, v_q_norm_g, v_w_uq, v_kv_norm_g, v_w_ukv, v_conv_w, v_conv_b, v_conv_ln_g, v_conv_ln_b, v_w_pool, v_pool_scale, v_w_out, v_ln1_g, v_ln1_b, v_w_up, v_ffn_conv_w, v_ffn_conv_b, v_w_down, v_ln2_g, v_ln2_b):
    a = dict(locals())
    B, S, D = a['x'].shape
    T = B * S
    L = a['w_in'].shape[0]
    QL, H = 4 * a['w_uq'].shape[1], a['w_uq'].shape[2]
    KVL = 4 * a['w_ukv'].shape[1]
    CW, PW = a['conv_b'].shape[1], a['pool_scale'].shape[1]
    F = 4 * a['w_down'].shape[1]
    dims = (D, QL, KVL, CW, PW, H, F)
    chip = 2 * lax.axis_index("x") + lax.axis_index("y")
    c_idx = lax.axis_index("c").astype(jnp.int32).reshape(1)

    def shard2d(w):
        return w.reshape(-1, w.shape[-1]) if w.ndim == 3 else w.reshape(w.shape[0] * w.shape[1], -1)

    small = {n: a[n] for n in _SMALL}
    for n, o in zip(_SMALL_SHARDED, gather_small([shard2d(a[n]) for n in _SMALL_SHARDED], name="gather_small")):
        k = a[n].shape[1]
        small[n] = o.reshape(4, L, k, -1).transpose(1, 2, 0, 3).reshape(L, k, -1)
    xi, yi = lax.axis_index("x"), lax.axis_index("y")
    chip_idx = chip.astype(jnp.int32).reshape(1)
    sum_idx = [v.astype(jnp.int32).reshape(1) for v in [chip] + _other_chips(xi, yi)[1] + [lax.axis_index("c")]]
    ex = _Exchange(a, dims, L, chip_idx, c_idx, sum_idx)

    loss_cols, gx, gsm = _local_step(a['x'].reshape(T, D), a['positions'], a['loss_target'].reshape(T, D),
                                     small, dims, B, S, L, ex)
    loss = lax.psum(jnp.sum(loss_cols), ("x", "y", "c"))
    g_big = {n: jnp.concatenate([ex.reduced[l][n] for l in range(L)]).reshape(a[n].shape) for n in _BIG}

    sm_like = [gsm[n] for n in _SMALL]
    sm_sum = sum_slots(all_devices_exchange(_flat_pad(sm_like), name="small_exchange"), name="small_sum")
    g_small = dict(zip(_SMALL, _split_like(sm_sum, sm_like)))
    for n in _SMALL_SHARDED:
        w = a[n].shape[-1]
        g_small[n] = lax.dynamic_slice_in_dim(g_small[n], chip * w, w, axis=2)

    grads, delta, new_m, new_v = {}, {}, {}, {}
    for n in _BIG:
        grads[n] = g_big[n]
        d_, m_, v_ = adamw(shard2d(a[n]), shard2d(g_big[n]), shard2d(a['m_' + n]), shard2d(a['v_' + n]), name="adamw_" + n)
        delta[n], new_m[n], new_v[n] = (t.reshape(a[n].shape) for t in (d_, m_, v_))
    def at_least_2d(t):
        return t.reshape(1, -1) if t.ndim == 1 else t

    d_, m_, v_ = adamw_small(*[[at_least_2d(src[n]) for n in _SMALL] for src in (
        a, g_small, {n: a['m_' + n] for n in _SMALL}, {n: a['v_' + n] for n in _SMALL})], name="adamw_small")
    for n, dd, mm, vv in zip(_SMALL, d_, m_, v_):
        grads[n], delta[n], new_m[n], new_v[n] = g_small[n], dd.reshape(a[n].shape), mm.reshape(a[n].shape), vv.reshape(a[n].shape)

    return (loss, gx.reshape(B, S, D), *[grads[n] for n in _WEIGHTS], *[delta[n] for n in _WEIGHTS],
            *[new_m[n] for n in _WEIGHTS], *[new_v[n] for n in _WEIGHTS])
```

```python
import functools
import math

import jax
import jax.numpy as jnp
from jax import lax
from jax.experimental import pallas as pl
from jax.experimental.pallas import tpu as pltpu

_BF = jnp.bfloat16
_F32 = jnp.float32
_VMEM_LIMIT = 56 * 1024 * 1024

QK_NOPE = 128
QK_ROPE = 64
V_HEAD = 128
HEAD_PAD = 256
ROPE_THETA = 10000.0
LN_EPS = 1e-5
RMS_EPS = 1e-6
POOL_WINDOWS = (2, 4, 8, 16)
ADAM_LR, ADAM_B1, ADAM_B2, ADAM_EPS, ADAM_WD, ADAM_STEP = 0.001, 0.9, 0.999, 1e-8, 0.01, 10


def _cparams(sem=None):
    kw = dict(vmem_limit_bytes=_VMEM_LIMIT)
    if sem is not None:
        kw["dimension_semantics"] = sem
    return pltpu.CompilerParams(**kw)


def _tile(n, target, unit=128):
    if n <= target:
        return n
    t = (target // unit) * unit
    while t >= unit:
        if n % t == 0:
            return t
        t -= unit
    return n


_MM_VMEM_BUDGET = 40 * 1024 * 1024


def matmul(a, b, *, ta=False, tb=False, out_dtype=_F32, tm=1024, tn=1536, tk=4096, side=None, a_halves=False,
           b_halves=False, name="mm"):
    assert not (a_halves and ta) and not (b_halves and tb)
    if a_halves:
        M, K = a.shape[1], 2 * a.shape[2]
    elif ta:
        K, M = a.shape
    else:
        M, K = a.shape
    if b_halves:
        K2, N = b.shape[1], 2 * b.shape[2]
    elif tb:
        N, K2 = b.shape
    else:
        K2, N = b.shape
    assert K == K2, (a.shape, b.shape, ta, tb)
    tm, tn, tk = _tile(M, tm), _tile(N // 2 if b_halves else N, tn), _tile(K // 2 if a_halves else K, tk)
    ab, bb, ob = a.dtype.itemsize, b.dtype.itemsize, jnp.dtype(out_dtype).itemsize

    def vmem(tk_):
        return 2 * (tm * tk_ * ab + tk_ * tn * bb) + 2 * tm * tn * ob + tm * tn * 4 * (2 if K // tk_ > 1 else 1)

    k_part = K // 2 if a_halves else K
    while vmem(tk) > _MM_VMEM_BUDGET and tk > 256 and _tile(k_part, tk // 2) < tk:
        tk = _tile(k_part, tk // 2)
    nk = K // tk
    dn = (((0,) if ta else (1,), (1,) if tb else (0,)), ((), ()))

    s_ops, s_in, s_out, s_shapes, s_scratch, s_alias = _side_specs(side)
    ni, no, nacc = len(s_ops), len(s_shapes), int(nk > 1)
    grid = (M // tm, N // tn, nk)

    def body(a_ref, b_ref, *rest):
        s_ins, o_ref, s_outs = rest[:ni], rest[ni], rest[ni + 1:ni + 1 + no]
        acc, sems = rest[ni + 1 + no:ni + 1 + no + nacc], rest[ni + 1 + no + nacc:]
        i, j, k = pl.program_id(0), pl.program_id(1), pl.program_id(2)
        if side is not None:
            @pl.when((i == 0) & (j == 0) & (k == 0))
            def _():
                side.start(s_ins, s_outs, sems)

        prod = lax.dot_general(a_ref[...].astype(_BF), b_ref[...].astype(_BF), dn, preferred_element_type=_F32)
        if nk == 1:
            o_ref[...] = prod.astype(o_ref.dtype)
        else:
            acc_ref = acc[0]

            @pl.when(k == 0)
            def _():
                acc_ref[...] = prod

            @pl.when(k > 0)
            def _():
                acc_ref[...] += prod

            @pl.when(k == nk - 1)
            def _():
                o_ref[...] = acc_ref[...].astype(o_ref.dtype)

        if side is not None:
            @pl.when((i == grid[0] - 1) & (j == grid[1] - 1) & (k == nk - 1))
            def _():
                side.wait(s_ins, s_outs, sems)

    a_spec = pl.BlockSpec((tk, tm), lambda i, j, k: (k, i)) if ta else pl.BlockSpec((tm, tk), lambda i, j, k: (i, k))
    b_spec = pl.BlockSpec((tn, tk), lambda i, j, k: (j, k)) if tb else pl.BlockSpec((tk, tn), lambda i, j, k: (k, j))
    if a_halves:
        kh = nk // 2
        a_spec = pl.BlockSpec((None, tm, tk), lambda i, j, k: (k // kh, i, k % kh))
    if b_halves:
        jh = N // tn // 2
        b_spec = pl.BlockSpec((None, tk, tn), lambda i, j, k: (j // jh, k, j % jh))
    res = pl.pallas_call(
        body, name=name,
        grid=grid,
        in_specs=[a_spec, b_spec] + s_in,
        out_specs=[pl.BlockSpec((tm, tn), lambda i, j, k: (i, j))] + s_out,
        out_shape=[jax.ShapeDtypeStruct((M, N), out_dtype)] + s_shapes,
        input_output_aliases={2 + i_: 1 + o_ for i_, o_ in s_alias.items()},
        scratch_shapes=([pltpu.VMEM((tm, tn), _F32)] if nk > 1 else []) + s_scratch,
        compiler_params=_cparams(("arbitrary",) * 3 if side is not None else ("parallel", "parallel", "arbitrary")),
    )(a, b, *s_ops)
    return res[0] if side is None else (res[0], list(res[1:]))


def _row_tile(T, C, budget_rows=256):
    return _tile(T, budget_rows, 16)


def ln_fwd(xs, coefs, g, b, *, want_r, name):
    T, C = xs[0].shape
    tr = _row_tile(T, C)
    n = len(xs)

    def body(*refs):
        x_refs, (g_ref, b_ref), outs = refs[:n], refs[n:n + 2], refs[n + 2:]
        r = coefs[0] * x_refs[0][...]
        for c, xr in zip(coefs[1:], x_refs[1:]):
            r = r + c * xr[...]
        mu = jnp.mean(r, axis=-1, keepdims=True)
        d = r - mu
        var = jnp.mean(d * d, axis=-1, keepdims=True)
        y = d * lax.rsqrt(var + LN_EPS) * g_ref[...] + b_ref[...]
        if want_r:
            outs[0][...] = r
        outs[-2][...] = y
        outs[-1][...] = y.astype(_BF)

    row = pl.BlockSpec((tr, C), lambda i: (i, 0))
    vec = pl.BlockSpec((1, C), lambda i: (0, 0))
    f = jax.ShapeDtypeStruct((T, C), _F32)
    out_shape = ([f] if want_r else []) + [f, jax.ShapeDtypeStruct((T, C), _BF)]
    return pl.pallas_call(
        body, name=name, grid=(T // tr,),
        in_specs=[row] * n + [vec, vec],
        out_specs=[row] * len(out_shape), out_shape=out_shape,
        compiler_params=_cparams(("parallel",)),
    )(*xs, g.reshape(1, C), b.reshape(1, C))


def ln_bwd(dys, coefs, r, g, *, name):
    T, C = r.shape
    tr = _row_tile(T, C)
    n = len(dys)

    def body(*refs):
        dy_refs, r_ref, g_ref = refs[:n], refs[n], refs[n + 1]
        dr_ref, drb_ref, dg_ref, db_ref = refs[n + 2:]
        dy = coefs[0] * dy_refs[0][...]
        for c, dr_ in zip(coefs[1:], dy_refs[1:]):
            dy = dy + c * dr_[...]
        rr = r_ref[...]
        mu = jnp.mean(rr, axis=-1, keepdims=True)
        d = rr - mu
        var = jnp.mean(d * d, axis=-1, keepdims=True)
        rstd = lax.rsqrt(var + LN_EPS)
        xh = d * rstd
        gdy = dy * g_ref[...]
        m1 = jnp.mean(gdy, axis=-1, keepdims=True)
        m2 = jnp.mean(gdy * xh, axis=-1, keepdims=True)
        dr = rstd * (gdy - m1 - xh * m2)
        dr_ref[...] = dr
        drb_ref[...] = dr.astype(_BF)

        @pl.when(pl.program_id(0) == 0)
        def _():
            dg_ref[...] = jnp.zeros_like(dg_ref)
            db_ref[...] = jnp.zeros_like(db_ref)

        dg_ref[...] += jnp.sum(dy * xh, axis=0, keepdims=True)
        db_ref[...] += jnp.sum(dy, axis=0, keepdims=True)

    row = pl.BlockSpec((tr, C), lambda i: (i, 0))
    vec = pl.BlockSpec((1, C), lambda i: (0, 0))
    return pl.pallas_call(
        body, name=name, grid=(T // tr,),
        in_specs=[row] * (n + 1) + [vec],
        out_specs=[row, row, vec, vec],
        out_shape=[jax.ShapeDtypeStruct((T, C), _F32), jax.ShapeDtypeStruct((T, C), _BF),
                   jax.ShapeDtypeStruct((1, C), _F32), jax.ShapeDtypeStruct((1, C), _F32)],
        compiler_params=_cparams(("arbitrary",)),
    )(*dys, r, g.reshape(1, C))


def rms_fwd(h, cb, W, g, *, name):
    T = h.shape[0]
    tr = _tile(T, 512, 16)

    def body(c_ref, g_ref, o_ref):
        c = c_ref[...]
        ms = jnp.mean(c * c, axis=-1, keepdims=True)
        o_ref[...] = (c * lax.rsqrt(ms + RMS_EPS) * g_ref[...]).astype(_BF)

    return pl.pallas_call(
        body, name=name, grid=(T // tr,),
        in_specs=[pl.BlockSpec((tr, W), lambda i: (i, cb)), pl.BlockSpec((1, W), lambda i: (0, 0))],
        out_specs=pl.BlockSpec((tr, W), lambda i: (i, 0)),
        out_shape=jax.ShapeDtypeStruct((T, W), _BF),
        compiler_params=_cparams(("parallel",)),
    )(h, g.reshape(1, W))


def rms_bwd(dy, h, cb, W, g, *, name):
    T = h.shape[0]
    tr = _tile(T, 512, 16)

    def body(dy_ref, c_ref, g_ref, dc_ref, dg_ref):
        c = c_ref[...]
        dyv = dy_ref[...]
        ms = jnp.mean(c * c, axis=-1, keepdims=True)
        r = lax.rsqrt(ms + RMS_EPS)
        u = dyv * g_ref[...]
        m = jnp.mean(c * u, axis=-1, keepdims=True)
        dc_ref[...] = (r * u - c * (r * r * r) * m).astype(_BF)

        @pl.when(pl.program_id(0) == 0)
        def _():
            dg_ref[...] = jnp.zeros_like(dg_ref)

        dg_ref[...] += jnp.sum(dyv * c * r, axis=0, keepdims=True)

    return pl.pallas_call(
        body, name=name, grid=(T // tr,),
        in_specs=[pl.BlockSpec((tr, W), lambda i: (i, 0)), pl.BlockSpec((tr, W), lambda i: (i, cb)),
                  pl.BlockSpec((1, W), lambda i: (0, 0))],
        out_specs=[pl.BlockSpec((tr, W), lambda i: (i, 0)), pl.BlockSpec((1, W), lambda i: (0, 0))],
        out_shape=[jax.ShapeDtypeStruct((T, W), _BF), jax.ShapeDtypeStruct((1, W), _F32)],
        compiler_params=_cparams(("arbitrary",)),
    )(dy, h, g.reshape(1, W))


def _rope(u, cc, sa, sb, sign):
    return u * cc + sign * (pltpu.roll(u, 96, 1) * sa + pltpu.roll(u, 32, 1) * sb)


def mla_pack(q, kv, h, kr_cb, cc, sa, sb, *, H, scale, name):
    T = q.shape[0]
    tr = _tile(T, 256, 16)

    def body(q_ref, kv_ref, kr_ref, cc_ref, sa_ref, sb_ref, qp_ref, kp_ref, v_ref):
        cc_, sa_, sb_ = cc_ref[...], sa_ref[...], sb_ref[...]
        kr = _rope(kr_ref[...], cc_, sa_, sb_, 1.0).astype(_BF)
        for hh in range(H):
            o = hh * HEAD_PAD
            qp_ref[:, o:o + 128] = (q_ref[:, o:o + 128] * scale).astype(_BF)
            qp_ref[:, o + 128:o + 256] = (_rope(q_ref[:, o + 128:o + 256], cc_, sa_, sb_, 1.0) * scale).astype(_BF)
            kp_ref[:, o:o + 128] = kv_ref[:, o:o + 128].astype(_BF)
            kp_ref[:, o + 128:o + 256] = kr
            v_ref[:, hh * 128:(hh + 1) * 128] = kv_ref[:, o + 128:o + 256].astype(_BF)

    wide = pl.BlockSpec((tr, H * HEAD_PAD), lambda i: (i, 0))
    tab = pl.BlockSpec((tr, 128), lambda i: (i, 0))
    return pl.pallas_call(
        body, name=name, grid=(T // tr,),
        in_specs=[wide, wide, pl.BlockSpec((tr, 128), lambda i: (i, kr_cb)), tab, tab, tab],
        out_specs=[wide, wide, pl.BlockSpec((tr, H * 128), lambda i: (i, 0))],
        out_shape=[jax.ShapeDtypeStruct((T, H * HEAD_PAD), _BF), jax.ShapeDtypeStruct((T, H * HEAD_PAD), _BF),
                   jax.ShapeDtypeStruct((T, H * 128), _BF)],
        compiler_params=_cparams(("parallel",)),
    )(q, kv, h, cc, sa, sb)


def mla_unpack(dqp, dkp, dv, cc, sa, sb, *, H, name):
    T = dqp.shape[0]
    tr = _tile(T, 256, 16)

    def body(dq_ref, dk_ref, dv_ref, cc_ref, sa_ref, sb_ref, oq_ref, okv_ref, okr_ref):
        cc_, sa_, sb_ = cc_ref[...], sa_ref[...], sb_ref[...]
        kr = jnp.zeros((tr, 128), _F32)
        for hh in range(H):
            o = hh * HEAD_PAD
            oq_ref[:, o:o + 128] = dq_ref[:, o:o + 128].astype(_BF)
            oq_ref[:, o + 128:o + 256] = _rope(dq_ref[:, o + 128:o + 256], cc_, sa_, sb_, -1.0).astype(_BF)
            okv_ref[:, o:o + 128] = dk_ref[:, o:o + 128].astype(_BF)
            okv_ref[:, o + 128:o + 256] = dv_ref[:, hh * 128:(hh + 1) * 128].astype(_BF)
            kr = kr + dk_ref[:, o + 128:o + 256]
        okr_ref[...] = _rope(kr, cc_, sa_, sb_, -1.0).astype(_BF)

    wide = pl.BlockSpec((tr, H * HEAD_PAD), lambda i: (i, 0))
    tab = pl.BlockSpec((tr, 128), lambda i: (i, 0))
    return pl.pallas_call(
        body, name=name, grid=(T // tr,),
        in_specs=[wide, wide, pl.BlockSpec((tr, H * 128), lambda i: (i, 0)), tab, tab, tab],
        out_specs=[wide, wide, tab],
        out_shape=[jax.ShapeDtypeStruct((T, H * HEAD_PAD), _BF), jax.ShapeDtypeStruct((T, H * HEAD_PAD), _BF),
                   jax.ShapeDtypeStruct((T, 128), _BF)],
        compiler_params=_cparams(("parallel",)),
    )(dqp, dkp, dv, cc, sa, sb)


_NEG = -1e30


def _rows(ref, j, t):
    return ref[pl.ds(pl.multiple_of(j * t, t), t), :]


def _qk(q, k):
    return lax.dot_general(q, k, (((1,), (1,)), ((), ())), preferred_element_type=_F32)


def _scores(q, k, t, masked):
    s = _qk(q, k)
    if not masked:
        return s
    row = lax.broadcasted_iota(jnp.int32, (t, t), 0)
    col = lax.broadcasted_iota(jnp.int32, (t, t), 1)
    return jnp.where(col <= row, s, _NEG)


def flash_fwd(qp, kp, v, *, B, S, H, scale, side=None, name):
    T = B * S
    t = _tile(S, 512, 128)
    nq = S // t
    s_ops, s_in, s_out, s_shapes, s_scratch, s_alias = _side_specs(side)
    ni, no = len(s_ops), len(s_shapes)

    def body(q_ref, k_ref, v_ref, *rest):
        s_ins, (o_ref, lse_ref), s_outs = rest[:ni], rest[ni:ni + 2], rest[ni + 2:ni + 2 + no]
        (m_sc, l_sc, acc_sc), sems = rest[ni + 2 + no:ni + 5 + no], rest[ni + 5 + no:]
        i = pl.program_id(2)
        first = (pl.program_id(0) == 0) & (pl.program_id(1) == 0) & (i == 0)
        last = (pl.program_id(0) == B - 1) & (pl.program_id(1) == H - 1) & (i == nq - 1)
        if side is not None:
            @pl.when(first)
            def _():
                side.start(s_ins, s_outs, sems)

        m_sc[...] = jnp.full_like(m_sc, _NEG)
        l_sc[...] = jnp.zeros_like(l_sc)
        acc_sc[...] = jnp.zeros_like(acc_sc)

        def step(j, masked):
            s = _scores(q_ref[...], _rows(k_ref, j, t), t, masked)
            m_old = m_sc[...]
            m_new = jnp.maximum(m_old, jnp.max(s, axis=-1, keepdims=True))
            p = jnp.exp(s - m_new)
            a = jnp.exp(m_old - m_new)
            l_sc[...] = a * l_sc[...] + jnp.sum(p, axis=-1, keepdims=True)
            acc_sc[...] = a * acc_sc[...] + jnp.dot(p.astype(_BF), _rows(v_ref, j, t), preferred_element_type=_F32)
            m_sc[...] = m_new

        @pl.loop(0, i)
        def _(j):
            step(j, False)

        step(i, True)
        l = l_sc[...]
        o_ref[...] = acc_sc[...] / l
        lse_ref[...] = jnp.broadcast_to(m_sc[...] + jnp.log(l), lse_ref.shape)
        if side is not None:
            @pl.when(last)
            def _():
                side.wait(s_ins, s_outs, sems)

    qmap = lambda b, h, i: (b * nq + i, h)
    smap = lambda b, h, i: (b, h)
    res = pl.pallas_call(
        body, name=name, grid=(B, H, nq),
        in_specs=[pl.BlockSpec((t, HEAD_PAD), qmap), pl.BlockSpec((S, HEAD_PAD), smap), pl.BlockSpec((S, 128), smap)] + s_in,
        out_specs=[pl.BlockSpec((t, 128), qmap), pl.BlockSpec((t, 128), qmap)] + s_out,
        out_shape=[jax.ShapeDtypeStruct((T, H * 128), _F32), jax.ShapeDtypeStruct((T, H * 128), _F32)] + s_shapes,
        input_output_aliases={3 + i_: 2 + o_ for i_, o_ in s_alias.items()},
        scratch_shapes=[pltpu.VMEM((t, 1), _F32), pltpu.VMEM((t, 1), _F32), pltpu.VMEM((t, 128), _F32)] + s_scratch,
        compiler_params=_cparams(("arbitrary",) * 3 if side is not None else ("parallel",) * 3),
    )(qp, kp, v, *s_ops)
    return (res[0], res[1]) if side is None else (res[0], res[1], list(res[2:]))


def _grid_ends(grid):
    ids = [pl.program_id(d) for d in range(len(grid))]
    first, last = ids[0] == 0, ids[0] == grid[0] - 1
    for d in range(1, len(grid)):
        first, last = first & (ids[d] == 0), last & (ids[d] == grid[d] - 1)
    return first, last


def flash_bwd_dq(qp, kp, v, o, lse, do, do_cb0, *, B, S, H, scale, side=None, name):
    T = B * S
    t = _tile(S, 512, 128)
    nq = S // t
    s_ops, s_in, s_out, s_shapes, s_scratch, s_alias = _side_specs(side)
    ni, no = len(s_ops), len(s_shapes)

    def body(q_ref, k_ref, v_ref, o_ref, lse_ref, do_ref, *rest):
        s_ins, dq_ref, s_outs = rest[:ni], rest[ni], rest[ni + 1:ni + 1 + no]
        (acc_sc, dl_sc), sems = rest[ni + 1 + no:ni + 3 + no], rest[ni + 3 + no:]
        first, last = _grid_ends((B, H, nq))
        if side is not None:
            @pl.when(first)
            def _():
                side.start(s_ins, s_outs, sems)

        i = pl.program_id(2)
        acc_sc[...] = jnp.zeros_like(acc_sc)
        dl_sc[...] = jnp.sum(do_ref[...].astype(_F32) * o_ref[...], axis=-1, keepdims=True)

        def step(j, masked):
            k = _rows(k_ref, j, t)
            s = _scores(q_ref[...], k, t, masked)
            p = jnp.exp(s - lse_ref[:, 0:1])
            dp = _qk(do_ref[...].astype(_BF), _rows(v_ref, j, t))
            ds = p * (dp - dl_sc[...])
            acc_sc[...] += jnp.dot(ds.astype(_BF), k, preferred_element_type=_F32)

        @pl.loop(0, i)
        def _(j):
            step(j, False)

        step(i, True)
        dq_ref[...] = acc_sc[...] * scale
        if side is not None:
            @pl.when(last)
            def _():
                side.wait(s_ins, s_outs, sems)

    qmap = lambda b, h, i: (b * nq + i, h)
    domap = lambda b, h, i: (b * nq + i, do_cb0 + h)
    smap = lambda b, h, i: (b, h)
    res = pl.pallas_call(
        body, name=name, grid=(B, H, nq),
        in_specs=[pl.BlockSpec((t, HEAD_PAD), qmap), pl.BlockSpec((S, HEAD_PAD), smap), pl.BlockSpec((S, 128), smap),
                  pl.BlockSpec((t, 128), qmap), pl.BlockSpec((t, 128), qmap), pl.BlockSpec((t, 128), domap)] + s_in,
        out_specs=[pl.BlockSpec((t, HEAD_PAD), qmap)] + s_out,
        out_shape=[jax.ShapeDtypeStruct((T, H * HEAD_PAD), _F32)] + s_shapes,
        input_output_aliases={6 + i_: 1 + o_ for i_, o_ in s_alias.items()},
        scratch_shapes=[pltpu.VMEM((t, HEAD_PAD), _F32), pltpu.VMEM((t, 1), _F32)] + s_scratch,
        compiler_params=_cparams(("arbitrary",) * 3 if side is not None else ("parallel",) * 3),
    )(qp, kp, v, o, lse, do, *s_ops)
    return res[0] if side is None else (res[0], list(res[1:]))


def flash_bwd_dkv(qp, kp, v, o, lse, do, do_cb0, *, B, S, H, scale, side=None, name):
    T = B * S
    t = _tile(S, 512, 128)
    nk = S // t
    s_ops, s_in, s_out, s_shapes, s_scratch, s_alias = _side_specs(side)
    ni, no = len(s_ops), len(s_shapes)

    def body(q_ref, k_ref, v_ref, o_ref, lse_ref, do_ref, *rest):
        s_ins, (dk_ref, dv_ref), s_outs = rest[:ni], rest[ni:ni + 2], rest[ni + 2:ni + 2 + no]
        (dk_sc, dv_sc), sems = rest[ni + 2 + no:ni + 4 + no], rest[ni + 4 + no:]
        first, last = _grid_ends((B, H, nk))
        if side is not None:
            @pl.when(first)
            def _():
                side.start(s_ins, s_outs, sems)

        j = pl.program_id(2)
        dk_sc[...] = jnp.zeros_like(dk_sc)
        dv_sc[...] = jnp.zeros_like(dv_sc)

        def step(i, masked):
            q = _rows(q_ref, i, t)
            do = _rows(do_ref, i, t).astype(_F32)
            dob = do.astype(_BF)
            s = _scores(q, k_ref[...], t, masked)
            p = jnp.exp(s - _rows(lse_ref, i, t)[:, 0:1])
            dl = jnp.sum(do * _rows(o_ref, i, t), axis=-1, keepdims=True)
            dp = _qk(dob, v_ref[...])
            ds = p * (dp - dl)
            tn = (((0,), (0,)), ((), ()))
            dv_sc[...] += lax.dot_general(p.astype(_BF), dob, tn, preferred_element_type=_F32)
            dk_sc[...] += lax.dot_general(ds.astype(_BF), q, tn, preferred_element_type=_F32)

        step(j, True)

        @pl.loop(j + 1, nk)
        def _(i):
            step(i, False)

        dk_ref[...] = dk_sc[...]
        dv_ref[...] = dv_sc[...]
        if side is not None:
            @pl.when(last)
            def _():
                side.wait(s_ins, s_outs, sems)

    smap = lambda b, h, j: (b, h)
    domap = lambda b, h, j: (b, do_cb0 + h)
    kmap = lambda b, h, j: (b * nk + j, h)
    res = pl.pallas_call(
        body, name=name, grid=(B, H, nk),
        in_specs=[pl.BlockSpec((S, HEAD_PAD), smap), pl.BlockSpec((t, HEAD_PAD), kmap), pl.BlockSpec((t, 128), kmap),
                  pl.BlockSpec((S, 128), smap), pl.BlockSpec((S, 128), smap), pl.BlockSpec((S, 128), domap)] + s_in,
        out_specs=[pl.BlockSpec((t, HEAD_PAD), kmap), pl.BlockSpec((t, 128), kmap)] + s_out,
        out_shape=[jax.ShapeDtypeStruct((T, H * HEAD_PAD), _F32), jax.ShapeDtypeStruct((T, H * 128), _F32)] + s_shapes,
        input_output_aliases={6 + i_: 2 + o_ for i_, o_ in s_alias.items()},
        scratch_shapes=[pltpu.VMEM((t, HEAD_PAD), _F32), pltpu.VMEM((t, 128), _F32)] + s_scratch,
        compiler_params=_cparams(("arbitrary",) * 3 if side is not None else ("parallel",) * 3),
    )(qp, kp, v, o, lse, do, *s_ops)
    return (res[0], res[1]) if side is None else (res[0], res[1], list(res[2:]))


def _halo_specs(T, nT, tt, hr, cw, cb):
    k = tt // hr
    main = pl.BlockSpec((tt, cw), lambda b, t: (b * nT + t, cb))
    prev = pl.BlockSpec((hr, cw), lambda b, t: (jnp.maximum((b * nT + t) * k - 1, 0), cb))
    nxt = pl.BlockSpec((hr, cw), lambda b, t: (jnp.minimum((b * nT + t + 1) * k, T // hr - 1), cb))
    return main, prev, nxt


def _ln_rows(z, g, b):
    mu = jnp.mean(z, axis=-1, keepdims=True)
    d = z - mu
    var = jnp.mean(d * d, axis=-1, keepdims=True)
    rstd = lax.rsqrt(var + LN_EPS)
    xh = d * rstd
    return xh * g + b, xh, rstd


def conv_fwd(h, cb_a, cb_g, w, bias, lng, lnb, *, B, S, name):
    T = B * S
    K, C = w.shape
    hr = 32
    assert K - 1 <= hr
    tt = _tile(S, 512, hr)
    nT = S // tt
    a_m, a_p, _ = _halo_specs(T, nT, tt, hr, C, cb_a)
    g_m, g_p, _ = _halo_specs(T, nT, tt, hr, C, cb_g)

    def body(a_ref, g_ref, ap_ref, gp_ref, w_ref, b_ref, lg_ref, lb_ref, z_ref, y_ref, buf):
        t = pl.program_id(1)
        buf[pl.ds(hr, tt), :] = a_ref[...] * jax.nn.sigmoid(g_ref[...])
        hp = ap_ref[...] * jax.nn.sigmoid(gp_ref[...])
        buf[pl.ds(0, hr), :] = jnp.where(t == 0, 0.0, hp)
        z = jnp.broadcast_to(b_ref[...], (tt, C))
        for k in range(K):
            z = z + w_ref[k:k + 1, :] * buf[pl.ds(hr - (K - 1) + k, tt), :]
        z_ref[...] = z
        n, _, _ = _ln_rows(z, lg_ref[...], lb_ref[...])
        y_ref[...] = (n * jax.nn.sigmoid(n)).astype(_BF)

    vec = pl.BlockSpec((1, C), lambda b, t: (0, 0))
    out = pl.BlockSpec((tt, C), lambda b, t: (b * nT + t, 0))
    return pl.pallas_call(
        body, name=name, grid=(B, nT),
        in_specs=[a_m, g_m, a_p, g_p, pl.BlockSpec((K, C), lambda b, t: (0, 0)), vec, vec, vec],
        out_specs=[out, out],
        out_shape=[jax.ShapeDtypeStruct((T, C), _F32), jax.ShapeDtypeStruct((T, C), _BF)],
        scratch_shapes=[pltpu.VMEM((hr + tt, C), _F32)],
        compiler_params=_cparams(("parallel", "parallel")),
    )(h, h, h, h, w, bias.reshape(1, C), lng.reshape(1, C), lnb.reshape(1, C))


def conv_bwd(dmix, cb_dy, z, h, cb_a, cb_g, w, lng, lnb, *, B, S, name):
    T = B * S
    K, C = w.shape
    hr = 32
    tt = _tile(S, 512, hr)
    nT = S // tt
    a_m, a_p, _ = _halo_specs(T, nT, tt, hr, C, cb_a)
    g_m, g_p, _ = _halo_specs(T, nT, tt, hr, C, cb_g)
    dy_m, _, dy_n = _halo_specs(T, nT, tt, hr, C, cb_dy)
    z_m, _, z_n = _halo_specs(T, nT, tt, hr, C, 0)

    def body(dy_ref, dyn_ref, z_ref, zn_ref, a_ref, g_ref, ap_ref, gp_ref, w_ref, lg_ref, lb_ref,
             da_ref, dg_ref, dw_ref, db_ref, dlg_ref, dlb_ref, bufz, bufh):
        b, t = pl.program_id(0), pl.program_id(1)
        lg, lb = lg_ref[...], lb_ref[...]

        def dz_of(dy, zz):
            n, xh, rstd = _ln_rows(zz, lg, lb)
            sg = jax.nn.sigmoid(n)
            dn = dy.astype(_F32) * (sg * (1.0 + n * (1.0 - sg)))
            gdn = dn * lg
            m1 = jnp.mean(gdn, axis=-1, keepdims=True)
            m2 = jnp.mean(gdn * xh, axis=-1, keepdims=True)
            return rstd * (gdn - m1 - xh * m2), dn, xh

        dz, dn, xh = dz_of(dy_ref[...], z_ref[...])
        dzn, _, _ = dz_of(dyn_ref[...], zn_ref[...])
        bufz[pl.ds(0, tt), :] = dz
        bufz[pl.ds(tt, hr), :] = jnp.where(t == nT - 1, 0.0, dzn)
        a, g = a_ref[...], g_ref[...]
        sg = jax.nn.sigmoid(g)
        bufh[pl.ds(hr, tt), :] = a * sg
        bufh[pl.ds(0, hr), :] = jnp.where(t == 0, 0.0, ap_ref[...] * jax.nn.sigmoid(gp_ref[...]))

        @pl.when((b == 0) & (t == 0))
        def _():
            dw_ref[...] = jnp.zeros_like(dw_ref)
            db_ref[...] = jnp.zeros_like(db_ref)
            dlg_ref[...] = jnp.zeros_like(dlg_ref)
            dlb_ref[...] = jnp.zeros_like(dlb_ref)

        dhc = jnp.zeros((tt, C), _F32)
        for k in range(K):
            dhc = dhc + w_ref[k:k + 1, :] * bufz[pl.ds(K - 1 - k, tt), :]
            dw_ref[k:k + 1, :] += jnp.sum(dz * bufh[pl.ds(hr - (K - 1) + k, tt), :], axis=0, keepdims=True)
        da_ref[...] = (dhc * sg).astype(_BF)
        dg_ref[...] = (dhc * a * sg * (1.0 - sg)).astype(_BF)
        db_ref[...] += jnp.sum(dz, axis=0, keepdims=True)
        dlg_ref[...] += jnp.sum(dn * xh, axis=0, keepdims=True)
        dlb_ref[...] += jnp.sum(dn, axis=0, keepdims=True)

    vec = pl.BlockSpec((1, C), lambda b, t: (0, 0))
    out = pl.BlockSpec((tt, C), lambda b, t: (b * nT + t, 0))
    kc = pl.BlockSpec((K, C), lambda b, t: (0, 0))
    return pl.pallas_call(
        body, name=name, grid=(B, nT),
        in_specs=[dy_m, dy_n, z_m, z_n, a_m, g_m, a_p, g_p, kc, vec, vec],
        out_specs=[out, out, kc, vec, vec, vec],
        out_shape=[jax.ShapeDtypeStruct((T, C), _BF), jax.ShapeDtypeStruct((T, C), _BF),
                   jax.ShapeDtypeStruct((K, C), _F32)] + [jax.ShapeDtypeStruct((1, C), _F32)] * 3,
        scratch_shapes=[pltpu.VMEM((tt + hr, C), _F32), pltpu.VMEM((hr + tt, C), _F32)],
        compiler_params=_cparams(("arbitrary", "arbitrary")),
    )(dmix, dmix, z, z, h, h, h, h, w, lng.reshape(1, C), lnb.reshape(1, C))


def _pool_cnt(t, tt, w, rows):
    pos = t * tt + lax.broadcasted_iota(jnp.int32, (rows, 1), 0)
    return jnp.minimum(pos + 1, w).astype(_F32)


def pool_fwd(h, cb, wp, scale, *, B, S, name):
    T = B * S
    G, pg, _ = wp.shape
    C = G * pg
    assert pg == 128 and G == len(POOL_WINDOWS)
    hr = 16
    tt = _tile(S, 512, hr)
    nT = S // tt
    u_m, u_p, _ = _halo_specs(T, nT, tt, hr, C, cb)

    def body(u_ref, up_ref, wp_ref, sc_ref, y_ref, buf):
        t = pl.program_id(1)
        buf[pl.ds(hr, tt), :] = u_ref[...]
        buf[pl.ds(0, hr), :] = jnp.where(t == 0, 0.0, up_ref[...])
        for gi, w in enumerate(POOL_WINDOWS):
            ln = slice(gi * pg, (gi + 1) * pg)
            acc = buf[pl.ds(hr, tt), ln]
            for j in range(1, w):
                acc = acc + buf[pl.ds(hr - j, tt), ln]
            d = acc / _pool_cnt(t, tt, w, tt) - u_ref[:, ln]
            yg = jnp.dot(d.astype(_BF), wp_ref[gi].astype(_BF), preferred_element_type=_F32)
            y_ref[:, ln] = (yg * sc_ref[:, ln]).astype(_BF)

    return pl.pallas_call(
        body, name=name, grid=(B, nT),
        in_specs=[u_m, u_p, pl.BlockSpec((G, pg, pg), lambda b, t: (0, 0, 0)), pl.BlockSpec((1, C), lambda b, t: (0, 0))],
        out_specs=pl.BlockSpec((tt, C), lambda b, t: (b * nT + t, 0)),
        out_shape=jax.ShapeDtypeStruct((T, C), _BF),
        scratch_shapes=[pltpu.VMEM((hr + tt, C), _F32)],
        compiler_params=_cparams(("parallel", "parallel")),
    )(h, h, wp, scale.reshape(1, C))


def pool_bwd(dmix, cb_dy, h, cb, wp, scale, *, B, S, name):
    T = B * S
    G, pg, _ = wp.shape
    C = G * pg
    hr = 16
    tt = _tile(S, 512, hr)
    nT = S // tt
    u_m, u_p, _ = _halo_specs(T, nT, tt, hr, C, cb)
    dy_m, _, dy_n = _halo_specs(T, nT, tt, hr, C, cb_dy)

    def body(dy_ref, dyn_ref, u_ref, up_ref, wp_ref, sc_ref, du_ref, dwp_ref, dsc_ref, buf, bufe):
        b, t = pl.program_id(0), pl.program_id(1)
        buf[pl.ds(hr, tt), :] = u_ref[...]
        buf[pl.ds(0, hr), :] = jnp.where(t == 0, 0.0, up_ref[...])

        @pl.when((b == 0) & (t == 0))
        def _():
            dwp_ref[...] = jnp.zeros_like(dwp_ref)
            dsc_ref[...] = jnp.zeros_like(dsc_ref)

        nt = (((1,), (1,)), ((), ()))
        tn = (((0,), (0,)), ((), ()))
        for gi, w in enumerate(POOL_WINDOWS):
            ln = slice(gi * pg, (gi + 1) * pg)
            wg = wp_ref[gi].astype(_BF)
            sc = sc_ref[:, ln]
            dy = dy_ref[:, ln].astype(_F32)
            dz = (dy * sc).astype(_BF)
            dzn = (dyn_ref[:, ln].astype(_F32) * sc).astype(_BF)
            dd = lax.dot_general(dz, wg, nt, preferred_element_type=_F32)
            ddn = lax.dot_general(dzn, wg, nt, preferred_element_type=_F32)
            bufe[pl.ds(0, tt), ln] = dd / _pool_cnt(t, tt, w, tt)
            bufe[pl.ds(tt, hr), ln] = jnp.where(t == nT - 1, 0.0, ddn / _pool_cnt(t + 1, tt, w, hr))
            du = -dd
            for j in range(w):
                du = du + bufe[pl.ds(j, tt), ln]
            du_ref[:, ln] = du.astype(_BF)
            acc = buf[pl.ds(hr, tt), ln]
            for j in range(1, w):
                acc = acc + buf[pl.ds(hr - j, tt), ln]
            d = (acc / _pool_cnt(t, tt, w, tt) - u_ref[:, ln]).astype(_BF)
            dwp_ref[gi] += lax.dot_general(d, dz, tn, preferred_element_type=_F32)
            yg = jnp.dot(d, wg, preferred_element_type=_F32)
            dsc_ref[:, ln] += jnp.sum(dy * yg, axis=0, keepdims=True)

    return pl.pallas_call(
        body, name=name, grid=(B, nT),
        in_specs=[dy_m, dy_n, u_m, u_p, pl.BlockSpec((G, pg, pg), lambda b, t: (0, 0, 0)),
                  pl.BlockSpec((1, C), lambda b, t: (0, 0))],
        out_specs=[pl.BlockSpec((tt, C), lambda b, t: (b * nT + t, 0)), pl.BlockSpec((G, pg, pg), lambda b, t: (0, 0, 0)),
                   pl.BlockSpec((1, C), lambda b, t: (0, 0))],
        out_shape=[jax.ShapeDtypeStruct((T, C), _BF), jax.ShapeDtypeStruct((G, pg, pg), _F32),
                   jax.ShapeDtypeStruct((1, C), _F32)],
        scratch_shapes=[pltpu.VMEM((hr + tt, C), _F32), pltpu.VMEM((tt + hr, C), _F32)],
        compiler_params=_cparams(("arbitrary", "arbitrary")),
    )(dmix, dmix, h, h, wp, scale.reshape(1, C))


_FFN_HR = 16
_FFN_ROWS, _FFN_LANES = 32, 256


def _fold8(x):
    out = x[0:8]
    for r in range(8, x.shape[0], 8):
        out = out + x[r:r + 8]
    return out


def _silu_grad(x, sg):
    return sg * (1.0 + x * (1.0 - sg))


def _conv3(buf, w_ref, b_ref, off, rows, ln):
    c = b_ref[:, ln] + w_ref[0:1, ln] * buf[pl.ds(off, rows), ln]
    for k in (1, 2):
        c = c + w_ref[k:k + 1, ln] * buf[pl.ds(off + k, rows), ln]
    return c


def _ffn_chunks(nrows, ncols):
    lw = min(ncols, _FFN_LANES)
    return [(r, min(_FFN_ROWS, nrows - r), slice(l0, l0 + lw))
            for l0 in range(0, ncols, lw) for r in range(0, nrows, _FFN_ROWS)]


def gate_fwd(up, w, bias, *, B, S, name):
    T, F2 = up.shape
    F = F2 // 2
    hr = _FFN_HR
    tt = _tile(S, 512, hr)
    nT = S // tt
    tn = _tile(F, 512, 128)
    nC = F // tn
    k = tt // hr

    def body(a_ref, g_ref, ap_ref, gp_ref, wa_ref, wg_ref, ba_ref, bg_ref, o_ref, bufa, bufg):
        t = pl.program_id(2)
        for buf, m_ref, p_ref in ((bufa, a_ref, ap_ref), (bufg, g_ref, gp_ref)):
            buf[pl.ds(hr, tt), :] = m_ref[...].astype(_F32)
            buf[pl.ds(0, hr), :] = jnp.where(t == 0, 0.0, p_ref[...].astype(_F32))
        for r, rc, ln in _ffn_chunks(tt, tn):
            ca = _conv3(bufa, wa_ref, ba_ref, hr - 2 + r, rc, ln)
            cg = _conv3(bufg, wg_ref, bg_ref, hr - 2 + r, rc, ln)
            o_ref[pl.ds(r, rc), ln] = (ca * cg * jax.nn.sigmoid(cg)).astype(_BF)

    def main(off):
        return pl.BlockSpec((tt, tn), lambda b, j, t: (b * nT + t, j + off))

    def prev(off):
        return pl.BlockSpec((hr, tn), lambda b, j, t: (jnp.maximum((b * nT + t) * k - 1, 0), j + off))

    def wspec(rows, off):
        return pl.BlockSpec((rows, tn), lambda b, j, t: (0, j + off))

    return pl.pallas_call(
        body, name=name, grid=(B, nC, nT),
        in_specs=[main(0), main(nC), prev(0), prev(nC), wspec(3, 0), wspec(3, nC), wspec(1, 0), wspec(1, nC)],
        out_specs=pl.BlockSpec((tt, tn), lambda b, j, t: (b * nT + t, j)),
        out_shape=jax.ShapeDtypeStruct((T, F), _BF),
        scratch_shapes=[pltpu.VMEM((hr + tt, tn), _F32)] * 2,
        compiler_params=_cparams(("parallel", "parallel", "parallel")),
    )(up, up, up, up, w, w, bias.reshape(1, F2), bias.reshape(1, F2))


def gate_bwd(up, dact, w, bias, *, B, S, name):
    T, F2 = up.shape
    F = F2 // 2
    hr = _FFN_HR
    tt = _tile(S, 512, hr)
    nT = S // tt
    tn = _tile(F, 512, 128)
    nC = F // tn
    k = tt // hr
    ext = tt + hr

    def body(a_ref, g_ref, ap_ref, gp_ref, an_ref, gn_ref, d_ref, dn_ref, wa_ref, wg_ref, ba_ref, bg_ref,
             du_ref, dwa_ref, dwg_ref, dba_ref, dbg_ref, bufa, bufg, bufda, bufdg):
        b, t = pl.program_id(1), pl.program_id(2)
        last = t == nT - 1
        for buf, m_ref, p_ref, n_ref in ((bufa, a_ref, ap_ref, an_ref), (bufg, g_ref, gp_ref, gn_ref)):
            buf[pl.ds(hr, tt), :] = m_ref[...].astype(_F32)
            buf[pl.ds(0, hr), :] = jnp.where(t == 0, 0.0, p_ref[...].astype(_F32))
            buf[pl.ds(hr + tt, hr), :] = jnp.where(last, 0.0, n_ref[...].astype(_F32))
        for r, rc, ln in _ffn_chunks(ext, tn):
            ca = _conv3(bufa, wa_ref, ba_ref, hr - 2 + r, rc, ln)
            cg = _conv3(bufg, wg_ref, bg_ref, hr - 2 + r, rc, ln)
            sg = jax.nn.sigmoid(cg)
            if r < tt:
                da = d_ref[pl.ds(r, rc), ln].astype(_F32)
            else:
                da = jnp.where(last, 0.0, dn_ref[pl.ds(r - tt, rc), ln].astype(_F32))
            bufda[pl.ds(r, rc), ln] = da * cg * sg
            bufdg[pl.ds(r, rc), ln] = da * ca * _silu_grad(cg, sg)

        @pl.when((b == 0) & (t == 0))
        def _():
            for r in (dwa_ref, dwg_ref, dba_ref, dbg_ref):
                r[...] = jnp.zeros_like(r)

        lw, rc = min(tn, _FFN_LANES), _FFN_ROWS
        assert tt % rc == 0
        for half, bufd, buf, w_ref, dw_ref, db_ref in ((0, bufda, bufa, wa_ref, dwa_ref, dba_ref),
                                                       (1, bufdg, bufg, wg_ref, dwg_ref, dbg_ref)):
            for l0 in range(0, tn, lw):
                ln = slice(l0, l0 + lw)
                acc = [jnp.zeros((8, lw), _F32) for _ in range(4)]
                for r in range(0, tt, rc):
                    dc = [bufd[pl.ds(r + s_, rc), ln] for s_ in range(3)]
                    u = buf[pl.ds(hr + r, rc), ln]
                    du = w_ref[2:3, ln] * dc[0] + w_ref[1:2, ln] * dc[1] + w_ref[0:1, ln] * dc[2]
                    du_ref[half, pl.ds(r, rc), ln] = du.astype(_BF)
                    for kk in range(3):
                        acc[kk] = acc[kk] + _fold8(dc[2 - kk] * u)
                    acc[3] = acc[3] + _fold8(dc[0])
                for kk in range(3):
                    dw_ref[kk:kk + 1, ln] += jnp.sum(acc[kk], axis=0, keepdims=True)
                db_ref[:, ln] += jnp.sum(acc[3], axis=0, keepdims=True)

    def main(off):
        return pl.BlockSpec((tt, tn), lambda j, b, t: (b * nT + t, j + off))

    def prev(off):
        return pl.BlockSpec((hr, tn), lambda j, b, t: (jnp.maximum((b * nT + t) * k - 1, 0), j + off))

    def nxt(off):
        return pl.BlockSpec((hr, tn), lambda j, b, t: (jnp.minimum((b * nT + t + 1) * k, T // hr - 1), j + off))

    def wspec(rows, off):
        return pl.BlockSpec((rows, tn), lambda j, b, t: (0, j + off))

    both = pl.BlockSpec((2, tt, tn), lambda j, b, t: (0, b * nT + t, j))
    return pl.pallas_call(
        body, name=name, grid=(nC, B, nT),
        in_specs=[main(0), main(nC), prev(0), prev(nC), nxt(0), nxt(nC), main(0), nxt(0),
                  wspec(3, 0), wspec(3, nC), wspec(1, 0), wspec(1, nC)],
        out_specs=[both, wspec(3, 0), wspec(3, 0), wspec(1, 0), wspec(1, 0)],
        out_shape=[jax.ShapeDtypeStruct((2, T, F), _BF), jax.ShapeDtypeStruct((3, F), _F32), jax.ShapeDtypeStruct((3, F), _F32),
                   jax.ShapeDtypeStruct((1, F), _F32), jax.ShapeDtypeStruct((1, F), _F32)],
        scratch_shapes=[pltpu.VMEM((hr + ext, tn), _F32)] * 2 + [pltpu.VMEM((ext, tn), _F32)] * 2,
        compiler_params=_cparams(("parallel", "arbitrary", "arbitrary")),
    )(up, up, up, up, up, up, dact, dact, w, w, bias.reshape(1, F2), bias.reshape(1, F2))


def loss_head(y, target, *, name):
    T, C = y.shape
    tr = _row_tile(T, C)

    def body(y_ref, t_ref, dy_ref, acc_ref):
        @pl.when(pl.program_id(0) == 0)
        def _():
            acc_ref[...] = jnp.zeros_like(acc_ref)

        e = y_ref[...] - t_ref[...]
        dy_ref[...] = e * (1.0 / C)
        acc_ref[...] += jnp.sum(e * e, axis=0, keepdims=True) * (0.5 / C)

    row = pl.BlockSpec((tr, C), lambda i: (i, 0))
    return pl.pallas_call(
        body, name=name, grid=(T // tr,),
        in_specs=[row, row], out_specs=[row, pl.BlockSpec((1, C), lambda i: (0, 0))],
        out_shape=[jax.ShapeDtypeStruct((T, C), _F32), jax.ShapeDtypeStruct((1, C), _F32)],
        compiler_params=_cparams(("arbitrary",)),
    )(y, target)


def _adamw_update(w_ref, g_ref, m_ref, v_ref, d_ref, mo_ref, vo_ref):
    c1 = 1.0 - ADAM_B1 ** ADAM_STEP
    c2 = 1.0 - ADAM_B2 ** ADAM_STEP
    gg = g_ref[...]
    mn = ADAM_B1 * m_ref[...] + (1.0 - ADAM_B1) * gg
    vn = ADAM_B2 * v_ref[...] + (1.0 - ADAM_B2) * (gg * gg)
    d_ref[...] = -ADAM_LR * ((mn / c1) / (jnp.sqrt(vn / c2) + ADAM_EPS) + ADAM_WD * w_ref[...])
    mo_ref[...] = mn
    vo_ref[...] = vn


def adamw_small(ws, gs, ms, vs, *, name):
    n = len(ws)

    def body(*refs):
        for p in range(n):
            _adamw_update(*[refs[k * n + p] for k in range(7)])

    vm = pl.BlockSpec(memory_space=pltpu.VMEM)
    shapes = [jax.ShapeDtypeStruct(w.shape, _F32) for w in ws]
    res = pl.pallas_call(
        body, name=name, in_specs=[vm] * (4 * n), out_specs=[vm] * (3 * n), out_shape=shapes * 3,
        compiler_params=_cparams(),
    )(*ws, *gs, *ms, *vs)
    return res[:n], res[n:2 * n], res[2 * n:]


def adamw(w, g, m, v, *, name):
    R, C = w.shape
    tr = _tile(R, max(8, (256 * 1024) // C // 8 * 8), 8)

    def body(w_ref, g_ref, m_ref, v_ref, d_ref, mo_ref, vo_ref):
        _adamw_update(w_ref, g_ref, m_ref, v_ref, d_ref, mo_ref, vo_ref)

    blk = pl.BlockSpec((tr, C), lambda i: (i, 0))
    s = jax.ShapeDtypeStruct((R, C), _F32)
    return pl.pallas_call(
        body, name=name, grid=(R // tr,),
        in_specs=[blk] * 4, out_specs=[blk] * 3, out_shape=[s, s, s],
        compiler_params=_cparams(("parallel",)),
    )(w, g, m, v)


_ANY = pl.BlockSpec(memory_space=pl.ANY)
_MESH = pl.DeviceIdType.MESH


def _place():
    return lax.axis_index("x"), lax.axis_index("y"), lax.axis_index("c")


def _other_chips(x, y):
    chips = [(1 - x, y), (x, 1 - y), (1 - x, 1 - y)]
    return chips, [2 * a + b for a, b in chips]


def _rcopy(src, dst, ssem, rsem, dev):
    return pltpu.make_async_remote_copy(src_ref=src, dst_ref=dst, send_sem=ssem, recv_sem=rsem,
                                        device_id=dev, device_id_type=_MESH)


def place_shard(w, l, chip_idx, *, name):
    _, _, hR, C = w.shape
    tr = _tile(hR, max(16, (512 * 1024) // C // 16 * 16), 16)

    def body(ci_ref, w_ref, o_ref):
        o_ref[...] = w_ref[...].astype(_BF)

    return pl.pallas_call(
        body, name=name,
        grid_spec=pltpu.PrefetchScalarGridSpec(
            num_scalar_prefetch=1, grid=(2, hR // tr),
            in_specs=[pl.BlockSpec((None, None, tr, C), lambda h, i, ci: (l, h, i, 0))],
            out_specs=pl.BlockSpec((None, None, tr, C), lambda h, i, ci: (ci[0], h, i, 0))),
        out_shape=jax.ShapeDtypeStruct((4, 2, hR, C), _BF),
        compiler_params=_cparams(("parallel", "parallel")),
    )(chip_idx, w)


class _Side:
    def __init__(self, arrays, out_shapes, aliases, n_sems, start, wait):
        self.arrays, self.out_shapes, self.aliases, self.n_sems = arrays, out_shapes, aliases, n_sems
        self.start, self.wait = start, wait


def gather_stage1_side(bufs):
    n = len(bufs)

    def copies(outs, sems, sending):
        x, y, c = _place()
        me = 2 * x + y
        chips, cidx = _other_chips(x, y)
        send, recv = sems
        out, back = [], []
        for k, chip in enumerate(chips):
            for p in range(n):
                mine, got = outs[p].at[me, c], outs[p].at[cidx[k], c]
                out.append(_rcopy(mine, mine, send.at[p * 3 + k], recv.at[p * 3 + k], (*chip, c)))
                if not sending:
                    back.append(_rcopy(got, got, send.at[p * 3 + k], recv.at[p * 3 + k], (*chip, c)))
        return out, back

    def start(ins, outs, sems):
        for cp in copies(outs, sems, True)[0]:
            cp.start()

    def wait(ins, outs, sems):
        out, back = copies(outs, sems, False)
        for cp in back:
            cp.wait_recv()
        for cp in out:
            cp.wait_send()

    return _Side(list(bufs), [jax.ShapeDtypeStruct(b.shape, b.dtype) for b in bufs], {p: p for p in range(n)},
                 [n * 3, n * 3], start, wait)


def gather_stage2_side(bufs):
    n = len(bufs)

    def copies(outs, sems, sending):
        x, y, c = _place()
        sib = (x, y, 1 - c)
        _, cidx = _other_chips(x, y)
        send, recv = sems
        out, back = [], []
        for k in range(3):
            for p in range(n):
                mine, got = outs[p].at[cidx[k], c], outs[p].at[cidx[k], 1 - c]
                out.append(_rcopy(mine, mine, send.at[p * 3 + k], recv.at[p * 3 + k], sib))
                if not sending:
                    back.append(_rcopy(got, got, send.at[p * 3 + k], recv.at[p * 3 + k], sib))
        return out, back

    def start(ins, outs, sems):
        for cp in copies(outs, sems, True)[0]:
            cp.start()

    def wait(ins, outs, sems):
        out, back = copies(outs, sems, False)
        for cp in back:
            cp.wait_recv()
        for cp in out:
            cp.wait_send()

    return _Side(list(bufs), [jax.ShapeDtypeStruct(b.shape, b.dtype) for b in bufs], {p: p for p in range(n)},
                 [n * 3, n * 3], start, wait)


def chip_exchange_side(ps):
    n = len(ps)

    def copies(ins, outs, sems, sending):
        x, y, c = _place()
        me = 2 * x + y
        chips, cidx = _other_chips(x, y)
        send, recv = sems
        out, back = [], []
        for k, chip in enumerate(chips):
            for p in range(n):
                got = outs[p].at[cidx[k]]
                out.append(_rcopy(ins[p].at[cidx[k]], outs[p].at[me], send.at[p * 3 + k], recv.at[p * 3 + k], (*chip, c)))
                if not sending:
                    back.append(_rcopy(got, got, send.at[p * 3 + k], recv.at[p * 3 + k], (*chip, c)))
        return out, back

    def start(ins, outs, sems):
        for cp in copies(ins, outs, sems, True)[0]:
            cp.start()

    def wait(ins, outs, sems):
        out, back = copies(ins, outs, sems, False)
        for cp in back:
            cp.wait_recv()
        for cp in out:
            cp.wait_send()

    return _Side(list(ps), [jax.ShapeDtypeStruct(a.shape, a.dtype) for a in ps], {}, [n * 3, n * 3], start, wait)


def _side_specs(side):
    if side is None:
        return [], [], [], [], [], {}
    return (side.arrays, [_ANY] * len(side.arrays), [_ANY] * len(side.out_shapes), side.out_shapes,
            [pltpu.SemaphoreType.DMA((k,)) for k in side.n_sems], side.aliases)


def run_side(side, *, name):
    ni, no = len(side.arrays), len(side.out_shapes)

    def body(*refs):
        ins, outs, sems = refs[:ni], refs[ni:ni + no], refs[ni + no:]
        side.start(ins, outs, sems)
        side.wait(ins, outs, sems)

    ops, in_specs, out_specs, out_shapes, scratch, aliases = _side_specs(side)
    return pl.pallas_call(body, name=name, in_specs=in_specs, out_specs=out_specs, out_shape=out_shapes,
                          input_output_aliases=aliases, scratch_shapes=scratch)(*ops)


def gather_small(small, *, name):
    ns = len(small)

    def body(*refs):
        s_in, s_out = refs[:ns], refs[ns:2 * ns]
        send, recv, lsem = refs[2 * ns:]
        x, y, c = _place()
        me = 2 * x + y
        chips, cidx = _other_chips(x, y)
        local = [pltpu.make_async_copy(s_in[q], s_out[q].at[me], lsem.at[q]) for q in range(ns)]
        cps = [_rcopy(s_in[q], s_out[q].at[me], send.at[q * 3 + k], recv.at[q * 3 + k], (*chip, c))
               for k, chip in enumerate(chips) for q in range(ns)]
        for cp in local + cps:
            cp.start()
        for k in range(3):
            for q in range(ns):
                got = s_out[q].at[cidx[k]]
                _rcopy(got, got, send.at[q * 3 + k], recv.at[q * 3 + k], (x, y, c)).wait_recv()
        for cp in cps:
            cp.wait_send()
        for cp in local:
            cp.wait()

    return pl.pallas_call(
        body, name=name, in_specs=[_ANY] * ns, out_specs=[_ANY] * ns,
        out_shape=[jax.ShapeDtypeStruct((4,) + a.shape, a.dtype) for a in small],
        scratch_shapes=[pltpu.SemaphoreType.DMA((ns * 3,))] * 2 + [pltpu.SemaphoreType.DMA((ns,))],
    )(*small)


def sibling_send_half(gs, *, name):
    n = len(gs)

    def body(*refs):
        g_in, g_out, send, recv = refs[:n], refs[n:2 * n], refs[2 * n], refs[2 * n + 1]
        x, y, c = _place()
        sib = (x, y, 1 - c)
        cps = [_rcopy(g_in[p].at[1 - c], g_out[p], send.at[p], recv.at[p], sib) for p in range(n)]
        for cp in cps:
            cp.start()
        for cp in cps:
            cp.wait()

    return pl.pallas_call(
        body, name=name, in_specs=[_ANY] * n, out_specs=[_ANY] * n,
        out_shape=[jax.ShapeDtypeStruct(a.shape[1:], a.dtype) for a in gs],
        scratch_shapes=[pltpu.SemaphoreType.DMA((n,))] * 2,
    )(*gs)


def sum_chips(p, slots, idx, *, name):
    _, N, C = p.shape
    tr = _tile(N, max(16, (512 * 1024) // C // 16 * 16), 16)

    def body(i0, i1, i2, i3, i4, p_ref, s0_ref, s1_ref, s2_ref, o_ref):
        s = p_ref[...].astype(_F32)
        for r in (s0_ref, s1_ref, s2_ref):
            s = s + r[...].astype(_F32)
        o_ref[...] = s

    def at(k):
        return pl.BlockSpec((None, tr, C), lambda i, *ix: (ix[k][0], i, 0))

    return pl.pallas_call(
        body, name=name,
        grid_spec=pltpu.PrefetchScalarGridSpec(
            num_scalar_prefetch=5, grid=(N // tr,),
            in_specs=[at(0), at(1), at(2), at(3)], out_specs=at(4)),
        out_shape=jax.ShapeDtypeStruct((2, N, C), _F32),
        compiler_params=_cparams(("parallel",)),
    )(*idx, p, slots, slots, slots)


def sibling_join(rs, *, name):
    n = len(rs)

    def body(*refs):
        r_out, send, recv = refs[n:2 * n], refs[2 * n], refs[2 * n + 1]
        x, y, c = _place()
        sib = (x, y, 1 - c)
        cps = [_rcopy(r_out[p].at[c], r_out[p].at[c], send.at[p], recv.at[p], sib) for p in range(n)]
        for cp in cps:
            cp.start()
        for p in range(n):
            got = r_out[p].at[1 - c]
            _rcopy(got, got, send.at[p], recv.at[p], sib).wait_recv()
        for cp in cps:
            cp.wait_send()

    return pl.pallas_call(
        body, name=name, in_specs=[_ANY] * n, out_specs=[_ANY] * n,
        out_shape=[jax.ShapeDtypeStruct(a.shape, a.dtype) for a in rs],
        input_output_aliases={p: p for p in range(n)},
        scratch_shapes=[pltpu.SemaphoreType.DMA((n,))] * 2,
    )(*rs)


def all_devices_exchange(v, *, name):
    def body(v_ref, o_ref, send, recv, lsem):
        x, y, c = _place()
        me = 4 * x + 2 * y + c
        local = pltpu.make_async_copy(v_ref, o_ref.at[me], lsem)
        local.start()
        peers = []
        for k in range(1, 8):
            px, py, pc = x ^ (k >> 2), y ^ ((k >> 1) & 1), c ^ (k & 1)
            peers.append((px, py, pc))
        cps = [_rcopy(v_ref, o_ref.at[me], send.at[k], recv.at[k], peer) for k, peer in enumerate(peers)]
        for cp in cps:
            cp.start()
        for k, (px, py, pc) in enumerate(peers):
            got = o_ref.at[4 * px + 2 * py + pc]
            _rcopy(got, got, send.at[k], recv.at[k], (x, y, c)).wait_recv()
        for cp in cps:
            cp.wait_send()
        local.wait()

    return pl.pallas_call(
        body, name=name, in_specs=[_ANY], out_specs=_ANY,
        out_shape=jax.ShapeDtypeStruct((8,) + v.shape, v.dtype),
        scratch_shapes=[pltpu.SemaphoreType.DMA((7,))] * 2 + [pltpu.SemaphoreType.DMA(())],
    )(v)


def add_halves(gs_and_rs, c_idx, *, name):
    outs = []
    for n_, (g, r) in enumerate(gs_and_rs):
        N, C = r.shape
        tr = _tile(N, max(16, (512 * 1024) // C // 16 * 16), 16)

        def body(c_ref, g_ref, r_ref, o_ref):
            o_ref[...] = (g_ref[...].astype(_F32) + r_ref[...].astype(_F32)).astype(o_ref.dtype)

        outs.append(pl.pallas_call(
            body, name=f"{name}_{n_}",
            grid_spec=pltpu.PrefetchScalarGridSpec(
                num_scalar_prefetch=1, grid=(N // tr,),
                in_specs=[pl.BlockSpec((None, tr, C), lambda i, c: (c[0], i, 0)), pl.BlockSpec((tr, C), lambda i, c: (i, 0))],
                out_specs=pl.BlockSpec((tr, C), lambda i, c: (i, 0))),
            out_shape=jax.ShapeDtypeStruct((N, C), r.dtype),
            compiler_params=_cparams(("parallel",)),
        )(c_idx, g, r))
    return outs


def sum_slots(a, *, name):
    n, N, C = a.shape
    tr = _tile(N, max(16, (512 * 1024) // C // 16 * 16), 16)

    def body(a_ref, o_ref):
        s = a_ref[0].astype(_F32)
        for k in range(1, n):
            s = s + a_ref[k].astype(_F32)
        o_ref[...] = s

    return pl.pallas_call(
        body, name=name, grid=(N // tr,),
        in_specs=[pl.BlockSpec((n, tr, C), lambda i: (0, i, 0))],
        out_specs=pl.BlockSpec((tr, C), lambda i: (i, 0)),
        out_shape=jax.ShapeDtypeStruct((N, C), _F32),
        compiler_params=_cparams(("parallel",)),
    )(a)


_WEIGHTS = ['ln_in_g', 'ln_in_b', 'w_in', 'q_norm_g', 'w_uq', 'kv_norm_g', 'w_ukv', 'conv_w', 'conv_b', 'conv_ln_g',
            'conv_ln_b', 'w_pool', 'pool_scale', 'w_out', 'ln1_g', 'ln1_b', 'w_up', 'ffn_conv_w', 'ffn_conv_b', 'w_down',
            'ln2_g', 'ln2_b']
_BIG = ['w_in', 'w_uq', 'w_ukv', 'w_out', 'w_up', 'w_down']
_GRADS_EARLY = ('w_up', 'w_down')
_GRADS_LATE = tuple(n for n in _BIG if n not in _GRADS_EARLY)
_SMALL_SHARDED = ['conv_w', 'ffn_conv_w']
_SMALL = [n for n in _WEIGHTS if n not in _BIG]


def _rope_tables(positions):
    half = QK_ROPE // 2
    inv = 1.0 / (ROPE_THETA ** (jnp.arange(0, QK_ROPE, 2, dtype=_F32) / QK_ROPE))
    ang = positions.reshape(-1).astype(_F32)[:, None] * inv
    c, s = jnp.cos(ang), jnp.sin(ang)
    z = jnp.zeros_like(c)
    cc = jnp.concatenate([c, c, z, z], axis=1)
    sa = jnp.concatenate([-s, z, z, z], axis=1)
    sb = jnp.concatenate([z, s, z, z], axis=1)
    assert cc.shape[1] == 128 and half == 32
    return cc, sa, sb


def _in_pad(dims):
    D, QL, KVL, CW, PW, H, F = dims
    return (-(QL + 2 * CW + PW + KVL + 128)) % 512


def _layer_weights(full, dims):
    D, QL, KVL, CW, PW, H, F = dims
    w_in = full['w_in'].transpose(1, 0, 2).reshape(D, -1)
    o1, o2, o3, o4 = QL, QL + KVL, QL + KVL + QK_ROPE, QL + KVL + QK_ROPE + 2 * CW
    w_in_p = jnp.concatenate([w_in[:, :o1], w_in[:, o3:o4], w_in[:, o4:], w_in[:, o1:o2], w_in[:, o2:o3],
                              jnp.zeros((D, 128 - QK_ROPE + _in_pad(dims)), w_in.dtype)], axis=1)
    w_uq = full['w_uq'].reshape(QL, H, QK_NOPE + QK_ROPE)
    w_uq_p = jnp.pad(w_uq, ((0, 0), (0, 0), (0, HEAD_PAD - QK_NOPE - QK_ROPE))).reshape(QL, H * HEAD_PAD)
    return dict(
        w_in=w_in_p, w_uq=w_uq_p,
        w_ukv=full['w_ukv'].reshape(KVL, H * (QK_NOPE + V_HEAD)),
        w_out=full['w_out'].reshape(D, D),
        w_up=full['w_up'].transpose(1, 0, 2).reshape(D, 2 * F),
        w_down=full['w_down'].reshape(F, D),
    )


def _unpermute_w_in_grad(g, dims):
    D, QL, KVL, CW, PW, H, F = dims
    a, b_, c_ = QL, QL + 2 * CW, QL + 2 * CW + PW
    return jnp.concatenate([g[:, :a], g[:, c_:c_ + KVL], g[:, c_ + KVL:c_ + KVL + QK_ROPE], g[:, a:b_], g[:, b_:c_]], axis=1)


class _NoExchange:
    def __init__(self, layer_weights):
        self.layer_weights, self.grads = layer_weights, {}

    def weights(self, l):
        return self.layer_weights[l]

    def side(self, where, l):
        return None

    def side_done(self, where, l, outs):
        pass

    def grads_ready(self, l, g):
        self.grads.setdefault(l, {}).update(g)


def _with_side(hooks, where, l, fn):
    sd = hooks.side(where, l)
    res = fn(sd)
    if sd is None:
        return res
    hooks.side_done(where, l, res[-1])
    return res[0] if len(res) == 2 else res[:-1]


def _local_step(x, positions, target, small, dims, B, S, L, hooks):
    D, QL, KVL, CW, PW, H, F = dims
    T = B * S
    alpha = (2.0 * L) ** 0.25
    scale = float(QK_NOPE + QK_ROPE) ** -0.5
    cc, sa, sb = _rope_tables(positions)
    cb_q, cb_a, cb_g, cb_p = 0, QL // CW, QL // CW + 1, (QL + 2 * CW) // PW
    cb_kv, cb_kr = (QL + 2 * CW + PW) // KVL, (QL + 2 * CW + PW + KVL) // 128
    assert QL % CW == 0 and (QL + 2 * CW) % PW == 0 and (QL + 2 * CW + PW) % KVL == 0 and (QL + 2 * CW + PW + KVL) % 128 == 0

    xs, xb = ln_fwd([x], [1.0], small['ln_in_g'], small['ln_in_b'], want_r=False, name="ln_in")
    saved = []
    fa = dict(B=B, S=S, H=H, scale=scale)
    for l in range(L):
        W = hooks.weights(l)
        h = matmul(xb, W['w_in'], name="mm_in")
        qn = rms_fwd(h, cb_q, QL, small['q_norm_g'][l], name="rms_q")
        kvn = rms_fwd(h, cb_kv, KVL, small['kv_norm_g'][l], name="rms_kv")
        q = matmul(qn, W['w_uq'], name="mm_uq")
        kv = matmul(kvn, W['w_ukv'], name="mm_ukv")
        qp, kp, v = mla_pack(q, kv, h, cb_kr, cc, sa, sb, H=H, scale=scale, name="mla_pack")
        o, lse = _with_side(hooks, 'flash', l, lambda sd: flash_fwd(qp, kp, v, side=sd, name="flash_fwd", **fa))
        z, yc = conv_fwd(h, cb_a, cb_g, small['conv_w'][l], small['conv_b'][l], small['conv_ln_g'][l],
                         small['conv_ln_b'][l], B=B, S=S, name="conv_fwd")
        yp = pool_fwd(h, cb_p, small['w_pool'][l], small['pool_scale'][l], B=B, S=S, name="pool_fwd")
        mixed = jnp.concatenate([o.astype(_BF), yc, yp], axis=1)
        y1 = matmul(mixed, W['w_out'], name="mm_out")
        r1, x1, x1b = ln_fwd([xs, y1], [alpha, 1.0], small['ln1_g'][l], small['ln1_b'][l], want_r=True, name="ln1")
        up = _with_side(hooks, 'up', l, lambda sd: matmul(x1b, W['w_up'], out_dtype=_BF, side=sd, name="mm_up"))
        act = gate_fwd(up, small['ffn_conv_w'][l], small['ffn_conv_b'][l], B=B, S=S, name="gate_fwd")
        y2 = _with_side(hooks, 'down', l, lambda sd: matmul(act, W['w_down'], side=sd, name="mm_down"))
        r2, x2, x2b = ln_fwd([x1, y2], [alpha, 1.0], small['ln2_g'][l], small['ln2_b'][l], want_r=True, name="ln2")
        saved.append(dict(W=W, xb=xb, h=h, qn=qn, kvn=kvn, qp=qp, kp=kp, v=v, o=o, lse=lse, z=z, mixed=mixed, r1=r1,
                          x1b=x1b, up=up, act=act, r2=r2))
        xs, xb = x2, x2b

    dy, loss_cols = loss_head(xs, target, name="loss_head")
    gs = {n: [None] * L for n in _SMALL if n not in ('ln_in_g', 'ln_in_b')}
    d_terms, d_coefs = [dy], [1.0]
    zpad = jnp.zeros((T, _in_pad(dims)), _BF) if _in_pad(dims) else None
    for l in reversed(range(L)):
        sv = saved[l]
        W = sv['W']
        gb = {}
        dr2, dr2b, gs['ln2_g'][l], gs['ln2_b'][l] = ln_bwd(d_terms, d_coefs, sv['r2'], small['ln2_g'][l], name="ln2_bwd")
        dact = matmul(dr2b, W['w_down'], tb=True, out_dtype=_BF, name="mm_down_dx")
        gb['w_down'] = matmul(sv['act'], dr2b, ta=True, out_dtype=_BF, name="mm_down_dw")
        dup, dwa, dwg, dba, dbg = gate_bwd(sv['up'], dact, small['ffn_conv_w'][l], small['ffn_conv_b'][l],
                                           B=B, S=S, name="gate_bwd")
        gs['ffn_conv_w'][l] = jnp.concatenate([dwa, dwg], axis=1)
        gs['ffn_conv_b'][l] = jnp.concatenate([dba, dbg], axis=1)
        gb['w_up'] = _with_side(hooks, 'up_dw', l, lambda sd: matmul(sv['x1b'], dup, ta=True, b_halves=True, out_dtype=_BF,
                                                                     side=sd, name="mm_up_dw"))
        dx1 = _with_side(hooks, 'up_dx', l, lambda sd: matmul(dup, W['w_up'], tb=True, a_halves=True, side=sd,
                                                              name="mm_up_dx"))
        hooks.grads_ready(l, {n: gb.pop(n) for n in _GRADS_EARLY})
        dr1, dr1b, gs['ln1_g'][l], gs['ln1_b'][l] = ln_bwd([dr2, dx1], [alpha, 1.0], sv['r1'], small['ln1_g'][l],
                                                            name="ln1_bwd")
        dmix = matmul(dr1b, W['w_out'], tb=True, name="mm_out_dx")
        gb['w_out'] = matmul(sv['mixed'], dr1b, ta=True, out_dtype=_BF, name="mm_out_dw")
        h = sv['h']
        ncb = (H * V_HEAD) // CW
        dca, dcg, gs['conv_w'][l], gs['conv_b'][l], gs['conv_ln_g'][l], gs['conv_ln_b'][l] = conv_bwd(
            dmix, ncb, sv['z'], h, cb_a, cb_g, small['conv_w'][l], small['conv_ln_g'][l], small['conv_ln_b'][l],
            B=B, S=S, name="conv_bwd")
        dpool, gs['w_pool'][l], gs['pool_scale'][l] = pool_bwd(
            dmix, (H * V_HEAD + CW) // PW, h, cb_p, small['w_pool'][l], small['pool_scale'][l], B=B, S=S, name="pool_bwd")
        att = (sv['qp'], sv['kp'], sv['v'], sv['o'], sv['lse'], dmix, 0)
        dqp = _with_side(hooks, 'flash_dq', l, lambda sd: flash_bwd_dq(*att, side=sd, name="flash_dq", **fa))
        dkp, dv = _with_side(hooks, 'flash_dkv', l, lambda sd: flash_bwd_dkv(*att, side=sd, name="flash_dkv", **fa))
        dq, dkv, dkr = mla_unpack(dqp, dkp, dv, cc, sa, sb, H=H, name="mla_unpack")
        dqn = matmul(dq, W['w_uq'], tb=True, name="mm_uq_dx")
        g_uq = matmul(sv['qn'], dq, ta=True, out_dtype=_BF, name="mm_uq_dw")
        gb['w_uq'] = g_uq.reshape(QL, H, HEAD_PAD)[:, :, :QK_NOPE + QK_ROPE].reshape(QL, -1)
        dkvn = matmul(dkv, W['w_ukv'], tb=True, name="mm_ukv_dx")
        gb['w_ukv'] = matmul(sv['kvn'], dkv, ta=True, out_dtype=_BF, name="mm_ukv_dw")
        dcq, gs['q_norm_g'][l] = rms_bwd(dqn, h, cb_q, QL, small['q_norm_g'][l], name="rms_q_bwd")
        dckv, gs['kv_norm_g'][l] = rms_bwd(dkvn, h, cb_kv, KVL, small['kv_norm_g'][l], name="rms_kv_bwd")
        dh = jnp.concatenate([dcq, dca, dcg, dpool, dckv, dkr] + ([zpad] if zpad is not None else []), axis=1)
        gb['w_in'] = _unpermute_w_in_grad(matmul(sv['xb'], dh, ta=True, out_dtype=_BF, name="mm_in_dw"), dims)
        dxm = matmul(dh, W['w_in'], tb=True, name="mm_in_dx")
        hooks.grads_ready(l, gb)
        d_terms, d_coefs = [dr1, dxm], [alpha, 1.0]
    gx, _, g_ln_g, g_ln_b = ln_bwd(d_terms, d_coefs, x, small['ln_in_g'], name="ln_in_bwd")
    gsm = {n: jnp.stack([a.reshape(small[n].shape[1:]) for a in gs[n]]) for n in gs}
    gsm['ln_in_g'], gsm['ln_in_b'] = g_ln_g.reshape(-1), g_ln_b.reshape(-1)
    return loss_cols, gx, gsm


_COL_SHARDED = ('w_in', 'w_up')


def _flat_pad(arrs, mult=512 * 128):
    v = jnp.concatenate([a.reshape(-1) for a in arrs])
    n = v.shape[0]
    return jnp.pad(v, (0, (-n) % mult)).reshape(-1, 128)


def _split_like(flat, like):
    out, off = [], 0
    v = flat.reshape(-1)
    for a in like:
        out.append(v[off:off + a.size].reshape(a.shape))
        off += a.size
    return out


_GROUP_A = ('w_up',)
_GROUP_B = tuple(n for n in _BIG if n not in _GROUP_A)


class _Exchange:
    def __init__(self, a, dims, L, chip_idx, c_idx, sum_idx):
        self.dims, self.L, self.c_idx, self.sum_idx = dims, L, c_idx, sum_idx
        self.rows = {n: (a[n].shape[1], math.prod(a[n].shape[2:])) for n in _BIG}
        self.bufs = []
        for l in range(L):
            self.bufs.append({n: place_shard(a[n].reshape(L, 2, self.rows[n][0] // 2, self.rows[n][1]), l, chip_idx,
                                             name="place_" + n) for n in _BIG})
        self._store(0, _BIG, run_side(gather_stage1_side(self._list(0, _BIG)), name="gather_first_ici"))
        self._store(0, _BIG, run_side(gather_stage2_side(self._list(0, _BIG)), name="gather_first_d2d"))
        self.pending = {}
        self.reduced = {}

    def _list(self, l, names):
        return [self.bufs[l][n] for n in names]

    def _store(self, l, names, outs):
        self.bufs[l].update(zip(names, outs))

    def weights(self, l):
        return _layer_weights({n: self.bufs[l][n].reshape(4, *self.rows[n]) for n in _BIG}, self.dims)

    def side(self, where, l):
        if where in ('flash', 'up', 'down'):
            if l + 1 >= self.L:
                return None
            if where == 'down':
                return gather_stage2_side(self._list(l + 1, _BIG))
            return gather_stage1_side(self._list(l + 1, _GROUP_B if where == 'flash' else _GROUP_A))
        tag, names = self._riders(where)
        if tag not in self.pending:
            return None
        return chip_exchange_side([self.pending[tag][1][n] for n in names])

    @staticmethod
    def _riders(where):
        return {'flash_dq': ('early', ('w_down',)), 'flash_dkv': ('early', ('w_up',)), 'up_dw': ('late', _GRADS_LATE),
                'up_dx': ('none', ())}[where]

    def side_done(self, where, l, outs):
        if where in ('flash', 'up', 'down'):
            self._store(l + 1, {'flash': _GROUP_B, 'up': _GROUP_A, 'down': _BIG}[where], outs)
            return
        tag, names = self._riders(where)
        self.pending[tag][2].update(zip(names, outs))
        if len(self.pending[tag][2]) == len(self.pending[tag][1]):
            self._finish(tag)

    def _finish(self, tag):
        l, sums, slots = self.pending.pop(tag)
        halves = [sum_chips(sums[n], slots[n], self.sum_idx, name="grad_sum_" + n) for n in sums]
        joined = sibling_join(halves, name="grad_sibling_join_" + tag)
        self.reduced.setdefault(l, {}).update({n: j.reshape(self.rows[n]) for n, j in zip(sums, joined)})

    def grads_ready(self, l, g):
        names = tuple(n for n in _BIG if n in g)
        tag = 'early' if names == tuple(n for n in _BIG if n in _GRADS_EARLY) else 'late'
        g_in = []
        for n in names:
            r, c = self.rows[n]
            st = g[n].reshape(g[n].shape[0], 4, c).transpose(1, 0, 2) if n in _COL_SHARDED else g[n].reshape(4, r, c)
            g_in.append(st.reshape(4, 2, r // 2, c).transpose(1, 0, 2, 3).reshape(2, 2 * r, c))
        from_sib = sibling_send_half(g_in, name="grad_sibling_send_" + tag)
        sums = add_halves(list(zip(g_in, from_sib)), self.c_idx, name="grad_presum_" + tag)
        self.pending[tag] = (l, {n: p.reshape(4, p.shape[0] // 4, p.shape[1]) for n, p in zip(names, sums)}, {})
        if tag == 'late' and l == 0:
            outs = run_side(chip_exchange_side(list(self.pending[tag][1].values())), name="grad_chip_exchange_last")
            self.pending[tag][2].update(zip(names, outs))
            self._finish(tag)


def kernel(x, positions, ln_in_g, ln_in_b, w_in, q_norm_g, w_uq, kv_norm_g, w_ukv, conv_w, conv_b, conv_ln_g, conv_ln_b, w_pool, pool_scale, w_out, ln1_g, ln1_b, w_up, ffn_conv_w, ffn_conv_b, w_down, ln2_g, ln2_b, loss_target, m_ln_in_g, m_ln_in_b, m_w_in, m_q_norm_g, m_w_uq, m_kv_norm_g, m_w_ukv, m_conv_w, m_conv_b, m_conv_ln_g, m_conv_ln_b, m_w_pool, m_pool_scale, m_w_out, m_ln1_g, m_ln1_b, m_w_up, m_ffn_conv_w, m_ffn_conv_b, m_w_down, m_ln2_g, m_ln2_b, v_ln_in_g, v_ln_in_b, v_w_in, v_q_norm_g, v_w_uq, v_kv_norm_g, v_w_ukv, v_conv_w, v_conv_b, v_conv_ln_g, v_conv_ln_b, v_w_pool, v_pool_scale, v_w_out, v_ln1_g, v_ln1_b, v_w_up, v_ffn_conv_w, v_ffn_conv_b, v_w_down, v_ln2_g, v_ln2_b):
    a = dict(locals())
    B, S, D = a['x'].shape
    T = B * S
    L = a['w_in'].shape[0]
    QL, H = 4 * a['w_uq'].shape[1], a['w_uq'].shape[2]
    KVL = 4 * a['w_ukv'].shape[1]
    CW, PW = a['conv_b'].shape[1], a['pool_scale'].shape[1]
    F = 4 * a['w_down'].shape[1]
    dims = (D, QL, KVL, CW, PW, H, F)
    chip = 2 * lax.axis_index("x") + lax.axis_index("y")
    c_idx = lax.axis_index("c").astype(jnp.int32).reshape(1)

    def shard2d(w):
        return w.reshape(-1, w.shape[-1]) if w.ndim == 3 else w.reshape(w.shape[0] * w.shape[1], -1)

    small = {n: a[n] for n in _SMALL}
    for n, o in zip(_SMALL_SHARDED, gather_small([shard2d(a[n]) for n in _SMALL_SHARDED], name="gather_small")):
        k = a[n].shape[1]
        small[n] = o.reshape(4, L, k, -1).transpose(1, 2, 0, 3).reshape(L, k, -1)
    xi, yi = lax.axis_index("x"), lax.axis_index("y")
    chip_idx = chip.astype(jnp.int32).reshape(1)
    sum_idx = [v.astype(jnp.int32).reshape(1) for v in [chip] + _other_chips(xi, yi)[1] + [lax.axis_index("c")]]
    ex = _Exchange(a, dims, L, chip_idx, c_idx, sum_idx)

    loss_cols, gx, gsm = _local_step(a['x'].reshape(T, D), a['positions'], a['loss_target'].reshape(T, D),
                                     small, dims, B, S, L, ex)
    loss = lax.psum(jnp.sum(loss_cols), ("x", "y", "c"))
    g_big = {n: jnp.concatenate([ex.reduced[l][n] for l in range(L)]).reshape(a[n].shape) for n in _BIG}

    sm_like = [gsm[n] for n in _SMALL]
    sm_sum = sum_slots(all_devices_exchange(_flat_pad(sm_like), name="small_exchange"), name="small_sum")
    g_small = dict(zip(_SMALL, _split_like(sm_sum, sm_like)))
    for n in _SMALL_SHARDED:
        w = a[n].shape[-1]
        g_small[n] = lax.dynamic_slice_in_dim(g_small[n], chip * w, w, axis=2)

    grads, delta, new_m, new_v = {}, {}, {}, {}
    for n in _BIG:
        grads[n] = g_big[n]
        d_, m_, v_ = adamw(shard2d(a[n]), shard2d(g_big[n]), shard2d(a['m_' + n]), shard2d(a['v_' + n]), name="adamw_" + n)
        delta[n], new_m[n], new_v[n] = (t.reshape(a[n].shape) for t in (d_, m_, v_))
    def at_least_2d(t):
        return t.reshape(1, -1) if t.ndim == 1 else t

    d_, m_, v_ = adamw_small(*[[at_least_2d(src[n]) for n in _SMALL] for src in (
        a, g_small, {n: a['m_' + n] for n in _SMALL}, {n: a['v_' + n] for n in _SMALL})], name="adamw_small")
    for n, dd, mm, vv in zip(_SMALL, d_, m_, v_):
        grads[n], delta[n], new_m[n], new_v[n] = g_small[n], dd.reshape(a[n].shape), mm.reshape(a[n].shape), vv.reshape(a[n].shape)

    return (loss, gx.reshape(B, S, D), *[grads[n] for n in _WEIGHTS], *[delta[n] for n in _WEIGHTS],
            *[new_m[n] for n in _WEIGHTS], *[new_v[n] for n in _WEIGHTS])
```

```python
import functools
import math

import jax
import jax.numpy as jnp
from jax import lax
from jax.experimental import pallas as pl
from jax.experimental.pallas import tpu as pltpu

_BF = jnp.bfloat16
_F32 = jnp.float32
_VMEM_LIMIT = 56 * 1024 * 1024

QK_NOPE = 128
QK_ROPE = 64
V_HEAD = 128
HEAD_PAD = 256
ROPE_THETA = 10000.0
LN_EPS = 1e-5
RMS_EPS = 1e-6
POOL_WINDOWS = (2, 4, 8, 16)
ADAM_LR, ADAM_B1, ADAM_B2, ADAM_EPS, ADAM_WD, ADAM_STEP = 0.001, 0.9, 0.999, 1e-8, 0.01, 10


def _cparams(sem=None):
    kw = dict(vmem_limit_bytes=_VMEM_LIMIT)
    if sem is not None:
        kw["dimension_semantics"] = sem
    return pltpu.CompilerParams(**kw)


def _tile(n, target, unit=128):
    if n <= target:
        return n
    t = (target // unit) * unit
    while t >= unit:
        if n % t == 0:
            return t
        t -= unit
    return n


_MM_VMEM_BUDGET = 40 * 1024 * 1024


def matmul(a, b, *, ta=False, tb=False, out_dtype=_F32, tm=1024, tn=1536, tk=4096, side=None, a_halves=False,
           b_halves=False, name="mm"):
    assert not (a_halves and ta) and not (b_halves and tb)
    if a_halves:
        M, K = a.shape[1], 2 * a.shape[2]
    elif ta:
        K, M = a.shape
    else:
        M, K = a.shape
    if b_halves:
        K2, N = b.shape[1], 2 * b.shape[2]
    elif tb:
        N, K2 = b.shape
    else:
        K2, N = b.shape
    assert K == K2, (a.shape, b.shape, ta, tb)
    tm, tn, tk = _tile(M, tm), _tile(N // 2 if b_halves else N, tn), _tile(K // 2 if a_halves else K, tk)
    ab, bb, ob = a.dtype.itemsize, b.dtype.itemsize, jnp.dtype(out_dtype).itemsize

    def vmem(tk_):
        return 2 * (tm * tk_ * ab + tk_ * tn * bb) + 2 * tm * tn * ob + tm * tn * 4 * (2 if K // tk_ > 1 else 1)

    k_part = K // 2 if a_halves else K
    while vmem(tk) > _MM_VMEM_BUDGET and tk > 256 and _tile(k_part, tk // 2) < tk:
        tk = _tile(k_part, tk // 2)
    nk = K // tk
    dn = (((0,) if ta else (1,), (1,) if tb else (0,)), ((), ()))

    s_ops, s_in, s_out, s_shapes, s_scratch, s_alias = _side_specs(side)
    ni, no, nacc = len(s_ops), len(s_shapes), int(nk > 1)
    grid = (M // tm, N // tn, nk)

    def body(a_ref, b_ref, *rest):
        s_ins, o_ref, s_outs = rest[:ni], rest[ni], rest[ni + 1:ni + 1 + no]
        acc, sems = rest[ni + 1 + no:ni + 1 + no + nacc], rest[ni + 1 + no + nacc:]
        i, j, k = pl.program_id(0), pl.program_id(1), pl.program_id(2)
        if side is not None:
            @pl.when((i == 0) & (j == 0) & (k == 0))
            def _():
                side.start(s_ins, s_outs, sems)

        prod = lax.dot_general(a_ref[...].astype(_BF), b_ref[...].astype(_BF), dn, preferred_element_type=_F32)
        if nk == 1:
            o_ref[...] = prod.astype(o_ref.dtype)
        else:
            acc_ref = acc[0]

            @pl.when(k == 0)
            def _():
                acc_ref[...] = prod

            @pl.when(k > 0)
            def _():
                acc_ref[...] += prod

            @pl.when(k == nk - 1)
            def _():
                o_ref[...] = acc_ref[...].astype(o_ref.dtype)

        if side is not None:
            @pl.when((i == grid[0] - 1) & (j == grid[1] - 1) & (k == nk - 1))
            def _():
                side.wait(s_ins, s_outs, sems)

    a_spec = pl.BlockSpec((tk, tm), lambda i, j, k: (k, i)) if ta else pl.BlockSpec((tm, tk), lambda i, j, k: (i, k))
    b_spec = pl.BlockSpec((tn, tk), lambda i, j, k: (j, k)) if tb else pl.BlockSpec((tk, tn), lambda i, j, k: (k, j))
    if a_halves:
        kh = nk // 2
        a_spec = pl.BlockSpec((None, tm, tk), lambda i, j, k: (k // kh, i, k % kh))
    if b_halves:
        jh = N // tn // 2
        b_spec = pl.BlockSpec((None, tk, tn), lambda i, j, k: (j // jh, k, j % jh))
    res = pl.pallas_call(
        body, name=name,
        grid=grid,
        in_specs=[a_spec, b_spec] + s_in,
        out_specs=[pl.BlockSpec((tm, tn), lambda i, j, k: (i, j))] + s_out,
        out_shape=[jax.ShapeDtypeStruct((M, N), out_dtype)] + s_shapes,
        input_output_aliases={2 + i_: 1 + o_ for i_, o_ in s_alias.items()},
        scratch_shapes=([pltpu.VMEM((tm, tn), _F32)] if nk > 1 else []) + s_scratch,
        compiler_params=_cparams(("arbitrary",) * 3 if side is not None else ("parallel", "parallel", "arbitrary")),
    )(a, b, *s_ops)
    return res[0] if side is None else (res[0], list(res[1:]))


def _row_tile(T, C, budget_rows=256):
    return _tile(T, budget_rows, 16)


def ln_fwd(xs, coefs, g, b, *, want_r, name):
    T, C = xs[0].shape
    tr = _row_tile(T, C)
    n = len(xs)

    def body(*refs):
        x_refs, (g_ref, b_ref), outs = refs[:n], refs[n:n + 2], refs[n + 2:]
        r = coefs[0] * x_refs[0][...]
        for c, xr in zip(coefs[1:], x_refs[1:]):
            r = r + c * xr[...]
        mu = jnp.mean(r, axis=-1, keepdims=True)
        d = r - mu
        var = jnp.mean(d * d, axis=-1, keepdims=True)
        y = d * lax.rsqrt(var + LN_EPS) * g_ref[...] + b_ref[...]
        if want_r:
            outs[0][...] = r
        outs[-2][...] = y
        outs[-1][...] = y.astype(_BF)

    row = pl.BlockSpec((tr, C), lambda i: (i, 0))
    vec = pl.BlockSpec((1, C), lambda i: (0, 0))
    f = jax.ShapeDtypeStruct((T, C), _F32)
    out_shape = ([f] if want_r else []) + [f, jax.ShapeDtypeStruct((T, C), _BF)]
    return pl.pallas_call(
        body, name=name, grid=(T // tr,),
        in_specs=[row] * n + [vec, vec],
        out_specs=[row] * len(out_shape), out_shape=out_shape,
        compiler_params=_cparams(("parallel",)),
    )(*xs, g.reshape(1, C), b.reshape(1, C))


def ln_bwd(dys, coefs, r, g, *, name):
    T, C = r.shape
    tr = _row_tile(T, C)
    n = len(dys)

    def body(*refs):
        dy_refs, r_ref, g_ref = refs[:n], refs[n], refs[n + 1]
        dr_ref, drb_ref, dg_ref, db_ref = refs[n + 2:]
        dy = coefs[0] * dy_refs[0][...]
        for c, dr_ in zip(coefs[1:], dy_refs[1:]):
            dy = dy + c * dr_[...]
        rr = r_ref[...]
        mu = jnp.mean(rr, axis=-1, keepdims=True)
        d = rr - mu
        var = jnp.mean(d * d, axis=-1, keepdims=True)
        rstd = lax.rsqrt(var + LN_EPS)
        xh = d * rstd
        gdy = dy * g_ref[...]
        m1 = jnp.mean(gdy, axis=-1, keepdims=True)
        m2 = jnp.mean(gdy * xh, axis=-1, keepdims=True)
        dr = rstd * (gdy - m1 - xh * m2)
        dr_ref[...] = dr
        drb_ref[...] = dr.astype(_BF)

        @pl.when(pl.program_id(0) == 0)
        def _():
            dg_ref[...] = jnp.zeros_like(dg_ref)
            db_ref[...] = jnp.zeros_like(db_ref)

        dg_ref[...] += jnp.sum(dy * xh, axis=0, keepdims=True)
        db_ref[...] += jnp.sum(dy, axis=0, keepdims=True)

    row = pl.BlockSpec((tr, C), lambda i: (i, 0))
    vec = pl.BlockSpec((1, C), lambda i: (0, 0))
    return pl.pallas_call(
        body, name=name, grid=(T // tr,),
        in_specs=[row] * (n + 1) + [vec],
        out_specs=[row, row, vec, vec],
        out_shape=[jax.ShapeDtypeStruct((T, C), _F32), jax.ShapeDtypeStruct((T, C), _BF),
                   jax.ShapeDtypeStruct((1, C), _F32), jax.ShapeDtypeStruct((1, C), _F32)],
        compiler_params=_cparams(("arbitrary",)),
    )(*dys, r, g.reshape(1, C))


def rms_fwd(h, cb, W, g, *, name):
    T = h.shape[0]
    tr = _tile(T, 512, 16)

    def body(c_ref, g_ref, o_ref):
        c = c_ref[...]
        ms = jnp.mean(c * c, axis=-1, keepdims=True)
        o_ref[...] = (c * lax.rsqrt(ms + RMS_EPS) * g_ref[...]).astype(_BF)

    return pl.pallas_call(
        body, name=name, grid=(T // tr,),
        in_specs=[pl.BlockSpec((tr, W), lambda i: (i, cb)), pl.BlockSpec((1, W), lambda i: (0, 0))],
        out_specs=pl.BlockSpec((tr, W), lambda i: (i, 0)),
        out_shape=jax.ShapeDtypeStruct((T, W), _BF),
        compiler_params=_cparams(("parallel",)),
    )(h, g.reshape(1, W))


def rms_bwd(dy, h, cb, W, g, *, name):
    T = h.shape[0]
    tr = _tile(T, 512, 16)

    def body(dy_ref, c_ref, g_ref, dc_ref, dg_ref):
        c = c_ref[...]
        dyv = dy_ref[...]
        ms = jnp.mean(c * c, axis=-1, keepdims=True)
        r = lax.rsqrt(ms + RMS_EPS)
        u = dyv * g_ref[...]
        m = jnp.mean(c * u, axis=-1, keepdims=True)
        dc_ref[...] = (r * u - c * (r * r * r) * m).astype(_BF)

        @pl.when(pl.program_id(0) == 0)
        def _():
            dg_ref[...] = jnp.zeros_like(dg_ref)

        dg_ref[...] += jnp.sum(dyv * c * r, axis=0, keepdims=True)

    return pl.pallas_call(
        body, name=name, grid=(T // tr,),
        in_specs=[pl.BlockSpec((tr, W), lambda i: (i, 0)), pl.BlockSpec((tr, W), lambda i: (i, cb)),
                  pl.BlockSpec((1, W), lambda i: (0, 0))],
        out_specs=[pl.BlockSpec((tr, W), lambda i: (i, 0)), pl.BlockSpec((1, W), lambda i: (0, 0))],
        out_shape=[jax.ShapeDtypeStruct((T, W), _BF), jax.ShapeDtypeStruct((1, W), _F32)],
        compiler_params=_cparams(("arbitrary",)),
    )(dy, h, g.reshape(1, W))


def _rope(u, cc, sa, sb, sign):
    return u * cc + sign * (pltpu.roll(u, 96, 1) * sa + pltpu.roll(u, 32, 1) * sb)


def mla_pack(q, kv, h, kr_cb, cc, sa, sb, *, H, scale, name):
    T = q.shape[0]
    tr = _tile(T, 256, 16)

    def body(q_ref, kv_ref, kr_ref, cc_ref, sa_ref, sb_ref, qp_ref, kp_ref, v_ref):
        cc_, sa_, sb_ = cc_ref[...], sa_ref[...], sb_ref[...]
        kr = _rope(kr_ref[...], cc_, sa_, sb_, 1.0).astype(_BF)
        for hh in range(H):
            o = hh * HEAD_PAD
            qp_ref[:, o:o + 128] = (q_ref[:, o:o + 128] * scale).astype(_BF)
            qp_ref[:, o + 128:o + 256] = (_rope(q_ref[:, o + 128:o + 256], cc_, sa_, sb_, 1.0) * scale).astype(_BF)
            kp_ref[:, o:o + 128] = kv_ref[:, o:o + 128].astype(_BF)
            kp_ref[:, o + 128:o + 256] = kr
            v_ref[:, hh * 128:(hh + 1) * 128] = kv_ref[:, o + 128:o + 256].astype(_BF)

    wide = pl.BlockSpec((tr, H * HEAD_PAD), lambda i: (i, 0))
    tab = pl.BlockSpec((tr, 128), lambda i: (i, 0))
    return pl.pallas_call(
        body, name=name, grid=(T // tr,),
        in_specs=[wide, wide, pl.BlockSpec((tr, 128), lambda i: (i, kr_cb)), tab, tab, tab],
        out_specs=[wide, wide, pl.BlockSpec((tr, H * 128), lambda i: (i, 0))],
        out_shape=[jax.ShapeDtypeStruct((T, H * HEAD_PAD), _BF), jax.ShapeDtypeStruct((T, H * HEAD_PAD), _BF),
                   jax.ShapeDtypeStruct((T, H * 128), _BF)],
        compiler_params=_cparams(("parallel",)),
    )(q, kv, h, cc, sa, sb)


def mla_unpack(dqp, dkp, dv, cc, sa, sb, *, H, name):
    T = dqp.shape[0]
    tr = _tile(T, 256, 16)

    def body(dq_ref, dk_ref, dv_ref, cc_ref, sa_ref, sb_ref, oq_ref, okv_ref, okr_ref):
        cc_, sa_, sb_ = cc_ref[...], sa_ref[...], sb_ref[...]
        kr = jnp.zeros((tr, 128), _F32)
        for hh in range(H):
            o = hh * HEAD_PAD
            oq_ref[:, o:o + 128] = dq_ref[:, o:o + 128].astype(_BF)
            oq_ref[:, o + 128:o + 256] = _rope(dq_ref[:, o + 128:o + 256], cc_, sa_, sb_, -1.0).astype(_BF)
            okv_ref[:, o:o + 128] = dk_ref[:, o:o + 128].astype(_BF)
            okv_ref[:, o + 128:o + 256] = dv_ref[:, hh * 128:(hh + 1) * 128].astype(_BF)
            kr = kr + dk_ref[:, o + 128:o + 256]
        okr_ref[...] = _rope(kr, cc_, sa_, sb_, -1.0).astype(_BF)

    wide = pl.BlockSpec((tr, H * HEAD_PAD), lambda i: (i, 0))
    tab = pl.BlockSpec((tr, 128), lambda i: (i, 0))
    return pl.pallas_call(
        body, name=name, grid=(T // tr,),
        in_specs=[wide, wide, pl.BlockSpec((tr, H * 128), lambda i: (i, 0)), tab, tab, tab],
        out_specs=[wide, wide, tab],
        out_shape=[jax.ShapeDtypeStruct((T, H * HEAD_PAD), _BF), jax.ShapeDtypeStruct((T, H * HEAD_PAD), _BF),
                   jax.ShapeDtypeStruct((T, 128), _BF)],
        compiler_params=_cparams(("parallel",)),
    )(dqp, dkp, dv, cc, sa, sb)


_NEG = -1e30


def _rows(ref, j, t):
    return ref[pl.ds(pl.multiple_of(j * t, t), t), :]


def _qk(q, k):
    return lax.dot_general(q, k, (((1,), (1,)), ((), ())), preferred_element_type=_F32)


def _scores(q, k, t, masked):
    s = _qk(q, k)
    if not masked:
        return s
    row = lax.broadcasted_iota(jnp.int32, (t, t), 0)
    col = lax.broadcasted_iota(jnp.int32, (t, t), 1)
    return jnp.where(col <= row, s, _NEG)


def flash_fwd(qp, kp, v, *, B, S, H, scale, side=None, name):
    T = B * S
    t = _tile(S, 512, 128)
    nq = S // t
    s_ops, s_in, s_out, s_shapes, s_scratch, s_alias = _side_specs(side)
    ni, no = len(s_ops), len(s_shapes)

    def body(q_ref, k_ref, v_ref, *rest):
        s_ins, (o_ref, lse_ref), s_outs = rest[:ni], rest[ni:ni + 2], rest[ni + 2:ni + 2 + no]
        (m_sc, l_sc, acc_sc), sems = rest[ni + 2 + no:ni + 5 + no], rest[ni + 5 + no:]
        i = pl.program_id(2)
        first = (pl.program_id(0) == 0) & (pl.program_id(1) == 0) & (i == 0)
        last = (pl.program_id(0) == B - 1) & (pl.program_id(1) == H - 1) & (i == nq - 1)
        if side is not None:
            @pl.when(first)
            def _():
                side.start(s_ins, s_outs, sems)

        m_sc[...] = jnp.full_like(m_sc, _NEG)
        l_sc[...] = jnp.zeros_like(l_sc)
        acc_sc[...] = jnp.zeros_like(acc_sc)

        def step(j, masked):
            s = _scores(q_ref[...], _rows(k_ref, j, t), t, masked)
            m_old = m_sc[...]
            m_new = jnp.maximum(m_old, jnp.max(s, axis=-1, keepdims=True))
            p = jnp.exp(s - m_new)
            a = jnp.exp(m_old - m_new)
            l_sc[...] = a * l_sc[...] + jnp.sum(p, axis=-1, keepdims=True)
            acc_sc[...] = a * acc_sc[...] + jnp.dot(p.astype(_BF), _rows(v_ref, j, t), preferred_element_type=_F32)
            m_sc[...] = m_new

        @pl.loop(0, i)
        def _(j):
            step(j, False)

        step(i, True)
        l = l_sc[...]
        o_ref[...] = acc_sc[...] / l
        lse_ref[...] = jnp.broadcast_to(m_sc[...] + jnp.log(l), lse_ref.shape)
        if side is not None:
            @pl.when(last)
            def _():
                side.wait(s_ins, s_outs, sems)

    qmap = lambda b, h, i: (b * nq + i, h)
    smap = lambda b, h, i: (b, h)
    res = pl.pallas_call(
        body, name=name, grid=(B, H, nq),
        in_specs=[pl.BlockSpec((t, HEAD_PAD), qmap), pl.BlockSpec((S, HEAD_PAD), smap), pl.BlockSpec((S, 128), smap)] + s_in,
        out_specs=[pl.BlockSpec((t, 128), qmap), pl.BlockSpec((t, 128), qmap)] + s_out,
        out_shape=[jax.ShapeDtypeStruct((T, H * 128), _F32), jax.ShapeDtypeStruct((T, H * 128), _F32)] + s_shapes,
        input_output_aliases={3 + i_: 2 + o_ for i_, o_ in s_alias.items()},
        scratch_shapes=[pltpu.VMEM((t, 1), _F32), pltpu.VMEM((t, 1), _F32), pltpu.VMEM((t, 128), _F32)] + s_scratch,
        compiler_params=_cparams(("arbitrary",) * 3 if side is not None else ("parallel",) * 3),
    )(qp, kp, v, *s_ops)
    return (res[0], res[1]) if side is None else (res[0], res[1], list(res[2:]))


def _grid_ends(grid):
    ids = [pl.program_id(d) for d in range(len(grid))]
    first, last = ids[0] == 0, ids[0] == grid[0] - 1
    for d in range(1, len(grid)):
        first, last = first & (ids[d] == 0), last & (ids[d] == grid[d] - 1)
    return first, last


def flash_bwd_dq(qp, kp, v, o, lse, do, do_cb0, *, B, S, H, scale, side=None, name):
    T = B * S
    t = _tile(S, 512, 128)
    nq = S // t
    s_ops, s_in, s_out, s_shapes, s_scratch, s_alias = _side_specs(side)
    ni, no = len(s_ops), len(s_shapes)

    def body(q_ref, k_ref, v_ref, o_ref, lse_ref, do_ref, *rest):
        s_ins, dq_ref, s_outs = rest[:ni], rest[ni], rest[ni + 1:ni + 1 + no]
        (acc_sc, dl_sc), sems = rest[ni + 1 + no:ni + 3 + no], rest[ni + 3 + no:]
        first, last = _grid_ends((B, H, nq))
        if side is not None:
            @pl.when(first)
            def _():
                side.start(s_ins, s_outs, sems)

        i = pl.program_id(2)
        acc_sc[...] = jnp.zeros_like(acc_sc)
        dl_sc[...] = jnp.sum(do_ref[...].astype(_F32) * o_ref[...], axis=-1, keepdims=True)

        def step(j, masked):
            k = _rows(k_ref, j, t)
            s = _scores(q_ref[...], k, t, masked)
            p = jnp.exp(s - lse_ref[:, 0:1])
            dp = _qk(do_ref[...].astype(_BF), _rows(v_ref, j, t))
            ds = p * (dp - dl_sc[...])
            acc_sc[...] += jnp.dot(ds.astype(_BF), k, preferred_element_type=_F32)

        @pl.loop(0, i)
        def _(j):
            step(j, False)

        step(i, True)
        dq_ref[...] = acc_sc[...] * scale
        if side is not None:
            @pl.when(last)
            def _():
                side.wait(s_ins, s_outs, sems)

    qmap = lambda b, h, i: (b * nq + i, h)
    domap = lambda b, h, i: (b * nq + i, do_cb0 + h)
    smap = lambda b, h, i: (b, h)
    res = pl.pallas_call(
        body, name=name, grid=(B, H, nq),
        in_specs=[pl.BlockSpec((t, HEAD_PAD), qmap), pl.BlockSpec((S, HEAD_PAD), smap), pl.BlockSpec((S, 128), smap),
                  pl.BlockSpec((t, 128), qmap), pl.BlockSpec((t, 128), qmap), pl.BlockSpec((t, 128), domap)] + s_in,
        out_specs=[pl.BlockSpec((t, HEAD_PAD), qmap)] + s_out,
        out_shape=[jax.ShapeDtypeStruct((T, H * HEAD_PAD), _F32)] + s_shapes,
        input_output_aliases={6 + i_: 1 + o_ for i_, o_ in s_alias.items()},
        scratch_shapes=[pltpu.VMEM((t, HEAD_PAD), _F32), pltpu.VMEM((t, 1), _F32)] + s_scratch,
        compiler_params=_cparams(("arbitrary",) * 3 if side is not None else ("parallel",) * 3),
    )(qp, kp, v, o, lse, do, *s_ops)
    return res[0] if side is None else (res[0], list(res[1:]))


def flash_bwd_dkv(qp, kp, v, o, lse, do, do_cb0, *, B, S, H, scale, side=None, name):
    T = B * S
    t = _tile(S, 512, 128)
    nk = S // t
    s_ops, s_in, s_out, s_shapes, s_scratch, s_alias = _side_specs(side)
    ni, no = len(s_ops), len(s_shapes)

    def body(q_ref, k_ref, v_ref, o_ref, lse_ref, do_ref, *rest):
        s_ins, (dk_ref, dv_ref), s_outs = rest[:ni], rest[ni:ni + 2], rest[ni + 2:ni + 2 + no]
        (dk_sc, dv_sc), sems = rest[ni + 2 + no:ni + 4 + no], rest[ni + 4 + no:]
        first, last = _grid_ends((B, H, nk))
        if side is not None:
            @pl.when(first)
            def _():
                side.start(s_ins, s_outs, sems)

        j = pl.program_id(2)
        dk_sc[...] = jnp.zeros_like(dk_sc)
        dv_sc[...] = jnp.zeros_like(dv_sc)

        def step(i, masked):
            q = _rows(q_ref, i, t)
            do = _rows(do_ref, i, t).astype(_F32)
            dob = do.astype(_BF)
            s = _scores(q, k_ref[...], t, masked)
            p = jnp.exp(s - _rows(lse_ref, i, t)[:, 0:1])
            dl = jnp.sum(do * _rows(o_ref, i, t), axis=-1, keepdims=True)
            dp = _qk(dob, v_ref[...])
            ds = p * (dp - dl)
            tn = (((0,), (0,)), ((), ()))
            dv_sc[...] += lax.dot_general(p.astype(_BF), dob, tn, preferred_element_type=_F32)
            dk_sc[...] += lax.dot_general(ds.astype(_BF), q, tn, preferred_element_type=_F32)

        step(j, True)

        @pl.loop(j + 1, nk)
        def _(i):
            step(i, False)

        dk_ref[...] = dk_sc[...]
        dv_ref[...] = dv_sc[...]
        if side is not None:
            @pl.when(last)
            def _():
                side.wait(s_ins, s_outs, sems)

    smap = lambda b, h, j: (b, h)
    domap = lambda b, h, j: (b, do_cb0 + h)
    kmap = lambda b, h, j: (b * nk + j, h)
    res = pl.pallas_call(
        body, name=name, grid=(B, H, nk),
        in_specs=[pl.BlockSpec((S, HEAD_PAD), smap), pl.BlockSpec((t, HEAD_PAD), kmap), pl.BlockSpec((t, 128), kmap),
                  pl.BlockSpec((S, 128), smap), pl.BlockSpec((S, 128), smap), pl.BlockSpec((S, 128), domap)] + s_in,
        out_specs=[pl.BlockSpec((t, HEAD_PAD), kmap), pl.BlockSpec((t, 128), kmap)] + s_out,
        out_shape=[jax.ShapeDtypeStruct((T, H * HEAD_PAD), _F32), jax.ShapeDtypeStruct((T, H * 128), _F32)] + s_shapes,
        input_output_aliases={6 + i_: 2 + o_ for i_, o_ in s_alias.items()},
        scratch_shapes=[pltpu.VMEM((t, HEAD_PAD), _F32), pltpu.VMEM((t, 128), _F32)] + s_scratch,
        compiler_params=_cparams(("arbitrary",) * 3 if side is not None else ("parallel",) * 3),
    )(qp, kp, v, o, lse, do, *s_ops)
    return (res[0], res[1]) if side is None else (res[0], res[1], list(res[2:]))


def _halo_specs(T, nT, tt, hr, cw, cb):
    k = tt // hr
    main = pl.BlockSpec((tt, cw), lambda b, t: (b * nT + t, cb))
    prev = pl.BlockSpec((hr, cw), lambda b, t: (jnp.maximum((b * nT + t) * k - 1, 0), cb))
    nxt = pl.BlockSpec((hr, cw), lambda b, t: (jnp.minimum((b * nT + t + 1) * k, T // hr - 1), cb))
    return main, prev, nxt


def _ln_rows(z, g, b):
    mu = jnp.mean(z, axis=-1, keepdims=True)
    d = z - mu
    var = jnp.mean(d * d, axis=-1, keepdims=True)
    rstd = lax.rsqrt(var + LN_EPS)
    xh = d * rstd
    return xh * g + b, xh, rstd


def conv_fwd(h, cb_a, cb_g, w, bias, lng, lnb, *, B, S, name):
    T = B * S
    K, C = w.shape
    hr = 32
    assert K - 1 <= hr
    tt = _tile(S, 512, hr)
    nT = S // tt
    a_m, a_p, _ = _halo_specs(T, nT, tt, hr, C, cb_a)
    g_m, g_p, _ = _halo_specs(T, nT, tt, hr, C, cb_g)

    def body(a_ref, g_ref, ap_ref, gp_ref, w_ref, b_ref, lg_ref, lb_ref, z_ref, y_ref, buf):
        t = pl.program_id(1)
        buf[pl.ds(hr, tt), :] = a_ref[...] * jax.nn.sigmoid(g_ref[...])
        hp = ap_ref[...] * jax.nn.sigmoid(gp_ref[...])
        buf[pl.ds(0, hr), :] = jnp.where(t == 0, 0.0, hp)
        z = jnp.broadcast_to(b_ref[...], (tt, C))
        for k in range(K):
            z = z + w_ref[k:k + 1, :] * buf[pl.ds(hr - (K - 1) + k, tt), :]
        z_ref[...] = z
        n, _, _ = _ln_rows(z, lg_ref[...], lb_ref[...])
        y_ref[...] = (n * jax.nn.sigmoid(n)).astype(_BF)

    vec = pl.BlockSpec((1, C), lambda b, t: (0, 0))
    out = pl.BlockSpec((tt, C), lambda b, t: (b * nT + t, 0))
    return pl.pallas_call(
        body, name=name, grid=(B, nT),
        in_specs=[a_m, g_m, a_p, g_p, pl.BlockSpec((K, C), lambda b, t: (0, 0)), vec, vec, vec],
        out_specs=[out, out],
        out_shape=[jax.ShapeDtypeStruct((T, C), _F32), jax.ShapeDtypeStruct((T, C), _BF)],
        scratch_shapes=[pltpu.VMEM((hr + tt, C), _F32)],
        compiler_params=_cparams(("parallel", "parallel")),
    )(h, h, h, h, w, bias.reshape(1, C), lng.reshape(1, C), lnb.reshape(1, C))


def conv_bwd(dmix, cb_dy, z, h, cb_a, cb_g, w, lng, lnb, *, B, S, name):
    T = B * S
    K, C = w.shape
    hr = 32
    tt = _tile(S, 512, hr)
    nT = S // tt
    a_m, a_p, _ = _halo_specs(T, nT, tt, hr, C, cb_a)
    g_m, g_p, _ = _halo_specs(T, nT, tt, hr, C, cb_g)
    dy_m, _, dy_n = _halo_specs(T, nT, tt, hr, C, cb_dy)
    z_m, _, z_n = _halo_specs(T, nT, tt, hr, C, 0)

    def body(dy_ref, dyn_ref, z_ref, zn_ref, a_ref, g_ref, ap_ref, gp_ref, w_ref, lg_ref, lb_ref,
             da_ref, dg_ref, dw_ref, db_ref, dlg_ref, dlb_ref, bufz, bufh):
        b, t = pl.program_id(0), pl.program_id(1)
        lg, lb = lg_ref[...], lb_ref[...]

        def dz_of(dy, zz):
            n, xh, rstd = _ln_rows(zz, lg, lb)
            sg = jax.nn.sigmoid(n)
            dn = dy.astype(_F32) * (sg * (1.0 + n * (1.0 - sg)))
            gdn = dn * lg
            m1 = jnp.mean(gdn, axis=-1, keepdims=True)
            m2 = jnp.mean(gdn * xh, axis=-1, keepdims=True)
            return rstd * (gdn - m1 - xh * m2), dn, xh

        dz, dn, xh = dz_of(dy_ref[...], z_ref[...])
        dzn, _, _ = dz_of(dyn_ref[...], zn_ref[...])
        bufz[pl.ds(0, tt), :] = dz
        bufz[pl.ds(tt, hr), :] = jnp.where(t == nT - 1, 0.0, dzn)
        a, g = a_ref[...], g_ref[...]
        sg = jax.nn.sigmoid(g)
        bufh[pl.ds(hr, tt), :] = a * sg
        bufh[pl.ds(0, hr), :] = jnp.where(t == 0, 0.0, ap_ref[...] * jax.nn.sigmoid(gp_ref[...]))

        @pl.when((b == 0) & (t == 0))
        def _():
            dw_ref[...] = jnp.zeros_like(dw_ref)
            db_ref[...] = jnp.zeros_like(db_ref)
            dlg_ref[...] = jnp.zeros_like(dlg_ref)
            dlb_ref[...] = jnp.zeros_like(dlb_ref)

        dhc = jnp.zeros((tt, C), _F32)
        for k in range(K):
            dhc = dhc + w_ref[k:k + 1, :] * bufz[pl.ds(K - 1 - k, tt), :]
            dw_ref[k:k + 1, :] += jnp.sum(dz * bufh[pl.ds(hr - (K - 1) + k, tt), :], axis=0, keepdims=True)
        da_ref[...] = (dhc * sg).astype(_BF)
        dg_ref[...] = (dhc * a * sg * (1.0 - sg)).astype(_BF)
        db_ref[...] += jnp.sum(dz, axis=0, keepdims=True)
        dlg_ref[...] += jnp.sum(dn * xh, axis=0, keepdims=True)
        dlb_ref[...] += jnp.sum(dn, axis=0, keepdims=True)

    vec = pl.BlockSpec((1, C), lambda b, t: (0, 0))
    out = pl.BlockSpec((tt, C), lambda b, t: (b * nT + t, 0))
    kc = pl.BlockSpec((K, C), lambda b, t: (0, 0))
    return pl.pallas_call(
        body, name=name, grid=(B, nT),
        in_specs=[dy_m, dy_n, z_m, z_n, a_m, g_m, a_p, g_p, kc, vec, vec],
        out_specs=[out, out, kc, vec, vec, vec],
        out_shape=[jax.ShapeDtypeStruct((T, C), _BF), jax.ShapeDtypeStruct((T, C), _BF),
                   jax.ShapeDtypeStruct((K, C), _F32)] + [jax.ShapeDtypeStruct((1, C), _F32)] * 3,
        scratch_shapes=[pltpu.VMEM((tt + hr, C), _F32), pltpu.VMEM((hr + tt, C), _F32)],
        compiler_params=_cparams(("arbitrary", "arbitrary")),
    )(dmix, dmix, z, z, h, h, h, h, w, lng.reshape(1, C), lnb.reshape(1, C))


def _pool_cnt(t, tt, w, rows):
    pos = t * tt + lax.broadcasted_iota(jnp.int32, (rows, 1), 0)
    return jnp.minimum(pos + 1, w).astype(_F32)


def pool_fwd(h, cb, wp, scale, *, B, S, name):
    T = B * S
    G, pg, _ = wp.shape
    C = G * pg
    assert pg == 128 and G == len(POOL_WINDOWS)
    hr = 16
    tt = _tile(S, 512, hr)
    nT = S // tt
    u_m, u_p, _ = _halo_specs(T, nT, tt, hr, C, cb)

    def body(u_ref, up_ref, wp_ref, sc_ref, y_ref, buf):
        t = pl.program_id(1)
        buf[pl.ds(hr, tt), :] = u_ref[...]
        buf[pl.ds(0, hr), :] = jnp.where(t == 0, 0.0, up_ref[...])
        for gi, w in enumerate(POOL_WINDOWS):
            ln = slice(gi * pg, (gi + 1) * pg)
            acc = buf[pl.ds(hr, tt), ln]
            for j in range(1, w):
                acc = acc + buf[pl.ds(hr - j, tt), ln]
            d = acc / _pool_cnt(t, tt, w, tt) - u_ref[:, ln]
            yg = jnp.dot(d.astype(_BF), wp_ref[gi].astype(_BF), preferred_element_type=_F32)
            y_ref[:, ln] = (yg * sc_ref[:, ln]).astype(_BF)

    return pl.pallas_call(
        body, name=name, grid=(B, nT),
        in_specs=[u_m, u_p, pl.BlockSpec((G, pg, pg), lambda b, t: (0, 0, 0)), pl.BlockSpec((1, C), lambda b, t: (0, 0))],
        out_specs=pl.BlockSpec((tt, C), lambda b, t: (b * nT + t, 0)),
        out_shape=jax.ShapeDtypeStruct((T, C), _BF),
        scratch_shapes=[pltpu.VMEM((hr + tt, C), _F32)],
        compiler_params=_cparams(("parallel", "parallel")),
    )(h, h, wp, scale.reshape(1, C))


def pool_bwd(dmix, cb_dy, h, cb, wp, scale, *, B, S, name):
    T = B * S
    G, pg, _ = wp.shape
    C = G * pg
    hr = 16
    tt = _tile(S, 512, hr)
    nT = S // tt
    u_m, u_p, _ = _halo_specs(T, nT, tt, hr, C, cb)
    dy_m, _, dy_n = _halo_specs(T, nT, tt, hr, C, cb_dy)

    def body(dy_ref, dyn_ref, u_ref, up_ref, wp_ref, sc_ref, du_ref, dwp_ref, dsc_ref, buf, bufe):
        b, t = pl.program_id(0), pl.program_id(1)
        buf[pl.ds(hr, tt), :] = u_ref[...]
        buf[pl.ds(0, hr), :] = jnp.where(t == 0, 0.0, up_ref[...])

        @pl.when((b == 0) & (t == 0))
        def _():
            dwp_ref[...] = jnp.zeros_like(dwp_ref)
            dsc_ref[...] = jnp.zeros_like(dsc_ref)

        nt = (((1,), (1,)), ((), ()))
        tn = (((0,), (0,)), ((), ()))
        for gi, w in enumerate(POOL_WINDOWS):
            ln = slice(gi * pg, (gi + 1) * pg)
            wg = wp_ref[gi].astype(_BF)
            sc = sc_ref[:, ln]
            dy = dy_ref[:, ln].astype(_F32)
            dz = (dy * sc).astype(_BF)
            dzn = (dyn_ref[:, ln].astype(_F32) * sc).astype(_BF)
            dd = lax.dot_general(dz, wg, nt, preferred_element_type=_F32)
            ddn = lax.dot_general(dzn, wg, nt, preferred_element_type=_F32)
            bufe[pl.ds(0, tt), ln] = dd / _pool_cnt(t, tt, w, tt)
            bufe[pl.ds(tt, hr), ln] = jnp.where(t == nT - 1, 0.0, ddn / _pool_cnt(t + 1, tt, w, hr))
            du = -dd
            for j in range(w):
                du = du + bufe[pl.ds(j, tt), ln]
            du_ref[:, ln] = du.astype(_BF)
            acc = buf[pl.ds(hr, tt), ln]
            for j in range(1, w):
                acc = acc + buf[pl.ds(hr - j, tt), ln]
            d = (acc / _pool_cnt(t, tt, w, tt) - u_ref[:, ln]).astype(_BF)
            dwp_ref[gi] += lax.dot_general(d, dz, tn, preferred_element_type=_F32)
            yg = jnp.dot(d, wg, preferred_element_type=_F32)
            dsc_ref[:, ln] += jnp.sum(dy * yg, axis=0, keepdims=True)

    return pl.pallas_call(
        body, name=name, grid=(B, nT),
        in_specs=[dy_m, dy_n, u_m, u_p, pl.BlockSpec((G, pg, pg), lambda b, t: (0, 0, 0)),
                  pl.BlockSpec((1, C), lambda b, t: (0, 0))],
        out_specs=[pl.BlockSpec((tt, C), lambda b, t: (b * nT + t, 0)), pl.BlockSpec((G, pg, pg), lambda b, t: (0, 0, 0)),
                   pl.BlockSpec((1, C), lambda b, t: (0, 0))],
        out_shape=[jax.ShapeDtypeStruct((T, C), _BF), jax.ShapeDtypeStruct((G, pg, pg), _F32),
                   jax.ShapeDtypeStruct((1, C), _F32)],
        scratch_shapes=[pltpu.VMEM((hr + tt, C), _F32), pltpu.VMEM((tt + hr, C), _F32)],
        compiler_params=_cparams(("arbitrary", "arbitrary")),
    )(dmix, dmix, h, h, wp, scale.reshape(1, C))


_FFN_HR = 16
_FFN_ROWS, _FFN_LANES = 32, 256


def _fold8(x):
    out = x[0:8]
    for r in range(8, x.shape[0], 8):
        out = out + x[r:r + 8]
    return out


def _silu_grad(x, sg):
    return sg * (1.0 + x * (1.0 - sg))


def _conv3(buf, w_ref, b_ref, off, rows, ln):
    c = b_ref[:, ln] + w_ref[0:1, ln] * buf[pl.ds(off, rows), ln]
    for k in (1, 2):
        c = c + w_ref[k:k + 1, ln] * buf[pl.ds(off + k, rows), ln]
    return c


def _ffn_chunks(nrows, ncols):
    lw = min(ncols, _FFN_LANES)
    return [(r, min(_FFN_ROWS, nrows - r), slice(l0, l0 + lw))
            for l0 in range(0, ncols, lw) for r in range(0, nrows, _FFN_ROWS)]


def gate_fwd(up, w, bias, *, B, S, name):
    T, F2 = up.shape
    F = F2 // 2
    hr = _FFN_HR
    tt = _tile(S, 512, hr)
    nT = S // tt
    tn = _tile(F, 512, 128)
    nC = F // tn
    k = tt // hr

    def body(a_ref, g_ref, ap_ref, gp_ref, wa_ref, wg_ref, ba_ref, bg_ref, o_ref, bufa, bufg):
        t = pl.program_id(2)
        for buf, m_ref, p_ref in ((bufa, a_ref, ap_ref), (bufg, g_ref, gp_ref)):
            buf[pl.ds(hr, tt), :] = m_ref[...].astype(_F32)
            buf[pl.ds(0, hr), :] = jnp.where(t == 0, 0.0, p_ref[...].astype(_F32))
        for r, rc, ln in _ffn_chunks(tt, tn):
            ca = _conv3(bufa, wa_ref, ba_ref, hr - 2 + r, rc, ln)
            cg = _conv3(bufg, wg_ref, bg_ref, hr - 2 + r, rc, ln)
            o_ref[pl.ds(r, rc), ln] = (ca * cg * jax.nn.sigmoid(cg)).astype(_BF)

    def main(off):
        return pl.BlockSpec((tt, tn), lambda b, j, t: (b * nT + t, j + off))

    def prev(off):
        return pl.BlockSpec((hr, tn), lambda b, j, t: (jnp.maximum((b * nT + t) * k - 1, 0), j + off))

    def wspec(rows, off):
        return pl.BlockSpec((rows, tn), lambda b, j, t: (0, j + off))

    return pl.pallas_call(
        body, name=name, grid=(B, nC, nT),
        in_specs=[main(0), main(nC), prev(0), prev(nC), wspec(3, 0), wspec(3, nC), wspec(1, 0), wspec(1, nC)],
        out_specs=pl.BlockSpec((tt, tn), lambda b, j, t: (b * nT + t, j)),
        out_shape=jax.ShapeDtypeStruct((T, F), _BF),
        scratch_shapes=[pltpu.VMEM((hr + tt, tn), _F32)] * 2,
        compiler_params=_cparams(("parallel", "parallel", "parallel")),
    )(up, up, up, up, w, w, bias.reshape(1, F2), bias.reshape(1, F2))


def gate_bwd(up, dact, w, bias, *, B, S, name):
    T, F2 = up.shape
    F = F2 // 2
    hr = _FFN_HR
    tt = _tile(S, 512, hr)
    nT = S // tt
    tn = _tile(F, 512, 128)
    nC = F // tn
    k = tt // hr
    ext = tt + hr

    def body(a_ref, g_ref, ap_ref, gp_ref, an_ref, gn_ref, d_ref, dn_ref, wa_ref, wg_ref, ba_ref, bg_ref,
             du_ref, dwa_ref, dwg_ref, dba_ref, dbg_ref, bufa, bufg, bufda, bufdg):
        b, t = pl.program_id(1), pl.program_id(2)
        last = t == nT - 1
        for buf, m_ref, p_ref, n_ref in ((bufa, a_ref, ap_ref, an_ref), (bufg, g_ref, gp_ref, gn_ref)):
            buf[pl.ds(hr, tt), :] = m_ref[...].astype(_F32)
            buf[pl.ds(0, hr), :] = jnp.where(t == 0, 0.0, p_ref[...].astype(_F32))
            buf[pl.ds(hr + tt, hr), :] = jnp.where(last, 0.0, n_ref[...].astype(_F32))
        for r, rc, ln in _ffn_chunks(ext, tn):
            ca = _conv3(bufa, wa_ref, ba_ref, hr - 2 + r, rc, ln)
            cg = _conv3(bufg, wg_ref, bg_ref, hr - 2 + r, rc, ln)
            sg = jax.nn.sigmoid(cg)
            if r < tt:
                da = d_ref[pl.ds(r, rc), ln].astype(_F32)
            else:
                da = jnp.where(last, 0.0, dn_ref[pl.ds(r - tt, rc), ln].astype(_F32))
            bufda[pl.ds(r, rc), ln] = da * cg * sg
            bufdg[pl.ds(r, rc), ln] = da * ca * _silu_grad(cg, sg)

        @pl.when((b == 0) & (t == 0))
        def _():
            for r in (dwa_ref, dwg_ref, dba_ref, dbg_ref):
                r[...] = jnp.zeros_like(r)

        lw, rc = min(tn, _FFN_LANES), _FFN_ROWS
        assert tt % rc == 0
        for half, bufd, buf, w_ref, dw_ref, db_ref in ((0, bufda, bufa, wa_ref, dwa_ref, dba_ref),
                                                       (1, bufdg, bufg, wg_ref, dwg_ref, dbg_ref)):
            for l0 in range(0, tn, lw):
                ln = slice(l0, l0 + lw)
                acc = [jnp.zeros((8, lw), _F32) for _ in range(4)]
                for r in range(0, tt, rc):
                    dc = [bufd[pl.ds(r + s_, rc), ln] for s_ in range(3)]
                    u = buf[pl.ds(hr + r, rc), ln]
                    du = w_ref[2:3, ln] * dc[0] + w_ref[1:2, ln] * dc[1] + w_ref[0:1, ln] * dc[2]
                    du_ref[half, pl.ds(r, rc), ln] = du.astype(_BF)
                    for kk in range(3):
                        acc[kk] = acc[kk] + _fold8(dc[2 - kk] * u)
                    acc[3] = acc[3] + _fold8(dc[0])
                for kk in range(3):
                    dw_ref[kk:kk + 1, ln] += jnp.sum(acc[kk], axis=0, keepdims=True)
                db_ref[:, ln] += jnp.sum(acc[3], axis=0, keepdims=True)

    def main(off):
        return pl.BlockSpec((tt, tn), lambda j, b, t: (b * nT + t, j + off))

    def prev(off):
        return pl.BlockSpec((hr, tn), lambda j, b, t: (jnp.maximum((b * nT + t) * k - 1, 0), j + off))

    def nxt(off):
        return pl.BlockSpec((hr, tn), lambda j, b, t: (jnp.minimum((b * nT + t + 1) * k, T // hr - 1), j + off))

    def wspec(rows, off):
        return pl.BlockSpec((rows, tn), lambda j, b, t: (0, j + off))

    both = pl.BlockSpec((2, tt, tn), lambda j, b, t: (0, b * nT + t, j))
    return pl.pallas_call(
        body, name=name, grid=(nC, B, nT),
        in_specs=[main(0), main(nC), prev(0), prev(nC), nxt(0), nxt(nC), main(0), nxt(0),
                  wspec(3, 0), wspec(3, nC), wspec(1, 0), wspec(1, nC)],
        out_specs=[both, wspec(3, 0), wspec(3, 0), wspec(1, 0), wspec(1, 0)],
        out_shape=[jax.ShapeDtypeStruct((2, T, F), _BF), jax.ShapeDtypeStruct((3, F), _F32), jax.ShapeDtypeStruct((3, F), _F32),
                   jax.ShapeDtypeStruct((1, F), _F32), jax.ShapeDtypeStruct((1, F), _F32)],
        scratch_shapes=[pltpu.VMEM((hr + ext, tn), _F32)] * 2 + [pltpu.VMEM((ext, tn), _F32)] * 2,
        compiler_params=_cparams(("parallel", "arbitrary", "arbitrary")),
    )(up, up, up, up, up, up, dact, dact, w, w, bias.reshape(1, F2), bias.reshape(1, F2))


def loss_head(y, target, *, name):
    T, C = y.shape
    tr = _row_tile(T, C)

    def body(y_ref, t_ref, dy_ref, acc_ref):
        @pl.when(pl.program_id(0) == 0)
        def _():
            acc_ref[...] = jnp.zeros_like(acc_ref)

        e = y_ref[...] - t_ref[...]
        dy_ref[...] = e * (1.0 / C)
        acc_ref[...] += jnp.sum(e * e, axis=0, keepdims=True) * (0.5 / C)

    row = pl.BlockSpec((tr, C), lambda i: (i, 0))
    return pl.pallas_call(
        body, name=name, grid=(T // tr,),
        in_specs=[row, row], out_specs=[row, pl.BlockSpec((1, C), lambda i: (0, 0))],
        out_shape=[jax.ShapeDtypeStruct((T, C), _F32), jax.ShapeDtypeStruct((1, C), _F32)],
        compiler_params=_cparams(("arbitrary",)),
    )(y, target)


def _adamw_update(w_ref, g_ref, m_ref, v_ref, d_ref, mo_ref, vo_ref):
    c1 = 1.0 - ADAM_B1 ** ADAM_STEP
    c2 = 1.0 - ADAM_B2 ** ADAM_STEP
    gg = g_ref[...]
    mn = ADAM_B1 * m_ref[...] + (1.0 - ADAM_B1) * gg
    vn = ADAM_B2 * v_ref[...] + (1.0 - ADAM_B2) * (gg * gg)
    d_ref[...] = -ADAM_LR * ((mn / c1) / (jnp.sqrt(vn / c2) + ADAM_EPS) + ADAM_WD * w_ref[...])
    mo_ref[...] = mn
    vo_ref[...] = vn


def adamw_small(ws, gs, ms, vs, *, name):
    n = len(ws)

    def body(*refs):
        for p in range(n):
            _adamw_update(*[refs[k * n + p] for k in range(7)])

    vm = pl.BlockSpec(memory_space=pltpu.VMEM)
    shapes = [jax.ShapeDtypeStruct(w.shape, _F32) for w in ws]
    res = pl.pallas_call(
        body, name=name, in_specs=[vm] * (4 * n), out_specs=[vm] * (3 * n), out_shape=shapes * 3,
        compiler_params=_cparams(),
    )(*ws, *gs, *ms, *vs)
    return res[:n], res[n:2 * n], res[2 * n:]


def adamw(w, g, m, v, *, side=None, name):
    R, C = w.shape
    tr = _tile(R, max(8, (256 * 1024) // C // 8 * 8), 8)
    s_ops, s_in, s_out, s_shapes, s_scratch, s_alias = _side_specs(side)
    ni, no = len(s_ops), len(s_shapes)

    def body(w_ref, g_ref, m_ref, v_ref, *rest):
        s_ins, outs, s_outs, sems = rest[:ni], rest[ni:ni + 3], rest[ni + 3:ni + 3 + no], rest[ni + 3 + no:]
        first, last = _grid_ends((R // tr,))
        if side is not None:
            @pl.when(first)
            def _():
                side.start(s_ins, s_outs, sems)

        _adamw_update(w_ref, g_ref, m_ref, v_ref, *outs)
        if side is not None:
            @pl.when(last)
            def _():
                side.wait(s_ins, s_outs, sems)

    blk = pl.BlockSpec((tr, C), lambda i: (i, 0))
    s = jax.ShapeDtypeStruct((R, C), _F32)
    res = pl.pallas_call(
        body, name=name, grid=(R // tr,),
        in_specs=[blk] * 4 + s_in, out_specs=[blk] * 3 + s_out, out_shape=[s, s, s] + s_shapes,
        input_output_aliases={4 + i_: 3 + o_ for i_, o_ in s_alias.items()},
        scratch_shapes=s_scratch,
        compiler_params=_cparams(("arbitrary",) if side is not None else ("parallel",)),
    )(w, g, m, v, *s_ops)
    return (res[0], res[1], res[2]) if side is None else (res[0], res[1], res[2], list(res[3:]))


_ANY = pl.BlockSpec(memory_space=pl.ANY)
_MESH = pl.DeviceIdType.MESH


def _place():
    return lax.axis_index("x"), lax.axis_index("y"), lax.axis_index("c")


def _other_chips(x, y):
    chips = [(1 - x, y), (x, 1 - y), (1 - x, 1 - y)]
    return chips, [2 * a + b for a, b in chips]


def _rcopy(src, dst, ssem, rsem, dev):
    return pltpu.make_async_remote_copy(src_ref=src, dst_ref=dst, send_sem=ssem, recv_sem=rsem,
                                        device_id=dev, device_id_type=_MESH)


def place_shard(w, l, chip_idx, *, name):
    _, _, hR, C = w.shape
    tr = _tile(hR, max(16, (512 * 1024) // C // 16 * 16), 16)

    def body(ci_ref, w_ref, o_ref):
        o_ref[...] = w_ref[...].astype(_BF)

    return pl.pallas_call(
        body, name=name,
        grid_spec=pltpu.PrefetchScalarGridSpec(
            num_scalar_prefetch=1, grid=(2, hR // tr),
            in_specs=[pl.BlockSpec((None, None, tr, C), lambda h, i, ci: (l, h, i, 0))],
            out_specs=pl.BlockSpec((None, None, tr, C), lambda h, i, ci: (ci[0], h, i, 0))),
        out_shape=jax.ShapeDtypeStruct((4, 2, hR, C), _BF),
        compiler_params=_cparams(("parallel", "parallel")),
    )(chip_idx, w)


class _Side:
    def __init__(self, arrays, out_shapes, aliases, n_sems, start, wait):
        self.arrays, self.out_shapes, self.aliases, self.n_sems = arrays, out_shapes, aliases, n_sems
        self.start, self.wait = start, wait


def gather_stage1_side(bufs):
    n = len(bufs)

    def copies(outs, sems, sending):
        x, y, c = _place()
        me = 2 * x + y
        chips, cidx = _other_chips(x, y)
        send, recv = sems
        out, back = [], []
        for k, chip in enumerate(chips):
            for p in range(n):
                mine, got = outs[p].at[me, c], outs[p].at[cidx[k], c]
                out.append(_rcopy(mine, mine, send.at[p * 3 + k], recv.at[p * 3 + k], (*chip, c)))
                if not sending:
                    back.append(_rcopy(got, got, send.at[p * 3 + k], recv.at[p * 3 + k], (*chip, c)))
        return out, back

    def start(ins, outs, sems):
        for cp in copies(outs, sems, True)[0]:
            cp.start()

    def wait(ins, outs, sems):
        out, back = copies(outs, sems, False)
        for cp in back:
            cp.wait_recv()
        for cp in out:
            cp.wait_send()

    return _Side(list(bufs), [jax.ShapeDtypeStruct(b.shape, b.dtype) for b in bufs], {p: p for p in range(n)},
                 [n * 3, n * 3], start, wait)


def gather_stage2_side(bufs):
    n = len(bufs)

    def copies(outs, sems, sending):
        x, y, c = _place()
        sib = (x, y, 1 - c)
        _, cidx = _other_chips(x, y)
        send, recv = sems
        out, back = [], []
        for k in range(3):
            for p in range(n):
                mine, got = outs[p].at[cidx[k], c], outs[p].at[cidx[k], 1 - c]
                out.append(_rcopy(mine, mine, send.at[p * 3 + k], recv.at[p * 3 + k], sib))
                if not sending:
                    back.append(_rcopy(got, got, send.at[p * 3 + k], recv.at[p * 3 + k], sib))
        return out, back

    def start(ins, outs, sems):
        for cp in copies(outs, sems, True)[0]:
            cp.start()

    def wait(ins, outs, sems):
        out, back = copies(outs, sems, False)
        for cp in back:
            cp.wait_recv()
        for cp in out:
            cp.wait_send()

    return _Side(list(bufs), [jax.ShapeDtypeStruct(b.shape, b.dtype) for b in bufs], {p: p for p in range(n)},
                 [n * 3, n * 3], start, wait)


def chip_exchange_side(ps):
    n = len(ps)

    def copies(ins, outs, sems, sending):
        x, y, c = _place()
        me = 2 * x + y
        chips, cidx = _other_chips(x, y)
        send, recv = sems
        out, back = [], []
        for k, chip in enumerate(chips):
            for p in range(n):
                got = outs[p].at[cidx[k]]
                out.append(_rcopy(ins[p].at[cidx[k]], outs[p].at[me], send.at[p * 3 + k], recv.at[p * 3 + k], (*chip, c)))
                if not sending:
                    back.append(_rcopy(got, got, send.at[p * 3 + k], recv.at[p * 3 + k], (*chip, c)))
        return out, back

    def start(ins, outs, sems):
        for cp in copies(ins, outs, sems, True)[0]:
            cp.start()

    def wait(ins, outs, sems):
        out, back = copies(ins, outs, sems, False)
        for cp in back:
            cp.wait_recv()
        for cp in out:
            cp.wait_send()

    return _Side(list(ps), [jax.ShapeDtypeStruct(a.shape, a.dtype) for a in ps], {}, [n * 3, n * 3], start, wait)


def _side_specs(side):
    if side is None:
        return [], [], [], [], [], {}
    return (side.arrays, [_ANY] * len(side.arrays), [_ANY] * len(side.out_shapes), side.out_shapes,
            [pltpu.SemaphoreType.DMA((k,)) for k in side.n_sems], side.aliases)


def run_side(side, *, name):
    ni, no = len(side.arrays), len(side.out_shapes)

    def body(*refs):
        ins, outs, sems = refs[:ni], refs[ni:ni + no], refs[ni + no:]
        side.start(ins, outs, sems)
        side.wait(ins, outs, sems)

    ops, in_specs, out_specs, out_shapes, scratch, aliases = _side_specs(side)
    return pl.pallas_call(body, name=name, in_specs=in_specs, out_specs=out_specs, out_shape=out_shapes,
                          input_output_aliases=aliases, scratch_shapes=scratch)(*ops)


def gather_small(small, *, name):
    ns = len(small)

    def body(*refs):
        s_in, s_out = refs[:ns], refs[ns:2 * ns]
        send, recv, lsem = refs[2 * ns:]
        x, y, c = _place()
        me = 2 * x + y
        chips, cidx = _other_chips(x, y)
        local = [pltpu.make_async_copy(s_in[q], s_out[q].at[me], lsem.at[q]) for q in range(ns)]
        cps = [_rcopy(s_in[q], s_out[q].at[me], send.at[q * 3 + k], recv.at[q * 3 + k], (*chip, c))
               for k, chip in enumerate(chips) for q in range(ns)]
        for cp in local + cps:
            cp.start()
        for k in range(3):
            for q in range(ns):
                got = s_out[q].at[cidx[k]]
                _rcopy(got, got, send.at[q * 3 + k], recv.at[q * 3 + k], (x, y, c)).wait_recv()
        for cp in cps:
            cp.wait_send()
        for cp in local:
            cp.wait()

    return pl.pallas_call(
        body, name=name, in_specs=[_ANY] * ns, out_specs=[_ANY] * ns,
        out_shape=[jax.ShapeDtypeStruct((4,) + a.shape, a.dtype) for a in small],
        scratch_shapes=[pltpu.SemaphoreType.DMA((ns * 3,))] * 2 + [pltpu.SemaphoreType.DMA((ns,))],
    )(*small)


def sibling_send_half(gs, *, name):
    n = len(gs)

    def body(*refs):
        g_in, g_out, send, recv = refs[:n], refs[n:2 * n], refs[2 * n], refs[2 * n + 1]
        x, y, c = _place()
        sib = (x, y, 1 - c)
        cps = [_rcopy(g_in[p].at[1 - c], g_out[p], send.at[p], recv.at[p], sib) for p in range(n)]
        for cp in cps:
            cp.start()
        for cp in cps:
            cp.wait()

    return pl.pallas_call(
        body, name=name, in_specs=[_ANY] * n, out_specs=[_ANY] * n,
        out_shape=[jax.ShapeDtypeStruct(a.shape[1:], a.dtype) for a in gs],
        scratch_shapes=[pltpu.SemaphoreType.DMA((n,))] * 2,
    )(*gs)


def sum_chips(p, slots, idx, *, name):
    _, N, C = p.shape
    tr = _tile(N, max(16, (512 * 1024) // C // 16 * 16), 16)

    def body(i0, i1, i2, i3, i4, p_ref, s0_ref, s1_ref, s2_ref, o_ref):
        s = p_ref[...].astype(_F32)
        for r in (s0_ref, s1_ref, s2_ref):
            s = s + r[...].astype(_F32)
        o_ref[...] = s

    def at(k):
        return pl.BlockSpec((None, tr, C), lambda i, *ix: (ix[k][0], i, 0))

    return pl.pallas_call(
        body, name=name,
        grid_spec=pltpu.PrefetchScalarGridSpec(
            num_scalar_prefetch=5, grid=(N // tr,),
            in_specs=[at(0), at(1), at(2), at(3)], out_specs=at(4)),
        out_shape=jax.ShapeDtypeStruct((2, N, C), _F32),
        compiler_params=_cparams(("parallel",)),
    )(*idx, p, slots, slots, slots)


def sibling_join(rs, *, name):
    n = len(rs)

    def body(*refs):
        r_out, send, recv = refs[n:2 * n], refs[2 * n], refs[2 * n + 1]
        x, y, c = _place()
        sib = (x, y, 1 - c)
        cps = [_rcopy(r_out[p].at[c], r_out[p].at[c], send.at[p], recv.at[p], sib) for p in range(n)]
        for cp in cps:
            cp.start()
        for p in range(n):
            got = r_out[p].at[1 - c]
            _rcopy(got, got, send.at[p], recv.at[p], sib).wait_recv()
        for cp in cps:
            cp.wait_send()

    return pl.pallas_call(
        body, name=name, in_specs=[_ANY] * n, out_specs=[_ANY] * n,
        out_shape=[jax.ShapeDtypeStruct(a.shape, a.dtype) for a in rs],
        input_output_aliases={p: p for p in range(n)},
        scratch_shapes=[pltpu.SemaphoreType.DMA((n,))] * 2,
    )(*rs)


def all_devices_exchange_side(v):
    def copies(ins, outs, sems):
        x, y, c = _place()
        me = 4 * x + 2 * y + c
        send, recv, lsem = sems
        peers = [(x ^ (k >> 2), y ^ ((k >> 1) & 1), c ^ (k & 1)) for k in range(1, 8)]
        local = pltpu.make_async_copy(ins[0], outs[0].at[me], lsem.at[0])
        out = [_rcopy(ins[0], outs[0].at[me], send.at[k], recv.at[k], peer) for k, peer in enumerate(peers)]
        return local, out, peers

    def start(ins, outs, sems):
        local, out, _ = copies(ins, outs, sems)
        local.start()
        for cp in out:
            cp.start()

    def wait(ins, outs, sems):
        local, out, peers = copies(ins, outs, sems)
        for k, (px, py, pc) in enumerate(peers):
            got = outs[0].at[4 * px + 2 * py + pc]
            _rcopy(got, got, sems[0].at[k], sems[1].at[k], peers[k]).wait_recv()
        for cp in out:
            cp.wait_send()
        local.wait()

    return _Side([v], [jax.ShapeDtypeStruct((8,) + v.shape, v.dtype)], {}, [7, 7, 1], start, wait)


def add_halves(gs_and_rs, c_idx, *, name):
    outs = []
    for n_, (g, r) in enumerate(gs_and_rs):
        N, C = r.shape
        tr = _tile(N, max(16, (512 * 1024) // C // 16 * 16), 16)

        def body(c_ref, g_ref, r_ref, o_ref):
            o_ref[...] = (g_ref[...].astype(_F32) + r_ref[...].astype(_F32)).astype(o_ref.dtype)

        outs.append(pl.pallas_call(
            body, name=f"{name}_{n_}",
            grid_spec=pltpu.PrefetchScalarGridSpec(
                num_scalar_prefetch=1, grid=(N // tr,),
                in_specs=[pl.BlockSpec((None, tr, C), lambda i, c: (c[0], i, 0)), pl.BlockSpec((tr, C), lambda i, c: (i, 0))],
                out_specs=pl.BlockSpec((tr, C), lambda i, c: (i, 0))),
            out_shape=jax.ShapeDtypeStruct((N, C), r.dtype),
            compiler_params=_cparams(("parallel",)),
        )(c_idx, g, r))
    return outs


def sum_slots(a, *, name):
    n, N, C = a.shape
    tr = _tile(N, max(16, (512 * 1024) // C // 16 * 16), 16)

    def body(a_ref, o_ref):
        s = a_ref[0].astype(_F32)
        for k in range(1, n):
            s = s + a_ref[k].astype(_F32)
        o_ref[...] = s

    return pl.pallas_call(
        body, name=name, grid=(N // tr,),
        in_specs=[pl.BlockSpec((n, tr, C), lambda i: (0, i, 0))],
        out_specs=pl.BlockSpec((tr, C), lambda i: (i, 0)),
        out_shape=jax.ShapeDtypeStruct((N, C), _F32),
        compiler_params=_cparams(("parallel",)),
    )(a)


_WEIGHTS = ['ln_in_g', 'ln_in_b', 'w_in', 'q_norm_g', 'w_uq', 'kv_norm_g', 'w_ukv', 'conv_w', 'conv_b', 'conv_ln_g',
            'conv_ln_b', 'w_pool', 'pool_scale', 'w_out', 'ln1_g', 'ln1_b', 'w_up', 'ffn_conv_w', 'ffn_conv_b', 'w_down',
            'ln2_g', 'ln2_b']
_BIG = ['w_in', 'w_uq', 'w_ukv', 'w_out', 'w_up', 'w_down']
_GRADS_EARLY = ('w_up', 'w_down')
_GRADS_LATE = tuple(n for n in _BIG if n not in _GRADS_EARLY)
_SMALL_SHARDED = ['conv_w', 'ffn_conv_w']
_SMALL = [n for n in _WEIGHTS if n not in _BIG]


def _rope_tables(positions):
    half = QK_ROPE // 2
    inv = 1.0 / (ROPE_THETA ** (jnp.arange(0, QK_ROPE, 2, dtype=_F32) / QK_ROPE))
    ang = positions.reshape(-1).astype(_F32)[:, None] * inv
    c, s = jnp.cos(ang), jnp.sin(ang)
    z = jnp.zeros_like(c)
    cc = jnp.concatenate([c, c, z, z], axis=1)
    sa = jnp.concatenate([-s, z, z, z], axis=1)
    sb = jnp.concatenate([z, s, z, z], axis=1)
    assert cc.shape[1] == 128 and half == 32
    return cc, sa, sb


def _in_pad(dims):
    D, QL, KVL, CW, PW, H, F = dims
    return (-(QL + 2 * CW + PW + KVL + 128)) % 512


def _layer_weights(full, dims):
    D, QL, KVL, CW, PW, H, F = dims
    w_in = full['w_in'].transpose(1, 0, 2).reshape(D, -1)
    o1, o2, o3, o4 = QL, QL + KVL, QL + KVL + QK_ROPE, QL + KVL + QK_ROPE + 2 * CW
    w_in_p = jnp.concatenate([w_in[:, :o1], w_in[:, o3:o4], w_in[:, o4:], w_in[:, o1:o2], w_in[:, o2:o3],
                              jnp.zeros((D, 128 - QK_ROPE + _in_pad(dims)), w_in.dtype)], axis=1)
    w_uq = full['w_uq'].reshape(QL, H, QK_NOPE + QK_ROPE)
    w_uq_p = jnp.pad(w_uq, ((0, 0), (0, 0), (0, HEAD_PAD - QK_NOPE - QK_ROPE))).reshape(QL, H * HEAD_PAD)
    return dict(
        w_in=w_in_p, w_uq=w_uq_p,
        w_ukv=full['w_ukv'].reshape(KVL, H * (QK_NOPE + V_HEAD)),
        w_out=full['w_out'].reshape(D, D),
        w_up=full['w_up'].transpose(1, 0, 2).reshape(D, 2 * F),
        w_down=full['w_down'].reshape(F, D),
    )


def _unpermute_w_in_grad(g, dims):
    D, QL, KVL, CW, PW, H, F = dims
    a, b_, c_ = QL, QL + 2 * CW, QL + 2 * CW + PW
    return jnp.concatenate([g[:, :a], g[:, c_:c_ + KVL], g[:, c_ + KVL:c_ + KVL + QK_ROPE], g[:, a:b_], g[:, b_:c_]], axis=1)


class _NoExchange:
    def __init__(self, layer_weights):
        self.layer_weights, self.grads = layer_weights, {}

    def weights(self, l):
        return self.layer_weights[l]

    def side(self, where, l):
        return None

    def side_done(self, where, l, outs):
        pass

    def grads_ready(self, l, g):
        self.grads.setdefault(l, {}).update(g)


def _with_side(hooks, where, l, fn):
    sd = hooks.side(where, l)
    res = fn(sd)
    if sd is None:
        return res
    hooks.side_done(where, l, res[-1])
    return res[0] if len(res) == 2 else res[:-1]


def _local_step(x, positions, target, small, dims, B, S, L, hooks):
    D, QL, KVL, CW, PW, H, F = dims
    T = B * S
    alpha = (2.0 * L) ** 0.25
    scale = float(QK_NOPE + QK_ROPE) ** -0.5
    cc, sa, sb = _rope_tables(positions)
    cb_q, cb_a, cb_g, cb_p = 0, QL // CW, QL // CW + 1, (QL + 2 * CW) // PW
    cb_kv, cb_kr = (QL + 2 * CW + PW) // KVL, (QL + 2 * CW + PW + KVL) // 128
    assert QL % CW == 0 and (QL + 2 * CW) % PW == 0 and (QL + 2 * CW + PW) % KVL == 0 and (QL + 2 * CW + PW + KVL) % 128 == 0

    xs, xb = ln_fwd([x], [1.0], small['ln_in_g'], small['ln_in_b'], want_r=False, name="ln_in")
    saved = []
    fa = dict(B=B, S=S, H=H, scale=scale)
    for l in range(L):
        W = hooks.weights(l)
        h = matmul(xb, W['w_in'], name="mm_in")
        qn = rms_fwd(h, cb_q, QL, small['q_norm_g'][l], name="rms_q")
        kvn = rms_fwd(h, cb_kv, KVL, small['kv_norm_g'][l], name="rms_kv")
        q = matmul(qn, W['w_uq'], name="mm_uq")
        kv = matmul(kvn, W['w_ukv'], name="mm_ukv")
        qp, kp, v = mla_pack(q, kv, h, cb_kr, cc, sa, sb, H=H, scale=scale, name="mla_pack")
        o, lse = _with_side(hooks, 'flash', l, lambda sd: flash_fwd(qp, kp, v, side=sd, name="flash_fwd", **fa))
        z, yc = conv_fwd(h, cb_a, cb_g, small['conv_w'][l], small['conv_b'][l], small['conv_ln_g'][l],
                         small['conv_ln_b'][l], B=B, S=S, name="conv_fwd")
        yp = pool_fwd(h, cb_p, small['w_pool'][l], small['pool_scale'][l], B=B, S=S, name="pool_fwd")
        mixed = jnp.concatenate([o.astype(_BF), yc, yp], axis=1)
        y1 = matmul(mixed, W['w_out'], name="mm_out")
        r1, x1, x1b = ln_fwd([xs, y1], [alpha, 1.0], small['ln1_g'][l], small['ln1_b'][l], want_r=True, name="ln1")
        up = _with_side(hooks, 'up', l, lambda sd: matmul(x1b, W['w_up'], out_dtype=_BF, side=sd, name="mm_up"))
        act = gate_fwd(up, small['ffn_conv_w'][l], small['ffn_conv_b'][l], B=B, S=S, name="gate_fwd")
        y2 = _with_side(hooks, 'down', l, lambda sd: matmul(act, W['w_down'], side=sd, name="mm_down"))
        r2, x2, x2b = ln_fwd([x1, y2], [alpha, 1.0], small['ln2_g'][l], small['ln2_b'][l], want_r=True, name="ln2")
        saved.append(dict(W=W, xb=xb, h=h, qn=qn, kvn=kvn, qp=qp, kp=kp, v=v, o=o, lse=lse, z=z, mixed=mixed, r1=r1,
                          x1b=x1b, up=up, act=act, r2=r2))
        xs, xb = x2, x2b

    dy, loss_cols = loss_head(xs, target, name="loss_head")
    gs = {n: [None] * L for n in _SMALL if n not in ('ln_in_g', 'ln_in_b')}
    d_terms, d_coefs = [dy], [1.0]
    zpad = jnp.zeros((T, _in_pad(dims)), _BF) if _in_pad(dims) else None
    for l in reversed(range(L)):
        sv = saved[l]
        W = sv['W']
        gb = {}
        dr2, dr2b, gs['ln2_g'][l], gs['ln2_b'][l] = ln_bwd(d_terms, d_coefs, sv['r2'], small['ln2_g'][l], name="ln2_bwd")
        dact = matmul(dr2b, W['w_down'], tb=True, out_dtype=_BF, name="mm_down_dx")
        gb['w_down'] = matmul(sv['act'], dr2b, ta=True, out_dtype=_BF, name="mm_down_dw")
        dup, dwa, dwg, dba, dbg = gate_bwd(sv['up'], dact, small['ffn_conv_w'][l], small['ffn_conv_b'][l],
                                           B=B, S=S, name="gate_bwd")
        gs['ffn_conv_w'][l] = jnp.concatenate([dwa, dwg], axis=1)
        gs['ffn_conv_b'][l] = jnp.concatenate([dba, dbg], axis=1)
        gb['w_up'] = _with_side(hooks, 'up_dw', l, lambda sd: matmul(sv['x1b'], dup, ta=True, b_halves=True, out_dtype=_BF,
                                                                     side=sd, name="mm_up_dw"))
        dx1 = _with_side(hooks, 'up_dx', l, lambda sd: matmul(dup, W['w_up'], tb=True, a_halves=True, side=sd,
                                                              name="mm_up_dx"))
        hooks.grads_ready(l, {n: gb.pop(n) for n in _GRADS_EARLY})
        dr1, dr1b, gs['ln1_g'][l], gs['ln1_b'][l] = ln_bwd([dr2, dx1], [alpha, 1.0], sv['r1'], small['ln1_g'][l],
                                                            name="ln1_bwd")
        dmix = matmul(dr1b, W['w_out'], tb=True, name="mm_out_dx")
        gb['w_out'] = matmul(sv['mixed'], dr1b, ta=True, out_dtype=_BF, name="mm_out_dw")
        h = sv['h']
        ncb = (H * V_HEAD) // CW
        dca, dcg, gs['conv_w'][l], gs['conv_b'][l], gs['conv_ln_g'][l], gs['conv_ln_b'][l] = conv_bwd(
            dmix, ncb, sv['z'], h, cb_a, cb_g, small['conv_w'][l], small['conv_ln_g'][l], small['conv_ln_b'][l],
            B=B, S=S, name="conv_bwd")
        dpool, gs['w_pool'][l], gs['pool_scale'][l] = pool_bwd(
            dmix, (H * V_HEAD + CW) // PW, h, cb_p, small['w_pool'][l], small['pool_scale'][l], B=B, S=S, name="pool_bwd")
        att = (sv['qp'], sv['kp'], sv['v'], sv['o'], sv['lse'], dmix, 0)
        dqp = _with_side(hooks, 'flash_dq', l, lambda sd: flash_bwd_dq(*att, side=sd, name="flash_dq", **fa))
        dkp, dv = _with_side(hooks, 'flash_dkv', l, lambda sd: flash_bwd_dkv(*att, side=sd, name="flash_dkv", **fa))
        dq, dkv, dkr = mla_unpack(dqp, dkp, dv, cc, sa, sb, H=H, name="mla_unpack")
        dqn = matmul(dq, W['w_uq'], tb=True, name="mm_uq_dx")
        g_uq = matmul(sv['qn'], dq, ta=True, out_dtype=_BF, name="mm_uq_dw")
        gb['w_uq'] = g_uq.reshape(QL, H, HEAD_PAD)[:, :, :QK_NOPE + QK_ROPE].reshape(QL, -1)
        dkvn = matmul(dkv, W['w_ukv'], tb=True, name="mm_ukv_dx")
        gb['w_ukv'] = matmul(sv['kvn'], dkv, ta=True, out_dtype=_BF, name="mm_ukv_dw")
        dcq, gs['q_norm_g'][l] = rms_bwd(dqn, h, cb_q, QL, small['q_norm_g'][l], name="rms_q_bwd")
        dckv, gs['kv_norm_g'][l] = rms_bwd(dkvn, h, cb_kv, KVL, small['kv_norm_g'][l], name="rms_kv_bwd")
        dh = jnp.concatenate([dcq, dca, dcg, dpool, dckv, dkr] + ([zpad] if zpad is not None else []), axis=1)
        gb['w_in'] = _unpermute_w_in_grad(matmul(sv['xb'], dh, ta=True, out_dtype=_BF, name="mm_in_dw"), dims)
        dxm = matmul(dh, W['w_in'], tb=True, name="mm_in_dx")
        hooks.grads_ready(l, gb)
        d_terms, d_coefs = [dr1, dxm], [alpha, 1.0]
    gx, _, g_ln_g, g_ln_b = ln_bwd(d_terms, d_coefs, x, small['ln_in_g'], name="ln_in_bwd")
    gsm = {n: jnp.stack([a.reshape(small[n].shape[1:]) for a in gs[n]]) for n in gs}
    gsm['ln_in_g'], gsm['ln_in_b'] = g_ln_g.reshape(-1), g_ln_b.reshape(-1)
    return loss_cols, gx, gsm


_COL_SHARDED = ('w_in', 'w_up')


def _flat_pad(arrs, mult=512 * 128):
    v = jnp.concatenate([a.reshape(-1) for a in arrs])
    n = v.shape[0]
    return jnp.pad(v, (0, (-n) % mult)).reshape(-1, 128)


def _split_like(flat, like):
    out, off = [], 0
    v = flat.reshape(-1)
    for a in like:
        out.append(v[off:off + a.size].reshape(a.shape))
        off += a.size
    return out


_GROUP_A = ('w_up',)
_GROUP_B = tuple(n for n in _BIG if n not in _GROUP_A)


class _Exchange:
    def __init__(self, a, dims, L, chip_idx, c_idx, sum_idx):
        self.dims, self.L, self.c_idx, self.sum_idx = dims, L, c_idx, sum_idx
        self.rows = {n: (a[n].shape[1], math.prod(a[n].shape[2:])) for n in _BIG}
        self.bufs = []
        for l in range(L):
            self.bufs.append({n: place_shard(a[n].reshape(L, 2, self.rows[n][0] // 2, self.rows[n][1]), l, chip_idx,
                                             name="place_" + n) for n in _BIG})
        self._store(0, _BIG, run_side(gather_stage1_side(self._list(0, _BIG)), name="gather_first_ici"))
        self._store(0, _BIG, run_side(gather_stage2_side(self._list(0, _BIG)), name="gather_first_d2d"))
        self.pending = {}
        self.reduced = {}

    def _list(self, l, names):
        return [self.bufs[l][n] for n in names]

    def _store(self, l, names, outs):
        self.bufs[l].update(zip(names, outs))

    def weights(self, l):
        return _layer_weights({n: self.bufs[l][n].reshape(4, *self.rows[n]) for n in _BIG}, self.dims)

    def side(self, where, l):
        if where in ('flash', 'up', 'down'):
            if l + 1 >= self.L:
                return None
            if where == 'down':
                return gather_stage2_side(self._list(l + 1, _BIG))
            return gather_stage1_side(self._list(l + 1, _GROUP_B if where == 'flash' else _GROUP_A))
        tag, names = self._riders(where)
        if tag not in self.pending:
            return None
        return chip_exchange_side([self.pending[tag][1][n] for n in names])

    @staticmethod
    def _riders(where):
        return {'flash_dq': ('early', ('w_down',)), 'flash_dkv': ('early', ('w_up',)), 'up_dw': ('late', _GRADS_LATE),
                'up_dx': ('none', ())}[where]

    def side_done(self, where, l, outs):
        if where in ('flash', 'up', 'down'):
            self._store(l + 1, {'flash': _GROUP_B, 'up': _GROUP_A, 'down': _BIG}[where], outs)
            return
        tag, names = self._riders(where)
        self.pending[tag][2].update(zip(names, outs))
        if len(self.pending[tag][2]) == len(self.pending[tag][1]):
            self._finish(tag)

    def _finish(self, tag):
        l, sums, slots = self.pending.pop(tag)
        halves = [sum_chips(sums[n], slots[n], self.sum_idx, name="grad_sum_" + n) for n in sums]
        joined = sibling_join(halves, name="grad_sibling_join_" + tag)
        self.reduced.setdefault(l, {}).update({n: j.reshape(self.rows[n]) for n, j in zip(sums, joined)})

    def grads_ready(self, l, g):
        names = tuple(n for n in _BIG if n in g)
        tag = 'early' if names == tuple(n for n in _BIG if n in _GRADS_EARLY) else 'late'
        g_in = []
        for n in names:
            r, c = self.rows[n]
            st = g[n].reshape(g[n].shape[0], 4, c).transpose(1, 0, 2) if n in _COL_SHARDED else g[n].reshape(4, r, c)
            g_in.append(st.reshape(4, 2, r // 2, c).transpose(1, 0, 2, 3).reshape(2, 2 * r, c))
        from_sib = sibling_send_half(g_in, name="grad_sibling_send_" + tag)
        sums = add_halves(list(zip(g_in, from_sib)), self.c_idx, name="grad_presum_" + tag)
        self.pending[tag] = (l, {n: p.reshape(4, p.shape[0] // 4, p.shape[1]) for n, p in zip(names, sums)}, {})

    def last_side(self):
        return chip_exchange_side(list(self.pending['late'][1].values()))

    def last_done(self, outs):
        self.pending['late'][2].update(zip(self.pending['late'][1], outs))
        self._finish('late')


def kernel(x, positions, ln_in_g, ln_in_b, w_in, q_norm_g, w_uq, kv_norm_g, w_ukv, conv_w, conv_b, conv_ln_g, conv_ln_b, w_pool, pool_scale, w_out, ln1_g, ln1_b, w_up, ffn_conv_w, ffn_conv_b, w_down, ln2_g, ln2_b, loss_target, m_ln_in_g, m_ln_in_b, m_w_in, m_q_norm_g, m_w_uq, m_kv_norm_g, m_w_ukv, m_conv_w, m_conv_b, m_conv_ln_g, m_conv_ln_b, m_w_pool, m_pool_scale, m_w_out, m_ln1_g, m_ln1_b, m_w_up, m_ffn_conv_w, m_ffn_conv_b, m_w_down, m_ln2_g, m_ln2_b, v_ln_in_g, v_ln_in_b, v_w_in, v_q_norm_g, v_w_uq, v_kv_norm_g, v_w_ukv, v_conv_w, v_conv_b, v_conv_ln_g, v_conv_ln_b, v_w_pool, v_pool_scale, v_w_out, v_ln1_g, v_ln1_b, v_w_up, v_ffn_conv_w, v_ffn_conv_b, v_w_down, v_ln2_g, v_ln2_b):
    a = dict(locals())
    B, S, D = a['x'].shape
    T = B * S
    L = a['w_in'].shape[0]
    QL, H = 4 * a['w_uq'].shape[1], a['w_uq'].shape[2]
    KVL = 4 * a['w_ukv'].shape[1]
    CW, PW = a['conv_b'].shape[1], a['pool_scale'].shape[1]
    F = 4 * a['w_down'].shape[1]
    dims = (D, QL, KVL, CW, PW, H, F)
    chip = 2 * lax.axis_index("x") + lax.axis_index("y")
    c_idx = lax.axis_index("c").astype(jnp.int32).reshape(1)

    def shard2d(w):
        return w.reshape(-1, w.shape[-1]) if w.ndim == 3 else w.reshape(w.shape[0] * w.shape[1], -1)

    small = {n: a[n] for n in _SMALL}
    for n, o in zip(_SMALL_SHARDED, gather_small([shard2d(a[n]) for n in _SMALL_SHARDED], name="gather_small")):
        k = a[n].shape[1]
        small[n] = o.reshape(4, L, k, -1).transpose(1, 2, 0, 3).reshape(L, k, -1)
    xi, yi = lax.axis_index("x"), lax.axis_index("y")
    chip_idx = chip.astype(jnp.int32).reshape(1)
    sum_idx = [v.astype(jnp.int32).reshape(1) for v in [chip] + _other_chips(xi, yi)[1] + [lax.axis_index("c")]]
    ex = _Exchange(a, dims, L, chip_idx, c_idx, sum_idx)

    loss_cols, gx, gsm = _local_step(a['x'].reshape(T, D), a['positions'], a['loss_target'].reshape(T, D),
                                     small, dims, B, S, L, ex)
    loss = lax.psum(jnp.sum(loss_cols), ("x", "y", "c"))

    grads, delta, new_m, new_v = {}, {}, {}, {}
    sm_like = [gsm[n] for n in _SMALL]
    riders = {'w_up': all_devices_exchange_side(_flat_pad(sm_like)), 'w_down': ex.last_side()}
    for n in _GRADS_EARLY + _GRADS_LATE:
        grads[n] = jnp.concatenate([ex.reduced[l][n] for l in range(L)]).reshape(a[n].shape)
        res = adamw(shard2d(a[n]), shard2d(grads[n]), shard2d(a['m_' + n]), shard2d(a['v_' + n]), side=riders.get(n),
                    name="adamw_" + n)
        delta[n], new_m[n], new_v[n] = (t.reshape(a[n].shape) for t in res[:3])
        if n == 'w_up':
            sm_slots = res[3][0]
        elif n == 'w_down':
            ex.last_done(res[3])

    g_small = dict(zip(_SMALL, _split_like(sum_slots(sm_slots, name="small_sum"), sm_like)))
    for n in _SMALL_SHARDED:
        w = a[n].shape[-1]
        g_small[n] = lax.dynamic_slice_in_dim(g_small[n], chip * w, w, axis=2)

    def at_least_2d(t):
        return t.reshape(1, -1) if t.ndim == 1 else t

    d_, m_, v_ = adamw_small(*[[at_least_2d(src[n]) for n in _SMALL] for src in (
        a, g_small, {n: a['m_' + n] for n in _SMALL}, {n: a['v_' + n] for n in _SMALL})], name="adamw_small")
    for n, dd, mm, vv in zip(_SMALL, d_, m_, v_):
        grads[n], delta[n], new_m[n], new_v[n] = g_small[n], dd.reshape(a[n].shape), mm.reshape(a[n].shape), vv.reshape(a[n].shape)

    return (loss, gx.reshape(B, S, D), *[grads[n] for n in _WEIGHTS], *[delta[n] for n in _WEIGHTS],
            *[new_m[n] for n in _WEIGHTS], *[new_v[n] for n in _WEIGHTS])
```

```python
import functools
import math

import jax
import jax.numpy as jnp
from jax import lax
from jax.experimental import pallas as pl
from jax.experimental.pallas import tpu as pltpu

_BF = jnp.bfloat16
_F32 = jnp.float32
_VMEM_LIMIT = 56 * 1024 * 1024

QK_NOPE = 128
QK_ROPE = 64
V_HEAD = 128
HEAD_PAD = 256
ROPE_THETA = 10000.0
LN_EPS = 1e-5
RMS_EPS = 1e-6
POOL_WINDOWS = (2, 4, 8, 16)
ADAM_LR, ADAM_B1, ADAM_B2, ADAM_EPS, ADAM_WD, ADAM_STEP = 0.001, 0.9, 0.999, 1e-8, 0.01, 10


def _cparams(sem=None):
    kw = dict(vmem_limit_bytes=_VMEM_LIMIT)
    if sem is not None:
        kw["dimension_semantics"] = sem
    return pltpu.CompilerParams(**kw)


def _tile(n, target, unit=128):
    if n <= target:
        return n
    t = (target // unit) * unit
    while t >= unit:
        if n % t == 0:
            return t
        t -= unit
    return n


_MM_VMEM_BUDGET = 40 * 1024 * 1024


def matmul(a, b, *, ta=False, tb=False, out_dtype=_F32, tm=1024, tn=1536, tk=4096, side=None, a_halves=False,
           b_halves=False, name="mm"):
    assert not (a_halves and ta) and not (b_halves and tb)
    if a_halves:
        M, K = a.shape[1], 2 * a.shape[2]
    elif ta:
        K, M = a.shape
    else:
        M, K = a.shape
    if b_halves:
        K2, N = b.shape[1], 2 * b.shape[2]
    elif tb:
        N, K2 = b.shape
    else:
        K2, N = b.shape
    assert K == K2, (a.shape, b.shape, ta, tb)
    tm, tn, tk = _tile(M, tm), _tile(N // 2 if b_halves else N, tn), _tile(K // 2 if a_halves else K, tk)
    ab, bb, ob = a.dtype.itemsize, b.dtype.itemsize, jnp.dtype(out_dtype).itemsize

    def vmem(tk_):
        return 2 * (tm * tk_ * ab + tk_ * tn * bb) + 2 * tm * tn * ob + tm * tn * 4 * (2 if K // tk_ > 1 else 1)

    k_part = K // 2 if a_halves else K
    while vmem(tk) > _MM_VMEM_BUDGET and tk > 256 and _tile(k_part, tk // 2) < tk:
        tk = _tile(k_part, tk // 2)
    nk = K // tk
    dn = (((0,) if ta else (1,), (1,) if tb else (0,)), ((), ()))

    s_ops, s_in, s_out, s_shapes, s_scratch, s_alias = _side_specs(side)
    ni, no, nacc = len(s_ops), len(s_shapes), int(nk > 1)
    grid = (M // tm, N // tn, nk)

    def body(a_ref, b_ref, *rest):
        s_ins, o_ref, s_outs = rest[:ni], rest[ni], rest[ni + 1:ni + 1 + no]
        acc, sems = rest[ni + 1 + no:ni + 1 + no + nacc], rest[ni + 1 + no + nacc:]
        i, j, k = pl.program_id(0), pl.program_id(1), pl.program_id(2)
        if side is not None:
            @pl.when((i == 0) & (j == 0) & (k == 0))
            def _():
                side.start(s_ins, s_outs, sems)

        prod = lax.dot_general(a_ref[...].astype(_BF), b_ref[...].astype(_BF), dn, preferred_element_type=_F32)
        if nk == 1:
            o_ref[...] = prod.astype(o_ref.dtype)
        else:
            acc_ref = acc[0]

            @pl.when(k == 0)
            def _():
                acc_ref[...] = prod

            @pl.when(k > 0)
            def _():
                acc_ref[...] += prod

            @pl.when(k == nk - 1)
            def _():
                o_ref[...] = acc_ref[...].astype(o_ref.dtype)

        if side is not None:
            @pl.when((i == grid[0] - 1) & (j == grid[1] - 1) & (k == nk - 1))
            def _():
                side.wait(s_ins, s_outs, sems)

    a_spec = pl.BlockSpec((tk, tm), lambda i, j, k: (k, i)) if ta else pl.BlockSpec((tm, tk), lambda i, j, k: (i, k))
    b_spec = pl.BlockSpec((tn, tk), lambda i, j, k: (j, k)) if tb else pl.BlockSpec((tk, tn), lambda i, j, k: (k, j))
    if a_halves:
        kh = nk // 2
        a_spec = pl.BlockSpec((None, tm, tk), lambda i, j, k: (k // kh, i, k % kh))
    if b_halves:
        jh = N // tn // 2
        b_spec = pl.BlockSpec((None, tk, tn), lambda i, j, k: (j // jh, k, j % jh))
    res = pl.pallas_call(
        body, name=name,
        grid=grid,
        in_specs=[a_spec, b_spec] + s_in,
        out_specs=[pl.BlockSpec((tm, tn), lambda i, j, k: (i, j))] + s_out,
        out_shape=[jax.ShapeDtypeStruct((M, N), out_dtype)] + s_shapes,
        input_output_aliases={2 + i_: 1 + o_ for i_, o_ in s_alias.items()},
        scratch_shapes=([pltpu.VMEM((tm, tn), _F32)] if nk > 1 else []) + s_scratch,
        compiler_params=_cparams(("arbitrary",) * 3 if side is not None else ("parallel", "parallel", "arbitrary")),
    )(a, b, *s_ops)
    return res[0] if side is None else (res[0], list(res[1:]))


def _row_tile(T, C, budget_rows=256):
    return _tile(T, budget_rows, 16)


def ln_fwd(xs, coefs, g, b, *, want_r, name):
    T, C = xs[0].shape
    tr = _row_tile(T, C)
    n = len(xs)

    def body(*refs):
        x_refs, (g_ref, b_ref), outs = refs[:n], refs[n:n + 2], refs[n + 2:]
        r = coefs[0] * x_refs[0][...]
        for c, xr in zip(coefs[1:], x_refs[1:]):
            r = r + c * xr[...]
        mu = jnp.mean(r, axis=-1, keepdims=True)
        d = r - mu
        var = jnp.mean(d * d, axis=-1, keepdims=True)
        y = d * lax.rsqrt(var + LN_EPS) * g_ref[...] + b_ref[...]
        if want_r:
            outs[0][...] = r
        outs[-2][...] = y
        outs[-1][...] = y.astype(_BF)

    row = pl.BlockSpec((tr, C), lambda i: (i, 0))
    vec = pl.BlockSpec((1, C), lambda i: (0, 0))
    f = jax.ShapeDtypeStruct((T, C), _F32)
    out_shape = ([f] if want_r else []) + [f, jax.ShapeDtypeStruct((T, C), _BF)]
    return pl.pallas_call(
        body, name=name, grid=(T // tr,),
        in_specs=[row] * n + [vec, vec],
        out_specs=[row] * len(out_shape), out_shape=out_shape,
        compiler_params=_cparams(("parallel",)),
    )(*xs, g.reshape(1, C), b.reshape(1, C))


def ln_bwd(dys, coefs, r, g, *, name):
    T, C = r.shape
    tr = _row_tile(T, C)
    n = len(dys)

    def body(*refs):
        dy_refs, r_ref, g_ref = refs[:n], refs[n], refs[n + 1]
        dr_ref, drb_ref, dg_ref, db_ref = refs[n + 2:]
        dy = coefs[0] * dy_refs[0][...]
        for c, dr_ in zip(coefs[1:], dy_refs[1:]):
            dy = dy + c * dr_[...]
        rr = r_ref[...]
        mu = jnp.mean(rr, axis=-1, keepdims=True)
        d = rr - mu
        var = jnp.mean(d * d, axis=-1, keepdims=True)
        rstd = lax.rsqrt(var + LN_EPS)
        xh = d * rstd
        gdy = dy * g_ref[...]
        m1 = jnp.mean(gdy, axis=-1, keepdims=True)
        m2 = jnp.mean(gdy * xh, axis=-1, keepdims=True)
        dr = rstd * (gdy - m1 - xh * m2)
        dr_ref[...] = dr
        drb_ref[...] = dr.astype(_BF)

        @pl.when(pl.program_id(0) == 0)
        def _():
            dg_ref[...] = jnp.zeros_like(dg_ref)
            db_ref[...] = jnp.zeros_like(db_ref)

        dg_ref[...] += jnp.sum(dy * xh, axis=0, keepdims=True)
        db_ref[...] += jnp.sum(dy, axis=0, keepdims=True)

    row = pl.BlockSpec((tr, C), lambda i: (i, 0))
    vec = pl.BlockSpec((1, C), lambda i: (0, 0))
    return pl.pallas_call(
        body, name=name, grid=(T // tr,),
        in_specs=[row] * (n + 1) + [vec],
        out_specs=[row, row, vec, vec],
        out_shape=[jax.ShapeDtypeStruct((T, C), _F32), jax.ShapeDtypeStruct((T, C), _BF),
                   jax.ShapeDtypeStruct((1, C), _F32), jax.ShapeDtypeStruct((1, C), _F32)],
        compiler_params=_cparams(("arbitrary",)),
    )(*dys, r, g.reshape(1, C))


def rms_fwd(h, cb, W, g, *, name):
    T = h.shape[0]
    tr = _tile(T, 512, 16)

    def body(c_ref, g_ref, o_ref):
        c = c_ref[...]
        ms = jnp.mean(c * c, axis=-1, keepdims=True)
        o_ref[...] = (c * lax.rsqrt(ms + RMS_EPS) * g_ref[...]).astype(_BF)

    return pl.pallas_call(
        body, name=name, grid=(T // tr,),
        in_specs=[pl.BlockSpec((tr, W), lambda i: (i, cb)), pl.BlockSpec((1, W), lambda i: (0, 0))],
        out_specs=pl.BlockSpec((tr, W), lambda i: (i, 0)),
        out_shape=jax.ShapeDtypeStruct((T, W), _BF),
        compiler_params=_cparams(("parallel",)),
    )(h, g.reshape(1, W))


def rms_bwd(dy, h, cb, W, g, *, name):
    T = h.shape[0]
    tr = _tile(T, 512, 16)

    def body(dy_ref, c_ref, g_ref, dc_ref, dg_ref):
        c = c_ref[...]
        dyv = dy_ref[...]
        ms = jnp.mean(c * c, axis=-1, keepdims=True)
        r = lax.rsqrt(ms + RMS_EPS)
        u = dyv * g_ref[...]
        m = jnp.mean(c * u, axis=-1, keepdims=True)
        dc_ref[...] = (r * u - c * (r * r * r) * m).astype(_BF)

        @pl.when(pl.program_id(0) == 0)
        def _():
            dg_ref[...] = jnp.zeros_like(dg_ref)

        dg_ref[...] += jnp.sum(dyv * c * r, axis=0, keepdims=True)

    return pl.pallas_call(
        body, name=name, grid=(T // tr,),
        in_specs=[pl.BlockSpec((tr, W), lambda i: (i, 0)), pl.BlockSpec((tr, W), lambda i: (i, cb)),
                  pl.BlockSpec((1, W), lambda i: (0, 0))],
        out_specs=[pl.BlockSpec((tr, W), lambda i: (i, 0)), pl.BlockSpec((1, W), lambda i: (0, 0))],
        out_shape=[jax.ShapeDtypeStruct((T, W), _BF), jax.ShapeDtypeStruct((1, W), _F32)],
        compiler_params=_cparams(("arbitrary",)),
    )(dy, h, g.reshape(1, W))


def _rope(u, cc, sa, sb, sign):
    return u * cc + sign * (pltpu.roll(u, 96, 1) * sa + pltpu.roll(u, 32, 1) * sb)


def mla_pack(q, kv, h, kr_cb, cc, sa, sb, *, H, scale, name):
    T = q.shape[0]
    tr = _tile(T, 256, 16)

    def body(q_ref, kv_ref, kr_ref, cc_ref, sa_ref, sb_ref, qp_ref, kp_ref, v_ref):
        cc_, sa_, sb_ = cc_ref[...], sa_ref[...], sb_ref[...]
        kr = _rope(kr_ref[...], cc_, sa_, sb_, 1.0).astype(_BF)
        for hh in range(H):
            o = hh * HEAD_PAD
            qp_ref[:, o:o + 128] = (q_ref[:, o:o + 128] * scale).astype(_BF)
            qp_ref[:, o + 128:o + 256] = (_rope(q_ref[:, o + 128:o + 256], cc_, sa_, sb_, 1.0) * scale).astype(_BF)
            kp_ref[:, o:o + 128] = kv_ref[:, o:o + 128].astype(_BF)
            kp_ref[:, o + 128:o + 256] = kr
            v_ref[:, hh * 128:(hh + 1) * 128] = kv_ref[:, o + 128:o + 256].astype(_BF)

    wide = pl.BlockSpec((tr, H * HEAD_PAD), lambda i: (i, 0))
    tab = pl.BlockSpec((tr, 128), lambda i: (i, 0))
    return pl.pallas_call(
        body, name=name, grid=(T // tr,),
        in_specs=[wide, wide, pl.BlockSpec((tr, 128), lambda i: (i, kr_cb)), tab, tab, tab],
        out_specs=[wide, wide, pl.BlockSpec((tr, H * 128), lambda i: (i, 0))],
        out_shape=[jax.ShapeDtypeStruct((T, H * HEAD_PAD), _BF), jax.ShapeDtypeStruct((T, H * HEAD_PAD), _BF),
                   jax.ShapeDtypeStruct((T, H * 128), _BF)],
        compiler_params=_cparams(("parallel",)),
    )(q, kv, h, cc, sa, sb)


def mla_unpack(dqp, dkp, dv, cc, sa, sb, *, H, name):
    T = dqp.shape[0]
    tr = _tile(T, 256, 16)

    def body(dq_ref, dk_ref, dv_ref, cc_ref, sa_ref, sb_ref, oq_ref, okv_ref, okr_ref):
        cc_, sa_, sb_ = cc_ref[...], sa_ref[...], sb_ref[...]
        kr = jnp.zeros((tr, 128), _F32)
        for hh in range(H):
            o = hh * HEAD_PAD
            oq_ref[:, o:o + 128] = dq_ref[:, o:o + 128].astype(_BF)
            oq_ref[:, o + 128:o + 256] = _rope(dq_ref[:, o + 128:o + 256], cc_, sa_, sb_, -1.0).astype(_BF)
            okv_ref[:, o:o + 128] = dk_ref[:, o:o + 128].astype(_BF)
            okv_ref[:, o + 128:o + 256] = dv_ref[:, hh * 128:(hh + 1) * 128].astype(_BF)
            kr = kr + dk_ref[:, o + 128:o + 256]
        okr_ref[...] = _rope(kr, cc_, sa_, sb_, -1.0).astype(_BF)

    wide = pl.BlockSpec((tr, H * HEAD_PAD), lambda i: (i, 0))
    tab = pl.BlockSpec((tr, 128), lambda i: (i, 0))
    return pl.pallas_call(
        body, name=name, grid=(T // tr,),
        in_specs=[wide, wide, pl.BlockSpec((tr, H * 128), lambda i: (i, 0)), tab, tab, tab],
        out_specs=[wide, wide, tab],
        out_shape=[jax.ShapeDtypeStruct((T, H * HEAD_PAD), _BF), jax.ShapeDtypeStruct((T, H * HEAD_PAD), _BF),
                   jax.ShapeDtypeStruct((T, 128), _BF)],
        compiler_params=_cparams(("parallel",)),
    )(dqp, dkp, dv, cc, sa, sb)


_NEG = -1e30


def _rows(ref, j, t):
    return ref[pl.ds(pl.multiple_of(j * t, t), t), :]


def _qk(q, k):
    return lax.dot_general(q, k, (((1,), (1,)), ((), ())), preferred_element_type=_F32)


def _scores(q, k, t, masked):
    s = _qk(q, k)
    if not masked:
        return s
    row = lax.broadcasted_iota(jnp.int32, (t, t), 0)
    col = lax.broadcasted_iota(jnp.int32, (t, t), 1)
    return jnp.where(col <= row, s, _NEG)


def flash_fwd(qp, kp, v, *, B, S, H, scale, side=None, name):
    T = B * S
    t = _tile(S, 512, 128)
    nq = S // t
    s_ops, s_in, s_out, s_shapes, s_scratch, s_alias = _side_specs(side)
    ni, no = len(s_ops), len(s_shapes)

    def body(q_ref, k_ref, v_ref, *rest):
        s_ins, (o_ref, lse_ref), s_outs = rest[:ni], rest[ni:ni + 2], rest[ni + 2:ni + 2 + no]
        (m_sc, l_sc, acc_sc), sems = rest[ni + 2 + no:ni + 5 + no], rest[ni + 5 + no:]
        i = pl.program_id(2)
        first = (pl.program_id(0) == 0) & (pl.program_id(1) == 0) & (i == 0)
        last = (pl.program_id(0) == B - 1) & (pl.program_id(1) == H - 1) & (i == nq - 1)
        if side is not None:
            @pl.when(first)
            def _():
                side.start(s_ins, s_outs, sems)

        m_sc[...] = jnp.full_like(m_sc, _NEG)
        l_sc[...] = jnp.zeros_like(l_sc)
        acc_sc[...] = jnp.zeros_like(acc_sc)

        def step(j, masked):
            s = _scores(q_ref[...], _rows(k_ref, j, t), t, masked)
            m_old = m_sc[...]
            m_new = jnp.maximum(m_old, jnp.max(s, axis=-1, keepdims=True))
            p = jnp.exp(s - m_new)
            a = jnp.exp(m_old - m_new)
            l_sc[...] = a * l_sc[...] + jnp.sum(p, axis=-1, keepdims=True)
            acc_sc[...] = a * acc_sc[...] + jnp.dot(p.astype(_BF), _rows(v_ref, j, t), preferred_element_type=_F32)
            m_sc[...] = m_new

        @pl.loop(0, i)
        def _(j):
            step(j, False)

        step(i, True)
        l = l_sc[...]
        o_ref[...] = acc_sc[...] / l
        lse_ref[...] = jnp.broadcast_to(m_sc[...] + jnp.log(l), lse_ref.shape)
        if side is not None:
            @pl.when(last)
            def _():
                side.wait(s_ins, s_outs, sems)

    qmap = lambda b, h, i: (b * nq + i, h)
    smap = lambda b, h, i: (b, h)
    res = pl.pallas_call(
        body, name=name, grid=(B, H, nq),
        in_specs=[pl.BlockSpec((t, HEAD_PAD), qmap), pl.BlockSpec((S, HEAD_PAD), smap), pl.BlockSpec((S, 128), smap)] + s_in,
        out_specs=[pl.BlockSpec((t, 128), qmap), pl.BlockSpec((t, 128), qmap)] + s_out,
        out_shape=[jax.ShapeDtypeStruct((T, H * 128), _F32), jax.ShapeDtypeStruct((T, H * 128), _F32)] + s_shapes,
        input_output_aliases={3 + i_: 2 + o_ for i_, o_ in s_alias.items()},
        scratch_shapes=[pltpu.VMEM((t, 1), _F32), pltpu.VMEM((t, 1), _F32), pltpu.VMEM((t, 128), _F32)] + s_scratch,
        compiler_params=_cparams(("arbitrary",) * 3 if side is not None else ("parallel",) * 3),
    )(qp, kp, v, *s_ops)
    return (res[0], res[1]) if side is None else (res[0], res[1], list(res[2:]))


def _grid_ends(grid):
    ids = [pl.program_id(d) for d in range(len(grid))]
    first, last = ids[0] == 0, ids[0] == grid[0] - 1
    for d in range(1, len(grid)):
        first, last = first & (ids[d] == 0), last & (ids[d] == grid[d] - 1)
    return first, last


def flash_bwd_dq(qp, kp, v, o, lse, do, do_cb0, *, B, S, H, scale, side=None, name):
    T = B * S
    t = _tile(S, 512, 128)
    nq = S // t
    s_ops, s_in, s_out, s_shapes, s_scratch, s_alias = _side_specs(side)
    ni, no = len(s_ops), len(s_shapes)

    def body(q_ref, k_ref, v_ref, o_ref, lse_ref, do_ref, *rest):
        s_ins, dq_ref, s_outs = rest[:ni], rest[ni], rest[ni + 1:ni + 1 + no]
        (acc_sc, dl_sc), sems = rest[ni + 1 + no:ni + 3 + no], rest[ni + 3 + no:]
        first, last = _grid_ends((B, H, nq))
        if side is not None:
            @pl.when(first)
            def _():
                side.start(s_ins, s_outs, sems)

        i = pl.program_id(2)
        acc_sc[...] = jnp.zeros_like(acc_sc)
        dl_sc[...] = jnp.sum(do_ref[...].astype(_F32) * o_ref[...], axis=-1, keepdims=True)

        def step(j, masked):
            k = _rows(k_ref, j, t)
            s = _scores(q_ref[...], k, t, masked)
            p = jnp.exp(s - lse_ref[:, 0:1])
            dp = _qk(do_ref[...].astype(_BF), _rows(v_ref, j, t))
            ds = p * (dp - dl_sc[...])
            acc_sc[...] += jnp.dot(ds.astype(_BF), k, preferred_element_type=_F32)

        @pl.loop(0, i)
        def _(j):
            step(j, False)

        step(i, True)
        dq_ref[...] = acc_sc[...] * scale
        if side is not None:
            @pl.when(last)
            def _():
                side.wait(s_ins, s_outs, sems)

    qmap = lambda b, h, i: (b * nq + i, h)
    domap = lambda b, h, i: (b * nq + i, do_cb0 + h)
    smap = lambda b, h, i: (b, h)
    res = pl.pallas_call(
        body, name=name, grid=(B, H, nq),
        in_specs=[pl.BlockSpec((t, HEAD_PAD), qmap), pl.BlockSpec((S, HEAD_PAD), smap), pl.BlockSpec((S, 128), smap),
                  pl.BlockSpec((t, 128), qmap), pl.BlockSpec((t, 128), qmap), pl.BlockSpec((t, 128), domap)] + s_in,
        out_specs=[pl.BlockSpec((t, HEAD_PAD), qmap)] + s_out,
        out_shape=[jax.ShapeDtypeStruct((T, H * HEAD_PAD), _F32)] + s_shapes,
        input_output_aliases={6 + i_: 1 + o_ for i_, o_ in s_alias.items()},
        scratch_shapes=[pltpu.VMEM((t, HEAD_PAD), _F32), pltpu.VMEM((t, 1), _F32)] + s_scratch,
        compiler_params=_cparams(("arbitrary",) * 3 if side is not None else ("parallel",) * 3),
    )(qp, kp, v, o, lse, do, *s_ops)
    return res[0] if side is None else (res[0], list(res[1:]))


def flash_bwd_dkv(qp, kp, v, o, lse, do, do_cb0, *, B, S, H, scale, side=None, name):
    T = B * S
    t = _tile(S, 512, 128)
    nk = S // t
    s_ops, s_in, s_out, s_shapes, s_scratch, s_alias = _side_specs(side)
    ni, no = len(s_ops), len(s_shapes)

    def body(q_ref, k_ref, v_ref, o_ref, lse_ref, do_ref, *rest):
        s_ins, (dk_ref, dv_ref), s_outs = rest[:ni], rest[ni:ni + 2], rest[ni + 2:ni + 2 + no]
        (dk_sc, dv_sc), sems = rest[ni + 2 + no:ni + 4 + no], rest[ni + 4 + no:]
        first, last = _grid_ends((B, H, nk))
        if side is not None:
            @pl.when(first)
            def _():
                side.start(s_ins, s_outs, sems)

        j = pl.program_id(2)
        dk_sc[...] = jnp.zeros_like(dk_sc)
        dv_sc[...] = jnp.zeros_like(dv_sc)

        def step(i, masked):
            q = _rows(q_ref, i, t)
            do = _rows(do_ref, i, t).astype(_F32)
            dob = do.astype(_BF)
            s = _scores(q, k_ref[...], t, masked)
            p = jnp.exp(s - _rows(lse_ref, i, t)[:, 0:1])
            dl = jnp.sum(do * _rows(o_ref, i, t), axis=-1, keepdims=True)
            dp = _qk(dob, v_ref[...])
            ds = p * (dp - dl)
            tn = (((0,), (0,)), ((), ()))
            dv_sc[...] += lax.dot_general(p.astype(_BF), dob, tn, preferred_element_type=_F32)
            dk_sc[...] += lax.dot_general(ds.astype(_BF), q, tn, preferred_element_type=_F32)

        step(j, True)

        @pl.loop(j + 1, nk)
        def _(i):
            step(i, False)

        dk_ref[...] = dk_sc[...]
        dv_ref[...] = dv_sc[...]
        if side is not None:
            @pl.when(last)
            def _():
                side.wait(s_ins, s_outs, sems)

    smap = lambda b, h, j: (b, h)
    domap = lambda b, h, j: (b, do_cb0 + h)
    kmap = lambda b, h, j: (b * nk + j, h)
    res = pl.pallas_call(
        body, name=name, grid=(B, H, nk),
        in_specs=[pl.BlockSpec((S, HEAD_PAD), smap), pl.BlockSpec((t, HEAD_PAD), kmap), pl.BlockSpec((t, 128), kmap),
                  pl.BlockSpec((S, 128), smap), pl.BlockSpec((S, 128), smap), pl.BlockSpec((S, 128), domap)] + s_in,
        out_specs=[pl.BlockSpec((t, HEAD_PAD), kmap), pl.BlockSpec((t, 128), kmap)] + s_out,
        out_shape=[jax.ShapeDtypeStruct((T, H * HEAD_PAD), _F32), jax.ShapeDtypeStruct((T, H * 128), _F32)] + s_shapes,
        input_output_aliases={6 + i_: 2 + o_ for i_, o_ in s_alias.items()},
        scratch_shapes=[pltpu.VMEM((t, HEAD_PAD), _F32), pltpu.VMEM((t, 128), _F32)] + s_scratch,
        compiler_params=_cparams(("arbitrary",) * 3 if side is not None else ("parallel",) * 3),
    )(qp, kp, v, o, lse, do, *s_ops)
    return (res[0], res[1]) if side is None else (res[0], res[1], list(res[2:]))


def _halo_specs(T, nT, tt, hr, cw, cb):
    k = tt // hr
    main = pl.BlockSpec((tt, cw), lambda b, t: (b * nT + t, cb))
    prev = pl.BlockSpec((hr, cw), lambda b, t: (jnp.maximum((b * nT + t) * k - 1, 0), cb))
    nxt = pl.BlockSpec((hr, cw), lambda b, t: (jnp.minimum((b * nT + t + 1) * k, T // hr - 1), cb))
    return main, prev, nxt


_CONV_ROWS = 32


def _ln_rows(z, g, b):
    mu = jnp.mean(z, axis=-1, keepdims=True)
    d = z - mu
    var = jnp.mean(d * d, axis=-1, keepdims=True)
    rstd = lax.rsqrt(var + LN_EPS)
    xh = d * rstd
    return xh * g + b, xh, rstd


def conv_fwd(h, cb_a, cb_g, w, bias, lng, lnb, *, B, S, name):
    T = B * S
    K, C = w.shape
    hr = 32
    assert K - 1 <= hr
    tt = _tile(S, 512, hr)
    nT = S // tt
    a_m, a_p, _ = _halo_specs(T, nT, tt, hr, C, cb_a)
    g_m, g_p, _ = _halo_specs(T, nT, tt, hr, C, cb_g)

    def body(a_ref, g_ref, ap_ref, gp_ref, w_ref, b_ref, lg_ref, lb_ref, z_ref, y_ref, buf):
        t = pl.program_id(1)
        buf[pl.ds(hr, tt), :] = a_ref[...] * jax.nn.sigmoid(g_ref[...])
        hp = ap_ref[...] * jax.nn.sigmoid(gp_ref[...])
        buf[pl.ds(0, hr), :] = jnp.where(t == 0, 0.0, hp)
        z = jnp.broadcast_to(b_ref[...], (tt, C))
        for k in range(K):
            z = z + w_ref[k:k + 1, :] * buf[pl.ds(hr - (K - 1) + k, tt), :]
        z_ref[...] = z
        n, _, _ = _ln_rows(z, lg_ref[...], lb_ref[...])
        y_ref[...] = (n * jax.nn.sigmoid(n)).astype(_BF)

    vec = pl.BlockSpec((1, C), lambda b, t: (0, 0))
    out = pl.BlockSpec((tt, C), lambda b, t: (b * nT + t, 0))
    return pl.pallas_call(
        body, name=name, grid=(B, nT),
        in_specs=[a_m, g_m, a_p, g_p, pl.BlockSpec((K, C), lambda b, t: (0, 0)), vec, vec, vec],
        out_specs=[out, out],
        out_shape=[jax.ShapeDtypeStruct((T, C), _F32), jax.ShapeDtypeStruct((T, C), _BF)],
        scratch_shapes=[pltpu.VMEM((hr + tt, C), _F32)],
        compiler_params=_cparams(("parallel", "parallel")),
    )(h, h, h, h, w, bias.reshape(1, C), lng.reshape(1, C), lnb.reshape(1, C))


def conv_bwd(dmix, cb_dy, z, h, cb_a, cb_g, w, lng, lnb, *, B, S, name):
    T = B * S
    K, C = w.shape
    hr = 32
    tt = _tile(S, 512, hr)
    nT = S // tt
    a_m, a_p, _ = _halo_specs(T, nT, tt, hr, C, cb_a)
    g_m, g_p, _ = _halo_specs(T, nT, tt, hr, C, cb_g)
    dy_m, _, dy_n = _halo_specs(T, nT, tt, hr, C, cb_dy)
    z_m, _, z_n = _halo_specs(T, nT, tt, hr, C, 0)

    def body(dy_ref, dyn_ref, z_ref, zn_ref, a_ref, g_ref, ap_ref, gp_ref, w_ref, lg_ref, lb_ref,
             da_ref, dg_ref, dw_ref, db_ref, dlg_ref, dlb_ref, bufz, bufh):
        b, t = pl.program_id(0), pl.program_id(1)
        lg, lb = lg_ref[...], lb_ref[...]

        def dz_of(dy, zz):
            n, xh, rstd = _ln_rows(zz, lg, lb)
            sg = jax.nn.sigmoid(n)
            dn = dy.astype(_F32) * (sg * (1.0 + n * (1.0 - sg)))
            gdn = dn * lg
            m1 = jnp.mean(gdn, axis=-1, keepdims=True)
            m2 = jnp.mean(gdn * xh, axis=-1, keepdims=True)
            return rstd * (gdn - m1 - xh * m2), dn, xh

        rc = _CONV_ROWS
        acc = [jnp.zeros((8, C), _F32) for _ in range(3)]
        for r in range(0, tt, rc):
            rows = pl.ds(r, rc)
            dz, dn, xh = dz_of(dy_ref[rows, :], z_ref[rows, :])
            bufz[rows, :] = dz
            acc = [acc[0] + _fold8(dz), acc[1] + _fold8(dn * xh), acc[2] + _fold8(dn)]
            bufh[pl.ds(hr + r, rc), :] = a_ref[rows, :] * jax.nn.sigmoid(g_ref[rows, :])
        dzn, _, _ = dz_of(dyn_ref[...], zn_ref[...])
        bufz[pl.ds(tt, hr), :] = jnp.where(t == nT - 1, 0.0, dzn)
        bufh[pl.ds(0, hr), :] = jnp.where(t == 0, 0.0, ap_ref[...] * jax.nn.sigmoid(gp_ref[...]))

        @pl.when((b == 0) & (t == 0))
        def _():
            dw_ref[...] = jnp.zeros_like(dw_ref)
            db_ref[...] = jnp.zeros_like(db_ref)
            dlg_ref[...] = jnp.zeros_like(dlg_ref)
            dlb_ref[...] = jnp.zeros_like(dlb_ref)

        for r in range(0, tt, rc):
            rows = pl.ds(r, rc)
            dhc = jnp.zeros((rc, C), _F32)
            for k in range(K):
                dhc = dhc + w_ref[k:k + 1, :] * bufz[pl.ds(K - 1 - k + r, rc), :]
            a, sg = a_ref[rows, :], jax.nn.sigmoid(g_ref[rows, :])
            da_ref[rows, :] = (dhc * sg).astype(_BF)
            dg_ref[rows, :] = (dhc * a * sg * (1.0 - sg)).astype(_BF)
        for k in range(K):
            tap = jnp.zeros((8, C), _F32)
            for r in range(0, tt, rc):
                tap = tap + _fold8(bufz[pl.ds(r, rc), :] * bufh[pl.ds(hr - (K - 1) + k + r, rc), :])
            dw_ref[k:k + 1, :] += jnp.sum(tap, axis=0, keepdims=True)
        db_ref[...] += jnp.sum(acc[0], axis=0, keepdims=True)
        dlg_ref[...] += jnp.sum(acc[1], axis=0, keepdims=True)
        dlb_ref[...] += jnp.sum(acc[2], axis=0, keepdims=True)

    vec = pl.BlockSpec((1, C), lambda b, t: (0, 0))
    out = pl.BlockSpec((tt, C), lambda b, t: (b * nT + t, 0))
    kc = pl.BlockSpec((K, C), lambda b, t: (0, 0))
    return pl.pallas_call(
        body, name=name, grid=(B, nT),
        in_specs=[dy_m, dy_n, z_m, z_n, a_m, g_m, a_p, g_p, kc, vec, vec],
        out_specs=[out, out, kc, vec, vec, vec],
        out_shape=[jax.ShapeDtypeStruct((T, C), _BF), jax.ShapeDtypeStruct((T, C), _BF),
                   jax.ShapeDtypeStruct((K, C), _F32)] + [jax.ShapeDtypeStruct((1, C), _F32)] * 3,
        scratch_shapes=[pltpu.VMEM((tt + hr, C), _F32), pltpu.VMEM((hr + tt, C), _F32)],
        compiler_params=_cparams(("arbitrary", "arbitrary")),
    )(dmix, dmix, z, z, h, h, h, h, w, lng.reshape(1, C), lnb.reshape(1, C))


def _pool_cnt(t, tt, w, rows):
    pos = t * tt + lax.broadcasted_iota(jnp.int32, (rows, 1), 0)
    return jnp.minimum(pos + 1, w).astype(_F32)


def pool_fwd(h, cb, wp, scale, *, B, S, name):
    T = B * S
    G, pg, _ = wp.shape
    C = G * pg
    assert pg == 128 and G == len(POOL_WINDOWS)
    hr = 16
    tt = _tile(S, 512, hr)
    nT = S // tt
    u_m, u_p, _ = _halo_specs(T, nT, tt, hr, C, cb)

    def body(u_ref, up_ref, wp_ref, sc_ref, y_ref, buf):
        t = pl.program_id(1)
        buf[pl.ds(hr, tt), :] = u_ref[...]
        buf[pl.ds(0, hr), :] = jnp.where(t == 0, 0.0, up_ref[...])
        for gi, w in enumerate(POOL_WINDOWS):
            ln = slice(gi * pg, (gi + 1) * pg)
            acc = buf[pl.ds(hr, tt), ln]
            for j in range(1, w):
                acc = acc + buf[pl.ds(hr - j, tt), ln]
            d = acc / _pool_cnt(t, tt, w, tt) - u_ref[:, ln]
            yg = jnp.dot(d.astype(_BF), wp_ref[gi].astype(_BF), preferred_element_type=_F32)
            y_ref[:, ln] = (yg * sc_ref[:, ln]).astype(_BF)

    return pl.pallas_call(
        body, name=name, grid=(B, nT),
        in_specs=[u_m, u_p, pl.BlockSpec((G, pg, pg), lambda b, t: (0, 0, 0)), pl.BlockSpec((1, C), lambda b, t: (0, 0))],
        out_specs=pl.BlockSpec((tt, C), lambda b, t: (b * nT + t, 0)),
        out_shape=jax.ShapeDtypeStruct((T, C), _BF),
        scratch_shapes=[pltpu.VMEM((hr + tt, C), _F32)],
        compiler_params=_cparams(("parallel", "parallel")),
    )(h, h, wp, scale.reshape(1, C))


def pool_bwd(dmix, cb_dy, h, cb, wp, scale, *, B, S, name):
    T = B * S
    G, pg, _ = wp.shape
    C = G * pg
    hr = 16
    tt = _tile(S, 512, hr)
    nT = S // tt
    u_m, u_p, _ = _halo_specs(T, nT, tt, hr, C, cb)
    dy_m, _, dy_n = _halo_specs(T, nT, tt, hr, C, cb_dy)

    def body(dy_ref, dyn_ref, u_ref, up_ref, wp_ref, sc_ref, du_ref, dwp_ref, dsc_ref, buf, bufe):
        b, t = pl.program_id(0), pl.program_id(1)
        buf[pl.ds(hr, tt), :] = u_ref[...]
        buf[pl.ds(0, hr), :] = jnp.where(t == 0, 0.0, up_ref[...])

        @pl.when((b == 0) & (t == 0))
        def _():
            dwp_ref[...] = jnp.zeros_like(dwp_ref)
            dsc_ref[...] = jnp.zeros_like(dsc_ref)

        nt = (((1,), (1,)), ((), ()))
        tn = (((0,), (0,)), ((), ()))
        for gi, w in enumerate(POOL_WINDOWS):
            ln = slice(gi * pg, (gi + 1) * pg)
            wg = wp_ref[gi].astype(_BF)
            sc = sc_ref[:, ln]
            dy = dy_ref[:, ln].astype(_F32)
            dz = (dy * sc).astype(_BF)
            dzn = (dyn_ref[:, ln].astype(_F32) * sc).astype(_BF)
            dd = lax.dot_general(dz, wg, nt, preferred_element_type=_F32)
            ddn = lax.dot_general(dzn, wg, nt, preferred_element_type=_F32)
            bufe[pl.ds(0, tt), ln] = dd / _pool_cnt(t, tt, w, tt)
            bufe[pl.ds(tt, hr), ln] = jnp.where(t == nT - 1, 0.0, ddn / _pool_cnt(t + 1, tt, w, hr))
            du = -dd
            for j in range(w):
                du = du + bufe[pl.ds(j, tt), ln]
            du_ref[:, ln] = du.astype(_BF)
            acc = buf[pl.ds(hr, tt), ln]
            for j in range(1, w):
                acc = acc + buf[pl.ds(hr - j, tt), ln]
            d = (acc / _pool_cnt(t, tt, w, tt) - u_ref[:, ln]).astype(_BF)
            dwp_ref[gi] += lax.dot_general(d, dz, tn, preferred_element_type=_F32)
            yg = jnp.dot(d, wg, preferred_element_type=_F32)
            dsc_ref[:, ln] += jnp.sum(dy * yg, axis=0, keepdims=True)

    return pl.pallas_call(
        body, name=name, grid=(B, nT),
        in_specs=[dy_m, dy_n, u_m, u_p, pl.BlockSpec((G, pg, pg), lambda b, t: (0, 0, 0)),
                  pl.BlockSpec((1, C), lambda b, t: (0, 0))],
        out_specs=[pl.BlockSpec((tt, C), lambda b, t: (b * nT + t, 0)), pl.BlockSpec((G, pg, pg), lambda b, t: (0, 0, 0)),
                   pl.BlockSpec((1, C), lambda b, t: (0, 0))],
        out_shape=[jax.ShapeDtypeStruct((T, C), _BF), jax.ShapeDtypeStruct((G, pg, pg), _F32),
                   jax.ShapeDtypeStruct((1, C), _F32)],
        scratch_shapes=[pltpu.VMEM((hr + tt, C), _F32), pltpu.VMEM((tt + hr, C), _F32)],
        compiler_params=_cparams(("arbitrary", "arbitrary")),
    )(dmix, dmix, h, h, wp, scale.reshape(1, C))


_FFN_HR = 16
_FFN_ROWS, _FFN_LANES = 32, 256


def _fold8(x):
    out = x[0:8]
    for r in range(8, x.shape[0], 8):
        out = out + x[r:r + 8]
    return out


def _silu_grad(x, sg):
    return sg * (1.0 + x * (1.0 - sg))


def _conv3(buf, w_ref, b_ref, off, rows, ln):
    c = b_ref[:, ln] + w_ref[0:1, ln] * buf[pl.ds(off, rows), ln]
    for k in (1, 2):
        c = c + w_ref[k:k + 1, ln] * buf[pl.ds(off + k, rows), ln]
    return c


def _ffn_chunks(nrows, ncols):
    lw = min(ncols, _FFN_LANES)
    return [(r, min(_FFN_ROWS, nrows - r), slice(l0, l0 + lw))
            for l0 in range(0, ncols, lw) for r in range(0, nrows, _FFN_ROWS)]


def gate_fwd(up, w, bias, *, B, S, name):
    T, F2 = up.shape
    F = F2 // 2
    hr = _FFN_HR
    tt = _tile(S, 512, hr)
    nT = S // tt
    tn = _tile(F, 512, 128)
    nC = F // tn
    k = tt // hr

    def body(a_ref, g_ref, ap_ref, gp_ref, wa_ref, wg_ref, ba_ref, bg_ref, o_ref, bufa, bufg):
        t = pl.program_id(2)
        for buf, m_ref, p_ref in ((bufa, a_ref, ap_ref), (bufg, g_ref, gp_ref)):
            buf[pl.ds(hr, tt), :] = m_ref[...].astype(_F32)
            buf[pl.ds(0, hr), :] = jnp.where(t == 0, 0.0, p_ref[...].astype(_F32))
        for r, rc, ln in _ffn_chunks(tt, tn):
            ca = _conv3(bufa, wa_ref, ba_ref, hr - 2 + r, rc, ln)
            cg = _conv3(bufg, wg_ref, bg_ref, hr - 2 + r, rc, ln)
            o_ref[pl.ds(r, rc), ln] = (ca * cg * jax.nn.sigmoid(cg)).astype(_BF)

    def main(off):
        return pl.BlockSpec((tt, tn), lambda b, j, t: (b * nT + t, j + off))

    def prev(off):
        return pl.BlockSpec((hr, tn), lambda b, j, t: (jnp.maximum((b * nT + t) * k - 1, 0), j + off))

    def wspec(rows, off):
        return pl.BlockSpec((rows, tn), lambda b, j, t: (0, j + off))

    return pl.pallas_call(
        body, name=name, grid=(B, nC, nT),
        in_specs=[main(0), main(nC), prev(0), prev(nC), wspec(3, 0), wspec(3, nC), wspec(1, 0), wspec(1, nC)],
        out_specs=pl.BlockSpec((tt, tn), lambda b, j, t: (b * nT + t, j)),
        out_shape=jax.ShapeDtypeStruct((T, F), _BF),
        scratch_shapes=[pltpu.VMEM((hr + tt, tn), _F32)] * 2,
        compiler_params=_cparams(("parallel", "parallel", "parallel")),
    )(up, up, up, up, w, w, bias.reshape(1, F2), bias.reshape(1, F2))


def gate_bwd(up, dact, w, bias, *, B, S, name):
    T, F2 = up.shape
    F = F2 // 2
    hr = _FFN_HR
    tt = _tile(S, 512, hr)
    nT = S // tt
    tn = _tile(F, 512, 128)
    nC = F // tn
    k = tt // hr
    ext = tt + hr

    def body(a_ref, g_ref, ap_ref, gp_ref, an_ref, gn_ref, d_ref, dn_ref, wa_ref, wg_ref, ba_ref, bg_ref,
             du_ref, dwa_ref, dwg_ref, dba_ref, dbg_ref, bufa, bufg, bufda, bufdg):
        b, t = pl.program_id(1), pl.program_id(2)
        last = t == nT - 1
        for buf, m_ref, p_ref, n_ref in ((bufa, a_ref, ap_ref, an_ref), (bufg, g_ref, gp_ref, gn_ref)):
            buf[pl.ds(hr, tt), :] = m_ref[...].astype(_F32)
            buf[pl.ds(0, hr), :] = jnp.where(t == 0, 0.0, p_ref[...].astype(_F32))
            buf[pl.ds(hr + tt, hr), :] = jnp.where(last, 0.0, n_ref[...].astype(_F32))
        for r, rc, ln in _ffn_chunks(ext, tn):
            ca = _conv3(bufa, wa_ref, ba_ref, hr - 2 + r, rc, ln)
            cg = _conv3(bufg, wg_ref, bg_ref, hr - 2 + r, rc, ln)
            sg = jax.nn.sigmoid(cg)
            if r < tt:
                da = d_ref[pl.ds(r, rc), ln].astype(_F32)
            else:
                da = jnp.where(last, 0.0, dn_ref[pl.ds(r - tt, rc), ln].astype(_F32))
            bufda[pl.ds(r, rc), ln] = da * cg * sg
            bufdg[pl.ds(r, rc), ln] = da * ca * _silu_grad(cg, sg)

        @pl.when((b == 0) & (t == 0))
        def _():
            for r in (dwa_ref, dwg_ref, dba_ref, dbg_ref):
                r[...] = jnp.zeros_like(r)

        lw, rc = min(tn, _FFN_LANES), _FFN_ROWS
        assert tt % rc == 0
        for half, bufd, buf, w_ref, dw_ref, db_ref in ((0, bufda, bufa, wa_ref, dwa_ref, dba_ref),
                                                       (1, bufdg, bufg, wg_ref, dwg_ref, dbg_ref)):
            for l0 in range(0, tn, lw):
                ln = slice(l0, l0 + lw)
                acc = [jnp.zeros((8, lw), _F32) for _ in range(4)]
                for r in range(0, tt, rc):
                    dc = [bufd[pl.ds(r + s_, rc), ln] for s_ in range(3)]
                    u = buf[pl.ds(hr + r, rc), ln]
                    du = w_ref[2:3, ln] * dc[0] + w_ref[1:2, ln] * dc[1] + w_ref[0:1, ln] * dc[2]
                    du_ref[half, pl.ds(r, rc), ln] = du.astype(_BF)
                    for kk in range(3):
                        acc[kk] = acc[kk] + _fold8(dc[2 - kk] * u)
                    acc[3] = acc[3] + _fold8(dc[0])
                for kk in range(3):
                    dw_ref[kk:kk + 1, ln] += jnp.sum(acc[kk], axis=0, keepdims=True)
                db_ref[:, ln] += jnp.sum(acc[3], axis=0, keepdims=True)

    def main(off):
        return pl.BlockSpec((tt, tn), lambda j, b, t: (b * nT + t, j + off))

    def prev(off):
        return pl.BlockSpec((hr, tn), lambda j, b, t: (jnp.maximum((b * nT + t) * k - 1, 0), j + off))

    def nxt(off):
        return pl.BlockSpec((hr, tn), lambda j, b, t: (jnp.minimum((b * nT + t + 1) * k, T // hr - 1), j + off))

    def wspec(rows, off):
        return pl.BlockSpec((rows, tn), lambda j, b, t: (0, j + off))

    both = pl.BlockSpec((2, tt, tn), lambda j, b, t: (0, b * nT + t, j))
    return pl.pallas_call(
        body, name=name, grid=(nC, B, nT),
        in_specs=[main(0), main(nC), prev(0), prev(nC), nxt(0), nxt(nC), main(0), nxt(0),
                  wspec(3, 0), wspec(3, nC), wspec(1, 0), wspec(1, nC)],
        out_specs=[both, wspec(3, 0), wspec(3, 0), wspec(1, 0), wspec(1, 0)],
        out_shape=[jax.ShapeDtypeStruct((2, T, F), _BF), jax.ShapeDtypeStruct((3, F), _F32), jax.ShapeDtypeStruct((3, F), _F32),
                   jax.ShapeDtypeStruct((1, F), _F32), jax.ShapeDtypeStruct((1, F), _F32)],
        scratch_shapes=[pltpu.VMEM((hr + ext, tn), _F32)] * 2 + [pltpu.VMEM((ext, tn), _F32)] * 2,
        compiler_params=_cparams(("parallel", "arbitrary", "arbitrary")),
    )(up, up, up, up, up, up, dact, dact, w, w, bias.reshape(1, F2), bias.reshape(1, F2))


def loss_head(y, target, *, name):
    T, C = y.shape
    tr = _row_tile(T, C)

    def body(y_ref, t_ref, dy_ref, acc_ref):
        @pl.when(pl.program_id(0) == 0)
        def _():
            acc_ref[...] = jnp.zeros_like(acc_ref)

        e = y_ref[...] - t_ref[...]
        dy_ref[...] = e * (1.0 / C)
        acc_ref[...] += jnp.sum(e * e, axis=0, keepdims=True) * (0.5 / C)

    row = pl.BlockSpec((tr, C), lambda i: (i, 0))
    return pl.pallas_call(
        body, name=name, grid=(T // tr,),
        in_specs=[row, row], out_specs=[row, pl.BlockSpec((1, C), lambda i: (0, 0))],
        out_shape=[jax.ShapeDtypeStruct((T, C), _F32), jax.ShapeDtypeStruct((1, C), _F32)],
        compiler_params=_cparams(("arbitrary",)),
    )(y, target)


def _adamw_update(w_ref, g_ref, m_ref, v_ref, d_ref, mo_ref, vo_ref):
    c1 = 1.0 - ADAM_B1 ** ADAM_STEP
    c2 = 1.0 - ADAM_B2 ** ADAM_STEP
    gg = g_ref[...]
    mn = ADAM_B1 * m_ref[...] + (1.0 - ADAM_B1) * gg
    vn = ADAM_B2 * v_ref[...] + (1.0 - ADAM_B2) * (gg * gg)
    d_ref[...] = -ADAM_LR * ((mn / c1) / (jnp.sqrt(vn / c2) + ADAM_EPS) + ADAM_WD * w_ref[...])
    mo_ref[...] = mn
    vo_ref[...] = vn


def adamw_small(ws, gs, ms, vs, *, name):
    n = len(ws)

    def body(*refs):
        for p in range(n):
            _adamw_update(*[refs[k * n + p] for k in range(7)])

    vm = pl.BlockSpec(memory_space=pltpu.VMEM)
    shapes = [jax.ShapeDtypeStruct(w.shape, _F32) for w in ws]
    res = pl.pallas_call(
        body, name=name, in_specs=[vm] * (4 * n), out_specs=[vm] * (3 * n), out_shape=shapes * 3,
        compiler_params=_cparams(),
    )(*ws, *gs, *ms, *vs)
    return res[:n], res[n:2 * n], res[2 * n:]


def adamw(w, g, m, v, *, side=None, name):
    R, C = w.shape
    tr = _tile(R, max(8, (256 * 1024) // C // 8 * 8), 8)
    s_ops, s_in, s_out, s_shapes, s_scratch, s_alias = _side_specs(side)
    ni, no = len(s_ops), len(s_shapes)

    def body(w_ref, g_ref, m_ref, v_ref, *rest):
        s_ins, outs, s_outs, sems = rest[:ni], rest[ni:ni + 3], rest[ni + 3:ni + 3 + no], rest[ni + 3 + no:]
        first, last = _grid_ends((R // tr,))
        if side is not None:
            @pl.when(first)
            def _():
                side.start(s_ins, s_outs, sems)

        _adamw_update(w_ref, g_ref, m_ref, v_ref, *outs)
        if side is not None:
            @pl.when(last)
            def _():
                side.wait(s_ins, s_outs, sems)

    blk = pl.BlockSpec((tr, C), lambda i: (i, 0))
    s = jax.ShapeDtypeStruct((R, C), _F32)
    res = pl.pallas_call(
        body, name=name, grid=(R // tr,),
        in_specs=[blk] * 4 + s_in, out_specs=[blk] * 3 + s_out, out_shape=[s, s, s] + s_shapes,
        input_output_aliases={4 + i_: 3 + o_ for i_, o_ in s_alias.items()},
        scratch_shapes=s_scratch,
        compiler_params=_cparams(("arbitrary",) if side is not None else ("parallel",)),
    )(w, g, m, v, *s_ops)
    return (res[0], res[1], res[2]) if side is None else (res[0], res[1], res[2], list(res[3:]))


_ANY = pl.BlockSpec(memory_space=pl.ANY)
_MESH = pl.DeviceIdType.MESH


def _place():
    return lax.axis_index("x"), lax.axis_index("y"), lax.axis_index("c")


def _other_chips(x, y):
    chips = [(1 - x, y), (x, 1 - y), (1 - x, 1 - y)]
    return chips, [2 * a + b for a, b in chips]


def _rcopy(src, dst, ssem, rsem, dev):
    return pltpu.make_async_remote_copy(src_ref=src, dst_ref=dst, send_sem=ssem, recv_sem=rsem,
                                        device_id=dev, device_id_type=_MESH)


def place_shard(w, l, chip_idx, *, name):
    _, _, hR, C = w.shape
    tr = _tile(hR, max(16, (512 * 1024) // C // 16 * 16), 16)

    def body(ci_ref, w_ref, o_ref):
        o_ref[...] = w_ref[...].astype(_BF)

    return pl.pallas_call(
        body, name=name,
        grid_spec=pltpu.PrefetchScalarGridSpec(
            num_scalar_prefetch=1, grid=(2, hR // tr),
            in_specs=[pl.BlockSpec((None, None, tr, C), lambda h, i, ci: (l, h, i, 0))],
            out_specs=pl.BlockSpec((None, None, tr, C), lambda h, i, ci: (ci[0], h, i, 0))),
        out_shape=jax.ShapeDtypeStruct((4, 2, hR, C), _BF),
        compiler_params=_cparams(("parallel", "parallel")),
    )(chip_idx, w)


class _Side:
    def __init__(self, arrays, out_shapes, aliases, n_sems, start, wait):
        self.arrays, self.out_shapes, self.aliases, self.n_sems = arrays, out_shapes, aliases, n_sems
        self.start, self.wait = start, wait


def gather_stage1_side(bufs):
    n = len(bufs)

    def copies(outs, sems, sending):
        x, y, c = _place()
        me = 2 * x + y
        chips, cidx = _other_chips(x, y)
        send, recv = sems
        out, back = [], []
        for k, chip in enumerate(chips):
            for p in range(n):
                mine, got = outs[p].at[me, c], outs[p].at[cidx[k], c]
                out.append(_rcopy(mine, mine, send.at[p * 3 + k], recv.at[p * 3 + k], (*chip, c)))
                if not sending:
                    back.append(_rcopy(got, got, send.at[p * 3 + k], recv.at[p * 3 + k], (*chip, c)))
        return out, back

    def start(ins, outs, sems):
        for cp in copies(outs, sems, True)[0]:
            cp.start()

    def wait(ins, outs, sems):
        out, back = copies(outs, sems, False)
        for cp in back:
            cp.wait_recv()
        for cp in out:
            cp.wait_send()

    return _Side(list(bufs), [jax.ShapeDtypeStruct(b.shape, b.dtype) for b in bufs], {p: p for p in range(n)},
                 [n * 3, n * 3], start, wait)


def gather_stage2_side(bufs):
    n = len(bufs)

    def copies(outs, sems, sending):
        x, y, c = _place()
        sib = (x, y, 1 - c)
        _, cidx = _other_chips(x, y)
        send, recv = sems
        out, back = [], []
        for k in range(3):
            for p in range(n):
                mine, got = outs[p].at[cidx[k], c], outs[p].at[cidx[k], 1 - c]
                out.append(_rcopy(mine, mine, send.at[p * 3 + k], recv.at[p * 3 + k], sib))
                if not sending:
                    back.append(_rcopy(got, got, send.at[p * 3 + k], recv.at[p * 3 + k], sib))
        return out, back

    def start(ins, outs, sems):
        for cp in copies(outs, sems, True)[0]:
            cp.start()

    def wait(ins, outs, sems):
        out, back = copies(outs, sems, False)
        for cp in back:
            cp.wait_recv()
        for cp in out:
            cp.wait_send()

    return _Side(list(bufs), [jax.ShapeDtypeStruct(b.shape, b.dtype) for b in bufs], {p: p for p in range(n)},
                 [n * 3, n * 3], start, wait)


def chip_exchange_side(ps):
    n = len(ps)

    def copies(ins, outs, sems, sending):
        x, y, c = _place()
        me = 2 * x + y
        chips, cidx = _other_chips(x, y)
        send, recv = sems
        out, back = [], []
        for k, chip in enumerate(chips):
            for p in range(n):
                got = outs[p].at[cidx[k]]
                out.append(_rcopy(ins[p].at[cidx[k]], outs[p].at[me], send.at[p * 3 + k], recv.at[p * 3 + k], (*chip, c)))
                if not sending:
                    back.append(_rcopy(got, got, send.at[p * 3 + k], recv.at[p * 3 + k], (*chip, c)))
        return out, back

    def start(ins, outs, sems):
        for cp in copies(ins, outs, sems, True)[0]:
            cp.start()

    def wait(ins, outs, sems):
        out, back = copies(ins, outs, sems, False)
        for cp in back:
            cp.wait_recv()
        for cp in out:
            cp.wait_send()

    return _Side(list(ps), [jax.ShapeDtypeStruct(a.shape, a.dtype) for a in ps], {}, [n * 3, n * 3], start, wait)


def _side_specs(side):
    if side is None:
        return [], [], [], [], [], {}
    return (side.arrays, [_ANY] * len(side.arrays), [_ANY] * len(side.out_shapes), side.out_shapes,
            [pltpu.SemaphoreType.DMA((k,)) for k in side.n_sems], side.aliases)


def run_side(side, *, name):
    ni, no = len(side.arrays), len(side.out_shapes)

    def body(*refs):
        ins, outs, sems = refs[:ni], refs[ni:ni + no], refs[ni + no:]
        side.start(ins, outs, sems)
        side.wait(ins, outs, sems)

    ops, in_specs, out_specs, out_shapes, scratch, aliases = _side_specs(side)
    return pl.pallas_call(body, name=name, in_specs=in_specs, out_specs=out_specs, out_shape=out_shapes,
                          input_output_aliases=aliases, scratch_shapes=scratch)(*ops)


def gather_small(small, *, name):
    ns = len(small)

    def body(*refs):
        s_in, s_out = refs[:ns], refs[ns:2 * ns]
        send, recv, lsem = refs[2 * ns:]
        x, y, c = _place()
        me = 2 * x + y
        chips, cidx = _other_chips(x, y)
        local = [pltpu.make_async_copy(s_in[q], s_out[q].at[me], lsem.at[q]) for q in range(ns)]
        cps = [_rcopy(s_in[q], s_out[q].at[me], send.at[q * 3 + k], recv.at[q * 3 + k], (*chip, c))
               for k, chip in enumerate(chips) for q in range(ns)]
        for cp in local + cps:
            cp.start()
        for k in range(3):
            for q in range(ns):
                got = s_out[q].at[cidx[k]]
                _rcopy(got, got, send.at[q * 3 + k], recv.at[q * 3 + k], (x, y, c)).wait_recv()
        for cp in cps:
            cp.wait_send()
        for cp in local:
            cp.wait()

    return pl.pallas_call(
        body, name=name, in_specs=[_ANY] * ns, out_specs=[_ANY] * ns,
        out_shape=[jax.ShapeDtypeStruct((4,) + a.shape, a.dtype) for a in small],
        scratch_shapes=[pltpu.SemaphoreType.DMA((ns * 3,))] * 2 + [pltpu.SemaphoreType.DMA((ns,))],
    )(*small)


def sibling_send_half(gs, *, name):
    n = len(gs)

    def body(*refs):
        g_in, g_out, send, recv = refs[:n], refs[n:2 * n], refs[2 * n], refs[2 * n + 1]
        x, y, c = _place()
        sib = (x, y, 1 - c)
        cps = [_rcopy(g_in[p].at[1 - c], g_out[p], send.at[p], recv.at[p], sib) for p in range(n)]
        for cp in cps:
            cp.start()
        for cp in cps:
            cp.wait()

    return pl.pallas_call(
        body, name=name, in_specs=[_ANY] * n, out_specs=[_ANY] * n,
        out_shape=[jax.ShapeDtypeStruct(a.shape[1:], a.dtype) for a in gs],
        scratch_shapes=[pltpu.SemaphoreType.DMA((n,))] * 2,
    )(*gs)


def sum_chips(p, slots, idx, *, name):
    _, N, C = p.shape
    tr = _tile(N, max(16, (512 * 1024) // C // 16 * 16), 16)

    def body(i0, i1, i2, i3, i4, p_ref, s0_ref, s1_ref, s2_ref, o_ref):
        s = p_ref[...].astype(_F32)
        for r in (s0_ref, s1_ref, s2_ref):
            s = s + r[...].astype(_F32)
        o_ref[...] = s

    def at(k):
        return pl.BlockSpec((None, tr, C), lambda i, *ix: (ix[k][0], i, 0))

    return pl.pallas_call(
        body, name=name,
        grid_spec=pltpu.PrefetchScalarGridSpec(
            num_scalar_prefetch=5, grid=(N // tr,),
            in_specs=[at(0), at(1), at(2), at(3)], out_specs=at(4)),
        out_shape=jax.ShapeDtypeStruct((2, N, C), _F32),
        compiler_params=_cparams(("parallel",)),
    )(*idx, p, slots, slots, slots)


def sibling_join(rs, *, name):
    n = len(rs)

    def body(*refs):
        r_out, send, recv = refs[n:2 * n], refs[2 * n], refs[2 * n + 1]
        x, y, c = _place()
        sib = (x, y, 1 - c)
        cps = [_rcopy(r_out[p].at[c], r_out[p].at[c], send.at[p], recv.at[p], sib) for p in range(n)]
        for cp in cps:
            cp.start()
        for p in range(n):
            got = r_out[p].at[1 - c]
            _rcopy(got, got, send.at[p], recv.at[p], sib).wait_recv()
        for cp in cps:
            cp.wait_send()

    return pl.pallas_call(
        body, name=name, in_specs=[_ANY] * n, out_specs=[_ANY] * n,
        out_shape=[jax.ShapeDtypeStruct(a.shape, a.dtype) for a in rs],
        input_output_aliases={p: p for p in range(n)},
        scratch_shapes=[pltpu.SemaphoreType.DMA((n,))] * 2,
    )(*rs)


def all_devices_exchange_side(v):
    def copies(ins, outs, sems):
        x, y, c = _place()
        me = 4 * x + 2 * y + c
        send, recv, lsem = sems
        peers = [(x ^ (k >> 2), y ^ ((k >> 1) & 1), c ^ (k & 1)) for k in range(1, 8)]
        local = pltpu.make_async_copy(ins[0], outs[0].at[me], lsem.at[0])
        out = [_rcopy(ins[0], outs[0].at[me], send.at[k], recv.at[k], peer) for k, peer in enumerate(peers)]
        return local, out, peers

    def start(ins, outs, sems):
        local, out, _ = copies(ins, outs, sems)
        local.start()
        for cp in out:
            cp.start()

    def wait(ins, outs, sems):
        local, out, peers = copies(ins, outs, sems)
        for k, (px, py, pc) in enumerate(peers):
            got = outs[0].at[4 * px + 2 * py + pc]
            _rcopy(got, got, sems[0].at[k], sems[1].at[k], peers[k]).wait_recv()
        for cp in out:
            cp.wait_send()
        local.wait()

    return _Side([v], [jax.ShapeDtypeStruct((8,) + v.shape, v.dtype)], {}, [7, 7, 1], start, wait)


def add_halves(gs_and_rs, c_idx, *, name):
    outs = []
    for n_, (g, r) in enumerate(gs_and_rs):
        N, C = r.shape
        tr = _tile(N, max(16, (512 * 1024) // C // 16 * 16), 16)

        def body(c_ref, g_ref, r_ref, o_ref):
            o_ref[...] = (g_ref[...].astype(_F32) + r_ref[...].astype(_F32)).astype(o_ref.dtype)

        outs.append(pl.pallas_call(
            body, name=f"{name}_{n_}",
            grid_spec=pltpu.PrefetchScalarGridSpec(
                num_scalar_prefetch=1, grid=(N // tr,),
                in_specs=[pl.BlockSpec((None, tr, C), lambda i, c: (c[0], i, 0)), pl.BlockSpec((tr, C), lambda i, c: (i, 0))],
                out_specs=pl.BlockSpec((tr, C), lambda i, c: (i, 0))),
            out_shape=jax.ShapeDtypeStruct((N, C), r.dtype),
            compiler_params=_cparams(("parallel",)),
        )(c_idx, g, r))
    return outs


def sum_slots(a, *, name):
    n, N, C = a.shape
    tr = _tile(N, max(16, (512 * 1024) // C // 16 * 16), 16)

    def body(a_ref, o_ref):
        s = a_ref[0].astype(_F32)
        for k in range(1, n):
            s = s + a_ref[k].astype(_F32)
        o_ref[...] = s

    return pl.pallas_call(
        body, name=name, grid=(N // tr,),
        in_specs=[pl.BlockSpec((n, tr, C), lambda i: (0, i, 0))],
        out_specs=pl.BlockSpec((tr, C), lambda i: (i, 0)),
        out_shape=jax.ShapeDtypeStruct((N, C), _F32),
        compiler_params=_cparams(("parallel",)),
    )(a)


_WEIGHTS = ['ln_in_g', 'ln_in_b', 'w_in', 'q_norm_g', 'w_uq', 'kv_norm_g', 'w_ukv', 'conv_w', 'conv_b', 'conv_ln_g',
            'conv_ln_b', 'w_pool', 'pool_scale', 'w_out', 'ln1_g', 'ln1_b', 'w_up', 'ffn_conv_w', 'ffn_conv_b', 'w_down',
            'ln2_g', 'ln2_b']
_BIG = ['w_in', 'w_uq', 'w_ukv', 'w_out', 'w_up', 'w_down']
_GRADS_EARLY = ('w_up', 'w_down')
_GRADS_LATE = tuple(n for n in _BIG if n not in _GRADS_EARLY)
_SMALL_SHARDED = ['conv_w', 'ffn_conv_w']
_SMALL = [n for n in _WEIGHTS if n not in _BIG]


def _rope_tables(positions):
    half = QK_ROPE // 2
    inv = 1.0 / (ROPE_THETA ** (jnp.arange(0, QK_ROPE, 2, dtype=_F32) / QK_ROPE))
    ang = positions.reshape(-1).astype(_F32)[:, None] * inv
    c, s = jnp.cos(ang), jnp.sin(ang)
    z = jnp.zeros_like(c)
    cc = jnp.concatenate([c, c, z, z], axis=1)
    sa = jnp.concatenate([-s, z, z, z], axis=1)
    sb = jnp.concatenate([z, s, z, z], axis=1)
    assert cc.shape[1] == 128 and half == 32
    return cc, sa, sb


def _in_pad(dims):
    D, QL, KVL, CW, PW, H, F = dims
    return (-(QL + 2 * CW + PW + KVL + 128)) % 512


def _layer_weights(full, dims):
    D, QL, KVL, CW, PW, H, F = dims
    w_in = full['w_in'].transpose(1, 0, 2).reshape(D, -1)
    o1, o2, o3, o4 = QL, QL + KVL, QL + KVL + QK_ROPE, QL + KVL + QK_ROPE + 2 * CW
    w_in_p = jnp.concatenate([w_in[:, :o1], w_in[:, o3:o4], w_in[:, o4:], w_in[:, o1:o2], w_in[:, o2:o3],
                              jnp.zeros((D, 128 - QK_ROPE + _in_pad(dims)), w_in.dtype)], axis=1)
    w_uq = full['w_uq'].reshape(QL, H, QK_NOPE + QK_ROPE)
    w_uq_p = jnp.pad(w_uq, ((0, 0), (0, 0), (0, HEAD_PAD - QK_NOPE - QK_ROPE))).reshape(QL, H * HEAD_PAD)
    return dict(
        w_in=w_in_p, w_uq=w_uq_p,
        w_ukv=full['w_ukv'].reshape(KVL, H * (QK_NOPE + V_HEAD)),
        w_out=full['w_out'].reshape(D, D),
        w_up=full['w_up'].transpose(1, 0, 2).reshape(D, 2 * F),
        w_down=full['w_down'].reshape(F, D),
    )


def _unpermute_w_in_grad(g, dims):
    D, QL, KVL, CW, PW, H, F = dims
    a, b_, c_ = QL, QL + 2 * CW, QL + 2 * CW + PW
    return jnp.concatenate([g[:, :a], g[:, c_:c_ + KVL], g[:, c_ + KVL:c_ + KVL + QK_ROPE], g[:, a:b_], g[:, b_:c_]], axis=1)


class _NoExchange:
    def __init__(self, layer_weights):
        self.layer_weights, self.grads = layer_weights, {}

    def weights(self, l):
        return self.layer_weights[l]

    def side(self, where, l):
        return None

    def side_done(self, where, l, outs):
        pass

    def grads_ready(self, l, g):
        self.grads.setdefault(l, {}).update(g)


def _with_side(hooks, where, l, fn):
    sd = hooks.side(where, l)
    res = fn(sd)
    if sd is None:
        return res
    hooks.side_done(where, l, res[-1])
    return res[0] if len(res) == 2 else res[:-1]


def _local_step(x, positions, target, small, dims, B, S, L, hooks):
    D, QL, KVL, CW, PW, H, F = dims
    T = B * S
    alpha = (2.0 * L) ** 0.25
    scale = float(QK_NOPE + QK_ROPE) ** -0.5
    cc, sa, sb = _rope_tables(positions)
    cb_q, cb_a, cb_g, cb_p = 0, QL // CW, QL // CW + 1, (QL + 2 * CW) // PW
    cb_kv, cb_kr = (QL + 2 * CW + PW) // KVL, (QL + 2 * CW + PW + KVL) // 128
    assert QL % CW == 0 and (QL + 2 * CW) % PW == 0 and (QL + 2 * CW + PW) % KVL == 0 and (QL + 2 * CW + PW + KVL) % 128 == 0

    xs, xb = ln_fwd([x], [1.0], small['ln_in_g'], small['ln_in_b'], want_r=False, name="ln_in")
    saved = []
    fa = dict(B=B, S=S, H=H, scale=scale)
    for l in range(L):
        W = hooks.weights(l)
        h = matmul(xb, W['w_in'], name="mm_in")
        qn = rms_fwd(h, cb_q, QL, small['q_norm_g'][l], name="rms_q")
        kvn = rms_fwd(h, cb_kv, KVL, small['kv_norm_g'][l], name="rms_kv")
        q = matmul(qn, W['w_uq'], name="mm_uq")
        kv = matmul(kvn, W['w_ukv'], name="mm_ukv")
        qp, kp, v = mla_pack(q, kv, h, cb_kr, cc, sa, sb, H=H, scale=scale, name="mla_pack")
        o, lse = _with_side(hooks, 'flash', l, lambda sd: flash_fwd(qp, kp, v, side=sd, name="flash_fwd", **fa))
        z, yc = conv_fwd(h, cb_a, cb_g, small['conv_w'][l], small['conv_b'][l], small['conv_ln_g'][l],
                         small['conv_ln_b'][l], B=B, S=S, name="conv_fwd")
        yp = pool_fwd(h, cb_p, small['w_pool'][l], small['pool_scale'][l], B=B, S=S, name="pool_fwd")
        mixed = jnp.concatenate([o.astype(_BF), yc, yp], axis=1)
        y1 = matmul(mixed, W['w_out'], name="mm_out")
        r1, x1, x1b = ln_fwd([xs, y1], [alpha, 1.0], small['ln1_g'][l], small['ln1_b'][l], want_r=True, name="ln1")
        up = _with_side(hooks, 'up', l, lambda sd: matmul(x1b, W['w_up'], out_dtype=_BF, side=sd, name="mm_up"))
        act = gate_fwd(up, small['ffn_conv_w'][l], small['ffn_conv_b'][l], B=B, S=S, name="gate_fwd")
        y2 = _with_side(hooks, 'down', l, lambda sd: matmul(act, W['w_down'], side=sd, name="mm_down"))
        r2, x2, x2b = ln_fwd([x1, y2], [alpha, 1.0], small['ln2_g'][l], small['ln2_b'][l], want_r=True, name="ln2")
        saved.append(dict(W=W, xb=xb, h=h, qn=qn, kvn=kvn, qp=qp, kp=kp, v=v, o=o, lse=lse, z=z, mixed=mixed, r1=r1,
                          x1b=x1b, up=up, act=act, r2=r2))
        xs, xb = x2, x2b

    dy, loss_cols = loss_head(xs, target, name="loss_head")
    gs = {n: [None] * L for n in _SMALL if n not in ('ln_in_g', 'ln_in_b')}
    d_terms, d_coefs = [dy], [1.0]
    zpad = jnp.zeros((T, _in_pad(dims)), _BF) if _in_pad(dims) else None
    for l in reversed(range(L)):
        sv = saved[l]
        W = sv['W']
        gb = {}
        dr2, dr2b, gs['ln2_g'][l], gs['ln2_b'][l] = ln_bwd(d_terms, d_coefs, sv['r2'], small['ln2_g'][l], name="ln2_bwd")
        dact = matmul(dr2b, W['w_down'], tb=True, out_dtype=_BF, name="mm_down_dx")
        gb['w_down'] = matmul(sv['act'], dr2b, ta=True, out_dtype=_BF, name="mm_down_dw")
        dup, dwa, dwg, dba, dbg = gate_bwd(sv['up'], dact, small['ffn_conv_w'][l], small['ffn_conv_b'][l],
                                           B=B, S=S, name="gate_bwd")
        gs['ffn_conv_w'][l] = jnp.concatenate([dwa, dwg], axis=1)
        gs['ffn_conv_b'][l] = jnp.concatenate([dba, dbg], axis=1)
        gb['w_up'] = _with_side(hooks, 'up_dw', l, lambda sd: matmul(sv['x1b'], dup, ta=True, b_halves=True, out_dtype=_BF,
                                                                     side=sd, name="mm_up_dw"))
        dx1 = _with_side(hooks, 'up_dx', l, lambda sd: matmul(dup, W['w_up'], tb=True, a_halves=True, side=sd,
                                                              name="mm_up_dx"))
        hooks.grads_ready(l, {n: gb.pop(n) for n in _GRADS_EARLY})
        dr1, dr1b, gs['ln1_g'][l], gs['ln1_b'][l] = ln_bwd([dr2, dx1], [alpha, 1.0], sv['r1'], small['ln1_g'][l],
                                                            name="ln1_bwd")
        dmix = matmul(dr1b, W['w_out'], tb=True, name="mm_out_dx")
        gb['w_out'] = matmul(sv['mixed'], dr1b, ta=True, out_dtype=_BF, name="mm_out_dw")
        h = sv['h']
        ncb = (H * V_HEAD) // CW
        dca, dcg, gs['conv_w'][l], gs['conv_b'][l], gs['conv_ln_g'][l], gs['conv_ln_b'][l] = conv_bwd(
            dmix, ncb, sv['z'], h, cb_a, cb_g, small['conv_w'][l], small['conv_ln_g'][l], small['conv_ln_b'][l],
            B=B, S=S, name="conv_bwd")
        dpool, gs['w_pool'][l], gs['pool_scale'][l] = pool_bwd(
            dmix, (H * V_HEAD + CW) // PW, h, cb_p, small['w_pool'][l], small['pool_scale'][l], B=B, S=S, name="pool_bwd")
        att = (sv['qp'], sv['kp'], sv['v'], sv['o'], sv['lse'], dmix, 0)
        dqp = _with_side(hooks, 'flash_dq', l, lambda sd: flash_bwd_dq(*att, side=sd, name="flash_dq", **fa))
        dkp, dv = _with_side(hooks, 'flash_dkv', l, lambda sd: flash_bwd_dkv(*att, side=sd, name="flash_dkv", **fa))
        dq, dkv, dkr = mla_unpack(dqp, dkp, dv, cc, sa, sb, H=H, name="mla_unpack")
        dqn = matmul(dq, W['w_uq'], tb=True, name="mm_uq_dx")
        g_uq = matmul(sv['qn'], dq, ta=True, out_dtype=_BF, name="mm_uq_dw")
        gb['w_uq'] = g_uq.reshape(QL, H, HEAD_PAD)[:, :, :QK_NOPE + QK_ROPE].reshape(QL, -1)
        dkvn = matmul(dkv, W['w_ukv'], tb=True, name="mm_ukv_dx")
        gb['w_ukv'] = matmul(sv['kvn'], dkv, ta=True, out_dtype=_BF, name="mm_ukv_dw")
        dcq, gs['q_norm_g'][l] = rms_bwd(dqn, h, cb_q, QL, small['q_norm_g'][l], name="rms_q_bwd")
        dckv, gs['kv_norm_g'][l] = rms_bwd(dkvn, h, cb_kv, KVL, small['kv_norm_g'][l], name="rms_kv_bwd")
        dh = jnp.concatenate([dcq, dca, dcg, dpool, dckv, dkr] + ([zpad] if zpad is not None else []), axis=1)
        gb['w_in'] = _unpermute_w_in_grad(matmul(sv['xb'], dh, ta=True, out_dtype=_BF, name="mm_in_dw"), dims)
        dxm = matmul(dh, W['w_in'], tb=True, name="mm_in_dx")
        hooks.grads_ready(l, gb)
        d_terms, d_coefs = [dr1, dxm], [alpha, 1.0]
    gx, _, g_ln_g, g_ln_b = ln_bwd(d_terms, d_coefs, x, small['ln_in_g'], name="ln_in_bwd")
    gsm = {n: jnp.stack([a.reshape(small[n].shape[1:]) for a in gs[n]]) for n in gs}
    gsm['ln_in_g'], gsm['ln_in_b'] = g_ln_g.reshape(-1), g_ln_b.reshape(-1)
    return loss_cols, gx, gsm


_COL_SHARDED = ('w_in', 'w_up')


def _flat_pad(arrs, mult=512 * 128):
    v = jnp.concatenate([a.reshape(-1) for a in arrs])
    n = v.shape[0]
    return jnp.pad(v, (0, (-n) % mult)).reshape(-1, 128)


def _split_like(flat, like):
    out, off = [], 0
    v = flat.reshape(-1)
    for a in like:
        out.append(v[off:off + a.size].reshape(a.shape))
        off += a.size
    return out


_GROUP_A = ('w_up',)
_GROUP_B = tuple(n for n in _BIG if n not in _GROUP_A)


class _Exchange:
    def __init__(self, a, dims, L, chip_idx, c_idx, sum_idx):
        self.dims, self.L, self.c_idx, self.sum_idx = dims, L, c_idx, sum_idx
        self.rows = {n: (a[n].shape[1], math.prod(a[n].shape[2:])) for n in _BIG}
        self.bufs = []
        for l in range(L):
            self.bufs.append({n: place_shard(a[n].reshape(L, 2, self.rows[n][0] // 2, self.rows[n][1]), l, chip_idx,
                                             name="place_" + n) for n in _BIG})
        self._store(0, _BIG, run_side(gather_stage1_side(self._list(0, _BIG)), name="gather_first_ici"))
        self._store(0, _BIG, run_side(gather_stage2_side(self._list(0, _BIG)), name="gather_first_d2d"))
        self.pending = {}
        self.reduced = {}

    def _list(self, l, names):
        return [self.bufs[l][n] for n in names]

    def _store(self, l, names, outs):
        self.bufs[l].update(zip(names, outs))

    def weights(self, l):
        return _layer_weights({n: self.bufs[l][n].reshape(4, *self.rows[n]) for n in _BIG}, self.dims)

    def side(self, where, l):
        if where in ('flash', 'up', 'down'):
            if l + 1 >= self.L:
                return None
            if where == 'down':
                return gather_stage2_side(self._list(l + 1, _BIG))
            return gather_stage1_side(self._list(l + 1, _GROUP_B if where == 'flash' else _GROUP_A))
        tag, names = self._riders(where)
        if tag not in self.pending:
            return None
        return chip_exchange_side([self.pending[tag][1][n] for n in names])

    @staticmethod
    def _riders(where):
        return {'flash_dq': ('early', ('w_down',)), 'flash_dkv': ('early', ('w_up',)), 'up_dw': ('late', _GRADS_LATE),
                'up_dx': ('none', ())}[where]

    def side_done(self, where, l, outs):
        if where in ('flash', 'up', 'down'):
            self._store(l + 1, {'flash': _GROUP_B, 'up': _GROUP_A, 'down': _BIG}[where], outs)
            return
        tag, names = self._riders(where)
        self.pending[tag][2].update(zip(names, outs))
        if len(self.pending[tag][2]) == len(self.pending[tag][1]):
            self._finish(tag)

    def _finish(self, tag):
        l, sums, slots = self.pending.pop(tag)
        halves = [sum_chips(sums[n], slots[n], self.sum_idx, name="grad_sum_" + n) for n in sums]
        joined = sibling_join(halves, name="grad_sibling_join_" + tag)
        self.reduced.setdefault(l, {}).update({n: j.reshape(self.rows[n]) for n, j in zip(sums, joined)})

    def grads_ready(self, l, g):
        names = tuple(n for n in _BIG if n in g)
        tag = 'early' if names == tuple(n for n in _BIG if n in _GRADS_EARLY) else 'late'
        g_in = []
        for n in names:
            r, c = self.rows[n]
            st = g[n].reshape(g[n].shape[0], 4, c).transpose(1, 0, 2) if n in _COL_SHARDED else g[n].reshape(4, r, c)
            g_in.append(st.reshape(4, 2, r // 2, c).transpose(1, 0, 2, 3).reshape(2, 2 * r, c))
        from_sib = sibling_send_half(g_in, name="grad_sibling_send_" + tag)
        sums = add_halves(list(zip(g_in, from_sib)), self.c_idx, name="grad_presum_" + tag)
        self.pending[tag] = (l, {n: p.reshape(4, p.shape[0] // 4, p.shape[1]) for n, p in zip(names, sums)}, {})

    def last_side(self):
        return chip_exchange_side(list(self.pending['late'][1].values()))

    def last_done(self, outs):
        self.pending['late'][2].update(zip(self.pending['late'][1], outs))
        self._finish('late')


def kernel(x, positions, ln_in_g, ln_in_b, w_in, q_norm_g, w_uq, kv_norm_g, w_ukv, conv_w, conv_b, conv_ln_g, conv_ln_b, w_pool, pool_scale, w_out, ln1_g, ln1_b, w_up, ffn_conv_w, ffn_conv_b, w_down, ln2_g, ln2_b, loss_target, m_ln_in_g, m_ln_in_b, m_w_in, m_q_norm_g, m_w_uq, m_kv_norm_g, m_w_ukv, m_conv_w, m_conv_b, m_conv_ln_g, m_conv_ln_b, m_w_pool, m_pool_scale, m_w_out, m_ln1_g, m_ln1_b, m_w_up, m_ffn_conv_w, m_ffn_conv_b, m_w_down, m_ln2_g, m_ln2_b, v_ln_in_g, v_ln_in_b, v_w_in, v_q_norm_g, v_w_uq, v_kv_norm_g, v_w_ukv, v_conv_w, v_conv_b, v_conv_ln_g, v_conv_ln_b, v_w_pool, v_pool_scale, v_w_out, v_ln1_g, v_ln1_b, v_w_up, v_ffn_conv_w, v_ffn_conv_b, v_w_down, v_ln2_g, v_ln2_b):
    a = dict(locals())
    B, S, D = a['x'].shape
    T = B * S
    L = a['w_in'].shape[0]
    QL, H = 4 * a['w_uq'].shape[1], a['w_uq'].shape[2]
    KVL = 4 * a['w_ukv'].shape[1]
    CW, PW = a['conv_b'].shape[1], a['pool_scale'].shape[1]
    F = 4 * a['w_down'].shape[1]
    dims = (D, QL, KVL, CW, PW, H, F)
    chip = 2 * lax.axis_index("x") + lax.axis_index("y")
    c_idx = lax.axis_index("c").astype(jnp.int32).reshape(1)

    def shard2d(w):
        return w.reshape(-1, w.shape[-1]) if w.ndim == 3 else w.reshape(w.shape[0] * w.shape[1], -1)

    small = {n: a[n] for n in _SMALL}
    for n, o in zip(_SMALL_SHARDED, gather_small([shard2d(a[n]) for n in _SMALL_SHARDED], name="gather_small")):
        k = a[n].shape[1]
        small[n] = o.reshape(4, L, k, -1).transpose(1, 2, 0, 3).reshape(L, k, -1)
    xi, yi = lax.axis_index("x"), lax.axis_index("y")
    chip_idx = chip.astype(jnp.int32).reshape(1)
    sum_idx = [v.astype(jnp.int32).reshape(1) for v in [chip] + _other_chips(xi, yi)[1] + [lax.axis_index("c")]]
    ex = _Exchange(a, dims, L, chip_idx, c_idx, sum_idx)

    loss_cols, gx, gsm = _local_step(a['x'].reshape(T, D), a['positions'], a['loss_target'].reshape(T, D),
                                     small, dims, B, S, L, ex)
    loss = lax.psum(jnp.sum(loss_cols), ("x", "y", "c"))

    grads, delta, new_m, new_v = {}, {}, {}, {}
    sm_like = [gsm[n] for n in _SMALL]
    riders = {'w_up': all_devices_exchange_side(_flat_pad(sm_like)), 'w_down': ex.last_side()}
    for n in _GRADS_EARLY + _GRADS_LATE:
        grads[n] = jnp.concatenate([ex.reduced[l][n] for l in range(L)]).reshape(a[n].shape)
        res = adamw(shard2d(a[n]), shard2d(grads[n]), shard2d(a['m_' + n]), shard2d(a['v_' + n]), side=riders.get(n),
                    name="adamw_" + n)
        delta[n], new_m[n], new_v[n] = (t.reshape(a[n].shape) for t in res[:3])
        if n == 'w_up':
            sm_slots = res[3][0]
        elif n == 'w_down':
            ex.last_done(res[3])

    g_small = dict(zip(_SMALL, _split_like(sum_slots(sm_slots, name="small_sum"), sm_like)))
    for n in _SMALL_SHARDED:
        w = a[n].shape[-1]
        g_small[n] = lax.dynamic_slice_in_dim(g_small[n], chip * w, w, axis=2)

    def at_least_2d(t):
        return t.reshape(1, -1) if t.ndim == 1 else t

    d_, m_, v_ = adamw_small(*[[at_least_2d(src[n]) for n in _SMALL] for src in (
        a, g_small, {n: a['m_' + n] for n in _SMALL}, {n: a['v_' + n] for n in _SMALL})], name="adamw_small")
    for n, dd, mm, vv in zip(_SMALL, d_, m_, v_):
        grads[n], delta[n], new_m[n], new_v[n] = g_small[n], dd.reshape(a[n].shape), mm.reshape(a[n].shape), vv.reshape(a[n].shape)

    return (loss, gx.reshape(B, S, D), *[grads[n] for n in _WEIGHTS], *[delta[n] for n in _WEIGHTS],
            *[new_m[n] for n in _WEIGHTS], *[new_v[n] for n in _WEIGHTS])
```

```python
import functools
import math

import jax
import jax.numpy as jnp
from jax import lax
from jax.experimental import pallas as pl
from jax.experimental.pallas import tpu as pltpu

_BF = jnp.bfloat16
_F32 = jnp.float32
_VMEM_LIMIT = 56 * 1024 * 1024

QK_NOPE = 128
QK_ROPE = 64
V_HEAD = 128
HEAD_PAD = 256
ROPE_THETA = 10000.0
LN_EPS = 1e-5
RMS_EPS = 1e-6
POOL_WINDOWS = (2, 4, 8, 16)
ADAM_LR, ADAM_B1, ADAM_B2, ADAM_EPS, ADAM_WD, ADAM_STEP = 0.001, 0.9, 0.999, 1e-8, 0.01, 10


def _cparams(sem=None):
    kw = dict(vmem_limit_bytes=_VMEM_LIMIT)
    if sem is not None:
        kw["dimension_semantics"] = sem
    return pltpu.CompilerParams(**kw)


def _tile(n, target, unit=128):
    if n <= target:
        return n
    t = (target // unit) * unit
    while t >= unit:
        if n % t == 0:
            return t
        t -= unit
    return n


_MM_VMEM_BUDGET = 40 * 1024 * 1024


def matmul(a, b, *, ta=False, tb=False, out_dtype=_F32, tm=1024, tn=1536, tk=4096, side=None, a_halves=False,
           b_halves=False, name="mm"):
    assert not (a_halves and ta) and not (b_halves and tb)
    if a_halves:
        M, K = a.shape[1], 2 * a.shape[2]
    elif ta:
        K, M = a.shape
    else:
        M, K = a.shape
    if b_halves:
        K2, N = b.shape[1], 2 * b.shape[2]
    elif tb:
        N, K2 = b.shape
    else:
        K2, N = b.shape
    assert K == K2, (a.shape, b.shape, ta, tb)
    tm, tn, tk = _tile(M, tm), _tile(N // 2 if b_halves else N, tn), _tile(K // 2 if a_halves else K, tk)
    ab, bb, ob = a.dtype.itemsize, b.dtype.itemsize, jnp.dtype(out_dtype).itemsize

    def vmem(tk_):
        return 2 * (tm * tk_ * ab + tk_ * tn * bb) + 2 * tm * tn * ob + tm * tn * 4 * (2 if K // tk_ > 1 else 1)

    k_part = K // 2 if a_halves else K
    while vmem(tk) > _MM_VMEM_BUDGET and tk > 256 and _tile(k_part, tk // 2) < tk:
        tk = _tile(k_part, tk // 2)
    nk = K // tk
    dn = (((0,) if ta else (1,), (1,) if tb else (0,)), ((), ()))

    s_ops, s_in, s_out, s_shapes, s_scratch, s_alias = _side_specs(side)
    ni, no, nacc = len(s_ops), len(s_shapes), int(nk > 1)
    grid = (M // tm, N // tn, nk)

    def body(a_ref, b_ref, *rest):
        s_ins, o_ref, s_outs = rest[:ni], rest[ni], rest[ni + 1:ni + 1 + no]
        acc, sems = rest[ni + 1 + no:ni + 1 + no + nacc], rest[ni + 1 + no + nacc:]
        i, j, k = pl.program_id(0), pl.program_id(1), pl.program_id(2)
        if side is not None:
            @pl.when((i == 0) & (j == 0) & (k == 0))
            def _():
                side.start(s_ins, s_outs, sems)

        prod = lax.dot_general(a_ref[...].astype(_BF), b_ref[...].astype(_BF), dn, preferred_element_type=_F32)
        if nk == 1:
            o_ref[...] = prod.astype(o_ref.dtype)
        else:
            acc_ref = acc[0]

            @pl.when(k == 0)
            def _():
                acc_ref[...] = prod

            @pl.when(k > 0)
            def _():
                acc_ref[...] += prod

            @pl.when(k == nk - 1)
            def _():
                o_ref[...] = acc_ref[...].astype(o_ref.dtype)

        if side is not None:
            @pl.when((i == grid[0] - 1) & (j == grid[1] - 1) & (k == nk - 1))
            def _():
                side.wait(s_ins, s_outs, sems)

    a_spec = pl.BlockSpec((tk, tm), lambda i, j, k: (k, i)) if ta else pl.BlockSpec((tm, tk), lambda i, j, k: (i, k))
    b_spec = pl.BlockSpec((tn, tk), lambda i, j, k: (j, k)) if tb else pl.BlockSpec((tk, tn), lambda i, j, k: (k, j))
    if a_halves:
        kh = nk // 2
        a_spec = pl.BlockSpec((None, tm, tk), lambda i, j, k: (k // kh, i, k % kh))
    if b_halves:
        jh = N // tn // 2
        b_spec = pl.BlockSpec((None, tk, tn), lambda i, j, k: (j // jh, k, j % jh))
    res = pl.pallas_call(
        body, name=name,
        grid=grid,
        in_specs=[a_spec, b_spec] + s_in,
        out_specs=[pl.BlockSpec((tm, tn), lambda i, j, k: (i, j))] + s_out,
        out_shape=[jax.ShapeDtypeStruct((M, N), out_dtype)] + s_shapes,
        input_output_aliases={2 + i_: 1 + o_ for i_, o_ in s_alias.items()},
        scratch_shapes=([pltpu.VMEM((tm, tn), _F32)] if nk > 1 else []) + s_scratch,
        compiler_params=_cparams(("arbitrary",) * 3 if side is not None else ("parallel", "parallel", "arbitrary")),
    )(a, b, *s_ops)
    return res[0] if side is None else (res[0], list(res[1:]))


def _row_tile(T, C, budget_rows=256):
    return _tile(T, budget_rows, 16)


def ln_fwd(xs, coefs, g, b, *, want_r, name):
    T, C = xs[0].shape
    tr = _row_tile(T, C)
    n = len(xs)

    def body(*refs):
        x_refs, (g_ref, b_ref), outs = refs[:n], refs[n:n + 2], refs[n + 2:]
        r = coefs[0] * x_refs[0][...]
        for c, xr in zip(coefs[1:], x_refs[1:]):
            r = r + c * xr[...]
        mu = jnp.mean(r, axis=-1, keepdims=True)
        d = r - mu
        var = jnp.mean(d * d, axis=-1, keepdims=True)
        y = d * lax.rsqrt(var + LN_EPS) * g_ref[...] + b_ref[...]
        if want_r:
            outs[0][...] = r
        outs[-2][...] = y
        outs[-1][...] = y.astype(_BF)

    row = pl.BlockSpec((tr, C), lambda i: (i, 0))
    vec = pl.BlockSpec((1, C), lambda i: (0, 0))
    f = jax.ShapeDtypeStruct((T, C), _F32)
    out_shape = ([f] if want_r else []) + [f, jax.ShapeDtypeStruct((T, C), _BF)]
    return pl.pallas_call(
        body, name=name, grid=(T // tr,),
        in_specs=[row] * n + [vec, vec],
        out_specs=[row] * len(out_shape), out_shape=out_shape,
        compiler_params=_cparams(("parallel",)),
    )(*xs, g.reshape(1, C), b.reshape(1, C))


def ln_bwd(dys, coefs, r, g, *, name):
    T, C = r.shape
    tr = _row_tile(T, C)
    n = len(dys)

    def body(*refs):
        dy_refs, r_ref, g_ref = refs[:n], refs[n], refs[n + 1]
        dr_ref, drb_ref, dg_ref, db_ref = refs[n + 2:]
        dy = coefs[0] * dy_refs[0][...]
        for c, dr_ in zip(coefs[1:], dy_refs[1:]):
            dy = dy + c * dr_[...]
        rr = r_ref[...]
        mu = jnp.mean(rr, axis=-1, keepdims=True)
        d = rr - mu
        var = jnp.mean(d * d, axis=-1, keepdims=True)
        rstd = lax.rsqrt(var + LN_EPS)
        xh = d * rstd
        gdy = dy * g_ref[...]
        m1 = jnp.mean(gdy, axis=-1, keepdims=True)
        m2 = jnp.mean(gdy * xh, axis=-1, keepdims=True)
        dr = rstd * (gdy - m1 - xh * m2)
        dr_ref[...] = dr
        drb_ref[...] = dr.astype(_BF)

        @pl.when(pl.program_id(0) == 0)
        def _():
            dg_ref[...] = jnp.zeros_like(dg_ref)
            db_ref[...] = jnp.zeros_like(db_ref)

        dg_ref[...] += jnp.sum(dy * xh, axis=0, keepdims=True)
        db_ref[...] += jnp.sum(dy, axis=0, keepdims=True)

    row = pl.BlockSpec((tr, C), lambda i: (i, 0))
    vec = pl.BlockSpec((1, C), lambda i: (0, 0))
    return pl.pallas_call(
        body, name=name, grid=(T // tr,),
        in_specs=[row] * (n + 1) + [vec],
        out_specs=[row, row, vec, vec],
        out_shape=[jax.ShapeDtypeStruct((T, C), _F32), jax.ShapeDtypeStruct((T, C), _BF),
                   jax.ShapeDtypeStruct((1, C), _F32), jax.ShapeDtypeStruct((1, C), _F32)],
        compiler_params=_cparams(("arbitrary",)),
    )(*dys, r, g.reshape(1, C))


def rms_fwd(h, cb, W, g, *, name):
    T = h.shape[0]
    tr = _tile(T, 512, 16)

    def body(c_ref, g_ref, o_ref):
        c = c_ref[...]
        ms = jnp.mean(c * c, axis=-1, keepdims=True)
        o_ref[...] = (c * lax.rsqrt(ms + RMS_EPS) * g_ref[...]).astype(_BF)

    return pl.pallas_call(
        body, name=name, grid=(T // tr,),
        in_specs=[pl.BlockSpec((tr, W), lambda i: (i, cb)), pl.BlockSpec((1, W), lambda i: (0, 0))],
        out_specs=pl.BlockSpec((tr, W), lambda i: (i, 0)),
        out_shape=jax.ShapeDtypeStruct((T, W), _BF),
        compiler_params=_cparams(("parallel",)),
    )(h, g.reshape(1, W))


def rms_bwd(dy, h, cb, W, g, *, name):
    T = h.shape[0]
    tr = _tile(T, 512, 16)

    def body(dy_ref, c_ref, g_ref, dc_ref, dg_ref):
        c = c_ref[...]
        dyv = dy_ref[...]
        ms = jnp.mean(c * c, axis=-1, keepdims=True)
        r = lax.rsqrt(ms + RMS_EPS)
        u = dyv * g_ref[...]
        m = jnp.mean(c * u, axis=-1, keepdims=True)
        dc_ref[...] = (r * u - c * (r * r * r) * m).astype(_BF)

        @pl.when(pl.program_id(0) == 0)
        def _():
            dg_ref[...] = jnp.zeros_like(dg_ref)

        dg_ref[...] += jnp.sum(dyv * c * r, axis=0, keepdims=True)

    return pl.pallas_call(
        body, name=name, grid=(T // tr,),
        in_specs=[pl.BlockSpec((tr, W), lambda i: (i, 0)), pl.BlockSpec((tr, W), lambda i: (i, cb)),
                  pl.BlockSpec((1, W), lambda i: (0, 0))],
        out_specs=[pl.BlockSpec((tr, W), lambda i: (i, 0)), pl.BlockSpec((1, W), lambda i: (0, 0))],
        out_shape=[jax.ShapeDtypeStruct((T, W), _BF), jax.ShapeDtypeStruct((1, W), _F32)],
        compiler_params=_cparams(("arbitrary",)),
    )(dy, h, g.reshape(1, W))


def _rope(u, cc, sa, sb, sign):
    return u * cc + sign * (pltpu.roll(u, 96, 1) * sa + pltpu.roll(u, 32, 1) * sb)


def mla_pack(q, kv, h, kr_cb, cc, sa, sb, *, H, scale, name):
    T = q.shape[0]
    tr = _tile(T, 256, 16)

    def body(q_ref, kv_ref, kr_ref, cc_ref, sa_ref, sb_ref, qp_ref, kp_ref, v_ref):
        cc_, sa_, sb_ = cc_ref[...], sa_ref[...], sb_ref[...]
        kr = _rope(kr_ref[...], cc_, sa_, sb_, 1.0).astype(_BF)
        for hh in range(H):
            o = hh * HEAD_PAD
            qp_ref[:, o:o + 128] = (q_ref[:, o:o + 128] * scale).astype(_BF)
            qp_ref[:, o + 128:o + 256] = (_rope(q_ref[:, o + 128:o + 256], cc_, sa_, sb_, 1.0) * scale).astype(_BF)
            kp_ref[:, o:o + 128] = kv_ref[:, o:o + 128].astype(_BF)
            kp_ref[:, o + 128:o + 256] = kr
            v_ref[:, hh * 128:(hh + 1) * 128] = kv_ref[:, o + 128:o + 256].astype(_BF)

    wide = pl.BlockSpec((tr, H * HEAD_PAD), lambda i: (i, 0))
    tab = pl.BlockSpec((tr, 128), lambda i: (i, 0))
    return pl.pallas_call(
        body, name=name, grid=(T // tr,),
        in_specs=[wide, wide, pl.BlockSpec((tr, 128), lambda i: (i, kr_cb)), tab, tab, tab],
        out_specs=[wide, wide, pl.BlockSpec((tr, H * 128), lambda i: (i, 0))],
        out_shape=[jax.ShapeDtypeStruct((T, H * HEAD_PAD), _BF), jax.ShapeDtypeStruct((T, H * HEAD_PAD), _BF),
                   jax.ShapeDtypeStruct((T, H * 128), _BF)],
        compiler_params=_cparams(("parallel",)),
    )(q, kv, h, cc, sa, sb)


def mla_unpack(dqp, dkp, dv, cc, sa, sb, *, H, name):
    T = dqp.shape[0]
    tr = _tile(T, 256, 16)

    def body(dq_ref, dk_ref, dv_ref, cc_ref, sa_ref, sb_ref, oq_ref, okv_ref, okr_ref):
        cc_, sa_, sb_ = cc_ref[...], sa_ref[...], sb_ref[...]
        kr = jnp.zeros((tr, 128), _F32)
        for hh in range(H):
            o = hh * HEAD_PAD
            oq_ref[:, o:o + 128] = dq_ref[:, o:o + 128].astype(_BF)
            oq_ref[:, o + 128:o + 256] = _rope(dq_ref[:, o + 128:o + 256], cc_, sa_, sb_, -1.0).astype(_BF)
            okv_ref[:, o:o + 128] = dk_ref[:, o:o + 128].astype(_BF)
            okv_ref[:, o + 128:o + 256] = dv_ref[:, hh * 128:(hh + 1) * 128].astype(_BF)
            kr = kr + dk_ref[:, o + 128:o + 256]
        okr_ref[...] = _rope(kr, cc_, sa_, sb_, -1.0).astype(_BF)

    wide = pl.BlockSpec((tr, H * HEAD_PAD), lambda i: (i, 0))
    tab = pl.BlockSpec((tr, 128), lambda i: (i, 0))
    return pl.pallas_call(
        body, name=name, grid=(T // tr,),
        in_specs=[wide, wide, pl.BlockSpec((tr, H * 128), lambda i: (i, 0)), tab, tab, tab],
        out_specs=[wide, wide, tab],
        out_shape=[jax.ShapeDtypeStruct((T, H * HEAD_PAD), _BF), jax.ShapeDtypeStruct((T, H * HEAD_PAD), _BF),
                   jax.ShapeDtypeStruct((T, 128), _BF)],
        compiler_params=_cparams(("parallel",)),
    )(dqp, dkp, dv, cc, sa, sb)


_NEG = -1e30


def _rows(ref, j, t):
    return ref[pl.ds(pl.multiple_of(j * t, t), t), :]


_FLASH_TILE = 1024


def _qk(q, k):
    return lax.dot_general(q, k, (((1,), (1,)), ((), ())), preferred_element_type=_F32)


def _scores(q, k, t, masked):
    s = _qk(q, k)
    if not masked:
        return s
    row = lax.broadcasted_iota(jnp.int32, (t, t), 0)
    col = lax.broadcasted_iota(jnp.int32, (t, t), 1)
    return jnp.where(col <= row, s, _NEG)


def flash_fwd(qp, kp, v, *, B, S, H, scale, side=None, name):
    T = B * S
    t = _tile(S, _FLASH_TILE, 128)
    nq = S // t
    s_ops, s_in, s_out, s_shapes, s_scratch, s_alias = _side_specs(side)
    ni, no = len(s_ops), len(s_shapes)

    def body(q_ref, k_ref, v_ref, *rest):
        s_ins, (o_ref, lse_ref), s_outs = rest[:ni], rest[ni:ni + 2], rest[ni + 2:ni + 2 + no]
        (m_sc, l_sc, acc_sc), sems = rest[ni + 2 + no:ni + 5 + no], rest[ni + 5 + no:]
        i = pl.program_id(2)
        first = (pl.program_id(0) == 0) & (pl.program_id(1) == 0) & (i == 0)
        last = (pl.program_id(0) == B - 1) & (pl.program_id(1) == H - 1) & (i == nq - 1)
        if side is not None:
            @pl.when(first)
            def _():
                side.start(s_ins, s_outs, sems)

        m_sc[...] = jnp.full_like(m_sc, _NEG)
        l_sc[...] = jnp.zeros_like(l_sc)
        acc_sc[...] = jnp.zeros_like(acc_sc)

        def step(j, masked):
            s = _scores(q_ref[...], _rows(k_ref, j, t), t, masked)
            m_old = m_sc[...]
            m_new = jnp.maximum(m_old, jnp.max(s, axis=-1, keepdims=True))
            p = jnp.exp(s - m_new)
            a = jnp.exp(m_old - m_new)
            l_sc[...] = a * l_sc[...] + jnp.sum(p, axis=-1, keepdims=True)
            acc_sc[...] = a * acc_sc[...] + jnp.dot(p.astype(_BF), _rows(v_ref, j, t), preferred_element_type=_F32)
            m_sc[...] = m_new

        @pl.loop(0, i)
        def _(j):
            step(j, False)

        step(i, True)
        l = l_sc[...]
        o_ref[...] = acc_sc[...] / l
        lse_ref[...] = jnp.broadcast_to(m_sc[...] + jnp.log(l), lse_ref.shape)
        if side is not None:
            @pl.when(last)
            def _():
                side.wait(s_ins, s_outs, sems)

    qmap = lambda b, h, i: (b * nq + i, h)
    smap = lambda b, h, i: (b, h)
    res = pl.pallas_call(
        body, name=name, grid=(B, H, nq),
        in_specs=[pl.BlockSpec((t, HEAD_PAD), qmap), pl.BlockSpec((S, HEAD_PAD), smap), pl.BlockSpec((S, 128), smap)] + s_in,
        out_specs=[pl.BlockSpec((t, 128), qmap), pl.BlockSpec((t, 128), qmap)] + s_out,
        out_shape=[jax.ShapeDtypeStruct((T, H * 128), _F32), jax.ShapeDtypeStruct((T, H * 128), _F32)] + s_shapes,
        input_output_aliases={3 + i_: 2 + o_ for i_, o_ in s_alias.items()},
        scratch_shapes=[pltpu.VMEM((t, 1), _F32), pltpu.VMEM((t, 1), _F32), pltpu.VMEM((t, 128), _F32)] + s_scratch,
        compiler_params=_cparams(("arbitrary",) * 3 if side is not None else ("parallel",) * 3),
    )(qp, kp, v, *s_ops)
    return (res[0], res[1]) if side is None else (res[0], res[1], list(res[2:]))


def _grid_ends(grid):
    ids = [pl.program_id(d) for d in range(len(grid))]
    first, last = ids[0] == 0, ids[0] == grid[0] - 1
    for d in range(1, len(grid)):
        first, last = first & (ids[d] == 0), last & (ids[d] == grid[d] - 1)
    return first, last


def flash_bwd_dq(qp, kp, v, o, lse, do, do_cb0, *, B, S, H, scale, side=None, name):
    T = B * S
    t = _tile(S, _FLASH_TILE, 128)
    nq = S // t
    s_ops, s_in, s_out, s_shapes, s_scratch, s_alias = _side_specs(side)
    ni, no = len(s_ops), len(s_shapes)

    def body(q_ref, k_ref, v_ref, o_ref, lse_ref, do_ref, *rest):
        s_ins, dq_ref, s_outs = rest[:ni], rest[ni], rest[ni + 1:ni + 1 + no]
        (acc_sc, dl_sc), sems = rest[ni + 1 + no:ni + 3 + no], rest[ni + 3 + no:]
        first, last = _grid_ends((B, H, nq))
        if side is not None:
            @pl.when(first)
            def _():
                side.start(s_ins, s_outs, sems)

        i = pl.program_id(2)
        acc_sc[...] = jnp.zeros_like(acc_sc)
        dl_sc[...] = jnp.sum(do_ref[...].astype(_F32) * o_ref[...], axis=-1, keepdims=True)

        def step(j, masked):
            k = _rows(k_ref, j, t)
            s = _scores(q_ref[...], k, t, masked)
            p = jnp.exp(s - lse_ref[:, 0:1])
            dp = _qk(do_ref[...].astype(_BF), _rows(v_ref, j, t))
            ds = p * (dp - dl_sc[...])
            acc_sc[...] += jnp.dot(ds.astype(_BF), k, preferred_element_type=_F32)

        @pl.loop(0, i)
        def _(j):
            step(j, False)

        step(i, True)
        dq_ref[...] = acc_sc[...] * scale
        if side is not None:
            @pl.when(last)
            def _():
                side.wait(s_ins, s_outs, sems)

    qmap = lambda b, h, i: (b * nq + i, h)
    domap = lambda b, h, i: (b * nq + i, do_cb0 + h)
    smap = lambda b, h, i: (b, h)
    res = pl.pallas_call(
        body, name=name, grid=(B, H, nq),
        in_specs=[pl.BlockSpec((t, HEAD_PAD), qmap), pl.BlockSpec((S, HEAD_PAD), smap), pl.BlockSpec((S, 128), smap),
                  pl.BlockSpec((t, 128), qmap), pl.BlockSpec((t, 128), qmap), pl.BlockSpec((t, 128), domap)] + s_in,
        out_specs=[pl.BlockSpec((t, HEAD_PAD), qmap)] + s_out,
        out_shape=[jax.ShapeDtypeStruct((T, H * HEAD_PAD), _F32)] + s_shapes,
        input_output_aliases={6 + i_: 1 + o_ for i_, o_ in s_alias.items()},
        scratch_shapes=[pltpu.VMEM((t, HEAD_PAD), _F32), pltpu.VMEM((t, 1), _F32)] + s_scratch,
        compiler_params=_cparams(("arbitrary",) * 3 if side is not None else ("parallel",) * 3),
    )(qp, kp, v, o, lse, do, *s_ops)
    return res[0] if side is None else (res[0], list(res[1:]))


def flash_bwd_dkv(qp, kp, v, o, lse, do, do_cb0, *, B, S, H, scale, side=None, name):
    T = B * S
    t = _tile(S, _FLASH_TILE, 128)
    nk = S // t
    s_ops, s_in, s_out, s_shapes, s_scratch, s_alias = _side_specs(side)
    ni, no = len(s_ops), len(s_shapes)

    def body(q_ref, k_ref, v_ref, o_ref, lse_ref, do_ref, *rest):
        s_ins, (dk_ref, dv_ref), s_outs = rest[:ni], rest[ni:ni + 2], rest[ni + 2:ni + 2 + no]
        (dk_sc, dv_sc), sems = rest[ni + 2 + no:ni + 4 + no], rest[ni + 4 + no:]
        first, last = _grid_ends((B, H, nk))
        if side is not None:
            @pl.when(first)
            def _():
                side.start(s_ins, s_outs, sems)

        j = pl.program_id(2)
        dk_sc[...] = jnp.zeros_like(dk_sc)
        dv_sc[...] = jnp.zeros_like(dv_sc)

        def step(i, masked):
            q = _rows(q_ref, i, t)
            do = _rows(do_ref, i, t).astype(_F32)
            dob = do.astype(_BF)
            s = _scores(q, k_ref[...], t, masked)
            p = jnp.exp(s - _rows(lse_ref, i, t)[:, 0:1])
            dl = jnp.sum(do * _rows(o_ref, i, t), axis=-1, keepdims=True)
            dp = _qk(dob, v_ref[...])
            ds = p * (dp - dl)
            tn = (((0,), (0,)), ((), ()))
            dv_sc[...] += lax.dot_general(p.astype(_BF), dob, tn, preferred_element_type=_F32)
            dk_sc[...] += lax.dot_general(ds.astype(_BF), q, tn, preferred_element_type=_F32)

        step(j, True)

        @pl.loop(j + 1, nk)
        def _(i):
            step(i, False)

        dk_ref[...] = dk_sc[...]
        dv_ref[...] = dv_sc[...]
        if side is not None:
            @pl.when(last)
            def _():
                side.wait(s_ins, s_outs, sems)

    smap = lambda b, h, j: (b, h)
    domap = lambda b, h, j: (b, do_cb0 + h)
    kmap = lambda b, h, j: (b * nk + j, h)
    res = pl.pallas_call(
        body, name=name, grid=(B, H, nk),
        in_specs=[pl.BlockSpec((S, HEAD_PAD), smap), pl.BlockSpec((t, HEAD_PAD), kmap), pl.BlockSpec((t, 128), kmap),
                  pl.BlockSpec((S, 128), smap), pl.BlockSpec((S, 128), smap), pl.BlockSpec((S, 128), domap)] + s_in,
        out_specs=[pl.BlockSpec((t, HEAD_PAD), kmap), pl.BlockSpec((t, 128), kmap)] + s_out,
        out_shape=[jax.ShapeDtypeStruct((T, H * HEAD_PAD), _F32), jax.ShapeDtypeStruct((T, H * 128), _F32)] + s_shapes,
        input_output_aliases={6 + i_: 2 + o_ for i_, o_ in s_alias.items()},
        scratch_shapes=[pltpu.VMEM((t, HEAD_PAD), _F32), pltpu.VMEM((t, 128), _F32)] + s_scratch,
        compiler_params=_cparams(("arbitrary",) * 3 if side is not None else ("parallel",) * 3),
    )(qp, kp, v, o, lse, do, *s_ops)
    return (res[0], res[1]) if side is None else (res[0], res[1], list(res[2:]))


def _halo_specs(T, nT, tt, hr, cw, cb):
    k = tt // hr
    main = pl.BlockSpec((tt, cw), lambda b, t: (b * nT + t, cb))
    prev = pl.BlockSpec((hr, cw), lambda b, t: (jnp.maximum((b * nT + t) * k - 1, 0), cb))
    nxt = pl.BlockSpec((hr, cw), lambda b, t: (jnp.minimum((b * nT + t + 1) * k, T // hr - 1), cb))
    return main, prev, nxt


_CONV_ROWS = 32


def _ln_rows(z, g, b):
    mu = jnp.mean(z, axis=-1, keepdims=True)
    d = z - mu
    var = jnp.mean(d * d, axis=-1, keepdims=True)
    rstd = lax.rsqrt(var + LN_EPS)
    xh = d * rstd
    return xh * g + b, xh, rstd


def conv_fwd(h, cb_a, cb_g, w, bias, lng, lnb, *, B, S, name):
    T = B * S
    K, C = w.shape
    hr = 32
    assert K - 1 <= hr
    tt = _tile(S, 512, hr)
    nT = S // tt
    a_m, a_p, _ = _halo_specs(T, nT, tt, hr, C, cb_a)
    g_m, g_p, _ = _halo_specs(T, nT, tt, hr, C, cb_g)

    def body(a_ref, g_ref, ap_ref, gp_ref, w_ref, b_ref, lg_ref, lb_ref, z_ref, y_ref, buf):
        t = pl.program_id(1)
        buf[pl.ds(hr, tt), :] = a_ref[...] * jax.nn.sigmoid(g_ref[...])
        hp = ap_ref[...] * jax.nn.sigmoid(gp_ref[...])
        buf[pl.ds(0, hr), :] = jnp.where(t == 0, 0.0, hp)
        z = jnp.broadcast_to(b_ref[...], (tt, C))
        for k in range(K):
            z = z + w_ref[k:k + 1, :] * buf[pl.ds(hr - (K - 1) + k, tt), :]
        z_ref[...] = z
        n, _, _ = _ln_rows(z, lg_ref[...], lb_ref[...])
        y_ref[...] = (n * jax.nn.sigmoid(n)).astype(_BF)

    vec = pl.BlockSpec((1, C), lambda b, t: (0, 0))
    out = pl.BlockSpec((tt, C), lambda b, t: (b * nT + t, 0))
    return pl.pallas_call(
        body, name=name, grid=(B, nT),
        in_specs=[a_m, g_m, a_p, g_p, pl.BlockSpec((K, C), lambda b, t: (0, 0)), vec, vec, vec],
        out_specs=[out, out],
        out_shape=[jax.ShapeDtypeStruct((T, C), _F32), jax.ShapeDtypeStruct((T, C), _BF)],
        scratch_shapes=[pltpu.VMEM((hr + tt, C), _F32)],
        compiler_params=_cparams(("parallel", "parallel")),
    )(h, h, h, h, w, bias.reshape(1, C), lng.reshape(1, C), lnb.reshape(1, C))


def conv_bwd(dmix, cb_dy, z, h, cb_a, cb_g, w, lng, lnb, *, B, S, name):
    T = B * S
    K, C = w.shape
    hr = 32
    tt = _tile(S, 512, hr)
    nT = S // tt
    a_m, a_p, _ = _halo_specs(T, nT, tt, hr, C, cb_a)
    g_m, g_p, _ = _halo_specs(T, nT, tt, hr, C, cb_g)
    dy_m, _, dy_n = _halo_specs(T, nT, tt, hr, C, cb_dy)
    z_m, _, z_n = _halo_specs(T, nT, tt, hr, C, 0)

    def body(dy_ref, dyn_ref, z_ref, zn_ref, a_ref, g_ref, ap_ref, gp_ref, w_ref, lg_ref, lb_ref,
             da_ref, dg_ref, dw_ref, db_ref, dlg_ref, dlb_ref, bufz, bufh):
        b, t = pl.program_id(0), pl.program_id(1)
        lg, lb = lg_ref[...], lb_ref[...]

        def dz_of(dy, zz):
            n, xh, rstd = _ln_rows(zz, lg, lb)
            sg = jax.nn.sigmoid(n)
            dn = dy.astype(_F32) * (sg * (1.0 + n * (1.0 - sg)))
            gdn = dn * lg
            m1 = jnp.mean(gdn, axis=-1, keepdims=True)
            m2 = jnp.mean(gdn * xh, axis=-1, keepdims=True)
            return rstd * (gdn - m1 - xh * m2), dn, xh

        rc = _CONV_ROWS
        acc = [jnp.zeros((8, C), _F32) for _ in range(3)]
        for r in range(0, tt, rc):
            rows = pl.ds(r, rc)
            dz, dn, xh = dz_of(dy_ref[rows, :], z_ref[rows, :])
            bufz[rows, :] = dz
            acc = [acc[0] + _fold8(dz), acc[1] + _fold8(dn * xh), acc[2] + _fold8(dn)]
            bufh[pl.ds(hr + r, rc), :] = a_ref[rows, :] * jax.nn.sigmoid(g_ref[rows, :])
        dzn, _, _ = dz_of(dyn_ref[...], zn_ref[...])
        bufz[pl.ds(tt, hr), :] = jnp.where(t == nT - 1, 0.0, dzn)
        bufh[pl.ds(0, hr), :] = jnp.where(t == 0, 0.0, ap_ref[...] * jax.nn.sigmoid(gp_ref[...]))

        @pl.when((b == 0) & (t == 0))
        def _():
            dw_ref[...] = jnp.zeros_like(dw_ref)
            db_ref[...] = jnp.zeros_like(db_ref)
            dlg_ref[...] = jnp.zeros_like(dlg_ref)
            dlb_ref[...] = jnp.zeros_like(dlb_ref)

        for r in range(0, tt, rc):
            rows = pl.ds(r, rc)
            dhc = jnp.zeros((rc, C), _F32)
            for k in range(K):
                dhc = dhc + w_ref[k:k + 1, :] * bufz[pl.ds(K - 1 - k + r, rc), :]
            a, sg = a_ref[rows, :], jax.nn.sigmoid(g_ref[rows, :])
            da_ref[rows, :] = (dhc * sg).astype(_BF)
            dg_ref[rows, :] = (dhc * a * sg * (1.0 - sg)).astype(_BF)
        for k in range(K):
            tap = jnp.zeros((8, C), _F32)
            for r in range(0, tt, rc):
                tap = tap + _fold8(bufz[pl.ds(r, rc), :] * bufh[pl.ds(hr - (K - 1) + k + r, rc), :])
            dw_ref[k:k + 1, :] += jnp.sum(tap, axis=0, keepdims=True)
        db_ref[...] += jnp.sum(acc[0], axis=0, keepdims=True)
        dlg_ref[...] += jnp.sum(acc[1], axis=0, keepdims=True)
        dlb_ref[...] += jnp.sum(acc[2], axis=0, keepdims=True)

    vec = pl.BlockSpec((1, C), lambda b, t: (0, 0))
    out = pl.BlockSpec((tt, C), lambda b, t: (b * nT + t, 0))
    kc = pl.BlockSpec((K, C), lambda b, t: (0, 0))
    return pl.pallas_call(
        body, name=name, grid=(B, nT),
        in_specs=[dy_m, dy_n, z_m, z_n, a_m, g_m, a_p, g_p, kc, vec, vec],
        out_specs=[out, out, kc, vec, vec, vec],
        out_shape=[jax.ShapeDtypeStruct((T, C), _BF), jax.ShapeDtypeStruct((T, C), _BF),
                   jax.ShapeDtypeStruct((K, C), _F32)] + [jax.ShapeDtypeStruct((1, C), _F32)] * 3,
        scratch_shapes=[pltpu.VMEM((tt + hr, C), _F32), pltpu.VMEM((hr + tt, C), _F32)],
        compiler_params=_cparams(("arbitrary", "arbitrary")),
    )(dmix, dmix, z, z, h, h, h, h, w, lng.reshape(1, C), lnb.reshape(1, C))


def _pool_cnt(t, tt, w, rows):
    pos = t * tt + lax.broadcasted_iota(jnp.int32, (rows, 1), 0)
    return jnp.minimum(pos + 1, w).astype(_F32)


def pool_fwd(h, cb, wp, scale, *, B, S, name):
    T = B * S
    G, pg, _ = wp.shape
    C = G * pg
    assert pg == 128 and G == len(POOL_WINDOWS)
    hr = 16
    tt = _tile(S, 512, hr)
    nT = S // tt
    u_m, u_p, _ = _halo_specs(T, nT, tt, hr, C, cb)

    def body(u_ref, up_ref, wp_ref, sc_ref, y_ref, buf):
        t = pl.program_id(1)
        buf[pl.ds(hr, tt), :] = u_ref[...]
        buf[pl.ds(0, hr), :] = jnp.where(t == 0, 0.0, up_ref[...])
        for gi, w in enumerate(POOL_WINDOWS):
            ln = slice(gi * pg, (gi + 1) * pg)
            acc = buf[pl.ds(hr, tt), ln]
            for j in range(1, w):
                acc = acc + buf[pl.ds(hr - j, tt), ln]
            d = acc / _pool_cnt(t, tt, w, tt) - u_ref[:, ln]
            yg = jnp.dot(d.astype(_BF), wp_ref[gi].astype(_BF), preferred_element_type=_F32)
            y_ref[:, ln] = (yg * sc_ref[:, ln]).astype(_BF)

    return pl.pallas_call(
        body, name=name, grid=(B, nT),
        in_specs=[u_m, u_p, pl.BlockSpec((G, pg, pg), lambda b, t: (0, 0, 0)), pl.BlockSpec((1, C), lambda b, t: (0, 0))],
        out_specs=pl.BlockSpec((tt, C), lambda b, t: (b * nT + t, 0)),
        out_shape=jax.ShapeDtypeStruct((T, C), _BF),
        scratch_shapes=[pltpu.VMEM((hr + tt, C), _F32)],
        compiler_params=_cparams(("parallel", "parallel")),
    )(h, h, wp, scale.reshape(1, C))


def pool_bwd(dmix, cb_dy, h, cb, wp, scale, *, B, S, name):
    T = B * S
    G, pg, _ = wp.shape
    C = G * pg
    hr = 16
    tt = _tile(S, 512, hr)
    nT = S // tt
    u_m, u_p, _ = _halo_specs(T, nT, tt, hr, C, cb)
    dy_m, _, dy_n = _halo_specs(T, nT, tt, hr, C, cb_dy)

    def body(dy_ref, dyn_ref, u_ref, up_ref, wp_ref, sc_ref, du_ref, dwp_ref, dsc_ref, buf, bufe):
        b, t = pl.program_id(0), pl.program_id(1)
        buf[pl.ds(hr, tt), :] = u_ref[...]
        buf[pl.ds(0, hr), :] = jnp.where(t == 0, 0.0, up_ref[...])

        @pl.when((b == 0) & (t == 0))
        def _():
            dwp_ref[...] = jnp.zeros_like(dwp_ref)
            dsc_ref[...] = jnp.zeros_like(dsc_ref)

        nt = (((1,), (1,)), ((), ()))
        tn = (((0,), (0,)), ((), ()))
        for gi, w in enumerate(POOL_WINDOWS):
            ln = slice(gi * pg, (gi + 1) * pg)
            wg = wp_ref[gi].astype(_BF)
            sc = sc_ref[:, ln]
            dy = dy_ref[:, ln].astype(_F32)
            dz = (dy * sc).astype(_BF)
            dzn = (dyn_ref[:, ln].astype(_F32) * sc).astype(_BF)
            dd = lax.dot_general(dz, wg, nt, preferred_element_type=_F32)
            ddn = lax.dot_general(dzn, wg, nt, preferred_element_type=_F32)
            bufe[pl.ds(0, tt), ln] = dd / _pool_cnt(t, tt, w, tt)
            bufe[pl.ds(tt, hr), ln] = jnp.where(t == nT - 1, 0.0, ddn / _pool_cnt(t + 1, tt, w, hr))
            du = -dd
            for j in range(w):
                du = du + bufe[pl.ds(j, tt), ln]
            du_ref[:, ln] = du.astype(_BF)
            acc = buf[pl.ds(hr, tt), ln]
            for j in range(1, w):
                acc = acc + buf[pl.ds(hr - j, tt), ln]
            d = (acc / _pool_cnt(t, tt, w, tt) - u_ref[:, ln]).astype(_BF)
            dwp_ref[gi] += lax.dot_general(d, dz, tn, preferred_element_type=_F32)
            yg = jnp.dot(d, wg, preferred_element_type=_F32)
            dsc_ref[:, ln] += jnp.sum(dy * yg, axis=0, keepdims=True)

    return pl.pallas_call(
        body, name=name, grid=(B, nT),
        in_specs=[dy_m, dy_n, u_m, u_p, pl.BlockSpec((G, pg, pg), lambda b, t: (0, 0, 0)),
                  pl.BlockSpec((1, C), lambda b, t: (0, 0))],
        out_specs=[pl.BlockSpec((tt, C), lambda b, t: (b * nT + t, 0)), pl.BlockSpec((G, pg, pg), lambda b, t: (0, 0, 0)),
                   pl.BlockSpec((1, C), lambda b, t: (0, 0))],
        out_shape=[jax.ShapeDtypeStruct((T, C), _BF), jax.ShapeDtypeStruct((G, pg, pg), _F32),
                   jax.ShapeDtypeStruct((1, C), _F32)],
        scratch_shapes=[pltpu.VMEM((hr + tt, C), _F32), pltpu.VMEM((tt + hr, C), _F32)],
        compiler_params=_cparams(("arbitrary", "arbitrary")),
    )(dmix, dmix, h, h, wp, scale.reshape(1, C))


_FFN_HR = 16
_FFN_ROWS, _FFN_LANES = 32, 256


def _fold8(x):
    out = x[0:8]
    for r in range(8, x.shape[0], 8):
        out = out + x[r:r + 8]
    return out


def _silu_grad(x, sg):
    return sg * (1.0 + x * (1.0 - sg))


def _conv3(buf, w_ref, b_ref, off, rows, ln):
    c = b_ref[:, ln] + w_ref[0:1, ln] * buf[pl.ds(off, rows), ln]
    for k in (1, 2):
        c = c + w_ref[k:k + 1, ln] * buf[pl.ds(off + k, rows), ln]
    return c


def _ffn_chunks(nrows, ncols):
    lw = min(ncols, _FFN_LANES)
    return [(r, min(_FFN_ROWS, nrows - r), slice(l0, l0 + lw))
            for l0 in range(0, ncols, lw) for r in range(0, nrows, _FFN_ROWS)]


def gate_fwd(up, w, bias, *, B, S, name):
    T, F2 = up.shape
    F = F2 // 2
    hr = _FFN_HR
    tt = _tile(S, 512, hr)
    nT = S // tt
    tn = _tile(F, 512, 128)
    nC = F // tn
    k = tt // hr

    def body(a_ref, g_ref, ap_ref, gp_ref, wa_ref, wg_ref, ba_ref, bg_ref, o_ref, bufa, bufg):
        t = pl.program_id(2)
        for buf, m_ref, p_ref in ((bufa, a_ref, ap_ref), (bufg, g_ref, gp_ref)):
            buf[pl.ds(hr, tt), :] = m_ref[...].astype(_F32)
            buf[pl.ds(0, hr), :] = jnp.where(t == 0, 0.0, p_ref[...].astype(_F32))
        for r, rc, ln in _ffn_chunks(tt, tn):
            ca = _conv3(bufa, wa_ref, ba_ref, hr - 2 + r, rc, ln)
            cg = _conv3(bufg, wg_ref, bg_ref, hr - 2 + r, rc, ln)
            o_ref[pl.ds(r, rc), ln] = (ca * cg * jax.nn.sigmoid(cg)).astype(_BF)

    def main(off):
        return pl.BlockSpec((tt, tn), lambda b, j, t: (b * nT + t, j + off))

    def prev(off):
        return pl.BlockSpec((hr, tn), lambda b, j, t: (jnp.maximum((b * nT + t) * k - 1, 0), j + off))

    def wspec(rows, off):
        return pl.BlockSpec((rows, tn), lambda b, j, t: (0, j + off))

    return pl.pallas_call(
        body, name=name, grid=(B, nC, nT),
        in_specs=[main(0), main(nC), prev(0), prev(nC), wspec(3, 0), wspec(3, nC), wspec(1, 0), wspec(1, nC)],
        out_specs=pl.BlockSpec((tt, tn), lambda b, j, t: (b * nT + t, j)),
        out_shape=jax.ShapeDtypeStruct((T, F), _BF),
        scratch_shapes=[pltpu.VMEM((hr + tt, tn), _F32)] * 2,
        compiler_params=_cparams(("parallel", "parallel", "parallel")),
    )(up, up, up, up, w, w, bias.reshape(1, F2), bias.reshape(1, F2))


def gate_bwd(up, dact, w, bias, *, B, S, name):
    T, F2 = up.shape
    F = F2 // 2
    hr = _FFN_HR
    tt = _tile(S, 512, hr)
    nT = S // tt
    tn = _tile(F, 512, 128)
    nC = F // tn
    k = tt // hr
    ext = tt + hr

    def body(a_ref, g_ref, ap_ref, gp_ref, an_ref, gn_ref, d_ref, dn_ref, wa_ref, wg_ref, ba_ref, bg_ref,
             du_ref, dwa_ref, dwg_ref, dba_ref, dbg_ref, bufa, bufg, bufda, bufdg):
        b, t = pl.program_id(1), pl.program_id(2)
        last = t == nT - 1
        for buf, m_ref, p_ref, n_ref in ((bufa, a_ref, ap_ref, an_ref), (bufg, g_ref, gp_ref, gn_ref)):
            buf[pl.ds(hr, tt), :] = m_ref[...].astype(_F32)
            buf[pl.ds(0, hr), :] = jnp.where(t == 0, 0.0, p_ref[...].astype(_F32))
            buf[pl.ds(hr + tt, hr), :] = jnp.where(last, 0.0, n_ref[...].astype(_F32))
        for r, rc, ln in _ffn_chunks(ext, tn):
            ca = _conv3(bufa, wa_ref, ba_ref, hr - 2 + r, rc, ln)
            cg = _conv3(bufg, wg_ref, bg_ref, hr - 2 + r, rc, ln)
            sg = jax.nn.sigmoid(cg)
            if r < tt:
                da = d_ref[pl.ds(r, rc), ln].astype(_F32)
            else:
                da = jnp.where(last, 0.0, dn_ref[pl.ds(r - tt, rc), ln].astype(_F32))
            bufda[pl.ds(r, rc), ln] = da * cg * sg
            bufdg[pl.ds(r, rc), ln] = da * ca * _silu_grad(cg, sg)

        @pl.when((b == 0) & (t == 0))
        def _():
            for r in (dwa_ref, dwg_ref, dba_ref, dbg_ref):
                r[...] = jnp.zeros_like(r)

        lw, rc = min(tn, _FFN_LANES), _FFN_ROWS
        assert tt % rc == 0
        for half, bufd, buf, w_ref, dw_ref, db_ref in ((0, bufda, bufa, wa_ref, dwa_ref, dba_ref),
                                                       (1, bufdg, bufg, wg_ref, dwg_ref, dbg_ref)):
            for l0 in range(0, tn, lw):
                ln = slice(l0, l0 + lw)
                acc = [jnp.zeros((8, lw), _F32) for _ in range(4)]
                for r in range(0, tt, rc):
                    dc = [bufd[pl.ds(r + s_, rc), ln] for s_ in range(3)]
                    u = buf[pl.ds(hr + r, rc), ln]
                    du = w_ref[2:3, ln] * dc[0] + w_ref[1:2, ln] * dc[1] + w_ref[0:1, ln] * dc[2]
                    du_ref[half, pl.ds(r, rc), ln] = du.astype(_BF)
                    for kk in range(3):
                        acc[kk] = acc[kk] + _fold8(dc[2 - kk] * u)
                    acc[3] = acc[3] + _fold8(dc[0])
                for kk in range(3):
                    dw_ref[kk:kk + 1, ln] += jnp.sum(acc[kk], axis=0, keepdims=True)
                db_ref[:, ln] += jnp.sum(acc[3], axis=0, keepdims=True)

    def main(off):
        return pl.BlockSpec((tt, tn), lambda j, b, t: (b * nT + t, j + off))

    def prev(off):
        return pl.BlockSpec((hr, tn), lambda j, b, t: (jnp.maximum((b * nT + t) * k - 1, 0), j + off))

    def nxt(off):
        return pl.BlockSpec((hr, tn), lambda j, b, t: (jnp.minimum((b * nT + t + 1) * k, T // hr - 1), j + off))

    def wspec(rows, off):
        return pl.BlockSpec((rows, tn), lambda j, b, t: (0, j + off))

    both = pl.BlockSpec((2, tt, tn), lambda j, b, t: (0, b * nT + t, j))
    return pl.pallas_call(
        body, name=name, grid=(nC, B, nT),
        in_specs=[main(0), main(nC), prev(0), prev(nC), nxt(0), nxt(nC), main(0), nxt(0),
                  wspec(3, 0), wspec(3, nC), wspec(1, 0), wspec(1, nC)],
        out_specs=[both, wspec(3, 0), wspec(3, 0), wspec(1, 0), wspec(1, 0)],
        out_shape=[jax.ShapeDtypeStruct((2, T, F), _BF), jax.ShapeDtypeStruct((3, F), _F32), jax.ShapeDtypeStruct((3, F), _F32),
                   jax.ShapeDtypeStruct((1, F), _F32), jax.ShapeDtypeStruct((1, F), _F32)],
        scratch_shapes=[pltpu.VMEM((hr + ext, tn), _F32)] * 2 + [pltpu.VMEM((ext, tn), _F32)] * 2,
        compiler_params=_cparams(("parallel", "arbitrary", "arbitrary")),
    )(up, up, up, up, up, up, dact, dact, w, w, bias.reshape(1, F2), bias.reshape(1, F2))


def loss_head(y, target, *, name):
    T, C = y.shape
    tr = _row_tile(T, C)

    def body(y_ref, t_ref, dy_ref, acc_ref):
        @pl.when(pl.program_id(0) == 0)
        def _():
            acc_ref[...] = jnp.zeros_like(acc_ref)

        e = y_ref[...] - t_ref[...]
        dy_ref[...] = e * (1.0 / C)
        acc_ref[...] += jnp.sum(e * e, axis=0, keepdims=True) * (0.5 / C)

    row = pl.BlockSpec((tr, C), lambda i: (i, 0))
    return pl.pallas_call(
        body, name=name, grid=(T // tr,),
        in_specs=[row, row], out_specs=[row, pl.BlockSpec((1, C), lambda i: (0, 0))],
        out_shape=[jax.ShapeDtypeStruct((T, C), _F32), jax.ShapeDtypeStruct((1, C), _F32)],
        compiler_params=_cparams(("arbitrary",)),
    )(y, target)


def _adamw_update(w_ref, g_ref, m_ref, v_ref, d_ref, mo_ref, vo_ref):
    c1 = 1.0 - ADAM_B1 ** ADAM_STEP
    c2 = 1.0 - ADAM_B2 ** ADAM_STEP
    gg = g_ref[...]
    mn = ADAM_B1 * m_ref[...] + (1.0 - ADAM_B1) * gg
    vn = ADAM_B2 * v_ref[...] + (1.0 - ADAM_B2) * (gg * gg)
    d_ref[...] = -ADAM_LR * ((mn / c1) / (jnp.sqrt(vn / c2) + ADAM_EPS) + ADAM_WD * w_ref[...])
    mo_ref[...] = mn
    vo_ref[...] = vn


def adamw_small(ws, gs, ms, vs, *, name):
    n = len(ws)

    def body(*refs):
        for p in range(n):
            _adamw_update(*[refs[k * n + p] for k in range(7)])

    vm = pl.BlockSpec(memory_space=pltpu.VMEM)
    shapes = [jax.ShapeDtypeStruct(w.shape, _F32) for w in ws]
    res = pl.pallas_call(
        body, name=name, in_specs=[vm] * (4 * n), out_specs=[vm] * (3 * n), out_shape=shapes * 3,
        compiler_params=_cparams(),
    )(*ws, *gs, *ms, *vs)
    return res[:n], res[n:2 * n], res[2 * n:]


def adamw(w, g, m, v, *, side=None, name):
    R, C = w.shape
    tr = _tile(R, max(8, (256 * 1024) // C // 8 * 8), 8)
    s_ops, s_in, s_out, s_shapes, s_scratch, s_alias = _side_specs(side)
    ni, no = len(s_ops), len(s_shapes)

    def body(w_ref, g_ref, m_ref, v_ref, *rest):
        s_ins, outs, s_outs, sems = rest[:ni], rest[ni:ni + 3], rest[ni + 3:ni + 3 + no], rest[ni + 3 + no:]
        first, last = _grid_ends((R // tr,))
        if side is not None:
            @pl.when(first)
            def _():
                side.start(s_ins, s_outs, sems)

        _adamw_update(w_ref, g_ref, m_ref, v_ref, *outs)
        if side is not None:
            @pl.when(last)
            def _():
                side.wait(s_ins, s_outs, sems)

    blk = pl.BlockSpec((tr, C), lambda i: (i, 0))
    s = jax.ShapeDtypeStruct((R, C), _F32)
    res = pl.pallas_call(
        body, name=name, grid=(R // tr,),
        in_specs=[blk] * 4 + s_in, out_specs=[blk] * 3 + s_out, out_shape=[s, s, s] + s_shapes,
        input_output_aliases={4 + i_: 3 + o_ for i_, o_ in s_alias.items()},
        scratch_shapes=s_scratch,
        compiler_params=_cparams(("arbitrary",) if side is not None else ("parallel",)),
    )(w, g, m, v, *s_ops)
    return (res[0], res[1], res[2]) if side is None else (res[0], res[1], res[2], list(res[3:]))


_ANY = pl.BlockSpec(memory_space=pl.ANY)
_MESH = pl.DeviceIdType.MESH


def _place():
    return lax.axis_index("x"), lax.axis_index("y"), lax.axis_index("c")


def _other_chips(x, y):
    chips = [(1 - x, y), (x, 1 - y), (1 - x, 1 - y)]
    return chips, [2 * a + b for a, b in chips]


def _rcopy(src, dst, ssem, rsem, dev):
    return pltpu.make_async_remote_copy(src_ref=src, dst_ref=dst, send_sem=ssem, recv_sem=rsem,
                                        device_id=dev, device_id_type=_MESH)


def place_shard(w, l, chip_idx, *, name):
    _, _, hR, C = w.shape
    tr = _tile(hR, max(16, (512 * 1024) // C // 16 * 16), 16)

    def body(ci_ref, w_ref, o_ref):
        o_ref[...] = w_ref[...].astype(_BF)

    return pl.pallas_call(
        body, name=name,
        grid_spec=pltpu.PrefetchScalarGridSpec(
            num_scalar_prefetch=1, grid=(2, hR // tr),
            in_specs=[pl.BlockSpec((None, None, tr, C), lambda h, i, ci: (l, h, i, 0))],
            out_specs=pl.BlockSpec((None, None, tr, C), lambda h, i, ci: (ci[0], h, i, 0))),
        out_shape=jax.ShapeDtypeStruct((4, 2, hR, C), _BF),
        compiler_params=_cparams(("parallel", "parallel")),
    )(chip_idx, w)


class _Side:
    def __init__(self, arrays, out_shapes, aliases, n_sems, start, wait):
        self.arrays, self.out_shapes, self.aliases, self.n_sems = arrays, out_shapes, aliases, n_sems
        self.start, self.wait = start, wait


def gather_stage1_side(bufs):
    n = len(bufs)

    def copies(outs, sems, sending):
        x, y, c = _place()
        me = 2 * x + y
        chips, cidx = _other_chips(x, y)
        send, recv = sems
        out, back = [], []
        for k, chip in enumerate(chips):
            for p in range(n):
                mine, got = outs[p].at[me, c], outs[p].at[cidx[k], c]
                out.append(_rcopy(mine, mine, send.at[p * 3 + k], recv.at[p * 3 + k], (*chip, c)))
                if not sending:
                    back.append(_rcopy(got, got, send.at[p * 3 + k], recv.at[p * 3 + k], (*chip, c)))
        return out, back

    def start(ins, outs, sems):
        for cp in copies(outs, sems, True)[0]:
            cp.start()

    def wait(ins, outs, sems):
        out, back = copies(outs, sems, False)
        for cp in back:
            cp.wait_recv()
        for cp in out:
            cp.wait_send()

    return _Side(list(bufs), [jax.ShapeDtypeStruct(b.shape, b.dtype) for b in bufs], {p: p for p in range(n)},
                 [n * 3, n * 3], start, wait)


def gather_stage2_side(bufs):
    n = len(bufs)

    def copies(outs, sems, sending):
        x, y, c = _place()
        sib = (x, y, 1 - c)
        _, cidx = _other_chips(x, y)
        send, recv = sems
        out, back = [], []
        for k in range(3):
            for p in range(n):
                mine, got = outs[p].at[cidx[k], c], outs[p].at[cidx[k], 1 - c]
                out.append(_rcopy(mine, mine, send.at[p * 3 + k], recv.at[p * 3 + k], sib))
                if not sending:
                    back.append(_rcopy(got, got, send.at[p * 3 + k], recv.at[p * 3 + k], sib))
        return out, back

    def start(ins, outs, sems):
        for cp in copies(outs, sems, True)[0]:
            cp.start()

    def wait(ins, outs, sems):
        out, back = copies(outs, sems, False)
        for cp in back:
            cp.wait_recv()
        for cp in out:
            cp.wait_send()

    return _Side(list(bufs), [jax.ShapeDtypeStruct(b.shape, b.dtype) for b in bufs], {p: p for p in range(n)},
                 [n * 3, n * 3], start, wait)


def chip_exchange_side(ps):
    n = len(ps)

    def copies(ins, outs, sems, sending):
        x, y, c = _place()
        me = 2 * x + y
        chips, cidx = _other_chips(x, y)
        send, recv = sems
        out, back = [], []
        for k, chip in enumerate(chips):
            for p in range(n):
                got = outs[p].at[cidx[k]]
                out.append(_rcopy(ins[p].at[cidx[k]], outs[p].at[me], send.at[p * 3 + k], recv.at[p * 3 + k], (*chip, c)))
                if not sending:
                    back.append(_rcopy(got, got, send.at[p * 3 + k], recv.at[p * 3 + k], (*chip, c)))
        return out, back

    def start(ins, outs, sems):
        for cp in copies(ins, outs, sems, True)[0]:
            cp.start()

    def wait(ins, outs, sems):
        out, back = copies(ins, outs, sems, False)
        for cp in back:
            cp.wait_recv()
        for cp in out:
            cp.wait_send()

    return _Side(list(ps), [jax.ShapeDtypeStruct(a.shape, a.dtype) for a in ps], {}, [n * 3, n * 3], start, wait)


def _side_specs(side):
    if side is None:
        return [], [], [], [], [], {}
    return (side.arrays, [_ANY] * len(side.arrays), [_ANY] * len(side.out_shapes), side.out_shapes,
            [pltpu.SemaphoreType.DMA((k,)) for k in side.n_sems], side.aliases)


def run_side(side, *, name):
    ni, no = len(side.arrays), len(side.out_shapes)

    def body(*refs):
        ins, outs, sems = refs[:ni], refs[ni:ni + no], refs[ni + no:]
        side.start(ins, outs, sems)
        side.wait(ins, outs, sems)

    ops, in_specs, out_specs, out_shapes, scratch, aliases = _side_specs(side)
    return pl.pallas_call(body, name=name, in_specs=in_specs, out_specs=out_specs, out_shape=out_shapes,
                          input_output_aliases=aliases, scratch_shapes=scratch)(*ops)


def gather_small(small, *, name):
    ns = len(small)

    def body(*refs):
        s_in, s_out = refs[:ns], refs[ns:2 * ns]
        send, recv, lsem = refs[2 * ns:]
        x, y, c = _place()
        me = 2 * x + y
        chips, cidx = _other_chips(x, y)
        local = [pltpu.make_async_copy(s_in[q], s_out[q].at[me], lsem.at[q]) for q in range(ns)]
        cps = [_rcopy(s_in[q], s_out[q].at[me], send.at[q * 3 + k], recv.at[q * 3 + k], (*chip, c))
               for k, chip in enumerate(chips) for q in range(ns)]
        for cp in local + cps:
            cp.start()
        for k in range(3):
            for q in range(ns):
                got = s_out[q].at[cidx[k]]
                _rcopy(got, got, send.at[q * 3 + k], recv.at[q * 3 + k], (x, y, c)).wait_recv()
        for cp in cps:
            cp.wait_send()
        for cp in local:
            cp.wait()

    return pl.pallas_call(
        body, name=name, in_specs=[_ANY] * ns, out_specs=[_ANY] * ns,
        out_shape=[jax.ShapeDtypeStruct((4,) + a.shape, a.dtype) for a in small],
        scratch_shapes=[pltpu.SemaphoreType.DMA((ns * 3,))] * 2 + [pltpu.SemaphoreType.DMA((ns,))],
    )(*small)


def sibling_send_half(gs, *, name):
    n = len(gs)

    def body(*refs):
        g_in, g_out, send, recv = refs[:n], refs[n:2 * n], refs[2 * n], refs[2 * n + 1]
        x, y, c = _place()
        sib = (x, y, 1 - c)
        cps = [_rcopy(g_in[p].at[1 - c], g_out[p], send.at[p], recv.at[p], sib) for p in range(n)]
        for cp in cps:
            cp.start()
        for cp in cps:
            cp.wait()

    return pl.pallas_call(
        body, name=name, in_specs=[_ANY] * n, out_specs=[_ANY] * n,
        out_shape=[jax.ShapeDtypeStruct(a.shape[1:], a.dtype) for a in gs],
        scratch_shapes=[pltpu.SemaphoreType.DMA((n,))] * 2,
    )(*gs)


def sum_chips(p, slots, idx, *, name):
    _, N, C = p.shape
    tr = _tile(N, max(16, (512 * 1024) // C // 16 * 16), 16)

    def body(i0, i1, i2, i3, i4, p_ref, s0_ref, s1_ref, s2_ref, o_ref):
        s = p_ref[...].astype(_F32)
        for r in (s0_ref, s1_ref, s2_ref):
            s = s + r[...].astype(_F32)
        o_ref[...] = s

    def at(k):
        return pl.BlockSpec((None, tr, C), lambda i, *ix: (ix[k][0], i, 0))

    return pl.pallas_call(
        body, name=name,
        grid_spec=pltpu.PrefetchScalarGridSpec(
            num_scalar_prefetch=5, grid=(N // tr,),
            in_specs=[at(0), at(1), at(2), at(3)], out_specs=at(4)),
        out_shape=jax.ShapeDtypeStruct((2, N, C), _F32),
        compiler_params=_cparams(("parallel",)),
    )(*idx, p, slots, slots, slots)


def sibling_join(rs, *, name):
    n = len(rs)

    def body(*refs):
        r_out, send, recv = refs[n:2 * n], refs[2 * n], refs[2 * n + 1]
        x, y, c = _place()
        sib = (x, y, 1 - c)
        cps = [_rcopy(r_out[p].at[c], r_out[p].at[c], send.at[p], recv.at[p], sib) for p in range(n)]
        for cp in cps:
            cp.start()
        for p in range(n):
            got = r_out[p].at[1 - c]
            _rcopy(got, got, send.at[p], recv.at[p], sib).wait_recv()
        for cp in cps:
            cp.wait_send()

    return pl.pallas_call(
        body, name=name, in_specs=[_ANY] * n, out_specs=[_ANY] * n,
        out_shape=[jax.ShapeDtypeStruct(a.shape, a.dtype) for a in rs],
        input_output_aliases={p: p for p in range(n)},
        scratch_shapes=[pltpu.SemaphoreType.DMA((n,))] * 2,
    )(*rs)


def all_devices_exchange_side(v):
    def copies(ins, outs, sems):
        x, y, c = _place()
        me = 4 * x + 2 * y + c
        send, recv, lsem = sems
        peers = [(x ^ (k >> 2), y ^ ((k >> 1) & 1), c ^ (k & 1)) for k in range(1, 8)]
        local = pltpu.make_async_copy(ins[0], outs[0].at[me], lsem.at[0])
        out = [_rcopy(ins[0], outs[0].at[me], send.at[k], recv.at[k], peer) for k, peer in enumerate(peers)]
        return local, out, peers

    def start(ins, outs, sems):
        local, out, _ = copies(ins, outs, sems)
        local.start()
        for cp in out:
            cp.start()

    def wait(ins, outs, sems):
        local, out, peers = copies(ins, outs, sems)
        for k, (px, py, pc) in enumerate(peers):
            got = outs[0].at[4 * px + 2 * py + pc]
            _rcopy(got, got, sems[0].at[k], sems[1].at[k], peers[k]).wait_recv()
        for cp in out:
            cp.wait_send()
        local.wait()

    return _Side([v], [jax.ShapeDtypeStruct((8,) + v.shape, v.dtype)], {}, [7, 7, 1], start, wait)


def add_halves(gs_and_rs, c_idx, *, name):
    outs = []
    for n_, (g, r) in enumerate(gs_and_rs):
        N, C = r.shape
        tr = _tile(N, max(16, (512 * 1024) // C // 16 * 16), 16)

        def body(c_ref, g_ref, r_ref, o_ref):
            o_ref[...] = (g_ref[...].astype(_F32) + r_ref[...].astype(_F32)).astype(o_ref.dtype)

        outs.append(pl.pallas_call(
            body, name=f"{name}_{n_}",
            grid_spec=pltpu.PrefetchScalarGridSpec(
                num_scalar_prefetch=1, grid=(N // tr,),
                in_specs=[pl.BlockSpec((None, tr, C), lambda i, c: (c[0], i, 0)), pl.BlockSpec((tr, C), lambda i, c: (i, 0))],
                out_specs=pl.BlockSpec((tr, C), lambda i, c: (i, 0))),
            out_shape=jax.ShapeDtypeStruct((N, C), r.dtype),
            compiler_params=_cparams(("parallel",)),
        )(c_idx, g, r))
    return outs


def sum_slots(a, *, name):
    n, N, C = a.shape
    tr = _tile(N, max(16, (512 * 1024) // C // 16 * 16), 16)

    def body(a_ref, o_ref):
        s = a_ref[0].astype(_F32)
        for k in range(1, n):
            s = s + a_ref[k].astype(_F32)
        o_ref[...] = s

    return pl.pallas_call(
        body, name=name, grid=(N // tr,),
        in_specs=[pl.BlockSpec((n, tr, C), lambda i: (0, i, 0))],
        out_specs=pl.BlockSpec((tr, C), lambda i: (i, 0)),
        out_shape=jax.ShapeDtypeStruct((N, C), _F32),
        compiler_params=_cparams(("parallel",)),
    )(a)


_WEIGHTS = ['ln_in_g', 'ln_in_b', 'w_in', 'q_norm_g', 'w_uq', 'kv_norm_g', 'w_ukv', 'conv_w', 'conv_b', 'conv_ln_g',
            'conv_ln_b', 'w_pool', 'pool_scale', 'w_out', 'ln1_g', 'ln1_b', 'w_up', 'ffn_conv_w', 'ffn_conv_b', 'w_down',
            'ln2_g', 'ln2_b']
_BIG = ['w_in', 'w_uq', 'w_ukv', 'w_out', 'w_up', 'w_down']
_GRADS_EARLY = ('w_up', 'w_down')
_GRADS_LATE = tuple(n for n in _BIG if n not in _GRADS_EARLY)
_SMALL_SHARDED = ['conv_w', 'ffn_conv_w']
_SMALL = [n for n in _WEIGHTS if n not in _BIG]


def _rope_tables(positions):
    half = QK_ROPE // 2
    inv = 1.0 / (ROPE_THETA ** (jnp.arange(0, QK_ROPE, 2, dtype=_F32) / QK_ROPE))
    ang = positions.reshape(-1).astype(_F32)[:, None] * inv
    c, s = jnp.cos(ang), jnp.sin(ang)
    z = jnp.zeros_like(c)
    cc = jnp.concatenate([c, c, z, z], axis=1)
    sa = jnp.concatenate([-s, z, z, z], axis=1)
    sb = jnp.concatenate([z, s, z, z], axis=1)
    assert cc.shape[1] == 128 and half == 32
    return cc, sa, sb


def _in_pad(dims):
    D, QL, KVL, CW, PW, H, F = dims
    return (-(QL + 2 * CW + PW + KVL + 128)) % 512


def _layer_weights(full, dims):
    D, QL, KVL, CW, PW, H, F = dims
    w_in = full['w_in'].transpose(1, 0, 2).reshape(D, -1)
    o1, o2, o3, o4 = QL, QL + KVL, QL + KVL + QK_ROPE, QL + KVL + QK_ROPE + 2 * CW
    w_in_p = jnp.concatenate([w_in[:, :o1], w_in[:, o3:o4], w_in[:, o4:], w_in[:, o1:o2], w_in[:, o2:o3],
                              jnp.zeros((D, 128 - QK_ROPE + _in_pad(dims)), w_in.dtype)], axis=1)
    w_uq = full['w_uq'].reshape(QL, H, QK_NOPE + QK_ROPE)
    w_uq_p = jnp.pad(w_uq, ((0, 0), (0, 0), (0, HEAD_PAD - QK_NOPE - QK_ROPE))).reshape(QL, H * HEAD_PAD)
    return dict(
        w_in=w_in_p, w_uq=w_uq_p,
        w_ukv=full['w_ukv'].reshape(KVL, H * (QK_NOPE + V_HEAD)),
        w_out=full['w_out'].reshape(D, D),
        w_up=full['w_up'].transpose(1, 0, 2).reshape(D, 2 * F),
        w_down=full['w_down'].reshape(F, D),
    )


def _unpermute_w_in_grad(g, dims):
    D, QL, KVL, CW, PW, H, F = dims
    a, b_, c_ = QL, QL + 2 * CW, QL + 2 * CW + PW
    return jnp.concatenate([g[:, :a], g[:, c_:c_ + KVL], g[:, c_ + KVL:c_ + KVL + QK_ROPE], g[:, a:b_], g[:, b_:c_]], axis=1)


class _NoExchange:
    def __init__(self, layer_weights):
        self.layer_weights, self.grads = layer_weights, {}

    def weights(self, l):
        return self.layer_weights[l]

    def side(self, where, l):
        return None

    def side_done(self, where, l, outs):
        pass

    def grads_ready(self, l, g):
        self.grads.setdefault(l, {}).update(g)


def _with_side(hooks, where, l, fn):
    sd = hooks.side(where, l)
    res = fn(sd)
    if sd is None:
        return res
    hooks.side_done(where, l, res[-1])
    return res[0] if len(res) == 2 else res[:-1]


def _local_step(x, positions, target, small, dims, B, S, L, hooks):
    D, QL, KVL, CW, PW, H, F = dims
    T = B * S
    alpha = (2.0 * L) ** 0.25
    scale = float(QK_NOPE + QK_ROPE) ** -0.5
    cc, sa, sb = _rope_tables(positions)
    cb_q, cb_a, cb_g, cb_p = 0, QL // CW, QL // CW + 1, (QL + 2 * CW) // PW
    cb_kv, cb_kr = (QL + 2 * CW + PW) // KVL, (QL + 2 * CW + PW + KVL) // 128
    assert QL % CW == 0 and (QL + 2 * CW) % PW == 0 and (QL + 2 * CW + PW) % KVL == 0 and (QL + 2 * CW + PW + KVL) % 128 == 0

    xs, xb = ln_fwd([x], [1.0], small['ln_in_g'], small['ln_in_b'], want_r=False, name="ln_in")
    saved = []
    fa = dict(B=B, S=S, H=H, scale=scale)
    for l in range(L):
        W = hooks.weights(l)
        h = matmul(xb, W['w_in'], name="mm_in")
        qn = rms_fwd(h, cb_q, QL, small['q_norm_g'][l], name="rms_q")
        kvn = rms_fwd(h, cb_kv, KVL, small['kv_norm_g'][l], name="rms_kv")
        q = matmul(qn, W['w_uq'], name="mm_uq")
        kv = matmul(kvn, W['w_ukv'], name="mm_ukv")
        qp, kp, v = mla_pack(q, kv, h, cb_kr, cc, sa, sb, H=H, scale=scale, name="mla_pack")
        o, lse = _with_side(hooks, 'flash', l, lambda sd: flash_fwd(qp, kp, v, side=sd, name="flash_fwd", **fa))
        z, yc = conv_fwd(h, cb_a, cb_g, small['conv_w'][l], small['conv_b'][l], small['conv_ln_g'][l],
                         small['conv_ln_b'][l], B=B, S=S, name="conv_fwd")
        yp = pool_fwd(h, cb_p, small['w_pool'][l], small['pool_scale'][l], B=B, S=S, name="pool_fwd")
        mixed = jnp.concatenate([o.astype(_BF), yc, yp], axis=1)
        y1 = matmul(mixed, W['w_out'], name="mm_out")
        r1, x1, x1b = ln_fwd([xs, y1], [alpha, 1.0], small['ln1_g'][l], small['ln1_b'][l], want_r=True, name="ln1")
        up = _with_side(hooks, 'up', l, lambda sd: matmul(x1b, W['w_up'], out_dtype=_BF, side=sd, name="mm_up"))
        act = gate_fwd(up, small['ffn_conv_w'][l], small['ffn_conv_b'][l], B=B, S=S, name="gate_fwd")
        y2 = _with_side(hooks, 'down', l, lambda sd: matmul(act, W['w_down'], side=sd, name="mm_down"))
        r2, x2, x2b = ln_fwd([x1, y2], [alpha, 1.0], small['ln2_g'][l], small['ln2_b'][l], want_r=True, name="ln2")
        saved.append(dict(W=W, xb=xb, h=h, qn=qn, kvn=kvn, qp=qp, kp=kp, v=v, o=o, lse=lse, z=z, mixed=mixed, r1=r1,
                          x1b=x1b, up=up, act=act, r2=r2))
        xs, xb = x2, x2b

    dy, loss_cols = loss_head(xs, target, name="loss_head")
    gs = {n: [None] * L for n in _SMALL if n not in ('ln_in_g', 'ln_in_b')}
    d_terms, d_coefs = [dy], [1.0]
    zpad = jnp.zeros((T, _in_pad(dims)), _BF) if _in_pad(dims) else None
    for l in reversed(range(L)):
        sv = saved[l]
        W = sv['W']
        gb = {}
        dr2, dr2b, gs['ln2_g'][l], gs['ln2_b'][l] = ln_bwd(d_terms, d_coefs, sv['r2'], small['ln2_g'][l], name="ln2_bwd")
        dact = matmul(dr2b, W['w_down'], tb=True, out_dtype=_BF, name="mm_down_dx")
        gb['w_down'] = matmul(sv['act'], dr2b, ta=True, out_dtype=_BF, name="mm_down_dw")
        dup, dwa, dwg, dba, dbg = gate_bwd(sv['up'], dact, small['ffn_conv_w'][l], small['ffn_conv_b'][l],
                                           B=B, S=S, name="gate_bwd")
        gs['ffn_conv_w'][l] = jnp.concatenate([dwa, dwg], axis=1)
        gs['ffn_conv_b'][l] = jnp.concatenate([dba, dbg], axis=1)
        gb['w_up'] = _with_side(hooks, 'up_dw', l, lambda sd: matmul(sv['x1b'], dup, ta=True, b_halves=True, out_dtype=_BF,
                                                                     side=sd, name="mm_up_dw"))
        dx1 = _with_side(hooks, 'up_dx', l, lambda sd: matmul(dup, W['w_up'], tb=True, a_halves=True, side=sd,
                                                              name="mm_up_dx"))
        hooks.grads_ready(l, {n: gb.pop(n) for n in _GRADS_EARLY})
        dr1, dr1b, gs['ln1_g'][l], gs['ln1_b'][l] = ln_bwd([dr2, dx1], [alpha, 1.0], sv['r1'], small['ln1_g'][l],
                                                            name="ln1_bwd")
        dmix = matmul(dr1b, W['w_out'], tb=True, name="mm_out_dx")
        gb['w_out'] = matmul(sv['mixed'], dr1b, ta=True, out_dtype=_BF, name="mm_out_dw")
        h = sv['h']
        ncb = (H * V_HEAD) // CW
        dca, dcg, gs['conv_w'][l], gs['conv_b'][l], gs['conv_ln_g'][l], gs['conv_ln_b'][l] = conv_bwd(
            dmix, ncb, sv['z'], h, cb_a, cb_g, small['conv_w'][l], small['conv_ln_g'][l], small['conv_ln_b'][l],
            B=B, S=S, name="conv_bwd")
        dpool, gs['w_pool'][l], gs['pool_scale'][l] = pool_bwd(
            dmix, (H * V_HEAD + CW) // PW, h, cb_p, small['w_pool'][l], small['pool_scale'][l], B=B, S=S, name="pool_bwd")
        att = (sv['qp'], sv['kp'], sv['v'], sv['o'], sv['lse'], dmix, 0)
        dqp = _with_side(hooks, 'flash_dq', l, lambda sd: flash_bwd_dq(*att, side=sd, name="flash_dq", **fa))
        dkp, dv = _with_side(hooks, 'flash_dkv', l, lambda sd: flash_bwd_dkv(*att, side=sd, name="flash_dkv", **fa))
        dq, dkv, dkr = mla_unpack(dqp, dkp, dv, cc, sa, sb, H=H, name="mla_unpack")
        dqn = matmul(dq, W['w_uq'], tb=True, name="mm_uq_dx")
        g_uq = matmul(sv['qn'], dq, ta=True, out_dtype=_BF, name="mm_uq_dw")
        gb['w_uq'] = g_uq.reshape(QL, H, HEAD_PAD)[:, :, :QK_NOPE + QK_ROPE].reshape(QL, -1)
        dkvn = matmul(dkv, W['w_ukv'], tb=True, name="mm_ukv_dx")
        gb['w_ukv'] = matmul(sv['kvn'], dkv, ta=True, out_dtype=_BF, name="mm_ukv_dw")
        dcq, gs['q_norm_g'][l] = rms_bwd(dqn, h, cb_q, QL, small['q_norm_g'][l], name="rms_q_bwd")
        dckv, gs['kv_norm_g'][l] = rms_bwd(dkvn, h, cb_kv, KVL, small['kv_norm_g'][l], name="rms_kv_bwd")
        dh = jnp.concatenate([dcq, dca, dcg, dpool, dckv, dkr] + ([zpad] if zpad is not None else []), axis=1)
        gb['w_in'] = _unpermute_w_in_grad(matmul(sv['xb'], dh, ta=True, out_dtype=_BF, name="mm_in_dw"), dims)
        dxm = matmul(dh, W['w_in'], tb=True, name="mm_in_dx")
        hooks.grads_ready(l, gb)
        d_terms, d_coefs = [dr1, dxm], [alpha, 1.0]
    gx, _, g_ln_g, g_ln_b = ln_bwd(d_terms, d_coefs, x, small['ln_in_g'], name="ln_in_bwd")
    gsm = {n: jnp.stack([a.reshape(small[n].shape[1:]) for a in gs[n]]) for n in gs}
    gsm['ln_in_g'], gsm['ln_in_b'] = g_ln_g.reshape(-1), g_ln_b.reshape(-1)
    return loss_cols, gx, gsm


_COL_SHARDED = ('w_in', 'w_up')


def _flat_pad(arrs, mult=512 * 128):
    v = jnp.concatenate([a.reshape(-1) for a in arrs])
    n = v.shape[0]
    return jnp.pad(v, (0, (-n) % mult)).reshape(-1, 128)


def _split_like(flat, like):
    out, off = [], 0
    v = flat.reshape(-1)
    for a in like:
        out.append(v[off:off + a.size].reshape(a.shape))
        off += a.size
    return out


_GROUP_A = ('w_up',)
_GROUP_B = tuple(n for n in _BIG if n not in _GROUP_A)


class _Exchange:
    def __init__(self, a, dims, L, chip_idx, c_idx, sum_idx):
        self.dims, self.L, self.c_idx, self.sum_idx = dims, L, c_idx, sum_idx
        self.rows = {n: (a[n].shape[1], math.prod(a[n].shape[2:])) for n in _BIG}
        self.bufs = []
        for l in range(L):
            self.bufs.append({n: place_shard(a[n].reshape(L, 2, self.rows[n][0] // 2, self.rows[n][1]), l, chip_idx,
                                             name="place_" + n) for n in _BIG})
        self._store(0, _BIG, run_side(gather_stage1_side(self._list(0, _BIG)), name="gather_first_ici"))
        self._store(0, _BIG, run_side(gather_stage2_side(self._list(0, _BIG)), name="gather_first_d2d"))
        self.pending = {}
        self.reduced = {}

    def _list(self, l, names):
        return [self.bufs[l][n] for n in names]

    def _store(self, l, names, outs):
        self.bufs[l].update(zip(names, outs))

    def weights(self, l):
        return _layer_weights({n: self.bufs[l][n].reshape(4, *self.rows[n]) for n in _BIG}, self.dims)

    def side(self, where, l):
        if where in ('flash', 'up', 'down'):
            if l + 1 >= self.L:
                return None
            if where == 'down':
                return gather_stage2_side(self._list(l + 1, _BIG))
            return gather_stage1_side(self._list(l + 1, _GROUP_B if where == 'flash' else _GROUP_A))
        tag, names = self._riders(where)
        if tag not in self.pending:
            return None
        return chip_exchange_side([self.pending[tag][1][n] for n in names])

    @staticmethod
    def _riders(where):
        return {'flash_dq': ('early', ('w_down',)), 'flash_dkv': ('early', ('w_up',)), 'up_dw': ('late', _GRADS_LATE),
                'up_dx': ('none', ())}[where]

    def side_done(self, where, l, outs):
        if where in ('flash', 'up', 'down'):
            self._store(l + 1, {'flash': _GROUP_B, 'up': _GROUP_A, 'down': _BIG}[where], outs)
            return
        tag, names = self._riders(where)
        self.pending[tag][2].update(zip(names, outs))
        if len(self.pending[tag][2]) == len(self.pending[tag][1]):
            self._finish(tag)

    def _finish(self, tag):
        l, sums, slots = self.pending.pop(tag)
        halves = [sum_chips(sums[n], slots[n], self.sum_idx, name="grad_sum_" + n) for n in sums]
        joined = sibling_join(halves, name="grad_sibling_join_" + tag)
        self.reduced.setdefault(l, {}).update({n: j.reshape(self.rows[n]) for n, j in zip(sums, joined)})

    def grads_ready(self, l, g):
        names = tuple(n for n in _BIG if n in g)
        tag = 'early' if names == tuple(n for n in _BIG if n in _GRADS_EARLY) else 'late'
        g_in = []
        for n in names:
            r, c = self.rows[n]
            st = g[n].reshape(g[n].shape[0], 4, c).transpose(1, 0, 2) if n in _COL_SHARDED else g[n].reshape(4, r, c)
            g_in.append(st.reshape(4, 2, r // 2, c).transpose(1, 0, 2, 3).reshape(2, 2 * r, c))
        from_sib = sibling_send_half(g_in, name="grad_sibling_send_" + tag)
        sums = add_halves(list(zip(g_in, from_sib)), self.c_idx, name="grad_presum_" + tag)
        self.pending[tag] = (l, {n: p.reshape(4, p.shape[0] // 4, p.shape[1]) for n, p in zip(names, sums)}, {})

    def last_side(self):
        return chip_exchange_side(list(self.pending['late'][1].values()))

    def last_done(self, outs):
        self.pending['late'][2].update(zip(self.pending['late'][1], outs))
        self._finish('late')


def kernel(x, positions, ln_in_g, ln_in_b, w_in, q_norm_g, w_uq, kv_norm_g, w_ukv, conv_w, conv_b, conv_ln_g, conv_ln_b, w_pool, pool_scale, w_out, ln1_g, ln1_b, w_up, ffn_conv_w, ffn_conv_b, w_down, ln2_g, ln2_b, loss_target, m_ln_in_g, m_ln_in_b, m_w_in, m_q_norm_g, m_w_uq, m_kv_norm_g, m_w_ukv, m_conv_w, m_conv_b, m_conv_ln_g, m_conv_ln_b, m_w_pool, m_pool_scale, m_w_out, m_ln1_g, m_ln1_b, m_w_up, m_ffn_conv_w, m_ffn_conv_b, m_w_down, m_ln2_g, m_ln2_b, v_ln_in_g, v_ln_in_b, v_w_in, v_q_norm_g, v_w_uq, v_kv_norm_g, v_w_ukv, v_conv_w, v_conv_b, v_conv_ln_g, v_conv_ln_b, v_w_pool, v_pool_scale, v_w_out, v_ln1_g, v_ln1_b, v_w_up, v_ffn_conv_w, v_ffn_conv_b, v_w_down, v_ln2_g, v_ln2_b):
    a = dict(locals())
    B, S, D = a['x'].shape
    T = B * S
    L = a['w_in'].shape[0]
    QL, H = 4 * a['w_uq'].shape[1], a['w_uq'].shape[2]
    KVL = 4 * a['w_ukv'].shape[1]
    CW, PW = a['conv_b'].shape[1], a['pool_scale'].shape[1]
    F = 4 * a['w_down'].shape[1]
    dims = (D, QL, KVL, CW, PW, H, F)
    chip = 2 * lax.axis_index("x") + lax.axis_index("y")
    c_idx = lax.axis_index("c").astype(jnp.int32).reshape(1)

    def shard2d(w):
        return w.reshape(-1, w.shape[-1]) if w.ndim == 3 else w.reshape(w.shape[0] * w.shape[1], -1)

    small = {n: a[n] for n in _SMALL}
    for n, o in zip(_SMALL_SHARDED, gather_small([shard2d(a[n]) for n in _SMALL_SHARDED], name="gather_small")):
        k = a[n].shape[1]
        small[n] = o.reshape(4, L, k, -1).transpose(1, 2, 0, 3).reshape(L, k, -1)
    xi, yi = lax.axis_index("x"), lax.axis_index("y")
    chip_idx = chip.astype(jnp.int32).reshape(1)
    sum_idx = [v.astype(jnp.int32).reshape(1) for v in [chip] + _other_chips(xi, yi)[1] + [lax.axis_index("c")]]
    ex = _Exchange(a, dims, L, chip_idx, c_idx, sum_idx)

    loss_cols, gx, gsm = _local_step(a['x'].reshape(T, D), a['positions'], a['loss_target'].reshape(T, D),
                                     small, dims, B, S, L, ex)
    loss = lax.psum(jnp.sum(loss_cols), ("x", "y", "c"))

    grads, delta, new_m, new_v = {}, {}, {}, {}
    sm_like = [gsm[n] for n in _SMALL]
    riders = {'w_up': all_devices_exchange_side(_flat_pad(sm_like)), 'w_down': ex.last_side()}
    for n in _GRADS_EARLY + _GRADS_LATE:
        grads[n] = jnp.concatenate([ex.reduced[l][n] for l in range(L)]).reshape(a[n].shape)
        res = adamw(shard2d(a[n]), shard2d(grads[n]), shard2d(a['m_' + n]), shard2d(a['v_' + n]), side=riders.get(n),
                    name="adamw_" + n)
        delta[n], new_m[n], new_v[n] = (t.reshape(a[n].shape) for t in res[:3])
        if n == 'w_up':
            sm_slots = res[3][0]
        elif n == 'w_down':
            ex.last_done(res[3])

    g_small = dict(zip(_SMALL, _split_like(sum_slots(sm_slots, name="small_sum"), sm_like)))
    for n in _SMALL_SHARDED:
        w = a[n].shape[-1]
        g_small[n] = lax.dynamic_slice_in_dim(g_small[n], chip * w, w, axis=2)

    def at_least_2d(t):
        return t.reshape(1, -1) if t.ndim == 1 else t

    d_, m_, v_ = adamw_small(*[[at_least_2d(src[n]) for n in _SMALL] for src in (
        a, g_small, {n: a['m_' + n] for n in _SMALL}, {n: a['v_' + n] for n in _SMALL})], name="adamw_small")
    for n, dd, mm, vv in zip(_SMALL, d_, m_, v_):
        grads[n], delta[n], new_m[n], new_v[n] = g_small[n], dd.reshape(a[n].shape), mm.reshape(a[n].shape), vv.reshape(a[n].shape)

    return (loss, gx.reshape(B, S, D), *[grads[n] for n in _WEIGHTS], *[delta[n] for n in _WEIGHTS],
            *[new_m[n] for n in _WEIGHTS], *[new_v[n] for n in _WEIGHTS])
```

```python
import functools
import math

import jax
import jax.numpy as jnp
from jax import lax
from jax.experimental import pallas as pl
from jax.experimental.pallas import tpu as pltpu

_BF = jnp.bfloat16
_F32 = jnp.float32
_VMEM_LIMIT = 56 * 1024 * 1024

QK_NOPE = 128
QK_ROPE = 64
V_HEAD = 128
HEAD_PAD = 256
ROPE_THETA = 10000.0
LN_EPS = 1e-5
RMS_EPS = 1e-6
POOL_WINDOWS = (2, 4, 8, 16)
ADAM_LR, ADAM_B1, ADAM_B2, ADAM_EPS, ADAM_WD, ADAM_STEP = 0.001, 0.9, 0.999, 1e-8, 0.01, 10


def _cparams(sem=None):
    kw = dict(vmem_limit_bytes=_VMEM_LIMIT)
    if sem is not None:
        kw["dimension_semantics"] = sem
    return pltpu.CompilerParams(**kw)


def _tile(n, target, unit=128):
    if n <= target:
        return n
    t = (target // unit) * unit
    while t >= unit:
        if n % t == 0:
            return t
        t -= unit
    return n


_MM_VMEM_BUDGET = 40 * 1024 * 1024


def matmul(a, b, *, ta=False, tb=False, out_dtype=_F32, tm=1024, tn=1536, tk=4096, side=None, a_halves=False,
           b_nparts=0, b_kparts=0, out_nparts=0, name="mm"):
    assert not (a_halves and ta) and not (b_nparts and tb) and not (b_kparts and not tb)
    if a_halves:
        M, K = a.shape[1], 2 * a.shape[2]
    elif ta:
        K, M = a.shape
    else:
        M, K = a.shape
    if b_nparts:
        K2, N = b.shape[1], b_nparts * b.shape[2]
    elif b_kparts:
        N, K2 = b.shape[1], b_kparts * b.shape[2]
    elif tb:
        N, K2 = b.shape
    else:
        K2, N = b.shape
    assert K == K2, (a.shape, b.shape, ta, tb)
    n_part, k_part = N // max(b_nparts, out_nparts, 1), K // max(2 if a_halves else 1, b_kparts, 1)
    tm, tn, tk = _tile(M, tm), _tile(n_part, tn), _tile(k_part, tk)
    ab, bb, ob = a.dtype.itemsize, b.dtype.itemsize, jnp.dtype(out_dtype).itemsize

    def vmem(tk_):
        return 2 * (tm * tk_ * ab + tk_ * tn * bb) + 2 * tm * tn * ob + tm * tn * 4 * (2 if K // tk_ > 1 else 1)

    while vmem(tk) > _MM_VMEM_BUDGET and tk > 256 and _tile(k_part, tk // 2) < tk:
        tk = _tile(k_part, tk // 2)
    nk = K // tk
    dn = (((0,) if ta else (1,), (1,) if tb else (0,)), ((), ()))

    s_ops, s_in, s_out, s_shapes, s_scratch, s_alias = _side_specs(side)
    ni, no, nacc = len(s_ops), len(s_shapes), int(nk > 1)
    grid = (M // tm, N // tn, nk)

    def body(a_ref, b_ref, *rest):
        s_ins, o_ref, s_outs = rest[:ni], rest[ni], rest[ni + 1:ni + 1 + no]
        acc, sems = rest[ni + 1 + no:ni + 1 + no + nacc], rest[ni + 1 + no + nacc:]
        i, j, k = pl.program_id(0), pl.program_id(1), pl.program_id(2)
        if side is not None:
            @pl.when((i == 0) & (j == 0) & (k == 0))
            def _():
                side.start(s_ins, s_outs, sems)

        prod = lax.dot_general(a_ref[...].astype(_BF), b_ref[...].astype(_BF), dn, preferred_element_type=_F32)
        if nk == 1:
            o_ref[...] = prod.astype(o_ref.dtype)
        else:
            acc_ref = acc[0]

            @pl.when(k == 0)
            def _():
                acc_ref[...] = prod

            @pl.when(k > 0)
            def _():
                acc_ref[...] += prod

            @pl.when(k == nk - 1)
            def _():
                o_ref[...] = acc_ref[...].astype(o_ref.dtype)

        if side is not None:
            @pl.when((i == grid[0] - 1) & (j == grid[1] - 1) & (k == nk - 1))
            def _():
                side.wait(s_ins, s_outs, sems)

    a_spec = pl.BlockSpec((tk, tm), lambda i, j, k: (k, i)) if ta else pl.BlockSpec((tm, tk), lambda i, j, k: (i, k))
    b_spec = pl.BlockSpec((tn, tk), lambda i, j, k: (j, k)) if tb else pl.BlockSpec((tk, tn), lambda i, j, k: (k, j))
    if a_halves:
        kh = nk // 2
        a_spec = pl.BlockSpec((None, tm, tk), lambda i, j, k: (k // kh, i, k % kh))
    if b_nparts:
        jb = N // tn // b_nparts
        b_spec = pl.BlockSpec((None, tk, tn), lambda i, j, k: (j // jb, k, j % jb))
    if b_kparts:
        kb = nk // b_kparts
        b_spec = pl.BlockSpec((None, tn, tk), lambda i, j, k: (k // kb, j, k % kb))
    o_spec, o_shape = pl.BlockSpec((tm, tn), lambda i, j, k: (i, j)), (M, N)
    if out_nparts:
        jo = N // tn // out_nparts
        o_spec, o_shape = pl.BlockSpec((None, tm, tn), lambda i, j, k: (j // jo, i, j % jo)), (out_nparts, M, N // out_nparts)
    res = pl.pallas_call(
        body, name=name,
        grid=grid,
        in_specs=[a_spec, b_spec] + s_in,
        out_specs=[o_spec] + s_out,
        out_shape=[jax.ShapeDtypeStruct(o_shape, out_dtype)] + s_shapes,
        input_output_aliases={2 + i_: 1 + o_ for i_, o_ in s_alias.items()},
        scratch_shapes=([pltpu.VMEM((tm, tn), _F32)] if nk > 1 else []) + s_scratch,
        compiler_params=_cparams(("arbitrary",) * 3 if side is not None else ("parallel", "parallel", "arbitrary")),
    )(a, b, *s_ops)
    return res[0] if side is None else (res[0], list(res[1:]))


def _row_tile(T, C, budget_rows=256):
    return _tile(T, budget_rows, 16)


def ln_fwd(xs, coefs, g, b, *, want_r, name):
    T, C = xs[0].shape
    tr = _row_tile(T, C)
    n = len(xs)

    def body(*refs):
        x_refs, (g_ref, b_ref), outs = refs[:n], refs[n:n + 2], refs[n + 2:]
        r = coefs[0] * x_refs[0][...]
        for c, xr in zip(coefs[1:], x_refs[1:]):
            r = r + c * xr[...]
        mu = jnp.mean(r, axis=-1, keepdims=True)
        d = r - mu
        var = jnp.mean(d * d, axis=-1, keepdims=True)
        y = d * lax.rsqrt(var + LN_EPS) * g_ref[...] + b_ref[...]
        if want_r:
            outs[0][...] = r
        outs[-2][...] = y
        outs[-1][...] = y.astype(_BF)

    row = pl.BlockSpec((tr, C), lambda i: (i, 0))
    vec = pl.BlockSpec((1, C), lambda i: (0, 0))
    f = jax.ShapeDtypeStruct((T, C), _F32)
    out_shape = ([f] if want_r else []) + [f, jax.ShapeDtypeStruct((T, C), _BF)]
    return pl.pallas_call(
        body, name=name, grid=(T // tr,),
        in_specs=[row] * n + [vec, vec],
        out_specs=[row] * len(out_shape), out_shape=out_shape,
        compiler_params=_cparams(("parallel",)),
    )(*xs, g.reshape(1, C), b.reshape(1, C))


def ln_bwd(dys, coefs, r, g, *, name):
    T, C = r.shape
    tr = _row_tile(T, C)
    n = len(dys)

    def body(*refs):
        dy_refs, r_ref, g_ref = refs[:n], refs[n], refs[n + 1]
        dr_ref, drb_ref, dg_ref, db_ref = refs[n + 2:]
        dy = coefs[0] * dy_refs[0][...]
        for c, dr_ in zip(coefs[1:], dy_refs[1:]):
            dy = dy + c * dr_[...]
        rr = r_ref[...]
        mu = jnp.mean(rr, axis=-1, keepdims=True)
        d = rr - mu
        var = jnp.mean(d * d, axis=-1, keepdims=True)
        rstd = lax.rsqrt(var + LN_EPS)
        xh = d * rstd
        gdy = dy * g_ref[...]
        m1 = jnp.mean(gdy, axis=-1, keepdims=True)
        m2 = jnp.mean(gdy * xh, axis=-1, keepdims=True)
        dr = rstd * (gdy - m1 - xh * m2)
        dr_ref[...] = dr
        drb_ref[...] = dr.astype(_BF)

        @pl.when(pl.program_id(0) == 0)
        def _():
            dg_ref[...] = jnp.zeros_like(dg_ref)
            db_ref[...] = jnp.zeros_like(db_ref)

        dg_ref[...] += jnp.sum(dy * xh, axis=0, keepdims=True)
        db_ref[...] += jnp.sum(dy, axis=0, keepdims=True)

    row = pl.BlockSpec((tr, C), lambda i: (i, 0))
    vec = pl.BlockSpec((1, C), lambda i: (0, 0))
    return pl.pallas_call(
        body, name=name, grid=(T // tr,),
        in_specs=[row] * (n + 1) + [vec],
        out_specs=[row, row, vec, vec],
        out_shape=[jax.ShapeDtypeStruct((T, C), _F32), jax.ShapeDtypeStruct((T, C), _BF),
                   jax.ShapeDtypeStruct((1, C), _F32), jax.ShapeDtypeStruct((1, C), _F32)],
        compiler_params=_cparams(("arbitrary",)),
    )(*dys, r, g.reshape(1, C))


def rms_fwd(h, cb, W, g, *, name):
    T = h.shape[0]
    tr = _tile(T, 512, 16)

    def body(c_ref, g_ref, o_ref):
        c = c_ref[...]
        ms = jnp.mean(c * c, axis=-1, keepdims=True)
        o_ref[...] = (c * lax.rsqrt(ms + RMS_EPS) * g_ref[...]).astype(_BF)

    return pl.pallas_call(
        body, name=name, grid=(T // tr,),
        in_specs=[pl.BlockSpec((tr, W), lambda i: (i, cb)), pl.BlockSpec((1, W), lambda i: (0, 0))],
        out_specs=pl.BlockSpec((tr, W), lambda i: (i, 0)),
        out_shape=jax.ShapeDtypeStruct((T, W), _BF),
        compiler_params=_cparams(("parallel",)),
    )(h, g.reshape(1, W))


def rms_bwd(dy, h, cb, W, g, *, name):
    T = h.shape[0]
    tr = _tile(T, 512, 16)

    def body(dy_ref, c_ref, g_ref, dc_ref, dg_ref):
        c = c_ref[...]
        dyv = dy_ref[...]
        ms = jnp.mean(c * c, axis=-1, keepdims=True)
        r = lax.rsqrt(ms + RMS_EPS)
        u = dyv * g_ref[...]
        m = jnp.mean(c * u, axis=-1, keepdims=True)
        dc_ref[...] = (r * u - c * (r * r * r) * m).astype(_BF)

        @pl.when(pl.program_id(0) == 0)
        def _():
            dg_ref[...] = jnp.zeros_like(dg_ref)

        dg_ref[...] += jnp.sum(dyv * c * r, axis=0, keepdims=True)

    return pl.pallas_call(
        body, name=name, grid=(T // tr,),
        in_specs=[pl.BlockSpec((tr, W), lambda i: (i, 0)), pl.BlockSpec((tr, W), lambda i: (i, cb)),
                  pl.BlockSpec((1, W), lambda i: (0, 0))],
        out_specs=[pl.BlockSpec((tr, W), lambda i: (i, 0)), pl.BlockSpec((1, W), lambda i: (0, 0))],
        out_shape=[jax.ShapeDtypeStruct((T, W), _BF), jax.ShapeDtypeStruct((1, W), _F32)],
        compiler_params=_cparams(("arbitrary",)),
    )(dy, h, g.reshape(1, W))


def _rope(u, cc, sa, sb, sign):
    return u * cc + sign * (pltpu.roll(u, 96, 1) * sa + pltpu.roll(u, 32, 1) * sb)


def mla_pack(q, kv, h, kr_cb, cc, sa, sb, *, H, scale, name):
    T = q.shape[0]
    tr = _tile(T, 256, 16)

    def body(q_ref, kv_ref, kr_ref, cc_ref, sa_ref, sb_ref, qp_ref, kp_ref, v_ref):
        cc_, sa_, sb_ = cc_ref[...], sa_ref[...], sb_ref[...]
        kr = _rope(kr_ref[...], cc_, sa_, sb_, 1.0).astype(_BF)
        for hh in range(H):
            o = hh * HEAD_PAD
            qp_ref[:, o:o + 128] = (q_ref[:, o:o + 128] * scale).astype(_BF)
            qp_ref[:, o + 128:o + 256] = (_rope(q_ref[:, o + 128:o + 256], cc_, sa_, sb_, 1.0) * scale).astype(_BF)
            kp_ref[:, o:o + 128] = kv_ref[:, o:o + 128].astype(_BF)
            kp_ref[:, o + 128:o + 256] = kr
            v_ref[:, hh * 128:(hh + 1) * 128] = kv_ref[:, o + 128:o + 256].astype(_BF)

    wide = pl.BlockSpec((tr, H * HEAD_PAD), lambda i: (i, 0))
    tab = pl.BlockSpec((tr, 128), lambda i: (i, 0))
    return pl.pallas_call(
        body, name=name, grid=(T // tr,),
        in_specs=[wide, wide, pl.BlockSpec((tr, 128), lambda i: (i, kr_cb)), tab, tab, tab],
        out_specs=[wide, wide, pl.BlockSpec((tr, H * 128), lambda i: (i, 0))],
        out_shape=[jax.ShapeDtypeStruct((T, H * HEAD_PAD), _BF), jax.ShapeDtypeStruct((T, H * HEAD_PAD), _BF),
                   jax.ShapeDtypeStruct((T, H * 128), _BF)],
        compiler_params=_cparams(("parallel",)),
    )(q, kv, h, cc, sa, sb)


def mla_unpack(dqp, dkp, dv, cc, sa, sb, *, H, name):
    T = dqp.shape[0]
    tr = _tile(T, 256, 16)

    def body(dq_ref, dk_ref, dv_ref, cc_ref, sa_ref, sb_ref, oq_ref, okv_ref, okr_ref):
        cc_, sa_, sb_ = cc_ref[...], sa_ref[...], sb_ref[...]
        kr = jnp.zeros((tr, 128), _F32)
        for hh in range(H):
            o = hh * HEAD_PAD
            oq_ref[:, o:o + 128] = dq_ref[:, o:o + 128].astype(_BF)
            oq_ref[:, o + 128:o + 256] = _rope(dq_ref[:, o + 128:o + 256], cc_, sa_, sb_, -1.0).astype(_BF)
            okv_ref[:, o:o + 128] = dk_ref[:, o:o + 128].astype(_BF)
            okv_ref[:, o + 128:o + 256] = dv_ref[:, hh * 128:(hh + 1) * 128].astype(_BF)
            kr = kr + dk_ref[:, o + 128:o + 256]
        okr_ref[...] = _rope(kr, cc_, sa_, sb_, -1.0).astype(_BF)

    wide = pl.BlockSpec((tr, H * HEAD_PAD), lambda i: (i, 0))
    tab = pl.BlockSpec((tr, 128), lambda i: (i, 0))
    return pl.pallas_call(
        body, name=name, grid=(T // tr,),
        in_specs=[wide, wide, pl.BlockSpec((tr, H * 128), lambda i: (i, 0)), tab, tab, tab],
        out_specs=[wide, wide, tab],
        out_shape=[jax.ShapeDtypeStruct((T, H * HEAD_PAD), _BF), jax.ShapeDtypeStruct((T, H * HEAD_PAD), _BF),
                   jax.ShapeDtypeStruct((T, 128), _BF)],
        compiler_params=_cparams(("parallel",)),
    )(dqp, dkp, dv, cc, sa, sb)


_NEG = -1e30


def _rows(ref, j, t):
    return ref[pl.ds(pl.multiple_of(j * t, t), t), :]


_FLASH_TILE = 1024


def _qk(q, k):
    return lax.dot_general(q, k, (((1,), (1,)), ((), ())), preferred_element_type=_F32)


def _scores(q, k, t, masked):
    s = _qk(q, k)
    if not masked:
        return s
    row = lax.broadcasted_iota(jnp.int32, (t, t), 0)
    col = lax.broadcasted_iota(jnp.int32, (t, t), 1)
    return jnp.where(col <= row, s, _NEG)


def flash_fwd(qp, kp, v, *, B, S, H, scale, side=None, name):
    T = B * S
    t = _tile(S, _FLASH_TILE, 128)
    nq = S // t
    s_ops, s_in, s_out, s_shapes, s_scratch, s_alias = _side_specs(side)
    ni, no = len(s_ops), len(s_shapes)

    def body(q_ref, k_ref, v_ref, *rest):
        s_ins, (o_ref, lse_ref), s_outs = rest[:ni], rest[ni:ni + 2], rest[ni + 2:ni + 2 + no]
        (m_sc, l_sc, acc_sc), sems = rest[ni + 2 + no:ni + 5 + no], rest[ni + 5 + no:]
        i = pl.program_id(2)
        first = (pl.program_id(0) == 0) & (pl.program_id(1) == 0) & (i == 0)
        last = (pl.program_id(0) == B - 1) & (pl.program_id(1) == H - 1) & (i == nq - 1)
        if side is not None:
            @pl.when(first)
            def _():
                side.start(s_ins, s_outs, sems)

        m_sc[...] = jnp.full_like(m_sc, _NEG)
        l_sc[...] = jnp.zeros_like(l_sc)
        acc_sc[...] = jnp.zeros_like(acc_sc)

        def step(j, masked):
            s = _scores(q_ref[...], _rows(k_ref, j, t), t, masked)
            m_old = m_sc[...]
            m_new = jnp.maximum(m_old, jnp.max(s, axis=-1, keepdims=True))
            p = jnp.exp(s - m_new)
            a = jnp.exp(m_old - m_new)
            l_sc[...] = a * l_sc[...] + jnp.sum(p, axis=-1, keepdims=True)
            acc_sc[...] = a * acc_sc[...] + jnp.dot(p.astype(_BF), _rows(v_ref, j, t), preferred_element_type=_F32)
            m_sc[...] = m_new

        @pl.loop(0, i)
        def _(j):
            step(j, False)

        step(i, True)
        l = l_sc[...]
        o_ref[...] = acc_sc[...] / l
        lse_ref[...] = jnp.broadcast_to(m_sc[...] + jnp.log(l), lse_ref.shape)
        if side is not None:
            @pl.when(last)
            def _():
                side.wait(s_ins, s_outs, sems)

    qmap = lambda b, h, i: (b * nq + i, h)
    smap = lambda b, h, i: (b, h)
    res = pl.pallas_call(
        body, name=name, grid=(B, H, nq),
        in_specs=[pl.BlockSpec((t, HEAD_PAD), qmap), pl.BlockSpec((S, HEAD_PAD), smap), pl.BlockSpec((S, 128), smap)] + s_in,
        out_specs=[pl.BlockSpec((t, 128), qmap), pl.BlockSpec((t, 128), qmap)] + s_out,
        out_shape=[jax.ShapeDtypeStruct((T, H * 128), _F32), jax.ShapeDtypeStruct((T, H * 128), _F32)] + s_shapes,
        input_output_aliases={3 + i_: 2 + o_ for i_, o_ in s_alias.items()},
        scratch_shapes=[pltpu.VMEM((t, 1), _F32), pltpu.VMEM((t, 1), _F32), pltpu.VMEM((t, 128), _F32)] + s_scratch,
        compiler_params=_cparams(("arbitrary",) * 3 if side is not None else ("parallel",) * 3),
    )(qp, kp, v, *s_ops)
    return (res[0], res[1]) if side is None else (res[0], res[1], list(res[2:]))


def _grid_ends(grid):
    ids = [pl.program_id(d) for d in range(len(grid))]
    first, last = ids[0] == 0, ids[0] == grid[0] - 1
    for d in range(1, len(grid)):
        first, last = first & (ids[d] == 0), last & (ids[d] == grid[d] - 1)
    return first, last


def flash_bwd_dq(qp, kp, v, o, lse, do, do_cb0, *, B, S, H, scale, side=None, name):
    T = B * S
    t = _tile(S, _FLASH_TILE, 128)
    nq = S // t
    s_ops, s_in, s_out, s_shapes, s_scratch, s_alias = _side_specs(side)
    ni, no = len(s_ops), len(s_shapes)

    def body(q_ref, k_ref, v_ref, o_ref, lse_ref, do_ref, *rest):
        s_ins, dq_ref, s_outs = rest[:ni], rest[ni], rest[ni + 1:ni + 1 + no]
        (acc_sc, dl_sc), sems = rest[ni + 1 + no:ni + 3 + no], rest[ni + 3 + no:]
        first, last = _grid_ends((B, H, nq))
        if side is not None:
            @pl.when(first)
            def _():
                side.start(s_ins, s_outs, sems)

        i = pl.program_id(2)
        acc_sc[...] = jnp.zeros_like(acc_sc)
        dl_sc[...] = jnp.sum(do_ref[...].astype(_F32) * o_ref[...], axis=-1, keepdims=True)

        def step(j, masked):
            k = _rows(k_ref, j, t)
            s = _scores(q_ref[...], k, t, masked)
            p = jnp.exp(s - lse_ref[:, 0:1])
            dp = _qk(do_ref[...].astype(_BF), _rows(v_ref, j, t))
            ds = p * (dp - dl_sc[...])
            acc_sc[...] += jnp.dot(ds.astype(_BF), k, preferred_element_type=_F32)

        @pl.loop(0, i)
        def _(j):
            step(j, False)

        step(i, True)
        dq_ref[...] = acc_sc[...] * scale
        if side is not None:
            @pl.when(last)
            def _():
                side.wait(s_ins, s_outs, sems)

    qmap = lambda b, h, i: (b * nq + i, h)
    domap = lambda b, h, i: (b * nq + i, do_cb0 + h)
    smap = lambda b, h, i: (b, h)
    res = pl.pallas_call(
        body, name=name, grid=(B, H, nq),
        in_specs=[pl.BlockSpec((t, HEAD_PAD), qmap), pl.BlockSpec((S, HEAD_PAD), smap), pl.BlockSpec((S, 128), smap),
                  pl.BlockSpec((t, 128), qmap), pl.BlockSpec((t, 128), qmap), pl.BlockSpec((t, 128), domap)] + s_in,
        out_specs=[pl.BlockSpec((t, HEAD_PAD), qmap)] + s_out,
        out_shape=[jax.ShapeDtypeStruct((T, H * HEAD_PAD), _F32)] + s_shapes,
        input_output_aliases={6 + i_: 1 + o_ for i_, o_ in s_alias.items()},
        scratch_shapes=[pltpu.VMEM((t, HEAD_PAD), _F32), pltpu.VMEM((t, 1), _F32)] + s_scratch,
        compiler_params=_cparams(("arbitrary",) * 3 if side is not None else ("parallel",) * 3),
    )(qp, kp, v, o, lse, do, *s_ops)
    return res[0] if side is None else (res[0], list(res[1:]))


def flash_bwd_dkv(qp, kp, v, o, lse, do, do_cb0, *, B, S, H, scale, side=None, name):
    T = B * S
    t = _tile(S, _FLASH_TILE, 128)
    nk = S // t
    s_ops, s_in, s_out, s_shapes, s_scratch, s_alias = _side_specs(side)
    ni, no = len(s_ops), len(s_shapes)

    def body(q_ref, k_ref, v_ref, o_ref, lse_ref, do_ref, *rest):
        s_ins, (dk_ref, dv_ref), s_outs = rest[:ni], rest[ni:ni + 2], rest[ni + 2:ni + 2 + no]
        (dk_sc, dv_sc), sems = rest[ni + 2 + no:ni + 4 + no], rest[ni + 4 + no:]
        first, last = _grid_ends((B, H, nk))
        if side is not None:
            @pl.when(first)
            def _():
                side.start(s_ins, s_outs, sems)

        j = pl.program_id(2)
        dk_sc[...] = jnp.zeros_like(dk_sc)
        dv_sc[...] = jnp.zeros_like(dv_sc)

        def step(i, masked):
            q = _rows(q_ref, i, t)
            do = _rows(do_ref, i, t).astype(_F32)
            dob = do.astype(_BF)
            s = _scores(q, k_ref[...], t, masked)
            p = jnp.exp(s - _rows(lse_ref, i, t)[:, 0:1])
            dl = jnp.sum(do * _rows(o_ref, i, t), axis=-1, keepdims=True)
            dp = _qk(dob, v_ref[...])
            ds = p * (dp - dl)
            tn = (((0,), (0,)), ((), ()))
            dv_sc[...] += lax.dot_general(p.astype(_BF), dob, tn, preferred_element_type=_F32)
            dk_sc[...] += lax.dot_general(ds.astype(_BF), q, tn, preferred_element_type=_F32)

        step(j, True)

        @pl.loop(j + 1, nk)
        def _(i):
            step(i, False)

        dk_ref[...] = dk_sc[...]
        dv_ref[...] = dv_sc[...]
        if side is not None:
            @pl.when(last)
            def _():
                side.wait(s_ins, s_outs, sems)

    smap = lambda b, h, j: (b, h)
    domap = lambda b, h, j: (b, do_cb0 + h)
    kmap = lambda b, h, j: (b * nk + j, h)
    res = pl.pallas_call(
        body, name=name, grid=(B, H, nk),
        in_specs=[pl.BlockSpec((S, HEAD_PAD), smap), pl.BlockSpec((t, HEAD_PAD), kmap), pl.BlockSpec((t, 128), kmap),
                  pl.BlockSpec((S, 128), smap), pl.BlockSpec((S, 128), smap), pl.BlockSpec((S, 128), domap)] + s_in,
        out_specs=[pl.BlockSpec((t, HEAD_PAD), kmap), pl.BlockSpec((t, 128), kmap)] + s_out,
        out_shape=[jax.ShapeDtypeStruct((T, H * HEAD_PAD), _F32), jax.ShapeDtypeStruct((T, H * 128), _F32)] + s_shapes,
        input_output_aliases={6 + i_: 2 + o_ for i_, o_ in s_alias.items()},
        scratch_shapes=[pltpu.VMEM((t, HEAD_PAD), _F32), pltpu.VMEM((t, 128), _F32)] + s_scratch,
        compiler_params=_cparams(("arbitrary",) * 3 if side is not None else ("parallel",) * 3),
    )(qp, kp, v, o, lse, do, *s_ops)
    return (res[0], res[1]) if side is None else (res[0], res[1], list(res[2:]))


def _halo_specs(T, nT, tt, hr, cw, cb):
    k = tt // hr
    main = pl.BlockSpec((tt, cw), lambda b, t: (b * nT + t, cb))
    prev = pl.BlockSpec((hr, cw), lambda b, t: (jnp.maximum((b * nT + t) * k - 1, 0), cb))
    nxt = pl.BlockSpec((hr, cw), lambda b, t: (jnp.minimum((b * nT + t + 1) * k, T // hr - 1), cb))
    return main, prev, nxt


_CONV_ROWS = 32


def _ln_rows(z, g, b):
    mu = jnp.mean(z, axis=-1, keepdims=True)
    d = z - mu
    var = jnp.mean(d * d, axis=-1, keepdims=True)
    rstd = lax.rsqrt(var + LN_EPS)
    xh = d * rstd
    return xh * g + b, xh, rstd


def conv_fwd(h, cb_a, cb_g, w, bias, lng, lnb, *, B, S, name):
    T = B * S
    K, C = w.shape
    hr = 32
    assert K - 1 <= hr
    tt = _tile(S, 512, hr)
    nT = S // tt
    a_m, a_p, _ = _halo_specs(T, nT, tt, hr, C, cb_a)
    g_m, g_p, _ = _halo_specs(T, nT, tt, hr, C, cb_g)

    def body(a_ref, g_ref, ap_ref, gp_ref, w_ref, b_ref, lg_ref, lb_ref, z_ref, y_ref, buf):
        t = pl.program_id(1)
        buf[pl.ds(hr, tt), :] = a_ref[...] * jax.nn.sigmoid(g_ref[...])
        hp = ap_ref[...] * jax.nn.sigmoid(gp_ref[...])
        buf[pl.ds(0, hr), :] = jnp.where(t == 0, 0.0, hp)
        z = jnp.broadcast_to(b_ref[...], (tt, C))
        for k in range(K):
            z = z + w_ref[k:k + 1, :] * buf[pl.ds(hr - (K - 1) + k, tt), :]
        z_ref[...] = z
        n, _, _ = _ln_rows(z, lg_ref[...], lb_ref[...])
        y_ref[...] = (n * jax.nn.sigmoid(n)).astype(_BF)

    vec = pl.BlockSpec((1, C), lambda b, t: (0, 0))
    out = pl.BlockSpec((tt, C), lambda b, t: (b * nT + t, 0))
    return pl.pallas_call(
        body, name=name, grid=(B, nT),
        in_specs=[a_m, g_m, a_p, g_p, pl.BlockSpec((K, C), lambda b, t: (0, 0)), vec, vec, vec],
        out_specs=[out, out],
        out_shape=[jax.ShapeDtypeStruct((T, C), _F32), jax.ShapeDtypeStruct((T, C), _BF)],
        scratch_shapes=[pltpu.VMEM((hr + tt, C), _F32)],
        compiler_params=_cparams(("parallel", "parallel")),
    )(h, h, h, h, w, bias.reshape(1, C), lng.reshape(1, C), lnb.reshape(1, C))


def conv_bwd(dmix, cb_dy, z, h, cb_a, cb_g, w, lng, lnb, *, B, S, name):
    T = B * S
    K, C = w.shape
    hr = 32
    tt = _tile(S, 512, hr)
    nT = S // tt
    a_m, a_p, _ = _halo_specs(T, nT, tt, hr, C, cb_a)
    g_m, g_p, _ = _halo_specs(T, nT, tt, hr, C, cb_g)
    dy_m, _, dy_n = _halo_specs(T, nT, tt, hr, C, cb_dy)
    z_m, _, z_n = _halo_specs(T, nT, tt, hr, C, 0)

    def body(dy_ref, dyn_ref, z_ref, zn_ref, a_ref, g_ref, ap_ref, gp_ref, w_ref, lg_ref, lb_ref,
             da_ref, dg_ref, dw_ref, db_ref, dlg_ref, dlb_ref, bufz, bufh):
        b, t = pl.program_id(0), pl.program_id(1)
        lg, lb = lg_ref[...], lb_ref[...]

        def dz_of(dy, zz):
            n, xh, rstd = _ln_rows(zz, lg, lb)
            sg = jax.nn.sigmoid(n)
            dn = dy.astype(_F32) * (sg * (1.0 + n * (1.0 - sg)))
            gdn = dn * lg
            m1 = jnp.mean(gdn, axis=-1, keepdims=True)
            m2 = jnp.mean(gdn * xh, axis=-1, keepdims=True)
            return rstd * (gdn - m1 - xh * m2), dn, xh

        rc = _CONV_ROWS
        acc = [jnp.zeros((8, C), _F32) for _ in range(3)]
        for r in range(0, tt, rc):
            rows = pl.ds(r, rc)
            dz, dn, xh = dz_of(dy_ref[rows, :], z_ref[rows, :])
            bufz[rows, :] = dz
            acc = [acc[0] + _fold8(dz), acc[1] + _fold8(dn * xh), acc[2] + _fold8(dn)]
            bufh[pl.ds(hr + r, rc), :] = a_ref[rows, :] * jax.nn.sigmoid(g_ref[rows, :])
        dzn, _, _ = dz_of(dyn_ref[...], zn_ref[...])
        bufz[pl.ds(tt, hr), :] = jnp.where(t == nT - 1, 0.0, dzn)
        bufh[pl.ds(0, hr), :] = jnp.where(t == 0, 0.0, ap_ref[...] * jax.nn.sigmoid(gp_ref[...]))

        @pl.when((b == 0) & (t == 0))
        def _():
            dw_ref[...] = jnp.zeros_like(dw_ref)
            db_ref[...] = jnp.zeros_like(db_ref)
            dlg_ref[...] = jnp.zeros_like(dlg_ref)
            dlb_ref[...] = jnp.zeros_like(dlb_ref)

        for r in range(0, tt, rc):
            rows = pl.ds(r, rc)
            dhc = jnp.zeros((rc, C), _F32)
            for k in range(K):
                dhc = dhc + w_ref[k:k + 1, :] * bufz[pl.ds(K - 1 - k + r, rc), :]
            a, sg = a_ref[rows, :], jax.nn.sigmoid(g_ref[rows, :])
            da_ref[rows, :] = (dhc * sg).astype(_BF)
            dg_ref[rows, :] = (dhc * a * sg * (1.0 - sg)).astype(_BF)
        for k in range(K):
            tap = jnp.zeros((8, C), _F32)
            for r in range(0, tt, rc):
                tap = tap + _fold8(bufz[pl.ds(r, rc), :] * bufh[pl.ds(hr - (K - 1) + k + r, rc), :])
            dw_ref[k:k + 1, :] += jnp.sum(tap, axis=0, keepdims=True)
        db_ref[...] += jnp.sum(acc[0], axis=0, keepdims=True)
        dlg_ref[...] += jnp.sum(acc[1], axis=0, keepdims=True)
        dlb_ref[...] += jnp.sum(acc[2], axis=0, keepdims=True)

    vec = pl.BlockSpec((1, C), lambda b, t: (0, 0))
    out = pl.BlockSpec((tt, C), lambda b, t: (b * nT + t, 0))
    kc = pl.BlockSpec((K, C), lambda b, t: (0, 0))
    return pl.pallas_call(
        body, name=name, grid=(B, nT),
        in_specs=[dy_m, dy_n, z_m, z_n, a_m, g_m, a_p, g_p, kc, vec, vec],
        out_specs=[out, out, kc, vec, vec, vec],
        out_shape=[jax.ShapeDtypeStruct((T, C), _BF), jax.ShapeDtypeStruct((T, C), _BF),
                   jax.ShapeDtypeStruct((K, C), _F32)] + [jax.ShapeDtypeStruct((1, C), _F32)] * 3,
        scratch_shapes=[pltpu.VMEM((tt + hr, C), _F32), pltpu.VMEM((hr + tt, C), _F32)],
        compiler_params=_cparams(("arbitrary", "arbitrary")),
    )(dmix, dmix, z, z, h, h, h, h, w, lng.reshape(1, C), lnb.reshape(1, C))


def _pool_cnt(t, tt, w, rows):
    pos = t * tt + lax.broadcasted_iota(jnp.int32, (rows, 1), 0)
    return jnp.minimum(pos + 1, w).astype(_F32)


def pool_fwd(h, cb, wp, scale, *, B, S, name):
    T = B * S
    G, pg, _ = wp.shape
    C = G * pg
    assert pg == 128 and G == len(POOL_WINDOWS)
    hr = 16
    tt = _tile(S, 512, hr)
    nT = S // tt
    u_m, u_p, _ = _halo_specs(T, nT, tt, hr, C, cb)

    def body(u_ref, up_ref, wp_ref, sc_ref, y_ref, buf):
        t = pl.program_id(1)
        buf[pl.ds(hr, tt), :] = u_ref[...]
        buf[pl.ds(0, hr), :] = jnp.where(t == 0, 0.0, up_ref[...])
        for gi, w in enumerate(POOL_WINDOWS):
            ln = slice(gi * pg, (gi + 1) * pg)
            acc = buf[pl.ds(hr, tt), ln]
            for j in range(1, w):
                acc = acc + buf[pl.ds(hr - j, tt), ln]
            d = acc / _pool_cnt(t, tt, w, tt) - u_ref[:, ln]
            yg = jnp.dot(d.astype(_BF), wp_ref[gi].astype(_BF), preferred_element_type=_F32)
            y_ref[:, ln] = (yg * sc_ref[:, ln]).astype(_BF)

    return pl.pallas_call(
        body, name=name, grid=(B, nT),
        in_specs=[u_m, u_p, pl.BlockSpec((G, pg, pg), lambda b, t: (0, 0, 0)), pl.BlockSpec((1, C), lambda b, t: (0, 0))],
        out_specs=pl.BlockSpec((tt, C), lambda b, t: (b * nT + t, 0)),
        out_shape=jax.ShapeDtypeStruct((T, C), _BF),
        scratch_shapes=[pltpu.VMEM((hr + tt, C), _F32)],
        compiler_params=_cparams(("parallel", "parallel")),
    )(h, h, wp, scale.reshape(1, C))


def pool_bwd(dmix, cb_dy, h, cb, wp, scale, *, B, S, name):
    T = B * S
    G, pg, _ = wp.shape
    C = G * pg
    hr = 16
    tt = _tile(S, 512, hr)
    nT = S // tt
    u_m, u_p, _ = _halo_specs(T, nT, tt, hr, C, cb)
    dy_m, _, dy_n = _halo_specs(T, nT, tt, hr, C, cb_dy)

    def body(dy_ref, dyn_ref, u_ref, up_ref, wp_ref, sc_ref, du_ref, dwp_ref, dsc_ref, buf, bufe):
        b, t = pl.program_id(0), pl.program_id(1)
        buf[pl.ds(hr, tt), :] = u_ref[...]
        buf[pl.ds(0, hr), :] = jnp.where(t == 0, 0.0, up_ref[...])

        @pl.when((b == 0) & (t == 0))
        def _():
            dwp_ref[...] = jnp.zeros_like(dwp_ref)
            dsc_ref[...] = jnp.zeros_like(dsc_ref)

        nt = (((1,), (1,)), ((), ()))
        tn = (((0,), (0,)), ((), ()))
        for gi, w in enumerate(POOL_WINDOWS):
            ln = slice(gi * pg, (gi + 1) * pg)
            wg = wp_ref[gi].astype(_BF)
            sc = sc_ref[:, ln]
            dy = dy_ref[:, ln].astype(_F32)
            dz = (dy * sc).astype(_BF)
            dzn = (dyn_ref[:, ln].astype(_F32) * sc).astype(_BF)
            dd = lax.dot_general(dz, wg, nt, preferred_element_type=_F32)
            ddn = lax.dot_general(dzn, wg, nt, preferred_element_type=_F32)
            bufe[pl.ds(0, tt), ln] = dd / _pool_cnt(t, tt, w, tt)
            bufe[pl.ds(tt, hr), ln] = jnp.where(t == nT - 1, 0.0, ddn / _pool_cnt(t + 1, tt, w, hr))
            du = -dd
            for j in range(w):
                du = du + bufe[pl.ds(j, tt), ln]
            du_ref[:, ln] = du.astype(_BF)
            acc = buf[pl.ds(hr, tt), ln]
            for j in range(1, w):
                acc = acc + buf[pl.ds(hr - j, tt), ln]
            d = (acc / _pool_cnt(t, tt, w, tt) - u_ref[:, ln]).astype(_BF)
            dwp_ref[gi] += lax.dot_general(d, dz, tn, preferred_element_type=_F32)
            yg = jnp.dot(d, wg, preferred_element_type=_F32)
            dsc_ref[:, ln] += jnp.sum(dy * yg, axis=0, keepdims=True)

    return pl.pallas_call(
        body, name=name, grid=(B, nT),
        in_specs=[dy_m, dy_n, u_m, u_p, pl.BlockSpec((G, pg, pg), lambda b, t: (0, 0, 0)),
                  pl.BlockSpec((1, C), lambda b, t: (0, 0))],
        out_specs=[pl.BlockSpec((tt, C), lambda b, t: (b * nT + t, 0)), pl.BlockSpec((G, pg, pg), lambda b, t: (0, 0, 0)),
                   pl.BlockSpec((1, C), lambda b, t: (0, 0))],
        out_shape=[jax.ShapeDtypeStruct((T, C), _BF), jax.ShapeDtypeStruct((G, pg, pg), _F32),
                   jax.ShapeDtypeStruct((1, C), _F32)],
        scratch_shapes=[pltpu.VMEM((hr + tt, C), _F32), pltpu.VMEM((tt + hr, C), _F32)],
        compiler_params=_cparams(("arbitrary", "arbitrary")),
    )(dmix, dmix, h, h, wp, scale.reshape(1, C))


_FFN_HR = 16
_FFN_ROWS, _FFN_LANES = 32, 256


def _fold8(x):
    out = x[0:8]
    for r in range(8, x.shape[0], 8):
        out = out + x[r:r + 8]
    return out


def _silu_grad(x, sg):
    return sg * (1.0 + x * (1.0 - sg))


def _conv3(buf, w_ref, b_ref, off, rows, ln):
    c = b_ref[:, ln] + w_ref[0:1, ln] * buf[pl.ds(off, rows), ln]
    for k in (1, 2):
        c = c + w_ref[k:k + 1, ln] * buf[pl.ds(off + k, rows), ln]
    return c


def _ffn_chunks(nrows, ncols):
    lw = min(ncols, _FFN_LANES)
    return [(r, min(_FFN_ROWS, nrows - r), slice(l0, l0 + lw))
            for l0 in range(0, ncols, lw) for r in range(0, nrows, _FFN_ROWS)]


def gate_fwd(up, w, bias, *, B, S, name):
    T, F2 = up.shape
    F = F2 // 2
    hr = _FFN_HR
    tt = _tile(S, 512, hr)
    nT = S // tt
    tn = _tile(F, 512, 128)
    nC = F // tn
    k = tt // hr

    def body(a_ref, g_ref, ap_ref, gp_ref, wa_ref, wg_ref, ba_ref, bg_ref, o_ref, bufa, bufg):
        t = pl.program_id(2)
        for buf, m_ref, p_ref in ((bufa, a_ref, ap_ref), (bufg, g_ref, gp_ref)):
            buf[pl.ds(hr, tt), :] = m_ref[...].astype(_F32)
            buf[pl.ds(0, hr), :] = jnp.where(t == 0, 0.0, p_ref[...].astype(_F32))
        for r, rc, ln in _ffn_chunks(tt, tn):
            ca = _conv3(bufa, wa_ref, ba_ref, hr - 2 + r, rc, ln)
            cg = _conv3(bufg, wg_ref, bg_ref, hr - 2 + r, rc, ln)
            o_ref[pl.ds(r, rc), ln] = (ca * cg * jax.nn.sigmoid(cg)).astype(_BF)

    def main(off):
        return pl.BlockSpec((tt, tn), lambda b, j, t: (b * nT + t, j + off))

    def prev(off):
        return pl.BlockSpec((hr, tn), lambda b, j, t: (jnp.maximum((b * nT + t) * k - 1, 0), j + off))

    def wspec(rows, off):
        return pl.BlockSpec((rows, tn), lambda b, j, t: (0, j + off))

    return pl.pallas_call(
        body, name=name, grid=(B, nC, nT),
        in_specs=[main(0), main(nC), prev(0), prev(nC), wspec(3, 0), wspec(3, nC), wspec(1, 0), wspec(1, nC)],
        out_specs=pl.BlockSpec((tt, tn), lambda b, j, t: (b * nT + t, j)),
        out_shape=jax.ShapeDtypeStruct((T, F), _BF),
        scratch_shapes=[pltpu.VMEM((hr + tt, tn), _F32)] * 2,
        compiler_params=_cparams(("parallel", "parallel", "parallel")),
    )(up, up, up, up, w, w, bias.reshape(1, F2), bias.reshape(1, F2))


def gate_bwd(up, dact, w, bias, *, B, S, name):
    T, F2 = up.shape
    F = F2 // 2
    hr = _FFN_HR
    tt = _tile(S, 512, hr)
    nT = S // tt
    tn = _tile(F, 512, 128)
    nC = F // tn
    k = tt // hr
    ext = tt + hr

    def body(a_ref, g_ref, ap_ref, gp_ref, an_ref, gn_ref, d_ref, dn_ref, wa_ref, wg_ref, ba_ref, bg_ref,
             du_ref, dwa_ref, dwg_ref, dba_ref, dbg_ref, bufa, bufg, bufda, bufdg):
        b, t = pl.program_id(1), pl.program_id(2)
        last = t == nT - 1
        for buf, m_ref, p_ref, n_ref in ((bufa, a_ref, ap_ref, an_ref), (bufg, g_ref, gp_ref, gn_ref)):
            buf[pl.ds(hr, tt), :] = m_ref[...].astype(_F32)
            buf[pl.ds(0, hr), :] = jnp.where(t == 0, 0.0, p_ref[...].astype(_F32))
            buf[pl.ds(hr + tt, hr), :] = jnp.where(last, 0.0, n_ref[...].astype(_F32))
        for r, rc, ln in _ffn_chunks(ext, tn):
            ca = _conv3(bufa, wa_ref, ba_ref, hr - 2 + r, rc, ln)
            cg = _conv3(bufg, wg_ref, bg_ref, hr - 2 + r, rc, ln)
            sg = jax.nn.sigmoid(cg)
            if r < tt:
                da = d_ref[pl.ds(r, rc), ln].astype(_F32)
            else:
                da = jnp.where(last, 0.0, dn_ref[pl.ds(r - tt, rc), ln].astype(_F32))
            bufda[pl.ds(r, rc), ln] = da * cg * sg
            bufdg[pl.ds(r, rc), ln] = da * ca * _silu_grad(cg, sg)

        @pl.when((b == 0) & (t == 0))
        def _():
            for r in (dwa_ref, dwg_ref, dba_ref, dbg_ref):
                r[...] = jnp.zeros_like(r)

        lw, rc = min(tn, _FFN_LANES), _FFN_ROWS
        assert tt % rc == 0
        for half, bufd, buf, w_ref, dw_ref, db_ref in ((0, bufda, bufa, wa_ref, dwa_ref, dba_ref),
                                                       (1, bufdg, bufg, wg_ref, dwg_ref, dbg_ref)):
            for l0 in range(0, tn, lw):
                ln = slice(l0, l0 + lw)
                acc = [jnp.zeros((8, lw), _F32) for _ in range(4)]
                for r in range(0, tt, rc):
                    dc = [bufd[pl.ds(r + s_, rc), ln] for s_ in range(3)]
                    u = buf[pl.ds(hr + r, rc), ln]
                    du = w_ref[2:3, ln] * dc[0] + w_ref[1:2, ln] * dc[1] + w_ref[0:1, ln] * dc[2]
                    du_ref[half, pl.ds(r, rc), ln] = du.astype(_BF)
                    for kk in range(3):
                        acc[kk] = acc[kk] + _fold8(dc[2 - kk] * u)
                    acc[3] = acc[3] + _fold8(dc[0])
                for kk in range(3):
                    dw_ref[kk:kk + 1, ln] += jnp.sum(acc[kk], axis=0, keepdims=True)
                db_ref[:, ln] += jnp.sum(acc[3], axis=0, keepdims=True)

    def main(off):
        return pl.BlockSpec((tt, tn), lambda j, b, t: (b * nT + t, j + off))

    def prev(off):
        return pl.BlockSpec((hr, tn), lambda j, b, t: (jnp.maximum((b * nT + t) * k - 1, 0), j + off))

    def nxt(off):
        return pl.BlockSpec((hr, tn), lambda j, b, t: (jnp.minimum((b * nT + t + 1) * k, T // hr - 1), j + off))

    def wspec(rows, off):
        return pl.BlockSpec((rows, tn), lambda j, b, t: (0, j + off))

    both = pl.BlockSpec((2, tt, tn), lambda j, b, t: (0, b * nT + t, j))
    return pl.pallas_call(
        body, name=name, grid=(nC, B, nT),
        in_specs=[main(0), main(nC), prev(0), prev(nC), nxt(0), nxt(nC), main(0), nxt(0),
                  wspec(3, 0), wspec(3, nC), wspec(1, 0), wspec(1, nC)],
        out_specs=[both, wspec(3, 0), wspec(3, 0), wspec(1, 0), wspec(1, 0)],
        out_shape=[jax.ShapeDtypeStruct((2, T, F), _BF), jax.ShapeDtypeStruct((3, F), _F32), jax.ShapeDtypeStruct((3, F), _F32),
                   jax.ShapeDtypeStruct((1, F), _F32), jax.ShapeDtypeStruct((1, F), _F32)],
        scratch_shapes=[pltpu.VMEM((hr + ext, tn), _F32)] * 2 + [pltpu.VMEM((ext, tn), _F32)] * 2,
        compiler_params=_cparams(("parallel", "arbitrary", "arbitrary")),
    )(up, up, up, up, up, up, dact, dact, w, w, bias.reshape(1, F2), bias.reshape(1, F2))


def loss_head(y, target, *, name):
    T, C = y.shape
    tr = _row_tile(T, C)

    def body(y_ref, t_ref, dy_ref, acc_ref):
        @pl.when(pl.program_id(0) == 0)
        def _():
            acc_ref[...] = jnp.zeros_like(acc_ref)

        e = y_ref[...] - t_ref[...]
        dy_ref[...] = e * (1.0 / C)
        acc_ref[...] += jnp.sum(e * e, axis=0, keepdims=True) * (0.5 / C)

    row = pl.BlockSpec((tr, C), lambda i: (i, 0))
    return pl.pallas_call(
        body, name=name, grid=(T // tr,),
        in_specs=[row, row], out_specs=[row, pl.BlockSpec((1, C), lambda i: (0, 0))],
        out_shape=[jax.ShapeDtypeStruct((T, C), _F32), jax.ShapeDtypeStruct((1, C), _F32)],
        compiler_params=_cparams(("arbitrary",)),
    )(y, target)


def _adamw_update(w_ref, g_ref, m_ref, v_ref, d_ref, mo_ref, vo_ref):
    c1 = 1.0 - ADAM_B1 ** ADAM_STEP
    c2 = 1.0 - ADAM_B2 ** ADAM_STEP
    gg = g_ref[...]
    mn = ADAM_B1 * m_ref[...] + (1.0 - ADAM_B1) * gg
    vn = ADAM_B2 * v_ref[...] + (1.0 - ADAM_B2) * (gg * gg)
    d_ref[...] = -ADAM_LR * ((mn / c1) / (jnp.sqrt(vn / c2) + ADAM_EPS) + ADAM_WD * w_ref[...])
    mo_ref[...] = mn
    vo_ref[...] = vn


def adamw_small(ws, gs, ms, vs, *, name):
    n = len(ws)

    def body(*refs):
        for p in range(n):
            _adamw_update(*[refs[k * n + p] for k in range(7)])

    vm = pl.BlockSpec(memory_space=pltpu.VMEM)
    shapes = [jax.ShapeDtypeStruct(w.shape, _F32) for w in ws]
    res = pl.pallas_call(
        body, name=name, in_specs=[vm] * (4 * n), out_specs=[vm] * (3 * n), out_shape=shapes * 3,
        compiler_params=_cparams(),
    )(*ws, *gs, *ms, *vs)
    return res[:n], res[n:2 * n], res[2 * n:]


def adamw(w, g, m, v, *, side=None, name):
    R, C = w.shape
    tr = _tile(R, max(8, (256 * 1024) // C // 8 * 8), 8)
    s_ops, s_in, s_out, s_shapes, s_scratch, s_alias = _side_specs(side)
    ni, no = len(s_ops), len(s_shapes)

    def body(w_ref, g_ref, m_ref, v_ref, *rest):
        s_ins, outs, s_outs, sems = rest[:ni], rest[ni:ni + 3], rest[ni + 3:ni + 3 + no], rest[ni + 3 + no:]
        first, last = _grid_ends((R // tr,))
        if side is not None:
            @pl.when(first)
            def _():
                side.start(s_ins, s_outs, sems)

        _adamw_update(w_ref, g_ref, m_ref, v_ref, *outs)
        if side is not None:
            @pl.when(last)
            def _():
                side.wait(s_ins, s_outs, sems)

    blk = pl.BlockSpec((tr, C), lambda i: (i, 0))
    s = jax.ShapeDtypeStruct((R, C), _F32)
    res = pl.pallas_call(
        body, name=name, grid=(R // tr,),
        in_specs=[blk] * 4 + s_in, out_specs=[blk] * 3 + s_out, out_shape=[s, s, s] + s_shapes,
        input_output_aliases={4 + i_: 3 + o_ for i_, o_ in s_alias.items()},
        scratch_shapes=s_scratch,
        compiler_params=_cparams(("arbitrary",) if side is not None else ("parallel",)),
    )(w, g, m, v, *s_ops)
    return (res[0], res[1], res[2]) if side is None else (res[0], res[1], res[2], list(res[3:]))


_ANY = pl.BlockSpec(memory_space=pl.ANY)
_MESH = pl.DeviceIdType.MESH


def _place():
    return lax.axis_index("x"), lax.axis_index("y"), lax.axis_index("c")


def _other_chips(x, y):
    chips = [(1 - x, y), (x, 1 - y), (1 - x, 1 - y)]
    return chips, [2 * a + b for a, b in chips]


def _rcopy(src, dst, ssem, rsem, dev):
    return pltpu.make_async_remote_copy(src_ref=src, dst_ref=dst, send_sem=ssem, recv_sem=rsem,
                                        device_id=dev, device_id_type=_MESH)


def place_shard(w, l, chip_idx, *, name):
    _, _, hR, C = w.shape
    tr = _tile(hR, max(16, (512 * 1024) // C // 16 * 16), 16)

    def body(ci_ref, w_ref, o_ref):
        o_ref[...] = w_ref[...].astype(_BF)

    return pl.pallas_call(
        body, name=name,
        grid_spec=pltpu.PrefetchScalarGridSpec(
            num_scalar_prefetch=1, grid=(2, hR // tr),
            in_specs=[pl.BlockSpec((None, None, tr, C), lambda h, i, ci: (l, h, i, 0))],
            out_specs=pl.BlockSpec((None, None, tr, C), lambda h, i, ci: (ci[0], h, i, 0))),
        out_shape=jax.ShapeDtypeStruct((4, 2, hR, C), _BF),
        compiler_params=_cparams(("parallel", "parallel")),
    )(chip_idx, w)


class _Side:
    def __init__(self, arrays, out_shapes, aliases, n_sems, start, wait):
        self.arrays, self.out_shapes, self.aliases, self.n_sems = arrays, out_shapes, aliases, n_sems
        self.start, self.wait = start, wait


def gather_stage1_side(bufs):
    n = len(bufs)

    def copies(outs, sems, sending):
        x, y, c = _place()
        me = 2 * x + y
        chips, cidx = _other_chips(x, y)
        send, recv = sems
        out, back = [], []
        for k, chip in enumerate(chips):
            for p in range(n):
                mine, got = outs[p].at[me, c], outs[p].at[cidx[k], c]
                out.append(_rcopy(mine, mine, send.at[p * 3 + k], recv.at[p * 3 + k], (*chip, c)))
                if not sending:
                    back.append(_rcopy(got, got, send.at[p * 3 + k], recv.at[p * 3 + k], (*chip, c)))
        return out, back

    def start(ins, outs, sems):
        for cp in copies(outs, sems, True)[0]:
            cp.start()

    def wait(ins, outs, sems):
        out, back = copies(outs, sems, False)
        for cp in back:
            cp.wait_recv()
        for cp in out:
            cp.wait_send()

    return _Side(list(bufs), [jax.ShapeDtypeStruct(b.shape, b.dtype) for b in bufs], {p: p for p in range(n)},
                 [n * 3, n * 3], start, wait)


def gather_stage2_side(bufs):
    n = len(bufs)

    def copies(outs, sems, sending):
        x, y, c = _place()
        sib = (x, y, 1 - c)
        _, cidx = _other_chips(x, y)
        send, recv = sems
        out, back = [], []
        for k in range(3):
            for p in range(n):
                mine, got = outs[p].at[cidx[k], c], outs[p].at[cidx[k], 1 - c]
                out.append(_rcopy(mine, mine, send.at[p * 3 + k], recv.at[p * 3 + k], sib))
                if not sending:
                    back.append(_rcopy(got, got, send.at[p * 3 + k], recv.at[p * 3 + k], sib))
        return out, back

    def start(ins, outs, sems):
        for cp in copies(outs, sems, True)[0]:
            cp.start()

    def wait(ins, outs, sems):
        out, back = copies(outs, sems, False)
        for cp in back:
            cp.wait_recv()
        for cp in out:
            cp.wait_send()

    return _Side(list(bufs), [jax.ShapeDtypeStruct(b.shape, b.dtype) for b in bufs], {p: p for p in range(n)},
                 [n * 3, n * 3], start, wait)


def chip_exchange_side(ps):
    n = len(ps)

    def copies(ins, outs, sems, sending):
        x, y, c = _place()
        me = 2 * x + y
        chips, cidx = _other_chips(x, y)
        send, recv = sems
        out, back = [], []
        for k, chip in enumerate(chips):
            for p in range(n):
                got = outs[p].at[cidx[k]]
                out.append(_rcopy(ins[p].at[cidx[k]], outs[p].at[me], send.at[p * 3 + k], recv.at[p * 3 + k], (*chip, c)))
                if not sending:
                    back.append(_rcopy(got, got, send.at[p * 3 + k], recv.at[p * 3 + k], (*chip, c)))
        return out, back

    def start(ins, outs, sems):
        for cp in copies(ins, outs, sems, True)[0]:
            cp.start()

    def wait(ins, outs, sems):
        out, back = copies(ins, outs, sems, False)
        for cp in back:
            cp.wait_recv()
        for cp in out:
            cp.wait_send()

    return _Side(list(ps), [jax.ShapeDtypeStruct(a.shape, a.dtype) for a in ps], {}, [n * 3, n * 3], start, wait)


def _side_specs(side):
    if side is None:
        return [], [], [], [], [], {}
    return (side.arrays, [_ANY] * len(side.arrays), [_ANY] * len(side.out_shapes), side.out_shapes,
            [pltpu.SemaphoreType.DMA((k,)) for k in side.n_sems], side.aliases)


def run_side(side, *, name):
    ni, no = len(side.arrays), len(side.out_shapes)

    def body(*refs):
        ins, outs, sems = refs[:ni], refs[ni:ni + no], refs[ni + no:]
        side.start(ins, outs, sems)
        side.wait(ins, outs, sems)

    ops, in_specs, out_specs, out_shapes, scratch, aliases = _side_specs(side)
    return pl.pallas_call(body, name=name, in_specs=in_specs, out_specs=out_specs, out_shape=out_shapes,
                          input_output_aliases=aliases, scratch_shapes=scratch)(*ops)


def gather_small(small, *, name):
    ns = len(small)

    def body(*refs):
        s_in, s_out = refs[:ns], refs[ns:2 * ns]
        send, recv, lsem = refs[2 * ns:]
        x, y, c = _place()
        me = 2 * x + y
        chips, cidx = _other_chips(x, y)
        local = [pltpu.make_async_copy(s_in[q], s_out[q].at[me], lsem.at[q]) for q in range(ns)]
        cps = [_rcopy(s_in[q], s_out[q].at[me], send.at[q * 3 + k], recv.at[q * 3 + k], (*chip, c))
               for k, chip in enumerate(chips) for q in range(ns)]
        for cp in local + cps:
            cp.start()
        for k in range(3):
            for q in range(ns):
                got = s_out[q].at[cidx[k]]
                _rcopy(got, got, send.at[q * 3 + k], recv.at[q * 3 + k], (x, y, c)).wait_recv()
        for cp in cps:
            cp.wait_send()
        for cp in local:
            cp.wait()

    return pl.pallas_call(
        body, name=name, in_specs=[_ANY] * ns, out_specs=[_ANY] * ns,
        out_shape=[jax.ShapeDtypeStruct((4,) + a.shape, a.dtype) for a in small],
        scratch_shapes=[pltpu.SemaphoreType.DMA((ns * 3,))] * 2 + [pltpu.SemaphoreType.DMA((ns,))],
    )(*small)


def sibling_send_half(gs, *, name):
    n = len(gs)

    def body(*refs):
        g_in, g_out, send, recv = refs[:n], refs[n:2 * n], refs[2 * n], refs[2 * n + 1]
        x, y, c = _place()
        sib = (x, y, 1 - c)
        cps = [_rcopy(g_in[p].at[1 - c], g_out[p], send.at[p], recv.at[p], sib) for p in range(n)]
        for cp in cps:
            cp.start()
        for cp in cps:
            cp.wait()

    return pl.pallas_call(
        body, name=name, in_specs=[_ANY] * n, out_specs=[_ANY] * n,
        out_shape=[jax.ShapeDtypeStruct(a.shape[1:], a.dtype) for a in gs],
        scratch_shapes=[pltpu.SemaphoreType.DMA((n,))] * 2,
    )(*gs)


def sum_chips(p, slots, idx, *, name):
    _, N, C = p.shape
    tr = _tile(N, max(16, (512 * 1024) // C // 16 * 16), 16)

    def body(i0, i1, i2, i3, i4, p_ref, s0_ref, s1_ref, s2_ref, o_ref):
        s = p_ref[...].astype(_F32)
        for r in (s0_ref, s1_ref, s2_ref):
            s = s + r[...].astype(_F32)
        o_ref[...] = s

    def at(k):
        return pl.BlockSpec((None, tr, C), lambda i, *ix: (ix[k][0], i, 0))

    return pl.pallas_call(
        body, name=name,
        grid_spec=pltpu.PrefetchScalarGridSpec(
            num_scalar_prefetch=5, grid=(N // tr,),
            in_specs=[at(0), at(1), at(2), at(3)], out_specs=at(4)),
        out_shape=jax.ShapeDtypeStruct((2, N, C), _F32),
        compiler_params=_cparams(("parallel",)),
    )(*idx, p, slots, slots, slots)


def sibling_join(rs, *, name):
    n = len(rs)

    def body(*refs):
        r_out, send, recv = refs[n:2 * n], refs[2 * n], refs[2 * n + 1]
        x, y, c = _place()
        sib = (x, y, 1 - c)
        cps = [_rcopy(r_out[p].at[c], r_out[p].at[c], send.at[p], recv.at[p], sib) for p in range(n)]
        for cp in cps:
            cp.start()
        for p in range(n):
            got = r_out[p].at[1 - c]
            _rcopy(got, got, send.at[p], recv.at[p], sib).wait_recv()
        for cp in cps:
            cp.wait_send()

    return pl.pallas_call(
        body, name=name, in_specs=[_ANY] * n, out_specs=[_ANY] * n,
        out_shape=[jax.ShapeDtypeStruct(a.shape, a.dtype) for a in rs],
        input_output_aliases={p: p for p in range(n)},
        scratch_shapes=[pltpu.SemaphoreType.DMA((n,))] * 2,
    )(*rs)


def all_devices_exchange_side(v):
    def copies(ins, outs, sems):
        x, y, c = _place()
        me = 4 * x + 2 * y + c
        send, recv, lsem = sems
        peers = [(x ^ (k >> 2), y ^ ((k >> 1) & 1), c ^ (k & 1)) for k in range(1, 8)]
        local = pltpu.make_async_copy(ins[0], outs[0].at[me], lsem.at[0])
        out = [_rcopy(ins[0], outs[0].at[me], send.at[k], recv.at[k], peer) for k, peer in enumerate(peers)]
        return local, out, peers

    def start(ins, outs, sems):
        local, out, _ = copies(ins, outs, sems)
        local.start()
        for cp in out:
            cp.start()

    def wait(ins, outs, sems):
        local, out, peers = copies(ins, outs, sems)
        for k, (px, py, pc) in enumerate(peers):
            got = outs[0].at[4 * px + 2 * py + pc]
            _rcopy(got, got, sems[0].at[k], sems[1].at[k], peers[k]).wait_recv()
        for cp in out:
            cp.wait_send()
        local.wait()

    return _Side([v], [jax.ShapeDtypeStruct((8,) + v.shape, v.dtype)], {}, [7, 7, 1], start, wait)


def add_halves(gs_and_rs, c_idx, *, name):
    outs = []
    for n_, (g, r) in enumerate(gs_and_rs):
        N, C = r.shape
        tr = _tile(N, max(16, (512 * 1024) // C // 16 * 16), 16)

        def body(c_ref, g_ref, r_ref, o_ref):
            o_ref[...] = (g_ref[...].astype(_F32) + r_ref[...].astype(_F32)).astype(o_ref.dtype)

        outs.append(pl.pallas_call(
            body, name=f"{name}_{n_}",
            grid_spec=pltpu.PrefetchScalarGridSpec(
                num_scalar_prefetch=1, grid=(N // tr,),
                in_specs=[pl.BlockSpec((None, tr, C), lambda i, c: (c[0], i, 0)), pl.BlockSpec((tr, C), lambda i, c: (i, 0))],
                out_specs=pl.BlockSpec((tr, C), lambda i, c: (i, 0))),
            out_shape=jax.ShapeDtypeStruct((N, C), r.dtype),
            compiler_params=_cparams(("parallel",)),
        )(c_idx, g, r))
    return outs


def sum_slots(a, *, name):
    n, N, C = a.shape
    tr = _tile(N, max(16, (512 * 1024) // C // 16 * 16), 16)

    def body(a_ref, o_ref):
        s = a_ref[0].astype(_F32)
        for k in range(1, n):
            s = s + a_ref[k].astype(_F32)
        o_ref[...] = s

    return pl.pallas_call(
        body, name=name, grid=(N // tr,),
        in_specs=[pl.BlockSpec((n, tr, C), lambda i: (0, i, 0))],
        out_specs=pl.BlockSpec((tr, C), lambda i: (i, 0)),
        out_shape=jax.ShapeDtypeStruct((N, C), _F32),
        compiler_params=_cparams(("parallel",)),
    )(a)


_WEIGHTS = ['ln_in_g', 'ln_in_b', 'w_in', 'q_norm_g', 'w_uq', 'kv_norm_g', 'w_ukv', 'conv_w', 'conv_b', 'conv_ln_g',
            'conv_ln_b', 'w_pool', 'pool_scale', 'w_out', 'ln1_g', 'ln1_b', 'w_up', 'ffn_conv_w', 'ffn_conv_b', 'w_down',
            'ln2_g', 'ln2_b']
_BIG = ['w_in', 'w_uq', 'w_ukv', 'w_out', 'w_up', 'w_down']
_GRADS_EARLY = ('w_up', 'w_down')
_GRADS_LATE = tuple(n for n in _BIG if n not in _GRADS_EARLY)
_SMALL_SHARDED = ['conv_w', 'ffn_conv_w']
_SMALL = [n for n in _WEIGHTS if n not in _BIG]


def _rope_tables(positions):
    half = QK_ROPE // 2
    inv = 1.0 / (ROPE_THETA ** (jnp.arange(0, QK_ROPE, 2, dtype=_F32) / QK_ROPE))
    ang = positions.reshape(-1).astype(_F32)[:, None] * inv
    c, s = jnp.cos(ang), jnp.sin(ang)
    z = jnp.zeros_like(c)
    cc = jnp.concatenate([c, c, z, z], axis=1)
    sa = jnp.concatenate([-s, z, z, z], axis=1)
    sb = jnp.concatenate([z, s, z, z], axis=1)
    assert cc.shape[1] == 128 and half == 32
    return cc, sa, sb


def _in_pad(dims):
    D, QL, KVL, CW, PW, H, F = dims
    return (-(QL + 2 * CW + PW + KVL + 128)) % 512


def _layer_weights(full, dims):
    D, QL, KVL, CW, PW, H, F = dims
    w_in = full['w_in'].transpose(1, 0, 2).reshape(D, -1)
    o1, o2, o3, o4 = QL, QL + KVL, QL + KVL + QK_ROPE, QL + KVL + QK_ROPE + 2 * CW
    w_in_p = jnp.concatenate([w_in[:, :o1], w_in[:, o3:o4], w_in[:, o4:], w_in[:, o1:o2], w_in[:, o2:o3],
                              jnp.zeros((D, 128 - QK_ROPE + _in_pad(dims)), w_in.dtype)], axis=1)
    w_uq = full['w_uq'].reshape(QL, H, QK_NOPE + QK_ROPE)
    w_uq_p = jnp.pad(w_uq, ((0, 0), (0, 0), (0, HEAD_PAD - QK_NOPE - QK_ROPE))).reshape(QL, H * HEAD_PAD)
    return dict(
        w_in=w_in_p, w_uq=w_uq_p,
        w_ukv=full['w_ukv'].reshape(KVL, H * (QK_NOPE + V_HEAD)),
        w_out=full['w_out'].reshape(D, D),
        w_up=full['w_up'],
        w_down=full['w_down'].reshape(F, D),
    )


def _unpermute_w_in_grad(g, dims):
    D, QL, KVL, CW, PW, H, F = dims
    a, b_, c_ = QL, QL + 2 * CW, QL + 2 * CW + PW
    return jnp.concatenate([g[:, :a], g[:, c_:c_ + KVL], g[:, c_ + KVL:c_ + KVL + QK_ROPE], g[:, a:b_], g[:, b_:c_]], axis=1)


class _NoExchange:
    def __init__(self, layer_weights):
        self.layer_weights, self.grads = layer_weights, {}

    def weights(self, l):
        return self.layer_weights[l]

    def side(self, where, l):
        return None

    def side_done(self, where, l, outs):
        pass

    def grads_ready(self, l, g):
        self.grads.setdefault(l, {}).update(g)


def _with_side(hooks, where, l, fn):
    sd = hooks.side(where, l)
    res = fn(sd)
    if sd is None:
        return res
    hooks.side_done(where, l, res[-1])
    return res[0] if len(res) == 2 else res[:-1]


def _local_step(x, positions, target, small, dims, B, S, L, hooks):
    D, QL, KVL, CW, PW, H, F = dims
    T = B * S
    alpha = (2.0 * L) ** 0.25
    scale = float(QK_NOPE + QK_ROPE) ** -0.5
    cc, sa, sb = _rope_tables(positions)
    cb_q, cb_a, cb_g, cb_p = 0, QL // CW, QL // CW + 1, (QL + 2 * CW) // PW
    cb_kv, cb_kr = (QL + 2 * CW + PW) // KVL, (QL + 2 * CW + PW + KVL) // 128
    assert QL % CW == 0 and (QL + 2 * CW) % PW == 0 and (QL + 2 * CW + PW) % KVL == 0 and (QL + 2 * CW + PW + KVL) % 128 == 0

    xs, xb = ln_fwd([x], [1.0], small['ln_in_g'], small['ln_in_b'], want_r=False, name="ln_in")
    saved = []
    fa = dict(B=B, S=S, H=H, scale=scale)
    for l in range(L):
        W = hooks.weights(l)
        h = matmul(xb, W['w_in'], name="mm_in")
        qn = rms_fwd(h, cb_q, QL, small['q_norm_g'][l], name="rms_q")
        kvn = rms_fwd(h, cb_kv, KVL, small['kv_norm_g'][l], name="rms_kv")
        q = matmul(qn, W['w_uq'], name="mm_uq")
        kv = matmul(kvn, W['w_ukv'], name="mm_ukv")
        qp, kp, v = mla_pack(q, kv, h, cb_kr, cc, sa, sb, H=H, scale=scale, name="mla_pack")
        o, lse = _with_side(hooks, 'flash', l, lambda sd: flash_fwd(qp, kp, v, side=sd, name="flash_fwd", **fa))
        z, yc = conv_fwd(h, cb_a, cb_g, small['conv_w'][l], small['conv_b'][l], small['conv_ln_g'][l],
                         small['conv_ln_b'][l], B=B, S=S, name="conv_fwd")
        yp = pool_fwd(h, cb_p, small['w_pool'][l], small['pool_scale'][l], B=B, S=S, name="pool_fwd")
        mixed = jnp.concatenate([o.astype(_BF), yc, yp], axis=1)
        y1 = matmul(mixed, W['w_out'], name="mm_out")
        r1, x1, x1b = ln_fwd([xs, y1], [alpha, 1.0], small['ln1_g'][l], small['ln1_b'][l], want_r=True, name="ln1")
        up = _with_side(hooks, 'up', l, lambda sd: matmul(x1b, W['w_up'], b_nparts=4, out_dtype=_BF, side=sd, name="mm_up"))
        act = gate_fwd(up, small['ffn_conv_w'][l], small['ffn_conv_b'][l], B=B, S=S, name="gate_fwd")
        y2 = _with_side(hooks, 'down', l, lambda sd: matmul(act, W['w_down'], side=sd, name="mm_down"))
        r2, x2, x2b = ln_fwd([x1, y2], [alpha, 1.0], small['ln2_g'][l], small['ln2_b'][l], want_r=True, name="ln2")
        saved.append(dict(W=W, xb=xb, h=h, qn=qn, kvn=kvn, qp=qp, kp=kp, v=v, o=o, lse=lse, z=z, mixed=mixed, r1=r1,
                          x1b=x1b, up=up, act=act, r2=r2))
        xs, xb = x2, x2b

    dy, loss_cols = loss_head(xs, target, name="loss_head")
    gs = {n: [None] * L for n in _SMALL if n not in ('ln_in_g', 'ln_in_b')}
    d_terms, d_coefs = [dy], [1.0]
    zpad = jnp.zeros((T, _in_pad(dims)), _BF) if _in_pad(dims) else None
    for l in reversed(range(L)):
        sv = saved[l]
        W = sv['W']
        gb = {}
        dr2, dr2b, gs['ln2_g'][l], gs['ln2_b'][l] = ln_bwd(d_terms, d_coefs, sv['r2'], small['ln2_g'][l], name="ln2_bwd")
        dact = matmul(dr2b, W['w_down'], tb=True, out_dtype=_BF, name="mm_down_dx")
        gb['w_down'] = matmul(sv['act'], dr2b, ta=True, out_dtype=_BF, name="mm_down_dw")
        dup, dwa, dwg, dba, dbg = gate_bwd(sv['up'], dact, small['ffn_conv_w'][l], small['ffn_conv_b'][l],
                                           B=B, S=S, name="gate_bwd")
        gs['ffn_conv_w'][l] = jnp.concatenate([dwa, dwg], axis=1)
        gs['ffn_conv_b'][l] = jnp.concatenate([dba, dbg], axis=1)
        gb['w_up'] = _with_side(hooks, 'up_dw', l, lambda sd: matmul(sv['x1b'], dup, ta=True, b_nparts=2, out_nparts=4, out_dtype=_BF,
                                                                     side=sd, name="mm_up_dw"))
        dx1 = _with_side(hooks, 'up_dx', l, lambda sd: matmul(dup, W['w_up'], tb=True, a_halves=True, b_kparts=4, side=sd,
                                                              name="mm_up_dx"))
        hooks.grads_ready(l, {n: gb.pop(n) for n in _GRADS_EARLY})
        dr1, dr1b, gs['ln1_g'][l], gs['ln1_b'][l] = ln_bwd([dr2, dx1], [alpha, 1.0], sv['r1'], small['ln1_g'][l],
                                                            name="ln1_bwd")
        dmix = matmul(dr1b, W['w_out'], tb=True, name="mm_out_dx")
        gb['w_out'] = matmul(sv['mixed'], dr1b, ta=True, out_dtype=_BF, name="mm_out_dw")
        h = sv['h']
        ncb = (H * V_HEAD) // CW
        dca, dcg, gs['conv_w'][l], gs['conv_b'][l], gs['conv_ln_g'][l], gs['conv_ln_b'][l] = conv_bwd(
            dmix, ncb, sv['z'], h, cb_a, cb_g, small['conv_w'][l], small['conv_ln_g'][l], small['conv_ln_b'][l],
            B=B, S=S, name="conv_bwd")
        dpool, gs['w_pool'][l], gs['pool_scale'][l] = pool_bwd(
            dmix, (H * V_HEAD + CW) // PW, h, cb_p, small['w_pool'][l], small['pool_scale'][l], B=B, S=S, name="pool_bwd")
        att = (sv['qp'], sv['kp'], sv['v'], sv['o'], sv['lse'], dmix, 0)
        dqp = _with_side(hooks, 'flash_dq', l, lambda sd: flash_bwd_dq(*att, side=sd, name="flash_dq", **fa))
        dkp, dv = _with_side(hooks, 'flash_dkv', l, lambda sd: flash_bwd_dkv(*att, side=sd, name="flash_dkv", **fa))
        dq, dkv, dkr = mla_unpack(dqp, dkp, dv, cc, sa, sb, H=H, name="mla_unpack")
        dqn = matmul(dq, W['w_uq'], tb=True, name="mm_uq_dx")
        g_uq = matmul(sv['qn'], dq, ta=True, out_dtype=_BF, name="mm_uq_dw")
        gb['w_uq'] = g_uq.reshape(QL, H, HEAD_PAD)[:, :, :QK_NOPE + QK_ROPE].reshape(QL, -1)
        dkvn = matmul(dkv, W['w_ukv'], tb=True, name="mm_ukv_dx")
        gb['w_ukv'] = matmul(sv['kvn'], dkv, ta=True, out_dtype=_BF, name="mm_ukv_dw")
        dcq, gs['q_norm_g'][l] = rms_bwd(dqn, h, cb_q, QL, small['q_norm_g'][l], name="rms_q_bwd")
        dckv, gs['kv_norm_g'][l] = rms_bwd(dkvn, h, cb_kv, KVL, small['kv_norm_g'][l], name="rms_kv_bwd")
        dh = jnp.concatenate([dcq, dca, dcg, dpool, dckv, dkr] + ([zpad] if zpad is not None else []), axis=1)
        gb['w_in'] = _unpermute_w_in_grad(matmul(sv['xb'], dh, ta=True, out_dtype=_BF, name="mm_in_dw"), dims)
        dxm = matmul(dh, W['w_in'], tb=True, name="mm_in_dx")
        hooks.grads_ready(l, gb)
        d_terms, d_coefs = [dr1, dxm], [alpha, 1.0]
    gx, _, g_ln_g, g_ln_b = ln_bwd(d_terms, d_coefs, x, small['ln_in_g'], name="ln_in_bwd")
    gsm = {n: jnp.stack([a.reshape(small[n].shape[1:]) for a in gs[n]]) for n in gs}
    gsm['ln_in_g'], gsm['ln_in_b'] = g_ln_g.reshape(-1), g_ln_b.reshape(-1)
    return loss_cols, gx, gsm


_COL_SHARDED = ('w_in', 'w_up')


def _flat_pad(arrs, mult=512 * 128):
    v = jnp.concatenate([a.reshape(-1) for a in arrs])
    n = v.shape[0]
    return jnp.pad(v, (0, (-n) % mult)).reshape(-1, 128)


def _split_like(flat, like):
    out, off = [], 0
    v = flat.reshape(-1)
    for a in like:
        out.append(v[off:off + a.size].reshape(a.shape))
        off += a.size
    return out


_GROUP_A = ('w_up',)
_GROUP_B = tuple(n for n in _BIG if n not in _GROUP_A)


class _Exchange:
    def __init__(self, a, dims, L, chip_idx, c_idx, sum_idx):
        self.dims, self.L, self.c_idx, self.sum_idx = dims, L, c_idx, sum_idx
        self.rows = {n: (a[n].shape[1], math.prod(a[n].shape[2:])) for n in _BIG}
        self.bufs = []
        for l in range(L):
            self.bufs.append({n: place_shard(a[n].reshape(L, 2, self.rows[n][0] // 2, self.rows[n][1]), l, chip_idx,
                                             name="place_" + n) for n in _BIG})
        self._store(0, _BIG, run_side(gather_stage1_side(self._list(0, _BIG)), name="gather_first_ici"))
        self._store(0, _BIG, run_side(gather_stage2_side(self._list(0, _BIG)), name="gather_first_d2d"))
        self.pending = {}
        self.reduced = {}

    def _list(self, l, names):
        return [self.bufs[l][n] for n in names]

    def _store(self, l, names, outs):
        self.bufs[l].update(zip(names, outs))

    def weights(self, l):
        return _layer_weights({n: self.bufs[l][n].reshape(4, *self.rows[n]) for n in _BIG}, self.dims)

    def side(self, where, l):
        if where in ('flash', 'up', 'down'):
            if l + 1 >= self.L:
                return None
            if where == 'down':
                return gather_stage2_side(self._list(l + 1, _BIG))
            return gather_stage1_side(self._list(l + 1, _GROUP_B if where == 'flash' else _GROUP_A))
        tag, names = self._riders(where)
        if tag not in self.pending:
            return None
        return chip_exchange_side([self.pending[tag][1][n] for n in names])

    @staticmethod
    def _riders(where):
        return {'flash_dq': ('early', ('w_down',)), 'flash_dkv': ('early', ('w_up',)), 'up_dw': ('late', _GRADS_LATE),
                'up_dx': ('none', ())}[where]

    def side_done(self, where, l, outs):
        if where in ('flash', 'up', 'down'):
            self._store(l + 1, {'flash': _GROUP_B, 'up': _GROUP_A, 'down': _BIG}[where], outs)
            return
        tag, names = self._riders(where)
        self.pending[tag][2].update(zip(names, outs))
        if len(self.pending[tag][2]) == len(self.pending[tag][1]):
            self._finish(tag)

    def _finish(self, tag):
        l, sums, slots = self.pending.pop(tag)
        halves = [sum_chips(sums[n], slots[n], self.sum_idx, name="grad_sum_" + n) for n in sums]
        joined = sibling_join(halves, name="grad_sibling_join_" + tag)
        self.reduced.setdefault(l, {}).update({n: j.reshape(self.rows[n]) for n, j in zip(sums, joined)})

    def grads_ready(self, l, g):
        names = tuple(n for n in _BIG if n in g)
        tag = 'early' if names == tuple(n for n in _BIG if n in _GRADS_EARLY) else 'late'
        g_in = []
        for n in names:
            r, c = self.rows[n]
            if g[n].ndim == 3:
                st = g[n]
            else:
                st = g[n].reshape(g[n].shape[0], 4, c).transpose(1, 0, 2) if n in _COL_SHARDED else g[n].reshape(4, r, c)
            g_in.append(st.reshape(4, 2, r // 2, c).transpose(1, 0, 2, 3).reshape(2, 2 * r, c))
        from_sib = sibling_send_half(g_in, name="grad_sibling_send_" + tag)
        sums = add_halves(list(zip(g_in, from_sib)), self.c_idx, name="grad_presum_" + tag)
        self.pending[tag] = (l, {n: p.reshape(4, p.shape[0] // 4, p.shape[1]) for n, p in zip(names, sums)}, {})

    def last_side(self):
        return chip_exchange_side(list(self.pending['late'][1].values()))

    def last_done(self, outs):
        self.pending['late'][2].update(zip(self.pending['late'][1], outs))
        self._finish('late')


def kernel(x, positions, ln_in_g, ln_in_b, w_in, q_norm_g, w_uq, kv_norm_g, w_ukv, conv_w, conv_b, conv_ln_g, conv_ln_b, w_pool, pool_scale, w_out, ln1_g, ln1_b, w_up, ffn_conv_w, ffn_conv_b, w_down, ln2_g, ln2_b, loss_target, m_ln_in_g, m_ln_in_b, m_w_in, m_q_norm_g, m_w_uq, m_kv_norm_g, m_w_ukv, m_conv_w, m_conv_b, m_conv_ln_g, m_conv_ln_b, m_w_pool, m_pool_scale, m_w_out, m_ln1_g, m_ln1_b, m_w_up, m_ffn_conv_w, m_ffn_conv_b, m_w_down, m_ln2_g, m_ln2_b, v_ln_in_g, v_ln_in_b, v_w_in, v_q_norm_g, v_w_uq, v_kv_norm_g, v_w_ukv, v_conv_w, v_conv_b, v_conv_ln_g, v_conv_ln_b, v_w_pool, v_pool_scale, v_w_out, v_ln1_g, v_ln1_b, v_w_up, v_ffn_conv_w, v_ffn_conv_b, v_w_down, v_ln2_g, v_ln2_b):
    a = dict(locals())
    B, S, D = a['x'].shape
    T = B * S
    L = a['w_in'].shape[0]
    QL, H = 4 * a['w_uq'].shape[1], a['w_uq'].shape[2]
    KVL = 4 * a['w_ukv'].shape[1]
    CW, PW = a['conv_b'].shape[1], a['pool_scale'].shape[1]
    F = 4 * a['w_down'].shape[1]
    dims = (D, QL, KVL, CW, PW, H, F)
    chip = 2 * lax.axis_index("x") + lax.axis_index("y")
    c_idx = lax.axis_index("c").astype(jnp.int32).reshape(1)

    def shard2d(w):
        return w.reshape(-1, w.shape[-1]) if w.ndim == 3 else w.reshape(w.shape[0] * w.shape[1], -1)

    small = {n: a[n] for n in _SMALL}
    for n, o in zip(_SMALL_SHARDED, gather_small([shard2d(a[n]) for n in _SMALL_SHARDED], name="gather_small")):
        k = a[n].shape[1]
        small[n] = o.reshape(4, L, k, -1).transpose(1, 2, 0, 3).reshape(L, k, -1)
    xi, yi = lax.axis_index("x"), lax.axis_index("y")
    chip_idx = chip.astype(jnp.int32).reshape(1)
    sum_idx = [v.astype(jnp.int32).reshape(1) for v in [chip] + _other_chips(xi, yi)[1] + [lax.axis_index("c")]]
    ex = _Exchange(a, dims, L, chip_idx, c_idx, sum_idx)

    loss_cols, gx, gsm = _local_step(a['x'].reshape(T, D), a['positions'], a['loss_target'].reshape(T, D),
                                     small, dims, B, S, L, ex)
    loss = lax.psum(jnp.sum(loss_cols), ("x", "y", "c"))

    grads, delta, new_m, new_v = {}, {}, {}, {}
    sm_like = [gsm[n] for n in _SMALL]
    riders = {'w_up': all_devices_exchange_side(_flat_pad(sm_like)), 'w_down': ex.last_side()}
    for n in _GRADS_EARLY + _GRADS_LATE:
        grads[n] = jnp.concatenate([ex.reduced[l][n] for l in range(L)]).reshape(a[n].shape)
        res = adamw(shard2d(a[n]), shard2d(grads[n]), shard2d(a['m_' + n]), shard2d(a['v_' + n]), side=riders.get(n),
                    name="adamw_" + n)
        delta[n], new_m[n], new_v[n] = (t.reshape(a[n].shape) for t in res[:3])
        if n == 'w_up':
            sm_slots = res[3][0]
        elif n == 'w_down':
            ex.last_done(res[3])

    g_small = dict(zip(_SMALL, _split_like(sum_slots(sm_slots, name="small_sum"), sm_like)))
    for n in _SMALL_SHARDED:
        w = a[n].shape[-1]
        g_small[n] = lax.dynamic_slice_in_dim(g_small[n], chip * w, w, axis=2)

    def at_least_2d(t):
        return t.reshape(1, -1) if t.ndim == 1 else t

    d_, m_, v_ = adamw_small(*[[at_least_2d(src[n]) for n in _SMALL] for src in (
        a, g_small, {n: a['m_' + n] for n in _SMALL}, {n: a['v_' + n] for n in _SMALL})], name="adamw_small")
    for n, dd, mm, vv in zip(_SMALL, d_, m_, v_):
        grads[n], delta[n], new_m[n], new_v[n] = g_small[n], dd.reshape(a[n].shape), mm.reshape(a[n].shape), vv.reshape(a[n].shape)

    return (loss, gx.reshape(B, S, D), *[grads[n] for n in _WEIGHTS], *[delta[n] for n in _WEIGHTS],
            *[new_m[n] for n in _WEIGHTS], *[new_v[n] for n in _WEIGHTS])
```
